```python
import jax, jax.numpy as jnp
from jax import lax
import numpy as np

D_MODEL = 4096
BATCH = 8
SEQ = 4096
DEPTH = 1

CTX_LEN = 256
GRID_W = 64
MIX_WIDTH = D_MODEL
ATTN_WIDTH = MIX_WIDTH // 2
POOL_WIDTH = MIX_WIDTH - ATTN_WIDTH
HEAD_DIM = 128
N_HEADS = ATTN_WIDTH // HEAD_DIM
N_KV_HEADS = max(1, N_HEADS // 4)
GQA_GROUP = N_HEADS // N_KV_HEADS
KV_WIDTH = N_KV_HEADS * HEAD_DIM
ROPE_PAIRS = HEAD_DIM // 4
ROPE_THETA = 10000.0
ATTN_SCALE = HEAD_DIM ** -0.5
Q_BLOCK = 128
POOL_WINDOWS = (2, 4, 8, 16)
N_POOL_GROUPS = len(POOL_WINDOWS)
POOL_GROUP = POOL_WIDTH // N_POOL_GROUPS
EPS = 1e-6
IN_WIDTH = ATTN_WIDTH + 2 * KV_WIDTH + ATTN_WIDTH + POOL_WIDTH + POOL_WIDTH
SPLITS = [ATTN_WIDTH,
          ATTN_WIDTH + KV_WIDTH,
          ATTN_WIDTH + 2 * KV_WIDTH,
          2 * ATTN_WIDTH + 2 * KV_WIDTH,
          2 * ATTN_WIDTH + 2 * KV_WIDTH + POOL_WIDTH]

kernel_name = "hymba_gqa_pool_prefix_dit_layer"


def rms_norm(x, gain):
    xf = x.astype(jnp.float32)
    y = xf * lax.rsqrt(jnp.mean(xf * xf, axis=-1, keepdims=True) + EPS)
    return (y * gain.astype(jnp.float32)).astype(x.dtype)


def axial_rope_tables(n_tokens):
    rows = n_tokens // GRID_W
    row = jnp.repeat(jnp.arange(rows, dtype=jnp.float32), GRID_W)
    col = jnp.tile(jnp.arange(GRID_W, dtype=jnp.float32), rows)
    inv = ROPE_THETA ** (-jnp.arange(ROPE_PAIRS, dtype=jnp.float32) / ROPE_PAIRS)
    ang = jnp.concatenate([row[:, None] * inv, col[:, None] * inv], axis=-1)
    return jnp.cos(ang), jnp.sin(ang)


def apply_axial_rope(x, cos, sin):
    xf = x.astype(jnp.float32)
    c = cos[None, :, None, :]
    s = sin[None, :, None, :]

    def rot(v, cc, ss):
        v1, v2 = jnp.split(v, 2, axis=-1)
        return jnp.concatenate([v1 * cc - v2 * ss, v1 * ss + v2 * cc], axis=-1)

    x_row, x_col = jnp.split(xf, 2, axis=-1)
    out = jnp.concatenate([rot(x_row, c[..., :ROPE_PAIRS], s[..., :ROPE_PAIRS]),
                           rot(x_col, c[..., ROPE_PAIRS:], s[..., ROPE_PAIRS:])], axis=-1)
    return out.astype(x.dtype)


def attention_scores_to_out(q_blk, k_all, v_all):
    s = jnp.einsum('bqkgd,bskd->bkgqs', q_blk, k_all).astype(jnp.float32) * ATTN_SCALE
    p = jax.nn.softmax(s, axis=-1).astype(v_all.dtype)
    return jnp.einsum('bkgqs,bskd->bqkgd', p, v_all)


def latent_attention(q, k_all, v_all):
    b, n = q.shape[0], q.shape[1]
    nb = n // Q_BLOCK
    qb = q.reshape(b, nb, Q_BLOCK, N_KV_HEADS, GQA_GROUP, HEAD_DIM).transpose(1, 0, 2, 3, 4, 5)
    o = lax.map(lambda q_blk: attention_scores_to_out(q_blk, k_all, v_all), qb)
    return o.transpose(1, 0, 2, 3, 4, 5).reshape(b, n, ATTN_WIDTH)


def context_attention(q, k, v):
    b, l = q.shape[0], q.shape[1]
    qg = q.reshape(b, l, N_KV_HEADS, GQA_GROUP, HEAD_DIM)
    return attention_scores_to_out(qg, k, v).reshape(b, l, ATTN_WIDTH)


def multiscale_pool(u, pool_w, pool_scale):
    n = u.shape[1]
    t = jnp.arange(n)
    cs = jnp.pad(jnp.cumsum(u.astype(jnp.float32), axis=1), ((0, 0), (1, 0), (0, 0)))
    outs = []
    for gi, w in enumerate(POOL_WINDOWS):
        half = w // 2
        lo = jnp.clip(t - half, 0, n)
        hi = jnp.clip(t + half, 0, n)
        csg = cs[..., gi * POOL_GROUP:(gi + 1) * POOL_GROUP]
        win = jnp.take(csg, hi, axis=1) - jnp.take(csg, lo, axis=1)
        cnt = (hi - lo).astype(jnp.float32)[None, :, None]
        ug = u[..., gi * POOL_GROUP:(gi + 1) * POOL_GROUP].astype(jnp.float32)
        d = (win / cnt - ug).astype(u.dtype)
        outs.append(jnp.einsum('bnc,cd->bnd', d, pool_w[gi]))
    return jnp.concatenate(outs, axis=-1) * pool_scale


def merge_branches(attn_o, g_attn, u_pool, g_pool, pool_w, pool_scale, w_out):
    pool_o = multiscale_pool(u_pool, pool_w, pool_scale)
    y = jnp.concatenate([attn_o * jax.nn.silu(g_attn), pool_o * jax.nn.silu(g_pool)], axis=-1)
    return y @ w_out


def hybrid_layer(x, x_ctx, c_act, cc_act, w_ada, b_ada, g_pre, g_post, w_in,
                 g_q, g_k, pool_w, pool_scale, w_out, cos, sin, update_ctx):
    b, n = x.shape[0], x.shape[1]
    l = x_ctx.shape[1]
    shift, scale, gate = jnp.split(c_act @ w_ada + b_ada, 3, axis=-1)
    shift_c, scale_c, gate_c = jnp.split(cc_act @ w_ada + b_ada, 3, axis=-1)
    h = rms_norm(x, g_pre) * (1 + scale[:, None, :]) + shift[:, None, :]
    hc = rms_norm(x_ctx, g_pre) * (1 + scale_c) + shift_c

    q, k, v, g_attn, u_pool, g_pool = jnp.split(h @ w_in, SPLITS, axis=-1)
    q = apply_axial_rope(rms_norm(q.reshape(b, n, N_HEADS, HEAD_DIM), g_q), cos, sin)
    k = apply_axial_rope(rms_norm(k.reshape(b, n, N_KV_HEADS, HEAD_DIM), g_k), cos, sin)
    v = v.reshape(b, n, N_KV_HEADS, HEAD_DIM)

    if update_ctx:
        qc, kc, vc, gc_attn, uc_pool, gc_pool = jnp.split(hc @ w_in, SPLITS, axis=-1)
    else:
        kc, vc = jnp.split(hc @ w_in[:, SPLITS[0]:SPLITS[2]], 2, axis=-1)
    kc = rms_norm(kc.reshape(b, l, N_KV_HEADS, HEAD_DIM), g_k)
    vc = vc.reshape(b, l, N_KV_HEADS, HEAD_DIM)

    k_all = jnp.concatenate([kc, k], axis=1)
    v_all = jnp.concatenate([vc, v], axis=1)
    attn_o = latent_attention(q, k_all, v_all)
    out = merge_branches(attn_o, g_attn, u_pool, g_pool, pool_w, pool_scale, w_out)
    x_new = x + gate[:, None, :] * rms_norm(out, g_post)

    if update_ctx:
        qc = rms_norm(qc.reshape(b, l, N_HEADS, HEAD_DIM), g_q)
        attn_c = context_attention(qc, kc, vc)
        out_c = merge_branches(attn_c, gc_attn, uc_pool, gc_pool, pool_w, pool_scale, w_out)
        x_ctx = x_ctx + gate_c * rms_norm(out_c, g_post)
    return x_new, x_ctx


def _fwd_setup_inputs(seed: int = 0) -> dict:
    key = jax.random.key(seed)
    ks = jax.random.split(key, 16)
    f32 = jnp.float32
    x = jax.random.normal(ks[0], (BATCH, SEQ, D_MODEL), f32)
    c = jax.random.normal(ks[1], (BATCH, D_MODEL), f32)
    ctx = jax.random.normal(ks[2], (BATCH, CTX_LEN, D_MODEL), f32)
    c_ctx = jax.random.normal(ks[3], (D_MODEL,), f32)
    w_ada = jax.random.normal(ks[4], (DEPTH, D_MODEL, 3 * D_MODEL), f32) * (0.5 * D_MODEL ** -0.5)
    b_ada = jax.random.normal(ks[5], (DEPTH, 3 * D_MODEL), f32) * 0.01
    norm_pre = 1.0 + 0.01 * jax.random.normal(ks[6], (DEPTH, D_MODEL), f32)
    norm_post = 1.0 + 0.01 * jax.random.normal(ks[7], (DEPTH, D_MODEL), f32)
    w_in = jax.random.normal(ks[8], (DEPTH, D_MODEL, IN_WIDTH), f32) * D_MODEL ** -0.5
    q_norm = 1.0 + 0.01 * jax.random.normal(ks[9], (DEPTH, HEAD_DIM), f32)
    k_norm = 1.0 + 0.01 * jax.random.normal(ks[10], (DEPTH, HEAD_DIM), f32)
    pool_w = jax.random.normal(ks[11], (DEPTH, N_POOL_GROUPS, POOL_GROUP, POOL_GROUP), f32) * POOL_GROUP ** -0.5
    pool_scale = 1.0 + 0.02 * jax.random.normal(ks[12], (DEPTH, POOL_WIDTH), f32)
    w_out = jax.random.normal(ks[13], (DEPTH, MIX_WIDTH, D_MODEL), f32) * MIX_WIDTH ** -0.5
    return {"x": x, "c": c, "ctx": ctx, "c_ctx": c_ctx, "w_ada": w_ada, "b_ada": b_ada,
            "norm_pre": norm_pre, "norm_post": norm_post, "w_in": w_in, "q_norm": q_norm,
            "k_norm": k_norm, "pool_w": pool_w, "pool_scale": pool_scale, "w_out": w_out}


def _fwd_reference(x, c, ctx, c_ctx, w_ada, b_ada, norm_pre, norm_post, w_in, q_norm,
              k_norm, pool_w, pool_scale, w_out):
    cos, sin = axial_rope_tables(x.shape[1])
    c_act = jax.nn.silu(c)
    cc_act = jax.nn.silu(c_ctx)
    x_ctx = ctx
    for layer in range(DEPTH):
        x, x_ctx = hybrid_layer(x, x_ctx, c_act, cc_act, w_ada[layer], b_ada[layer],
                                norm_pre[layer], norm_post[layer], w_in[layer],
                                q_norm[layer], k_norm[layer], pool_w[layer],
                                pool_scale[layer], w_out[layer], cos, sin,
                                update_ctx=(layer < DEPTH - 1))
    return x


import jax as _jax
import jax.numpy as _jnp

TWIN_FORMAT = 'train_step'
FWD_PARAMS = ['x', 'c', 'ctx', 'c_ctx', 'w_ada', 'b_ada', 'norm_pre', 'norm_post', 'w_in', 'q_norm', 'k_norm', 'pool_w', 'pool_scale', 'w_out']
TWIN_WEIGHTS = ['c_ctx', 'w_ada', 'b_ada', 'norm_pre', 'norm_post', 'w_in', 'q_norm', 'k_norm', 'pool_w', 'pool_scale', 'w_out']
TWIN_DIFF_INPUT = 'x'
TWIN_INPUTS = ['x', 'c', 'ctx', 'c_ctx', 'w_ada', 'b_ada', 'norm_pre', 'norm_post', 'w_in', 'q_norm', 'k_norm', 'pool_w', 'pool_scale', 'w_out', 'loss_target', 'm_c_ctx', 'm_w_ada', 'm_b_ada', 'm_norm_pre', 'm_norm_post', 'm_w_in', 'm_q_norm', 'm_k_norm', 'm_pool_w', 'm_pool_scale', 'm_w_out', 'v_c_ctx', 'v_w_ada', 'v_b_ada', 'v_norm_pre', 'v_norm_post', 'v_w_in', 'v_q_norm', 'v_k_norm', 'v_pool_w', 'v_pool_scale', 'v_w_out']
TWIN_OUTPUTS = ['loss', 'grad_x', 'grad_c_ctx', 'grad_w_ada', 'grad_b_ada', 'grad_norm_pre', 'grad_norm_post', 'grad_w_in', 'grad_q_norm', 'grad_k_norm', 'grad_pool_w', 'grad_pool_scale', 'grad_w_out', 'delta_c_ctx', 'delta_w_ada', 'delta_b_ada', 'delta_norm_pre', 'delta_norm_post', 'delta_w_in', 'delta_q_norm', 'delta_k_norm', 'delta_pool_w', 'delta_pool_scale', 'delta_w_out', 'new_m_c_ctx', 'new_m_w_ada', 'new_m_b_ada', 'new_m_norm_pre', 'new_m_norm_post', 'new_m_w_in', 'new_m_q_norm', 'new_m_k_norm', 'new_m_pool_w', 'new_m_pool_scale', 'new_m_w_out', 'new_v_c_ctx', 'new_v_w_ada', 'new_v_b_ada', 'new_v_norm_pre', 'new_v_norm_post', 'new_v_w_in', 'new_v_q_norm', 'new_v_k_norm', 'new_v_pool_w', 'new_v_pool_scale', 'new_v_w_out']
TWIN_LEAF_KINDS = {'loss': 'loss', 'grad_x': 'grad_x', 'grad_c_ctx': 'grad_w', 'grad_w_ada': 'grad_w', 'grad_b_ada': 'grad_w', 'grad_norm_pre': 'grad_w', 'grad_norm_post': 'grad_w', 'grad_w_in': 'grad_w', 'grad_q_norm': 'grad_w', 'grad_k_norm': 'grad_w', 'grad_pool_w': 'grad_w', 'grad_pool_scale': 'grad_w', 'grad_w_out': 'grad_w', 'delta_c_ctx': 'delta_w', 'delta_w_ada': 'delta_w', 'delta_b_ada': 'delta_w', 'delta_norm_pre': 'delta_w', 'delta_norm_post': 'delta_w', 'delta_w_in': 'delta_w', 'delta_q_norm': 'delta_w', 'delta_k_norm': 'delta_w', 'delta_pool_w': 'delta_w', 'delta_pool_scale': 'delta_w', 'delta_w_out': 'delta_w', 'new_m_c_ctx': 'new_m', 'new_m_w_ada': 'new_m', 'new_m_b_ada': 'new_m', 'new_m_norm_pre': 'new_m', 'new_m_norm_post': 'new_m', 'new_m_w_in': 'new_m', 'new_m_q_norm': 'new_m', 'new_m_k_norm': 'new_m', 'new_m_pool_w': 'new_m', 'new_m_pool_scale': 'new_m', 'new_m_w_out': 'new_m', 'new_v_c_ctx': 'new_v', 'new_v_w_ada': 'new_v', 'new_v_b_ada': 'new_v', 'new_v_norm_pre': 'new_v', 'new_v_norm_post': 'new_v', 'new_v_w_in': 'new_v', 'new_v_q_norm': 'new_v', 'new_v_k_norm': 'new_v', 'new_v_pool_w': 'new_v', 'new_v_pool_scale': 'new_v', 'new_v_w_out': 'new_v'}


def _forward(args):
    return _fwd_reference(*[args[k] for k in FWD_PARAMS])


def _output_shape():
    out = _jax.eval_shape(lambda: _forward(_fwd_setup_inputs(0)))
    return out.shape, out.dtype

N_MICROBATCH = 1
ADAM_LR = 0.001
ADAM_B1 = 0.9
ADAM_B2 = 0.999
ADAM_EPS = 1e-08
ADAM_WD = 0.01
ADAM_STEP = 10
PER_EXAMPLE_BATCH_AXIS = {'x': 0, 'c': 0, 'ctx': 0, 'loss_target': 0}
SHARED_INPUTS = []
_WEIGHT_DTYPES = {'c_ctx': _jnp.float32, 'w_ada': _jnp.float32, 'b_ada': _jnp.float32, 'norm_pre': _jnp.float32, 'norm_post': _jnp.float32, 'w_in': _jnp.float32, 'q_norm': _jnp.float32, 'k_norm': _jnp.float32, 'pool_w': _jnp.float32, 'pool_scale': _jnp.float32, 'w_out': _jnp.float32}
MOMENT_SCALE = {'c_ctx': 3.337970e-03, 'w_ada': 4.084248e-01, 'b_ada': 7.991764e-01, 'norm_pre': 2.774930e-02, 'norm_post': 8.899784e-01, 'w_in': 1.944195e-02, 'q_norm': 9.536585e-03, 'k_norm': 8.966244e-03, 'pool_w': 2.711925e-02, 'pool_scale': 3.089116e-02, 'w_out': 2.155584e-02}


def _to_microbatches(a, axis):
    t = _jnp.moveaxis(a, axis, 0)
    t = t.reshape((N_MICROBATCH, t.shape[0] // N_MICROBATCH) + t.shape[1:])
    return _jnp.moveaxis(t, 1, axis + 1)


def setup_inputs(seed: int = 0) -> dict:
    inp = _fwd_setup_inputs(seed)
    key = _jax.random.fold_in(_jax.random.key(seed), 7919)
    shape, _ = _output_shape()
    out = dict(inp)
    out["loss_target"] = _jax.random.normal(_jax.random.fold_in(key, 0), shape, _jnp.float32)
    for i, name in enumerate(TWIN_WEIGHTS):
        w = inp[name].astype(_jnp.float32)
        if MOMENT_SCALE is None:
            s = _jnp.sqrt(_jnp.mean(_jnp.square(w)) + 1e-30)
        else:
            s = MOMENT_SCALE[name]
        km, kv = _jax.random.split(_jax.random.fold_in(key, i + 1))
        out[name] = w
        out["m_" + name] = s * _jax.random.normal(km, w.shape, _jnp.float32)
        out["v_" + name] = (s * s) * _jax.random.uniform(kv, w.shape, _jnp.float32, 0.5, 1.5)
    if N_MICROBATCH > 1:
        for name, axis in PER_EXAMPLE_BATCH_AXIS.items():
            out[name] = _to_microbatches(out[name], axis)
    return {'x': out['x'], 'c': out['c'], 'ctx': out['ctx'], 'c_ctx': out['c_ctx'], 'w_ada': out['w_ada'], 'b_ada': out['b_ada'], 'norm_pre': out['norm_pre'], 'norm_post': out['norm_post'], 'w_in': out['w_in'], 'q_norm': out['q_norm'], 'k_norm': out['k_norm'], 'pool_w': out['pool_w'], 'pool_scale': out['pool_scale'], 'w_out': out['w_out'], 'loss_target': out['loss_target'], 'm_c_ctx': out['m_c_ctx'], 'm_w_ada': out['m_w_ada'], 'm_b_ada': out['m_b_ada'], 'm_norm_pre': out['m_norm_pre'], 'm_norm_post': out['m_norm_post'], 'm_w_in': out['m_w_in'], 'm_q_norm': out['m_q_norm'], 'm_k_norm': out['m_k_norm'], 'm_pool_w': out['m_pool_w'], 'm_pool_scale': out['m_pool_scale'], 'm_w_out': out['m_w_out'], 'v_c_ctx': out['v_c_ctx'], 'v_w_ada': out['v_w_ada'], 'v_b_ada': out['v_b_ada'], 'v_norm_pre': out['v_norm_pre'], 'v_norm_post': out['v_norm_post'], 'v_w_in': out['v_w_in'], 'v_q_norm': out['v_q_norm'], 'v_k_norm': out['v_k_norm'], 'v_pool_w': out['v_pool_w'], 'v_pool_scale': out['v_pool_scale'], 'v_w_out': out['v_w_out']}


def _loss(weights, diff, rest, loss_target):
    with _jax.named_scope("forward"):
        args = {**rest, TWIN_DIFF_INPUT: diff, **{k: w.astype(_WEIGHT_DTYPES[k]) for k, w in weights.items()}}
        y = _forward(args)
    with _jax.named_scope("loss_head"):
        err = _jnp.square(y.astype(_jnp.float32) - loss_target)
        return 0.5 * _jnp.sum(_jnp.mean(err, axis=-1)) if err.ndim else 0.5 * err


def _adamw(w, g, m, v):
    m = ADAM_B1 * m + (1.0 - ADAM_B1) * g
    v = ADAM_B2 * v + (1.0 - ADAM_B2) * _jnp.square(g)
    m_hat = m / (1.0 - ADAM_B1 ** ADAM_STEP)
    v_hat = v / (1.0 - ADAM_B2 ** ADAM_STEP)
    delta = -ADAM_LR * (m_hat / (_jnp.sqrt(v_hat) + ADAM_EPS) + ADAM_WD * w)
    return delta, m, v


def reference(x, c, ctx, c_ctx, w_ada, b_ada, norm_pre, norm_post, w_in, q_norm, k_norm, pool_w, pool_scale, w_out, loss_target, m_c_ctx, m_w_ada, m_b_ada, m_norm_pre, m_norm_post, m_w_in, m_q_norm, m_k_norm, m_pool_w, m_pool_scale, m_w_out, v_c_ctx, v_w_ada, v_b_ada, v_norm_pre, v_norm_post, v_w_in, v_q_norm, v_k_norm, v_pool_w, v_pool_scale, v_w_out):
    given = dict(x=x, c=c, ctx=ctx, c_ctx=c_ctx, w_ada=w_ada, b_ada=b_ada, norm_pre=norm_pre, norm_post=norm_post, w_in=w_in, q_norm=q_norm, k_norm=k_norm, pool_w=pool_w, pool_scale=pool_scale, w_out=w_out, loss_target=loss_target, m_c_ctx=m_c_ctx, m_w_ada=m_w_ada, m_b_ada=m_b_ada, m_norm_pre=m_norm_pre, m_norm_post=m_norm_post, m_w_in=m_w_in, m_q_norm=m_q_norm, m_k_norm=m_k_norm, m_pool_w=m_pool_w, m_pool_scale=m_pool_scale, m_w_out=m_w_out, v_c_ctx=v_c_ctx, v_w_ada=v_w_ada, v_b_ada=v_b_ada, v_norm_pre=v_norm_pre, v_norm_post=v_norm_post, v_w_in=v_w_in, v_q_norm=v_q_norm, v_k_norm=v_k_norm, v_pool_w=v_pool_w, v_pool_scale=v_pool_scale, v_w_out=v_w_out)
    weights = {n: given[n] for n in TWIN_WEIGHTS}
    shared = {n: given[n] for n in SHARED_INPUTS}
    per_example = {n: given[n] for n in ['x', 'c', 'ctx']}
    grad_fn = _jax.value_and_grad(_loss, argnums=(0, 1))

    def one_microbatch(ex, loss_target):
        ex = dict(ex)
        diff = ex.pop(TWIN_DIFF_INPUT)
        return grad_fn(weights, diff, {**shared, **ex}, loss_target)

    if N_MICROBATCH == 1:
        loss, (grad_w, grad_x) = one_microbatch(per_example, given["loss_target"])
    else:
        def body(carry, xs):
            loss_sum, grad_sum = carry
            l_k, (gw_k, gx_k) = one_microbatch(xs[0], xs[1])
            with _jax.named_scope("update"):
                return (loss_sum + l_k, _jax.tree.map(_jnp.add, grad_sum, gw_k)), gx_k

        init = (_jnp.zeros((), _jnp.float32), _jax.tree.map(_jnp.zeros_like, weights))
        (loss, grad_w), grad_x = _jax.lax.scan(body, init, (per_example, given["loss_target"]))
    with _jax.named_scope("update"):
        delta_w, new_m, new_v = {}, {}, {}
        for n in TWIN_WEIGHTS:
            delta_w[n], new_m[n], new_v[n] = _adamw(weights[n], grad_w[n], given["m_" + n], given["v_" + n])
    return (loss, grad_x, *[grad_w[n] for n in TWIN_WEIGHTS], *[delta_w[n] for n in TWIN_WEIGHTS],
            *[new_m[n] for n in TWIN_WEIGHTS], *[new_v[n] for n in TWIN_WEIGHTS])
```

```python
import functools

import jax
import jax.numpy as jnp
from jax import lax
from jax.experimental import pallas as pl
from jax.experimental.pallas import tpu as pltpu

F32 = jnp.float32
BF = jnp.bfloat16
MESH = pl.DeviceIdType.MESH

HEAD_DIM = 128
GQA_GROUP = 4
GRID_W = 64
ROPE_PAIRS = HEAD_DIM // 4
ROPE_THETA = 10000.0
EPS = 1e-6
N_POOL_GROUPS = 4
POOL_HALO = 128
ADAM_LR = 0.001
ADAM_B1 = 0.9
ADAM_B2 = 0.999
ADAM_EPS = 1e-08
ADAM_WD = 0.01
ADAM_STEP = 10
MIB = 2 ** 20
VMEM_LIMIT = 48 * MIB
CHIP_MASKS = ((1, 0, 0), (0, 1, 0), (1, 1, 0))
ALL_MASKS = ((0, 0, 1), (0, 1, 0), (0, 1, 1), (1, 0, 0), (1, 0, 1), (1, 1, 0), (1, 1, 1))
HBM_SPEC = pl.BlockSpec(memory_space=pl.ANY)
VMEM_SPEC = pl.BlockSpec(memory_space=pltpu.VMEM)


def _cparams(*sem):
    return pltpu.CompilerParams(dimension_semantics=sem, vmem_limit_bytes=VMEM_LIMIT)


def _sigmoid(v):
    return 1.0 / (1.0 + jnp.exp(-v))


def _silu_and_grad(v):
    s = _sigmoid(v)
    return v * s, s * (1.0 + v * (1.0 - s))


def _dot(a, b, ca, cb):
    return lax.dot_general(a, b, (((ca,), (cb,)), ((), ())), preferred_element_type=F32)


def _block_rows(rows, width, itemsize=4, target=MIB):
    best = 8
    for t in range(8, rows + 1, 8):
        if rows % t == 0 and t * width * itemsize <= target:
            best = t
    return best if rows % 8 == 0 else rows


def _my_pos():
    return lax.axis_index("x"), lax.axis_index("y"), lax.axis_index("c")


def _flip(pos, mask):
    return tuple(jnp.bitwise_xor(p, m) if m else p for p, m in zip(pos, mask))


def _allgather_small(v, *, chips_only, name):
    r, w = v.shape
    masks = CHIP_MASKS if chips_only else ALL_MASKS
    nslot = 4 if chips_only else 8

    def slot(pos):
        return 2 * pos[0] + pos[1] if chips_only else 4 * pos[0] + 2 * pos[1] + pos[2]

    def body(v_ref, o_ref, send_sems, recv_sems, local_sem):
        me = _my_pos()

        def copy(k, block_of, to):
            return pltpu.make_async_remote_copy(
                src_ref=v_ref, dst_ref=o_ref.at[slot(block_of)], send_sem=send_sems.at[k], recv_sem=recv_sems.at[k],
                device_id=to, device_id_type=MESH)

        mine = pltpu.make_async_copy(v_ref, o_ref.at[slot(me)], local_sem)
        mine.start()
        sends = [copy(k, me, _flip(me, m)) for k, m in enumerate(masks)]
        for cp in sends:
            cp.start()
        for k, m in enumerate(masks):
            copy(k, _flip(me, m), me).wait_recv()
        for cp in sends:
            cp.wait_send()
        mine.wait()

    return pl.pallas_call(
        body, name=name, out_shape=jax.ShapeDtypeStruct((nslot, r, w), v.dtype),
        in_specs=[VMEM_SPEC], out_specs=VMEM_SPEC,
        scratch_shapes=[pltpu.SemaphoreType.DMA((len(masks),)), pltpu.SemaphoreType.DMA((len(masks),)),
                        pltpu.SemaphoreType.DMA],
    )(v)


def _sl(ref, axis, start, size):
    idx = [slice(None)] * len(ref.shape)
    idx[axis] = pl.ds(start, size)
    return ref.at[tuple(idx)]


def _allgather_weights(shards, layouts, *, name):
    nm = len(shards)

    def body(*refs):
        srcs, outs = refs[:nm], refs[nm:2 * nm]
        send_sems, recv_sems, fsend_sems, frecv_sems, local_sems = refs[2 * nm:]
        x, y, c = _my_pos()
        me = (x, y, c)
        sib = (x, y, 1 - c)
        locals_, sends, fwds = [], [], []
        for m in range(nm):
            a_s, a_h = layouts[m]
            ns, nh = srcs[m].shape[a_s], srcs[m].shape[a_h] // 2

            def region(chip_pos, half, m=m, a_s=a_s, a_h=a_h, ns=ns, nh=nh):
                j = 2 * chip_pos[0] + chip_pos[1]
                return _sl(_sl(outs[m], a_s, j * ns, ns), a_h, half * nh, nh)

            lc = pltpu.make_async_copy(srcs[m], _sl(outs[m], a_s, (2 * x + y) * ns, ns), local_sems.at[m])
            lc.start()
            locals_.append(lc)
            for k, mask in enumerate(CHIP_MASKS):
                other = _flip(me, mask)
                cp = pltpu.make_async_remote_copy(
                    src_ref=_sl(srcs[m], a_h, c * nh, nh), dst_ref=region(me, c),
                    send_sem=send_sems.at[m, k], recv_sem=recv_sems.at[m, k], device_id=other, device_id_type=MESH)
                cp.start()
                sends.append(cp)
                fwds.append((m, k, region(other, c), region(other, 1 - c)))
        started = []
        for m, k, landed, _ in fwds:
            pltpu.make_async_remote_copy(
                src_ref=landed, dst_ref=landed, send_sem=send_sems.at[m, k], recv_sem=recv_sems.at[m, k],
                device_id=me, device_id_type=MESH).wait_recv()
            fw = pltpu.make_async_remote_copy(
                src_ref=landed, dst_ref=landed, send_sem=fsend_sems.at[m, k], recv_sem=frecv_sems.at[m, k],
                device_id=sib, device_id_type=MESH)
            fw.start()
            started.append(fw)
        for m, k, _, from_sib in fwds:
            pltpu.make_async_remote_copy(
                src_ref=from_sib, dst_ref=from_sib, send_sem=fsend_sems.at[m, k], recv_sem=frecv_sems.at[m, k],
                device_id=sib, device_id_type=MESH).wait_recv()
        for cp in sends + started:
            cp.wait_send()
        for lc in locals_:
            lc.wait()

    out_shapes = []
    for s, (a_s, _) in zip(shards, layouts):
        shp = list(s.shape)
        shp[a_s] *= 4
        out_shapes.append(jax.ShapeDtypeStruct(tuple(shp), s.dtype))
    return pl.pallas_call(
        body, name=name, out_shape=tuple(out_shapes),
        in_specs=[HBM_SPEC] * nm, out_specs=tuple([HBM_SPEC] * nm),
        scratch_shapes=[pltpu.SemaphoreType.DMA((nm, 3)), pltpu.SemaphoreType.DMA((nm, 3)),
                        pltpu.SemaphoreType.DMA((nm, 3)), pltpu.SemaphoreType.DMA((nm, 3)),
                        pltpu.SemaphoreType.DMA((nm,))],
    )(*shards)


def _half_to_sibling(mats, half_axes, *, name):
    nm = len(mats)

    def body(*refs):
        srcs, outs = refs[:nm], refs[nm:2 * nm]
        send_sems, recv_sems = refs[2 * nm:]
        x, y, c = _my_pos()
        sib = (x, y, 1 - c)
        cps = []
        for m in range(nm):
            nh = srcs[m].shape[half_axes[m]] // 2
            cp = pltpu.make_async_remote_copy(
                src_ref=_sl(srcs[m], half_axes[m], (1 - c) * nh, nh), dst_ref=outs[m],
                send_sem=send_sems.at[m], recv_sem=recv_sems.at[m], device_id=sib, device_id_type=MESH)
            cp.start()
            cps.append(cp)
        for cp in cps:
            cp.wait()

    out_shapes = []
    for s, a_h in zip(mats, half_axes):
        shp = list(s.shape)
        shp[a_h] //= 2
        out_shapes.append(jax.ShapeDtypeStruct(tuple(shp), s.dtype))
    return pl.pallas_call(
        body, name=name, out_shape=tuple(out_shapes),
        in_specs=[HBM_SPEC] * nm, out_specs=tuple([HBM_SPEC] * nm),
        scratch_shapes=[pltpu.SemaphoreType.DMA((nm,)), pltpu.SemaphoreType.DMA((nm,))],
    )(*mats)


def _pieces_to_owners(mats, shard_axes, *, name):
    nm = len(mats)

    def body(*refs):
        srcs, outs = refs[:nm], refs[nm:nm + 3 * nm]
        send_sems, recv_sems = refs[nm + 3 * nm:]
        x, y, c = _my_pos()
        me = (x, y, c)
        cps = []
        for m in range(nm):
            ns = srcs[m].shape[shard_axes[m]] // 4
            for k, mask in enumerate(CHIP_MASKS):
                other = _flip(me, mask)
                cp = pltpu.make_async_remote_copy(
                    src_ref=_sl(srcs[m], shard_axes[m], (2 * other[0] + other[1]) * ns, ns), dst_ref=outs[3 * m + k],
                    send_sem=send_sems.at[m, k], recv_sem=recv_sems.at[m, k], device_id=other, device_id_type=MESH)
                cp.start()
                cps.append(cp)
        for cp in cps:
            cp.wait()

    out_shapes = []
    for s, a_s in zip(mats, shard_axes):
        shp = list(s.shape)
        shp[a_s] //= 4
        out_shapes += [jax.ShapeDtypeStruct(tuple(shp), s.dtype)] * 3
    return pl.pallas_call(
        body, name=name, out_shape=tuple(out_shapes),
        in_specs=[HBM_SPEC] * nm, out_specs=tuple([HBM_SPEC] * (3 * nm)),
        scratch_shapes=[pltpu.SemaphoreType.DMA((nm, 3)), pltpu.SemaphoreType.DMA((nm, 3))],
    )(*mats)


def _halves_join(halves, half_axes, *, name):
    nm = len(halves)

    def body(*refs):
        srcs, outs = refs[:nm], refs[nm:2 * nm]
        send_sems, recv_sems, local_sems = refs[2 * nm:]
        x, y, c = _my_pos()
        sib = (x, y, 1 - c)
        cps, lcs = [], []
        for m in range(nm):
            nh = srcs[m].shape[half_axes[m]]
            lc = pltpu.make_async_copy(srcs[m], _sl(outs[m], half_axes[m], c * nh, nh), local_sems.at[m])
            lc.start()
            lcs.append(lc)
            cp = pltpu.make_async_remote_copy(
                src_ref=srcs[m], dst_ref=_sl(outs[m], half_axes[m], c * nh, nh),
                send_sem=send_sems.at[m], recv_sem=recv_sems.at[m], device_id=sib, device_id_type=MESH)
            cp.start()
            cps.append(cp)
        for m, cp in enumerate(cps):
            nh = srcs[m].shape[half_axes[m]]
            cp.wait_send()
            pltpu.make_async_remote_copy(
                src_ref=srcs[m], dst_ref=_sl(outs[m], half_axes[m], (1 - c) * nh, nh),
                send_sem=send_sems.at[m], recv_sem=recv_sems.at[m], device_id=sib, device_id_type=MESH).wait_recv()
        for lc in lcs:
            lc.wait()

    out_shapes = []
    for s, a_h in zip(halves, half_axes):
        shp = list(s.shape)
        shp[a_h] *= 2
        out_shapes.append(jax.ShapeDtypeStruct(tuple(shp), s.dtype))
    return pl.pallas_call(
        body, name=name, out_shape=tuple(out_shapes),
        in_specs=[HBM_SPEC] * nm, out_specs=tuple([HBM_SPEC] * nm),
        scratch_shapes=[pltpu.SemaphoreType.DMA((nm,)), pltpu.SemaphoreType.DMA((nm,)), pltpu.SemaphoreType.DMA((nm,))],
    )(*halves)


def _cast_bf16(w, *, name):
    w2 = w.reshape(-1, w.shape[-1])
    rows, width = w2.shape
    tm = _block_rows(rows, width, 4, 2 * MIB)

    def body(w_ref, o_ref):
        o_ref[...] = w_ref[...].astype(BF)

    out = pl.pallas_call(
        body, name=name, grid=(rows // tm,), out_shape=jax.ShapeDtypeStruct(w2.shape, BF),
        in_specs=[pl.BlockSpec((tm, width), lambda i: (i, 0))], out_specs=pl.BlockSpec((tm, width), lambda i: (i, 0)),
        compiler_params=_cparams("parallel"),
    )(w2)
    return out.reshape(w.shape)


def _add_own_half(full, recv, half_axis, c, *, name):
    shp = recv.shape
    if half_axis == len(shp) - 1:
        rows = 1
        for d in shp[:-1]:
            rows *= d
        width = shp[-1]
        f2 = full.reshape(rows, 2 * width)
        r2 = recv.reshape(rows, width)
        tm = _block_rows(rows, width, 2, MIB)
        f_spec = pl.BlockSpec((tm, width), lambda i, c_ref: (i, c_ref[0]))
        grid = (rows // tm,)
    else:
        assert len(shp) == 2 and half_axis == 0
        rows, width = shp
        f2, r2 = full, recv
        tm = _block_rows(rows, width, 2, MIB)
        nb = rows // tm
        f_spec = pl.BlockSpec((tm, width), lambda i, c_ref: (c_ref[0] * nb + i, 0))
        grid = (nb,)

    def body(c_ref, f_ref, r_ref, o_ref):
        o_ref[...] = (f_ref[...].astype(F32) + r_ref[...].astype(F32)).astype(BF)

    out = pl.pallas_call(
        body, name=name, out_shape=jax.ShapeDtypeStruct(r2.shape, BF),
        grid_spec=pltpu.PrefetchScalarGridSpec(
            num_scalar_prefetch=1, grid=grid,
            in_specs=[f_spec, pl.BlockSpec((tm, width), lambda i, c_ref: (i, 0))],
            out_specs=pl.BlockSpec((tm, width), lambda i, c_ref: (i, 0))),
        compiler_params=_cparams("parallel"),
    )(c.reshape(1), f2, r2)
    return out.reshape(shp)


def _sum_own_block(mine_all, recvs, shard_axis, j, *, name):
    shp = recvs[0].shape
    if shard_axis == len(shp) - 1:
        rows = 1
        for d in shp[:-1]:
            rows *= d
        width = shp[-1]
        a2 = mine_all.reshape(rows, 4 * width)
        r2 = [r.reshape(rows, width) for r in recvs]
        tm = _block_rows(rows, width, 4, MIB)
        a_spec = pl.BlockSpec((tm, width), lambda i, j_ref: (i, j_ref[0]))
        grid = (rows // tm,)
    elif len(shp) == 2:
        rows, width = shp
        a2, r2 = mine_all, list(recvs)
        tm = _block_rows(rows, width, 4, MIB)
        nb = rows // tm
        a_spec = pl.BlockSpec((tm, width), lambda i, j_ref: (j_ref[0] * nb + i, 0))
        grid = (nb,)
    else:
        assert len(shp) == 3 and shard_axis == 1
        g, rows, width = shp
        a2, r2 = mine_all, list(recvs)
        a_spec = pl.BlockSpec((1, rows, width), lambda i, j_ref: (i, j_ref[0], 0))
        grid = (g,)
        tm = None

    def body(j_ref, a_ref, r0, r1, r2_, o_ref):
        o_ref[...] = ((a_ref[...].astype(F32) + r0[...].astype(F32)) + r1[...].astype(F32)) + r2_[...].astype(F32)

    if tm is None:
        r_spec = pl.BlockSpec((1, rows, width), lambda i, j_ref: (i, 0, 0))
    else:
        r_spec = pl.BlockSpec((tm, width), lambda i, j_ref: (i, 0))
    out = pl.pallas_call(
        body, name=name, out_shape=jax.ShapeDtypeStruct(r2[0].shape, F32),
        grid_spec=pltpu.PrefetchScalarGridSpec(
            num_scalar_prefetch=1, grid=grid, in_specs=[a_spec, r_spec, r_spec, r_spec], out_specs=r_spec),
        compiler_params=_cparams("parallel"),
    )(j.reshape(1), a2, *r2)
    return out.reshape(shp)


def _adamw(w, g, m, v, *, name):
    shp = w.shape
    width = shp[-1] if len(shp) > 1 else shp[0]
    rows = 1
    for d in shp[:-1]:
        rows *= d
    if len(shp) == 1:
        rows = 1
    args = [a.reshape(rows, width) for a in (w, g, m, v)]
    tm = _block_rows(rows, width, 4, MIB // 2)

    def body(w_ref, g_ref, m_ref, v_ref, d_ref, nm_ref, nv_ref):
        gv = g_ref[...]
        nm = ADAM_B1 * m_ref[...] + (1.0 - ADAM_B1) * gv
        nv = ADAM_B2 * v_ref[...] + (1.0 - ADAM_B2) * (gv * gv)
        m_hat = nm / (1.0 - ADAM_B1 ** ADAM_STEP)
        v_hat = nv / (1.0 - ADAM_B2 ** ADAM_STEP)
        d_ref[...] = -ADAM_LR * (m_hat / (jnp.sqrt(v_hat) + ADAM_EPS) + ADAM_WD * w_ref[...])
        nm_ref[...] = nm
        nv_ref[...] = nv

    spec = pl.BlockSpec((tm, width), lambda i: (i, 0))
    outs = pl.pallas_call(
        body, name=name, grid=(rows // tm,), out_shape=tuple([jax.ShapeDtypeStruct((rows, width), F32)] * 3),
        in_specs=[spec] * 4, out_specs=tuple([spec] * 3), compiler_params=_cparams("parallel"),
    )(*args)
    return tuple(o.reshape(shp) for o in outs)


def _matmul(a, b, *, ca, cb, tm, tn, tk, out_dtype, name, b_resident=False):
    m, kdim = a.shape[1 - ca], a.shape[ca]
    n = b.shape[1 - cb]
    nk = kdim // tk
    assert m % tm == 0 and n % tn == 0 and kdim % tk == 0

    def body(a_ref, b_ref, o_ref, *acc):
        part = _dot(a_ref[...], b_ref[...], ca, cb)
        if nk == 1:
            o_ref[...] = part.astype(o_ref.dtype)
            return
        k = pl.program_id(2)

        @pl.when(k == 0)
        def _():
            acc[0][...] = part

        @pl.when(jnp.logical_and(k > 0, k < nk - 1))
        def _():
            acc[0][...] += part

        @pl.when(k == nk - 1)
        def _():
            o_ref[...] = (acc[0][...] + part).astype(o_ref.dtype)

    if b_resident:
        gi = lambda p, q, k: (q, p, k)
        grid = (n // tn, m // tm, nk)
    else:
        gi = lambda p, q, k: (p, q, k)
        grid = (m // tm, n // tn, nk)

    def a_map(p, q, k):
        i, _, kk = gi(p, q, k)
        return (i, kk) if ca == 1 else (kk, i)

    def b_map(p, q, k):
        _, j, kk = gi(p, q, k)
        return (kk, j) if cb == 0 else (j, kk)

    def o_map(p, q, k):
        i, j, _ = gi(p, q, k)
        return (i, j)

    a_spec = pl.BlockSpec((tm, tk) if ca == 1 else (tk, tm), a_map)
    b_spec = pl.BlockSpec((tk, tn) if cb == 0 else (tn, tk), b_map)
    return pl.pallas_call(
        body, name=name, grid=grid, out_shape=jax.ShapeDtypeStruct((m, n), out_dtype),
        in_specs=[a_spec, b_spec], out_specs=pl.BlockSpec((tm, tn), o_map),
        scratch_shapes=[pltpu.VMEM((tm, tn), F32)] if nk > 1 else [],
        compiler_params=_cparams("parallel", "parallel", "arbitrary"),
    )(a, b)


def _prenorm(x, ctx, g_pre, ss, *, tm):
    n, d = x.shape
    l = ctx.shape[0]
    nc = l // tm

    def body(x_ref, c_ref, g_ref, ss_ref, h_ref):
        t = pl.program_id(0)

        def go(src, scale, shift):
            v = src[...]
            rstd = lax.rsqrt(jnp.mean(v * v, axis=-1, keepdims=True) + EPS)
            h_ref[...] = ((v * rstd * g_ref[...]) * (1.0 + scale) + shift).astype(BF)

        @pl.when(t < nc)
        def _():
            go(c_ref, ss_ref[2:3, :], ss_ref[3:4, :])

        @pl.when(t >= nc)
        def _():
            go(x_ref, ss_ref[0:1, :], ss_ref[1:2, :])

    return pl.pallas_call(
        body, name="prenorm", grid=((l + n) // tm,), out_shape=jax.ShapeDtypeStruct((l + n, d), BF),
        in_specs=[pl.BlockSpec((tm, d), lambda t: (jnp.maximum(t - nc, 0), 0)),
                  pl.BlockSpec((tm, d), lambda t: (jnp.minimum(t, nc - 1), 0)),
                  pl.BlockSpec((1, d), lambda t: (0, 0)), pl.BlockSpec((4, d), lambda t: (0, 0))],
        out_specs=pl.BlockSpec((tm, d), lambda t: (t, 0)), compiler_params=_cparams("parallel"),
    )(x, ctx, g_pre, ss)


def _swap32(v):
    lane = lax.broadcasted_iota(jnp.int32, v.shape, 1)
    return jnp.where((lane % 64) < 32, pltpu.roll(v, 96, 1), pltpu.roll(v, 32, 1))


def _qk_prep(proj, gain, ctab, stab, *, row0, col0, width, nrows, tm, name):
    cw = min(512, width)
    rb0, cb0 = row0 // tm, col0 // cw
    assert row0 % tm == 0 and col0 % cw == 0 and width % cw == 0 and nrows % tm == 0

    def body(p_ref, g_ref, c_ref, s_ref, o_ref):
        for hd in range(cw // HEAD_DIM):
            cols = slice(hd * HEAD_DIM, (hd + 1) * HEAD_DIM)
            v = p_ref[:, cols].astype(F32)
            rstd = lax.rsqrt(jnp.mean(v * v, axis=-1, keepdims=True) + EPS)
            yv = v * rstd * g_ref[...]
            o_ref[:, cols] = (yv * c_ref[...] + _swap32(yv) * s_ref[...]).astype(BF)

    return pl.pallas_call(
        body, name=name, grid=(nrows // tm, width // cw), out_shape=jax.ShapeDtypeStruct((nrows, width), BF),
        in_specs=[pl.BlockSpec((tm, cw), lambda i, j: (i + rb0, j + cb0)), pl.BlockSpec((1, HEAD_DIM), lambda i, j: (0, 0)),
                  pl.BlockSpec((tm, HEAD_DIM), lambda i, j: (i, 0)), pl.BlockSpec((tm, HEAD_DIM), lambda i, j: (i, 0))],
        out_specs=pl.BlockSpec((tm, cw), lambda i, j: (i, j)), compiler_params=_cparams("parallel", "parallel"),
    )(proj, gain, ctab, stab)


def _qk_bwd(dy, proj, gain, ctab, stab, *, row0, col0, tm, name):
    nrows, width = dy.shape
    cw = min(512, width)
    rb0, cb0 = row0 // tm, col0 // cw

    def body(d_ref, p_ref, g_ref, c_ref, s_ref, o_ref, dg_ref):
        @pl.when(jnp.logical_and(pl.program_id(0) == 0, pl.program_id(1) == 0))
        def _():
            dg_ref[...] = jnp.zeros_like(dg_ref)

        dg = jnp.zeros((1, HEAD_DIM), F32)
        for hd in range(cw // HEAD_DIM):
            cols = slice(hd * HEAD_DIM, (hd + 1) * HEAD_DIM)
            v = p_ref[:, cols].astype(F32)
            rstd = lax.rsqrt(jnp.mean(v * v, axis=-1, keepdims=True) + EPS)
            nv = v * rstd
            d = d_ref[:, cols]
            dyu = d * c_ref[...] + _swap32(d * s_ref[...])
            dg = dg + jnp.sum(dyu * nv, axis=0, keepdims=True)
            dn = dyu * g_ref[...]
            o_ref[:, cols] = (rstd * (dn - nv * jnp.mean(dn * nv, axis=-1, keepdims=True))).astype(BF)
        dg_ref[0:1, :] += dg

    return pl.pallas_call(
        body, name=name, grid=(nrows // tm, width // cw),
        out_shape=(jax.ShapeDtypeStruct((nrows, width), BF), jax.ShapeDtypeStruct((8, HEAD_DIM), F32)),
        in_specs=[pl.BlockSpec((tm, cw), lambda i, j: (i, j)), pl.BlockSpec((tm, cw), lambda i, j: (i + rb0, j + cb0)),
                  pl.BlockSpec((1, HEAD_DIM), lambda i, j: (0, 0)),
                  pl.BlockSpec((tm, HEAD_DIM), lambda i, j: (i, 0)), pl.BlockSpec((tm, HEAD_DIM), lambda i, j: (i, 0))],
        out_specs=(pl.BlockSpec((tm, cw), lambda i, j: (i, j)), pl.BlockSpec((8, HEAD_DIM), lambda i, j: (0, 0))),
        compiler_params=_cparams("arbitrary", "arbitrary"),
    )(dy, proj, gain, ctab, stab)


def _flash_fwd(qr, k_all, vt_tiles, *, tq, tk, scale):
    n, aw = qr.shape
    s_len, kvw = k_all.shape
    kvh = kvw // HEAD_DIM
    n_i, n_j = n // tq, s_len // tk
    gw = GQA_GROUP * HEAD_DIM

    def body(q_ref, k_ref, vt_ref, o_ref, lse_ref, acc_ref):
        for g in range(GQA_GROUP):
            cols = slice(g * HEAD_DIM, (g + 1) * HEAD_DIM)
            qg = q_ref[:, cols]
            acc_ref[...] = jnp.zeros_like(acc_ref)

            def step(j, carry):
                m_run, l_run = carry
                kj = k_ref[pl.ds(pl.multiple_of(j * tk, tk), tk), :]
                st = _dot(kj, qg, 1, 1) * scale
                m_new = jnp.maximum(m_run, jnp.max(st, axis=0, keepdims=True))
                alpha = jnp.exp(m_run - m_new)
                pt = jnp.exp(st - m_new)
                l_new = alpha * l_run + jnp.sum(pt, axis=0, keepdims=True)
                acc_ref[...] = acc_ref[...] * alpha + _dot(vt_ref[0, j], pt.astype(BF), 1, 0)
                return m_new, l_new

            m_fin, l_fin = lax.fori_loop(0, n_j, step, (jnp.full((1, tq), -1e30, F32), jnp.zeros((1, tq), F32)))
            o_ref[:, cols] = (acc_ref[...] / l_fin).T.astype(BF)
            lse_ref[0, g, 0] = m_fin + jnp.log(l_fin)

    return pl.pallas_call(
        body, name="flash_fwd", grid=(kvh, n_i),
        out_shape=(jax.ShapeDtypeStruct((n, aw), BF), jax.ShapeDtypeStruct((kvh, GQA_GROUP, n_i, 1, tq), F32)),
        in_specs=[pl.BlockSpec((tq, gw), lambda h, i: (i, h)), pl.BlockSpec((s_len, HEAD_DIM), lambda h, i: (0, h)),
                  pl.BlockSpec((1, n_j, HEAD_DIM, tk), lambda h, i: (h, 0, 0, 0))],
        out_specs=(pl.BlockSpec((tq, gw), lambda h, i: (i, h)),
                   pl.BlockSpec((1, GQA_GROUP, 1, 1, tq), lambda h, i: (h, 0, i, 0, 0))),
        scratch_shapes=[pltpu.VMEM((HEAD_DIM, tq), F32)],
        compiler_params=_cparams("parallel", "parallel"),
    )(qr, k_all, vt_tiles)


def _attn_delta(o, do, *, tq):
    n, aw = o.shape
    kvh = aw // (GQA_GROUP * HEAD_DIM)
    gw = GQA_GROUP * HEAD_DIM

    def body(o_ref, do_ref, d_ref):
        for g in range(GQA_GROUP):
            cols = slice(g * HEAD_DIM, (g + 1) * HEAD_DIM)
            prod = o_ref[:, cols].astype(F32) * do_ref[:, cols].astype(F32)
            d_ref[0, g, 0] = jnp.sum(prod.T, axis=0, keepdims=True)

    return pl.pallas_call(
        body, name="attn_delta", grid=(kvh, n // tq),
        out_shape=jax.ShapeDtypeStruct((kvh, GQA_GROUP, n // tq, 1, tq), F32),
        in_specs=[pl.BlockSpec((tq, gw), lambda h, i: (i, h)), pl.BlockSpec((tq, gw), lambda h, i: (i, h))],
        out_specs=pl.BlockSpec((1, GQA_GROUP, 1, 1, tq), lambda h, i: (h, 0, i, 0, 0)),
        compiler_params=_cparams("parallel", "parallel"),
    )(o, do)


def _flash_bwd(qr, do, k_all, v_all, lse, delta, *, tq, tk, scale):
    n, aw = qr.shape
    s_len, kvw = k_all.shape
    kvh = kvw // HEAD_DIM
    n_i, n_j = n // tq, s_len // tk
    gw = GQA_GROUP * HEAD_DIM

    def body(q_ref, do_ref, k_ref, v_ref, lse_ref, dl_ref, dq_ref, dk_ref, dv_ref):
        @pl.when(pl.program_id(1) == 0)
        def _():
            dq_ref[...] = jnp.zeros_like(dq_ref)

        kj = k_ref[...]
        vj = v_ref[...]
        dk_ref[...] = jnp.zeros_like(dk_ref)
        dv_ref[...] = jnp.zeros_like(dv_ref)
        for g in range(GQA_GROUP):
            cols = slice(g * HEAD_DIM, (g + 1) * HEAD_DIM)

            def step(i, carry):
                rows = pl.ds(pl.multiple_of(i * tq, tq), tq)
                qg = q_ref[rows, cols]
                dog = do_ref[rows, cols]
                st = _dot(kj, qg, 1, 1) * scale
                pt = jnp.exp(st - lse_ref[0, g, i])
                dv_ref[...] += _dot(pt.astype(BF), dog, 1, 0)
                dpt = _dot(vj, dog, 1, 1)
                dst = (pt * (dpt - dl_ref[0, g, i]) * scale).astype(BF)
                dk_ref[...] += _dot(dst, qg, 1, 0)
                dq_ref[rows, cols] += _dot(dst, kj, 0, 0)
                return carry

            lax.fori_loop(0, n_i, step, 0)

    stat_spec = pl.BlockSpec((1, GQA_GROUP, n_i, 1, tq), lambda h, j: (h, 0, 0, 0, 0))
    return pl.pallas_call(
        body, name="flash_bwd", grid=(kvh, n_j),
        out_shape=(jax.ShapeDtypeStruct((n, aw), F32), jax.ShapeDtypeStruct((s_len, kvw), F32),
                   jax.ShapeDtypeStruct((s_len, kvw), F32)),
        in_specs=[pl.BlockSpec((n, gw), lambda h, j: (0, h)), pl.BlockSpec((n, gw), lambda h, j: (0, h)),
                  pl.BlockSpec((tk, HEAD_DIM), lambda h, j: (j, h)), pl.BlockSpec((tk, HEAD_DIM), lambda h, j: (j, h)),
                  stat_spec, stat_spec],
        out_specs=(pl.BlockSpec((n, gw), lambda h, j: (0, h)), pl.BlockSpec((tk, HEAD_DIM), lambda h, j: (j, h)),
                   pl.BlockSpec((tk, HEAD_DIM), lambda h, j: (j, h))),
        compiler_params=_cparams("parallel", "arbitrary"),
    )(qr, do, k_all, v_all, lse, delta)


def _pool_window(r, gi, l, n, tm):
    win = tm + 2 * POOL_HALO
    start = jnp.clip(l + r * tm - POOL_HALO, l, l + n - win)
    half = lax.shift_left(jnp.int32(1), gi)
    tok = r * tm + lax.broadcasted_iota(jnp.int32, (tm, win), 0)
    src = (start - l) + lax.broadcasted_iota(jnp.int32, (tm, win), 1)
    tok1 = r * tm + lax.broadcasted_iota(jnp.int32, (tm, 1), 0)
    cnt = (jnp.minimum(tok1 + half, n) - jnp.maximum(tok1 - half, 0)).astype(F32)
    return start, win, tok, src, half, cnt


def _pool_fwd(proj, pool_w, *, l, n, col0, tm):
    s_len = proj.shape[0]
    pg = pool_w.shape[-1]
    cb0 = col0 // pg
    assert col0 % pg == 0 and l % POOL_HALO == 0 and n >= tm + 2 * POOL_HALO

    def body(u_ref, w_ref, d_ref, po_ref):
        gi, r = pl.program_id(0), pl.program_id(1)
        start, win, tok, src, half, cnt = _pool_window(r, gi, l, n, tm)
        band = jnp.logical_and(src >= tok - half, src < tok + half).astype(BF)
        uw = u_ref[pl.ds(pl.multiple_of(start, POOL_HALO), win), :]
        ut = u_ref[pl.ds(pl.multiple_of(l + r * tm, POOL_HALO), tm), :].astype(F32)
        dv = (_dot(band, uw, 1, 0) / cnt - ut).astype(BF)
        d_ref[...] = dv
        po_ref[...] = _dot(dv, w_ref[0], 1, 0).astype(BF)

    return pl.pallas_call(
        body, name="pool_fwd", grid=(N_POOL_GROUPS, n // tm),
        out_shape=(jax.ShapeDtypeStruct((n, N_POOL_GROUPS * pg), BF), jax.ShapeDtypeStruct((n, N_POOL_GROUPS * pg), BF)),
        in_specs=[pl.BlockSpec((s_len, pg), lambda g, r: (0, cb0 + g)), pl.BlockSpec((1, pg, pg), lambda g, r: (g, 0, 0))],
        out_specs=(pl.BlockSpec((tm, pg), lambda g, r: (r, g)), pl.BlockSpec((tm, pg), lambda g, r: (r, g))),
        compiler_params=_cparams("parallel", "parallel"),
    )(proj, pool_w)


def _pool_bwd_map(dval, pool_w, *, n, col0, tm):
    pg = pool_w.shape[-1]
    cb0 = col0 // pg

    def body(d_ref, w_ref, o_ref):
        gi, r = pl.program_id(0), pl.program_id(1)
        half = lax.shift_left(jnp.int32(1), gi)
        tok1 = r * tm + lax.broadcasted_iota(jnp.int32, (tm, 1), 0)
        cnt = (jnp.minimum(tok1 + half, n) - jnp.maximum(tok1 - half, 0)).astype(F32)
        o_ref[...] = (_dot(d_ref[...], w_ref[0], 1, 1) / cnt).astype(BF)

    return pl.pallas_call(
        body, name="pool_bwd_map", grid=(N_POOL_GROUPS, n // tm),
        out_shape=jax.ShapeDtypeStruct((n, N_POOL_GROUPS * pg), BF),
        in_specs=[pl.BlockSpec((tm, pg), lambda g, r: (r, cb0 + g)), pl.BlockSpec((1, pg, pg), lambda g, r: (g, 0, 0))],
        out_specs=pl.BlockSpec((tm, pg), lambda g, r: (r, g)), compiler_params=_cparams("parallel", "parallel"),
    )(dval, pool_w)


def _pool_bwd_window(dds, *, n, tm):
    pg = dds.shape[1] // N_POOL_GROUPS

    def body(d_ref, o_ref):
        gi, r = pl.program_id(0), pl.program_id(1)
        start, win, tok, src, half, cnt = _pool_window(r, gi, 0, n, tm)
        band = jnp.logical_and(tok >= src - half, tok < src + half).astype(BF)
        dw = d_ref[pl.ds(pl.multiple_of(start, POOL_HALO), win), :]
        dt = d_ref[pl.ds(pl.multiple_of(r * tm, POOL_HALO), tm), :].astype(F32)
        o_ref[...] = (_dot(band, dw, 1, 0) - dt * cnt).astype(BF)

    return pl.pallas_call(
        body, name="pool_bwd_window", grid=(N_POOL_GROUPS, n // tm), out_shape=jax.ShapeDtypeStruct(dds.shape, BF),
        in_specs=[pl.BlockSpec((n, pg), lambda g, r: (0, g))], out_specs=pl.BlockSpec((tm, pg), lambda g, r: (r, g)),
        compiler_params=_cparams("parallel", "parallel"),
    )(dds)


def _pool_wgrad(dmat, dval, *, col0, tk):
    n, pw = dmat.shape
    pg = pw // N_POOL_GROUPS
    cb0 = col0 // pg
    nk = n // tk

    def body(a_ref, b_ref, o_ref, acc):
        k = pl.program_id(1)

        @pl.when(k == 0)
        def _():
            acc[...] = jnp.zeros_like(acc)

        acc[...] += _dot(a_ref[...], b_ref[...], 0, 0)

        @pl.when(k == nk - 1)
        def _():
            o_ref[0] = acc[...].astype(BF)

    return pl.pallas_call(
        body, name="pool_wgrad", grid=(N_POOL_GROUPS, nk), out_shape=jax.ShapeDtypeStruct((N_POOL_GROUPS, pg, pg), BF),
        in_specs=[pl.BlockSpec((tk, pg), lambda g, k: (k, g)), pl.BlockSpec((tk, pg), lambda g, k: (k, cb0 + g))],
        out_specs=pl.BlockSpec((1, pg, pg), lambda g, k: (g, 0, 0)), scratch_shapes=[pltpu.VMEM((pg, pg), F32)],
        compiler_params=_cparams("parallel", "arbitrary"),
    )(dmat, dval)


def _merge(attn_o, po_raw, proj, pool_scale, *, l, ga0, gp0, tm):
    n, aw = attn_o.shape
    bw = aw // 2
    rb0 = l // tm
    ga_b, gp_b = ga0 // bw, gp0 // bw
    assert ga0 % bw == 0 and gp0 % bw == 0 and l % tm == 0

    def body(a_ref, p_ref, g_ref, ps_ref, y_ref):
        cb = pl.program_id(1)
        sg, _ = _silu_and_grad(g_ref[...].astype(F32))

        @pl.when(cb < 2)
        def _():
            y_ref[...] = (a_ref[...].astype(F32) * sg).astype(BF)

        @pl.when(cb >= 2)
        def _():
            y_ref[...] = (p_ref[...].astype(F32) * ps_ref[...] * sg).astype(BF)

    return pl.pallas_call(
        body, name="merge", grid=(n // tm, 4), out_shape=jax.ShapeDtypeStruct((n, 2 * aw), BF),
        in_specs=[pl.BlockSpec((tm, bw), lambda i, cb: (i, jnp.minimum(cb, 1))),
                  pl.BlockSpec((tm, bw), lambda i, cb: (i, jnp.maximum(cb - 2, 0))),
                  pl.BlockSpec((tm, bw), lambda i, cb: (i + rb0, jnp.where(cb < 2, ga_b + cb, gp_b + cb - 2))),
                  pl.BlockSpec((1, bw), lambda i, cb: (0, jnp.maximum(cb - 2, 0)))],
        out_specs=pl.BlockSpec((tm, bw), lambda i, cb: (i, cb)), compiler_params=_cparams("parallel", "arbitrary"),
    )(attn_o, po_raw, proj, pool_scale)


def _merge_bwd(dy, attn_o, po_raw, proj, pool_scale, *, l, ga0, gp0, tm):
    n, aw = attn_o.shape
    bw = aw // 2
    rb0 = l // tm
    ga_b, gp_b = ga0 // bw, gp0 // bw

    def body(dy_ref, a_ref, p_ref, g_ref, ps_ref, dv_ref, dg_ref, dps_ref):
        cb, i = pl.program_id(0), pl.program_id(1)
        sg, sgrad = _silu_and_grad(g_ref[...].astype(F32))
        dyv = dy_ref[...].astype(F32)

        @pl.when(cb < 2)
        def _():
            dv_ref[...] = (dyv * sg).astype(BF)
            dg_ref[...] = (dyv * a_ref[...].astype(F32) * sgrad).astype(BF)

        @pl.when(cb >= 2)
        def _():
            @pl.when(i == 0)
            def _():
                dps_ref[...] = jnp.zeros_like(dps_ref)

            pr = p_ref[...].astype(F32)
            dpo = dyv * sg
            dv_ref[...] = (dpo * ps_ref[...]).astype(BF)
            dg_ref[...] = (dyv * (pr * ps_ref[...]) * sgrad).astype(BF)
            dps_ref[0:1, :] += jnp.sum(dpo * pr, axis=0, keepdims=True)

    blk = pl.BlockSpec((tm, bw), lambda cb, i: (i, cb))
    return pl.pallas_call(
        body, name="merge_bwd", grid=(4, n // tm),
        out_shape=(jax.ShapeDtypeStruct((n, 2 * aw), BF), jax.ShapeDtypeStruct((n, 2 * aw), BF),
                   jax.ShapeDtypeStruct((8, aw), F32)),
        in_specs=[blk, pl.BlockSpec((tm, bw), lambda cb, i: (i, jnp.minimum(cb, 1))),
                  pl.BlockSpec((tm, bw), lambda cb, i: (i, jnp.maximum(cb - 2, 0))),
                  pl.BlockSpec((tm, bw), lambda cb, i: (i + rb0, jnp.where(cb < 2, ga_b + cb, gp_b + cb - 2))),
                  pl.BlockSpec((1, bw), lambda cb, i: (0, jnp.maximum(cb - 2, 0)))],
        out_specs=(blk, blk, pl.BlockSpec((8, bw), lambda cb, i: (0, jnp.maximum(cb - 2, 0)))),
        compiler_params=_cparams("arbitrary", "arbitrary"),
    )(dy, attn_o, po_raw, proj, pool_scale)


def _post(x, out, target, gate, g_post, *, tm):
    n, d = x.shape

    def body(x_ref, o_ref, t_ref, gate_ref, gp_ref, dxn_ref, dout_ref, st_ref):
        @pl.when(pl.program_id(0) == 0)
        def _():
            st_ref[...] = jnp.zeros_like(st_ref)

        ov = o_ref[...]
        rstd = lax.rsqrt(jnp.mean(ov * ov, axis=-1, keepdims=True) + EPS)
        on = ov * rstd
        rn = on * gp_ref[...]
        err = (x_ref[...] + gate_ref[...] * rn) - t_ref[...]
        dxn = err / d
        dxn_ref[...] = dxn
        drn = dxn * gate_ref[...]
        don = drn * gp_ref[...]
        dout_ref[...] = (rstd * (don - on * jnp.mean(don * on, axis=-1, keepdims=True))).astype(BF)
        st_ref[0:1, :] += jnp.sum(dxn * rn, axis=0, keepdims=True)
        st_ref[1:2, :] += jnp.sum(drn * on, axis=0, keepdims=True)
        st_ref[2:3, :] += jnp.sum(err * err, axis=0, keepdims=True)

    row = pl.BlockSpec((tm, d), lambda i: (i, 0))
    vec = pl.BlockSpec((1, d), lambda i: (0, 0))
    return pl.pallas_call(
        body, name="post", grid=(n // tm,),
        out_shape=(jax.ShapeDtypeStruct((n, d), F32), jax.ShapeDtypeStruct((n, d), BF), jax.ShapeDtypeStruct((8, d), F32)),
        in_specs=[row, row, row, vec, vec], out_specs=(row, row, pl.BlockSpec((8, d), lambda i: (0, 0))),
        compiler_params=_cparams("arbitrary"),
    )(x, out, target, gate, g_post)


def _prenorm_bwd(x, ctx, dh, dxn, g_pre, ss, *, tm):
    n, d = x.shape
    l = ctx.shape[0]
    nc = l // tm

    def body(x_ref, c_ref, dh_ref, dxn_ref, g_ref, ss_ref, gx_ref, st_ref):
        t = pl.program_id(0)

        @pl.when(t == 0)
        def _():
            st_ref[...] = jnp.zeros_like(st_ref)

        def go(src, scale, row):
            v = src[...]
            dhv = dh_ref[...].astype(F32)
            rstd = lax.rsqrt(jnp.mean(v * v, axis=-1, keepdims=True) + EPS)
            xn = v * rstd
            st_ref[row:row + 1, :] += jnp.sum(dhv, axis=0, keepdims=True)
            st_ref[row + 1:row + 2, :] += jnp.sum(dhv * (xn * g_ref[...]), axis=0, keepdims=True)
            dg = dhv * (1.0 + scale)
            st_ref[4:5, :] += jnp.sum(dg * xn, axis=0, keepdims=True)
            dxn_ = dg * g_ref[...]
            return rstd * (dxn_ - xn * jnp.mean(dxn_ * xn, axis=-1, keepdims=True))

        @pl.when(t < nc)
        def _():
            go(c_ref, ss_ref[2:3, :], 2)

        @pl.when(t >= nc)
        def _():
            gx_ref[...] = dxn_ref[...] + go(x_ref, ss_ref[0:1, :], 0)

    lat = pl.BlockSpec((tm, d), lambda t: (jnp.maximum(t - nc, 0), 0))
    return pl.pallas_call(
        body, name="prenorm_bwd", grid=((l + n) // tm,),
        out_shape=(jax.ShapeDtypeStruct((n, d), F32), jax.ShapeDtypeStruct((8, d), F32)),
        in_specs=[lat, pl.BlockSpec((tm, d), lambda t: (jnp.minimum(t, nc - 1), 0)),
                  pl.BlockSpec((tm, d), lambda t: (t, 0)), lat,
                  pl.BlockSpec((1, d), lambda t: (0, 0)), pl.BlockSpec((4, d), lambda t: (0, 0))],
        out_specs=(lat, pl.BlockSpec((8, d), lambda t: (0, 0))), compiler_params=_cparams("arbitrary"),
    )(x, ctx, dh, dxn, g_pre, ss)


def _ada_fwd(craw, w_ada, b_loc, *, tn):
    d, wn = w_ada.shape

    def body(c_ref, w_ref, b_ref, o_ref):
        act, _ = _silu_and_grad(c_ref[...])
        o_ref[...] = _dot(act.astype(BF), w_ref[...].astype(BF), 1, 0) + b_ref[...]

    return pl.pallas_call(
        body, name="ada_fwd", grid=(wn // tn,), out_shape=jax.ShapeDtypeStruct((16, wn), F32),
        in_specs=[pl.BlockSpec((16, d), lambda j: (0, 0)), pl.BlockSpec((d, tn), lambda j: (0, j)),
                  pl.BlockSpec((1, tn), lambda j: (0, j))],
        out_specs=pl.BlockSpec((16, tn), lambda j: (0, j)), compiler_params=_cparams("parallel"),
    )(craw, w_ada, b_loc)


def _ada_bwd(craw_t, dm, w, m, v, *, tn):
    d, wn = w.shape

    def body(ct_ref, dm_ref, w_ref, m_ref, v_ref, g_ref, dl_ref, nm_ref, nv_ref, da_ref):
        @pl.when(pl.program_id(0) == 0)
        def _():
            da_ref[...] = jnp.zeros_like(da_ref)

        act, _ = _silu_and_grad(ct_ref[...])
        dmb = dm_ref[...].astype(BF)
        wv = w_ref[...]
        gv = _dot(act.astype(BF), dmb, 1, 0)
        da_ref[...] += _dot(dmb, wv.astype(BF), 1, 1)
        nm = ADAM_B1 * m_ref[...] + (1.0 - ADAM_B1) * gv
        nv = ADAM_B2 * v_ref[...] + (1.0 - ADAM_B2) * (gv * gv)
        m_hat = nm / (1.0 - ADAM_B1 ** ADAM_STEP)
        v_hat = nv / (1.0 - ADAM_B2 ** ADAM_STEP)
        g_ref[...] = gv
        dl_ref[...] = -ADAM_LR * (m_hat / (jnp.sqrt(v_hat) + ADAM_EPS) + ADAM_WD * wv)
        nm_ref[...] = nm
        nv_ref[...] = nv

    col = pl.BlockSpec((d, tn), lambda j: (0, j))
    return pl.pallas_call(
        body, name="ada_bwd", grid=(wn // tn,),
        out_shape=tuple([jax.ShapeDtypeStruct((d, wn), F32)] * 4) + (jax.ShapeDtypeStruct((16, d), F32),),
        in_specs=[pl.BlockSpec((d, 16), lambda j: (0, 0)), pl.BlockSpec((16, tn), lambda j: (0, j)), col, col, col],
        out_specs=(col, col, col, col, pl.BlockSpec((16, d), lambda j: (0, 0))),
        compiler_params=_cparams("arbitrary"),
    )(craw_t, dm, w, m, v)


def _reduce_small(gath, c_ctx, d3, *, off):
    t = gath.shape[-1]

    def body(g_ref, dm_ref, s_ref):
        tot = g_ref[0]
        for b in range(1, 8):
            tot = tot + g_ref[b]
        s_ref[...] = tot
        dm_ref[...] = jnp.zeros_like(dm_ref)
        for b in range(8):
            dm_ref[b:b + 1, :] = g_ref[b][:, 0:d3]
        dm_ref[8:9, :] = tot[:, d3:2 * d3]

    return pl.pallas_call(
        body, name="reduce_small", out_shape=(jax.ShapeDtypeStruct((16, d3), F32), jax.ShapeDtypeStruct((1, t), F32)),
        in_specs=[VMEM_SPEC], out_specs=(VMEM_SPEC, VMEM_SPEC),
    )(gath)


def _cctx_grad(parts, c_ctx):
    def body(p_ref, c_ref, o_ref):
        tot = (p_ref[0] + p_ref[1]) + (p_ref[2] + p_ref[3])
        _, sgrad = _silu_and_grad(c_ref[...])
        o_ref[...] = tot * sgrad

    return pl.pallas_call(
        body, name="cctx_grad", out_shape=jax.ShapeDtypeStruct(c_ctx.shape, F32),
        in_specs=[VMEM_SPEC, VMEM_SPEC], out_specs=VMEM_SPEC,
    )(parts, c_ctx)


def _rope_tables(n, l):
    rows = n // GRID_W
    row = jnp.repeat(jnp.arange(rows, dtype=F32), GRID_W)
    col = jnp.tile(jnp.arange(GRID_W, dtype=F32), rows)
    inv = ROPE_THETA ** (-jnp.arange(ROPE_PAIRS, dtype=F32) / ROPE_PAIRS)
    ang = jnp.concatenate([row[:, None] * inv, col[:, None] * inv], axis=-1)
    cos, sin = jnp.cos(ang), jnp.sin(ang)
    cr, cc, sr, sc = cos[:, :ROPE_PAIRS], cos[:, ROPE_PAIRS:], sin[:, :ROPE_PAIRS], sin[:, ROPE_PAIRS:]
    ctab = jnp.concatenate([cr, cr, cc, cc], axis=-1)
    stab = jnp.concatenate([-sr, sr, -sc, sc], axis=-1)
    ctab = jnp.concatenate([jnp.ones((l, HEAD_DIM), F32), ctab], axis=0)
    stab = jnp.concatenate([jnp.zeros((l, HEAD_DIM), F32), stab], axis=0)
    return ctab, stab


def kernel(x, c, ctx, c_ctx, w_ada, b_ada, norm_pre, norm_post, w_in, q_norm, k_norm, pool_w, pool_scale, w_out, loss_target, m_c_ctx, m_w_ada, m_b_ada, m_norm_pre, m_norm_post, m_w_in, m_q_norm, m_k_norm, m_pool_w, m_pool_scale, m_w_out, v_c_ctx, v_w_ada, v_b_ada, v_norm_pre, v_norm_post, v_w_in, v_q_norm, v_k_norm, v_pool_w, v_pool_scale, v_w_out):
    x2, ctx2, tgt = x[0], ctx[0], loss_target[0]
    n, d = x2.shape
    l = ctx2.shape[0]
    s_len = l + n
    aw = d // 2
    pw = d - aw
    n_heads = aw // HEAD_DIM
    kvw = (n_heads // GQA_GROUP) * HEAD_DIM
    pg = pw // N_POOL_GROUPS
    k0, v0, ga0 = aw, aw + kvw, aw + 2 * kvw
    up0, gp0 = ga0 + aw, ga0 + aw + pw
    in_w = gp0 + pw
    d3 = 3 * d
    ada_w = w_ada.shape[-1]
    px, py, pc = lax.axis_index("x"), lax.axis_index("y"), lax.axis_index("c")
    chip = 2 * px + py
    tr = min(256, l)
    tq = min(512, n)
    tk = min(256, l)
    ts = s_len // 8 if (s_len // 8) % 16 == 0 else tr
    scale = HEAD_DIM ** -0.5

    c_all = _allgather_small(c, chips_only=False, name="gather_c").reshape(8, d)
    craw = jnp.concatenate([c_all, c_ctx.reshape(1, d), jnp.zeros((7, d), F32)], axis=0)
    b_loc = lax.dynamic_slice(b_ada, (0, chip * ada_w), (1, ada_w))
    mod_part = _ada_fwd(craw, w_ada[0], b_loc, tn=min(512, ada_w))
    mod_all = _allgather_small(mod_part, chips_only=True, name="gather_mod")
    mod_all = jnp.transpose(mod_all, (1, 0, 2)).reshape(16, d3)
    me = 4 * px + 2 * py + pc
    mod_b = lax.dynamic_slice(mod_all, (me, 0), (1, d3))
    mod_c = mod_all[8:9]
    ss = jnp.concatenate([mod_b[:, d:2 * d], mod_b[:, 0:d], mod_c[:, d:2 * d], mod_c[:, 0:d]], axis=0)
    gate = mod_b[:, 2 * d:]

    w_in_f, w_out_f, pool_f = _allgather_weights(
        [_cast_bf16(w_in[0], name="cast_w_in"), _cast_bf16(w_out[0], name="cast_w_out"), _cast_bf16(pool_w[0], name="cast_pool_w")],
        [(1, 0), (0, 1), (1, 2)], name="gather_weights")

    ctab, stab = _rope_tables(n, l)
    h = _prenorm(x2, ctx2, norm_pre, ss, tm=tr)
    proj = _matmul(h, w_in_f, ca=1, cb=0, tm=ts, tn=min(512, in_w), tk=d, out_dtype=BF, name="in_proj")
    qr = _qk_prep(proj, q_norm, ctab[l:], stab[l:], row0=l, col0=0, width=aw, nrows=n, tm=tr, name="q_prep")
    kr = _qk_prep(proj, k_norm, ctab, stab, row0=0, col0=k0, width=kvw, nrows=s_len, tm=tr, name="k_prep")
    v_all = proj[:, v0:v0 + kvw]
    vt_tiles = jnp.transpose(v_all.reshape(s_len // tk, tk, kvw // HEAD_DIM, HEAD_DIM), (2, 0, 3, 1))
    attn_o, lse = _flash_fwd(qr, kr, vt_tiles, tq=tq, tk=tk, scale=scale)
    dmat, po_raw = _pool_fwd(proj, pool_f, l=l, n=n, col0=up0, tm=tr)
    yv = _merge(attn_o, po_raw, proj, pool_scale, l=l, ga0=ga0, gp0=gp0, tm=tr)
    out = _matmul(yv, w_out_f, ca=1, cb=0, tm=min(512, n), tn=min(512, d), tk=d, out_dtype=F32, name="out_proj")

    dxn, d_out, post_st = _post(x2, out, tgt, gate, norm_post, tm=min(128, n))
    loss = 0.5 * jnp.sum(post_st[2]) / d
    loss = lax.psum(loss, ("x", "y", "c"))
    g_wout = _matmul(yv, d_out, ca=0, cb=0, tm=min(512, d), tn=min(512, d), tk=n, out_dtype=BF, name="wgrad_out")
    dy = _matmul(d_out, w_out_f, ca=1, cb=1, tm=min(512, n), tn=min(512, d), tk=d, out_dtype=BF, name="dgrad_out")
    dval, dgate, dps = _merge_bwd(dy, attn_o, po_raw, proj, pool_scale, l=l, ga0=ga0, gp0=gp0, tm=tr)
    g_pool = _pool_wgrad(dmat, dval, col0=aw, tk=min(512, n))
    dds = _pool_bwd_map(dval, pool_f, n=n, col0=aw, tm=tr)
    du = _pool_bwd_window(dds, n=n, tm=tr)
    do = dval[:, :aw]
    delta = _attn_delta(attn_o, do, tq=tq)
    dq, dk, dv = _flash_bwd(qr, do, kr, v_all, lse, delta, tq=tq, tk=tk, scale=scale)
    dq_raw, dgq = _qk_bwd(dq, proj, q_norm, ctab[l:], stab[l:], row0=l, col0=0, tm=tr, name="q_bwd")
    dk_raw, dgk = _qk_bwd(dk, proj, k_norm, ctab, stab, row0=0, col0=k0, tm=tr, name="k_bwd")
    dvb = dv.astype(BF)
    dproj = jnp.concatenate([
        jnp.concatenate([jnp.zeros((l, aw), BF), dq_raw], axis=0), dk_raw, dvb,
        jnp.concatenate([jnp.zeros((l, aw + 2 * pw), BF),
                         jnp.concatenate([dgate[:, :aw], du, dgate[:, aw:]], axis=1)], axis=0)], axis=1)
    g_win = _matmul(h, dproj, ca=0, cb=0, tm=min(512, d), tn=min(512, in_w), tk=s_len, out_dtype=BF, name="wgrad_in")
    dh = _matmul(dproj, w_in_f, ca=1, cb=1, tm=ts, tn=min(512, d), tk=in_w // 2, out_dtype=BF, name="dgrad_in")
    grad_x, pre_st = _prenorm_bwd(x2, ctx2, dh, dxn, norm_pre, ss, tm=min(128, l))

    zero_d = jnp.zeros((1, d), F32)
    packed = jnp.concatenate([pre_st[0:1], pre_st[1:2], post_st[0:1], pre_st[2:3], pre_st[3:4], zero_d,
                              pre_st[4:5], post_st[1:2], dgq[0:1], dgk[0:1], dps[0:1]], axis=1)
    gath = _allgather_small(packed, chips_only=False, name="gather_small")
    dm, sums = _reduce_small(gath, c_ctx, d3, off=None)
    o1 = 2 * d3
    g_npre, g_npost = sums[:, o1:o1 + d], sums[:, o1 + d:o1 + 2 * d]
    g_q, g_k = sums[:, o1 + 2 * d:o1 + 2 * d + HEAD_DIM], sums[:, o1 + 2 * d + HEAD_DIM:o1 + 2 * d + 2 * HEAD_DIM]
    g_ps = sums[:, o1 + 2 * d + 2 * HEAD_DIM:]
    g_bada = sums[:, 0:d3] + sums[:, d3:2 * d3]
    dm_loc = lax.dynamic_slice(dm, (0, chip * ada_w), (16, ada_w))
    g_wada, dl_wada, nm_wada, nv_wada, dact = _ada_bwd(craw.T, dm_loc, w_ada[0], m_w_ada[0], v_w_ada[0], tn=128)
    cparts = _allgather_small(dact[8:9], chips_only=True, name="gather_cctx")
    g_cctx = _cctx_grad(cparts, c_ctx.reshape(1, d)).reshape(d)

    mats = [g_win, g_wout, g_pool]
    half_axes = [0, 1, 2]
    shard_axes = [1, 0, 1]
    recv_a = _half_to_sibling(mats, half_axes, name="reduce_pair")
    pair = [_add_own_half(mats[i], recv_a[i], half_axes[i], pc, name=f"pair_sum_{i}") for i in range(3)]
    recv_b = _pieces_to_owners(pair, shard_axes, name="reduce_chips")
    red = [_sum_own_block(pair[i], recv_b[3 * i:3 * i + 3], shard_axes[i], chip, name=f"chip_sum_{i}") for i in range(3)]
    g_win_s, g_wout_s, g_pool_s = _halves_join(red, half_axes, name="reduce_join")

    def upd(w, g, m, v, name):
        return _adamw(w, g.reshape(w.shape), m, v, name=name)

    grads = {"c_ctx": g_cctx, "b_ada": g_bada, "norm_pre": g_npre, "norm_post": g_npost, "w_in": g_win_s[None],
             "q_norm": g_q, "k_norm": g_k, "pool_w": g_pool_s[None], "pool_scale": g_ps, "w_out": g_wout_s[None]}
    res = {"w_ada": (g_wada[None], dl_wada[None], nm_wada[None], nv_wada[None])}
    given = {"c_ctx": (c_ctx, m_c_ctx, v_c_ctx), "b_ada": (b_ada, m_b_ada, v_b_ada),
             "norm_pre": (norm_pre, m_norm_pre, v_norm_pre), "norm_post": (norm_post, m_norm_post, v_norm_post),
             "w_in": (w_in, m_w_in, v_w_in), "q_norm": (q_norm, m_q_norm, v_q_norm), "k_norm": (k_norm, m_k_norm, v_k_norm),
             "pool_w": (pool_w, m_pool_w, v_pool_w), "pool_scale": (pool_scale, m_pool_scale, v_pool_scale),
             "w_out": (w_out, m_w_out, v_w_out)}
    for nme, (w, m, v) in given.items():
        g = grads[nme].reshape(w.shape)
        res[nme] = (g,) + upd(w, g, m, v, "adamw_" + nme)
    order = ["c_ctx", "w_ada", "b_ada", "norm_pre", "norm_post", "w_in", "q_norm", "k_norm", "pool_w", "pool_scale", "w_out"]
    return (loss, grad_x[None], *[res[k][0] for k in order], *[res[k][1] for k in order],
            *[res[k][2] for k in order], *[res[k][3] for k in order])
```

```python
import functools

import jax
import jax.numpy as jnp
from jax import lax
from jax.experimental import pallas as pl
from jax.experimental.pallas import tpu as pltpu

F32 = jnp.float32
BF = jnp.bfloat16
MESH = pl.DeviceIdType.MESH

HEAD_DIM = 128
GQA_GROUP = 4
GRID_W = 64
ROPE_PAIRS = HEAD_DIM // 4
ROPE_THETA = 10000.0
EPS = 1e-6
N_POOL_GROUPS = 4
POOL_HALO = 128
ADAM_LR = 0.001
ADAM_B1 = 0.9
ADAM_B2 = 0.999
ADAM_EPS = 1e-08
ADAM_WD = 0.01
ADAM_STEP = 10
LOG2E = 1.4426950408889634
MIB = 2 ** 20
VMEM_LIMIT = 48 * MIB
CHIP_MASKS = ((1, 0, 0), (0, 1, 0), (1, 1, 0))
ALL_MASKS = ((0, 0, 1), (0, 1, 0), (0, 1, 1), (1, 0, 0), (1, 0, 1), (1, 1, 0), (1, 1, 1))
HBM_SPEC = pl.BlockSpec(memory_space=pl.ANY)
VMEM_SPEC = pl.BlockSpec(memory_space=pltpu.VMEM)


def _cparams(*sem):
    return pltpu.CompilerParams(dimension_semantics=sem, vmem_limit_bytes=VMEM_LIMIT)


def _sigmoid(v):
    return 1.0 / (1.0 + jnp.exp(-v))


def _silu_and_grad(v):
    s = _sigmoid(v)
    return v * s, s * (1.0 + v * (1.0 - s))


def _dot(a, b, ca, cb):
    return lax.dot_general(a, b, (((ca,), (cb,)), ((), ())), preferred_element_type=F32)


def _block_rows(rows, width, itemsize=4, target=MIB):
    best = 8
    for t in range(8, rows + 1, 8):
        if rows % t == 0 and t * width * itemsize <= target:
            best = t
    return best if rows % 8 == 0 else rows


def _my_pos():
    return lax.axis_index("x"), lax.axis_index("y"), lax.axis_index("c")


def _flip(pos, mask):
    return tuple(jnp.bitwise_xor(p, m) if m else p for p, m in zip(pos, mask))


def _allgather_small(v, *, chips_only, name):
    r, w = v.shape
    masks = CHIP_MASKS if chips_only else ALL_MASKS
    nslot = 4 if chips_only else 8

    def slot(pos):
        return 2 * pos[0] + pos[1] if chips_only else 4 * pos[0] + 2 * pos[1] + pos[2]

    def body(v_ref, o_ref, send_sems, recv_sems, local_sem):
        me = _my_pos()

        def copy(k, block_of, to):
            return pltpu.make_async_remote_copy(
                src_ref=v_ref, dst_ref=o_ref.at[slot(block_of)], send_sem=send_sems.at[k], recv_sem=recv_sems.at[k],
                device_id=to, device_id_type=MESH)

        mine = pltpu.make_async_copy(v_ref, o_ref.at[slot(me)], local_sem)
        mine.start()
        sends = [copy(k, me, _flip(me, m)) for k, m in enumerate(masks)]
        for cp in sends:
            cp.start()
        for k, m in enumerate(masks):
            copy(k, _flip(me, m), me).wait_recv()
        for cp in sends:
            cp.wait_send()
        mine.wait()

    return pl.pallas_call(
        body, name=name, out_shape=jax.ShapeDtypeStruct((nslot, r, w), v.dtype),
        in_specs=[VMEM_SPEC], out_specs=VMEM_SPEC,
        scratch_shapes=[pltpu.SemaphoreType.DMA((len(masks),)), pltpu.SemaphoreType.DMA((len(masks),)),
                        pltpu.SemaphoreType.DMA],
    )(v)


def _sl(ref, axis, start, size):
    idx = [slice(None)] * len(ref.shape)
    idx[axis] = pl.ds(start, size)
    return ref.at[tuple(idx)]


def _allgather_weights(shards, layouts, *, name):
    nm = len(shards)

    def body(*refs):
        outs = refs[nm:2 * nm]
        send_sems, recv_sems, fsend_sems, frecv_sems = refs[2 * nm:]
        x, y, c = _my_pos()
        me = (x, y, c)
        sib = (x, y, 1 - c)
        sends, fwds = [], []
        for m in range(nm):
            a_s, a_h = layouts[m]
            ns, nh = outs[m].shape[a_s] // 4, outs[m].shape[a_h] // 2

            def region(chip_pos, half, m=m, a_s=a_s, a_h=a_h, ns=ns, nh=nh):
                j = 2 * chip_pos[0] + chip_pos[1]
                return _sl(_sl(outs[m], a_s, j * ns, ns), a_h, half * nh, nh)

            for k, mask in enumerate(CHIP_MASKS):
                other = _flip(me, mask)
                cp = pltpu.make_async_remote_copy(
                    src_ref=region(me, c), dst_ref=region(me, c),
                    send_sem=send_sems.at[m, k], recv_sem=recv_sems.at[m, k], device_id=other, device_id_type=MESH)
                cp.start()
                sends.append(cp)
                fwds.append((m, k, region(other, c), region(other, 1 - c)))
        started = []
        for m, k, landed, _ in fwds:
            pltpu.make_async_remote_copy(
                src_ref=landed, dst_ref=landed, send_sem=send_sems.at[m, k], recv_sem=recv_sems.at[m, k],
                device_id=me, device_id_type=MESH).wait_recv()
            fw = pltpu.make_async_remote_copy(
                src_ref=landed, dst_ref=landed, send_sem=fsend_sems.at[m, k], recv_sem=frecv_sems.at[m, k],
                device_id=sib, device_id_type=MESH)
            fw.start()
            started.append(fw)
        for m, k, _, from_sib in fwds:
            pltpu.make_async_remote_copy(
                src_ref=from_sib, dst_ref=from_sib, send_sem=fsend_sems.at[m, k], recv_sem=frecv_sems.at[m, k],
                device_id=sib, device_id_type=MESH).wait_recv()
        for cp in sends + started:
            cp.wait_send()

    return pl.pallas_call(
        body, name=name, out_shape=tuple(jax.ShapeDtypeStruct(s.shape, s.dtype) for s in shards),
        in_specs=[HBM_SPEC] * nm, out_specs=tuple([HBM_SPEC] * nm),
        input_output_aliases={i: i for i in range(nm)},
        scratch_shapes=[pltpu.SemaphoreType.DMA((nm, 3)), pltpu.SemaphoreType.DMA((nm, 3)),
                        pltpu.SemaphoreType.DMA((nm, 3)), pltpu.SemaphoreType.DMA((nm, 3))],
    )(*shards)


def _half_to_sibling(mats, half_axes, *, name):
    nm = len(mats)

    def body(*refs):
        srcs, outs = refs[:nm], refs[nm:2 * nm]
        send_sems, recv_sems = refs[2 * nm:]
        x, y, c = _my_pos()
        sib = (x, y, 1 - c)
        cps = []
        for m in range(nm):
            nh = srcs[m].shape[half_axes[m]] // 2
            cp = pltpu.make_async_remote_copy(
                src_ref=_sl(srcs[m], half_axes[m], (1 - c) * nh, nh), dst_ref=outs[m],
                send_sem=send_sems.at[m], recv_sem=recv_sems.at[m], device_id=sib, device_id_type=MESH)
            cp.start()
            cps.append(cp)
        for cp in cps:
            cp.wait()

    out_shapes = []
    for s, a_h in zip(mats, half_axes):
        shp = list(s.shape)
        shp[a_h] //= 2
        out_shapes.append(jax.ShapeDtypeStruct(tuple(shp), s.dtype))
    return pl.pallas_call(
        body, name=name, out_shape=tuple(out_shapes),
        in_specs=[HBM_SPEC] * nm, out_specs=tuple([HBM_SPEC] * nm),
        scratch_shapes=[pltpu.SemaphoreType.DMA((nm,)), pltpu.SemaphoreType.DMA((nm,))],
    )(*mats)


def _pieces_to_owners(mats, shard_axes, *, name):
    nm = len(mats)

    def body(*refs):
        srcs, outs = refs[:nm], refs[nm:nm + 3 * nm]
        send_sems, recv_sems = refs[nm + 3 * nm:]
        x, y, c = _my_pos()
        me = (x, y, c)
        cps = []
        for m in range(nm):
            ns = srcs[m].shape[shard_axes[m]] // 4
            for k, mask in enumerate(CHIP_MASKS):
                other = _flip(me, mask)
                cp = pltpu.make_async_remote_copy(
                    src_ref=_sl(srcs[m], shard_axes[m], (2 * other[0] + other[1]) * ns, ns), dst_ref=outs[3 * m + k],
                    send_sem=send_sems.at[m, k], recv_sem=recv_sems.at[m, k], device_id=other, device_id_type=MESH)
                cp.start()
                cps.append(cp)
        for cp in cps:
            cp.wait()

    out_shapes = []
    for s, a_s in zip(mats, shard_axes):
        shp = list(s.shape)
        shp[a_s] //= 4
        out_shapes += [jax.ShapeDtypeStruct(tuple(shp), s.dtype)] * 3
    return pl.pallas_call(
        body, name=name, out_shape=tuple(out_shapes),
        in_specs=[HBM_SPEC] * nm, out_specs=tuple([HBM_SPEC] * (3 * nm)),
        scratch_shapes=[pltpu.SemaphoreType.DMA((nm, 3)), pltpu.SemaphoreType.DMA((nm, 3))],
    )(*mats)


def _halves_join(halves, half_axes, *, name):
    nm = len(halves)

    def body(*refs):
        outs = refs[nm:2 * nm]
        send_sems, recv_sems = refs[2 * nm:]
        x, y, c = _my_pos()
        sib = (x, y, 1 - c)
        cps = []
        for m in range(nm):
            nh = outs[m].shape[half_axes[m]] // 2
            mine = _sl(outs[m], half_axes[m], c * nh, nh)
            cp = pltpu.make_async_remote_copy(
                src_ref=mine, dst_ref=mine, send_sem=send_sems.at[m], recv_sem=recv_sems.at[m],
                device_id=sib, device_id_type=MESH)
            cp.start()
            cps.append(cp)
        for m, cp in enumerate(cps):
            nh = outs[m].shape[half_axes[m]] // 2
            theirs = _sl(outs[m], half_axes[m], (1 - c) * nh, nh)
            cp.wait_send()
            pltpu.make_async_remote_copy(
                src_ref=theirs, dst_ref=theirs, send_sem=send_sems.at[m], recv_sem=recv_sems.at[m],
                device_id=sib, device_id_type=MESH).wait_recv()

    return pl.pallas_call(
        body, name=name, out_shape=tuple(jax.ShapeDtypeStruct(s.shape, s.dtype) for s in halves),
        in_specs=[HBM_SPEC] * nm, out_specs=tuple([HBM_SPEC] * nm),
        input_output_aliases={i: i for i in range(nm)},
        scratch_shapes=[pltpu.SemaphoreType.DMA((nm,)), pltpu.SemaphoreType.DMA((nm,))],
    )(*halves)


def _part_specs(shp, axis, slot, itemsize, target=MIB):
    if len(shp) == 2:
        rows, width = shp
        tm = _block_rows(rows, width, itemsize, target)
        nb = rows // tm
        own = pl.BlockSpec((tm, width), lambda i, p: (i, 0))
        if axis == 1:
            part = pl.BlockSpec((tm, width), lambda i, p: (i, p[slot]))
        else:
            part = pl.BlockSpec((tm, width), lambda i, p: (p[slot] * nb + i, 0))
        return (nb,), own, part
    assert len(shp) == 3 and axis in (1, 2)
    g, rows, width = shp
    own = pl.BlockSpec((1, rows, width), lambda i, p: (i, 0, 0))
    if axis == 1:
        part = pl.BlockSpec((1, rows, width), lambda i, p: (i, p[slot], 0))
    else:
        part = pl.BlockSpec((1, rows, width), lambda i, p: (i, 0, p[slot]))
    return (g,), own, part


def _cast_into(w, axis, p, *, name):
    grid, own, part = _part_specs(w.shape, axis, 0, 4, 2 * MIB)
    big = list(w.shape)
    big[axis] *= 4

    def body(p_ref, w_ref, o_ref):
        o_ref[...] = w_ref[...].astype(BF)

    return pl.pallas_call(
        body, name=name, out_shape=jax.ShapeDtypeStruct(tuple(big), BF),
        grid_spec=pltpu.PrefetchScalarGridSpec(num_scalar_prefetch=1, grid=grid, in_specs=[own], out_specs=part),
        compiler_params=_cparams("parallel"),
    )(p.reshape(1), w)


def _add_own_half(full, recv, half_axis, c, *, name):
    grid, own, part = _part_specs(recv.shape, half_axis, 0, 2)

    def body(c_ref, f_ref, r_ref, o_ref):
        o_ref[...] = (f_ref[...].astype(F32) + r_ref[...].astype(F32)).astype(BF)

    return pl.pallas_call(
        body, name=name, out_shape=jax.ShapeDtypeStruct(recv.shape, BF),
        grid_spec=pltpu.PrefetchScalarGridSpec(num_scalar_prefetch=1, grid=grid, in_specs=[part, own], out_specs=own),
        compiler_params=_cparams("parallel"),
    )(c.reshape(1), full, recv)


def _sum_own_block(mine_all, recvs, shard_axis, half_axis, j, c, *, name):
    shp = recvs[0].shape
    grid, own, mine = _part_specs(shp, shard_axis, 0, 4)
    _, _, place = _part_specs(shp, half_axis, 1, 4)
    big = list(shp)
    big[half_axis] *= 2

    def body(p_ref, a_ref, r0, r1, r2, o_ref):
        o_ref[...] = ((a_ref[...].astype(F32) + r0[...].astype(F32)) + r1[...].astype(F32)) + r2[...].astype(F32)

    return pl.pallas_call(
        body, name=name, out_shape=jax.ShapeDtypeStruct(tuple(big), F32),
        grid_spec=pltpu.PrefetchScalarGridSpec(
            num_scalar_prefetch=1, grid=grid, in_specs=[mine, own, own, own], out_specs=place),
        compiler_params=_cparams("parallel"),
    )(jnp.stack([j, c]), mine_all, *recvs)


def _adamw(w, g, m, v, *, name):
    shp = w.shape
    width = shp[-1] if len(shp) > 1 else shp[0]
    rows = 1
    for d in shp[:-1]:
        rows *= d
    if len(shp) == 1:
        rows = 1
    args = [a.reshape(rows, width) for a in (w, g, m, v)]
    tm = _block_rows(rows, width, 4, MIB // 2)

    def body(w_ref, g_ref, m_ref, v_ref, d_ref, nm_ref, nv_ref):
        gv = g_ref[...]
        nm = ADAM_B1 * m_ref[...] + (1.0 - ADAM_B1) * gv
        nv = ADAM_B2 * v_ref[...] + (1.0 - ADAM_B2) * (gv * gv)
        m_hat = nm / (1.0 - ADAM_B1 ** ADAM_STEP)
        v_hat = nv / (1.0 - ADAM_B2 ** ADAM_STEP)
        d_ref[...] = -ADAM_LR * (m_hat / (jnp.sqrt(v_hat) + ADAM_EPS) + ADAM_WD * w_ref[...])
        nm_ref[...] = nm
        nv_ref[...] = nv

    spec = pl.BlockSpec((tm, width), lambda i: (i, 0))
    outs = pl.pallas_call(
        body, name=name, grid=(rows // tm,), out_shape=tuple([jax.ShapeDtypeStruct((rows, width), F32)] * 3),
        in_specs=[spec] * 4, out_specs=tuple([spec] * 3), compiler_params=_cparams("parallel"),
    )(*args)
    return tuple(o.reshape(shp) for o in outs)


def _matmul(a, b, *, ca, cb, tm, tn, tk, out_dtype, name, b_resident=False):
    m, kdim = a.shape[1 - ca], a.shape[ca]
    n = b.shape[1 - cb]
    nk = kdim // tk
    assert m % tm == 0 and n % tn == 0 and kdim % tk == 0

    def body(a_ref, b_ref, o_ref, *acc):
        part = _dot(a_ref[...], b_ref[...], ca, cb)
        if nk == 1:
            o_ref[...] = part.astype(o_ref.dtype)
            return
        k = pl.program_id(2)

        @pl.when(k == 0)
        def _():
            acc[0][...] = part

        @pl.when(jnp.logical_and(k > 0, k < nk - 1))
        def _():
            acc[0][...] += part

        @pl.when(k == nk - 1)
        def _():
            o_ref[...] = (acc[0][...] + part).astype(o_ref.dtype)

    if b_resident:
        gi = lambda p, q, k: (q, p, k)
        grid = (n // tn, m // tm, nk)
    else:
        gi = lambda p, q, k: (p, q, k)
        grid = (m // tm, n // tn, nk)

    def a_map(p, q, k):
        i, _, kk = gi(p, q, k)
        return (i, kk) if ca == 1 else (kk, i)

    def b_map(p, q, k):
        _, j, kk = gi(p, q, k)
        return (kk, j) if cb == 0 else (j, kk)

    def o_map(p, q, k):
        i, j, _ = gi(p, q, k)
        return (i, j)

    a_spec = pl.BlockSpec((tm, tk) if ca == 1 else (tk, tm), a_map)
    b_spec = pl.BlockSpec((tk, tn) if cb == 0 else (tn, tk), b_map)
    return pl.pallas_call(
        body, name=name, grid=grid, out_shape=jax.ShapeDtypeStruct((m, n), out_dtype),
        in_specs=[a_spec, b_spec], out_specs=pl.BlockSpec((tm, tn), o_map),
        scratch_shapes=[pltpu.VMEM((tm, tn), F32)] if nk > 1 else [],
        compiler_params=_cparams("parallel", "parallel", "arbitrary"),
    )(a, b)


def _prenorm(x, ctx, g_pre, ss, *, tm):
    n, d = x.shape
    l = ctx.shape[0]
    nc = l // tm

    def body(x_ref, c_ref, g_ref, ss_ref, h_ref):
        t = pl.program_id(0)

        def go(src, scale, shift):
            v = src[...]
            rstd = lax.rsqrt(jnp.mean(v * v, axis=-1, keepdims=True) + EPS)
            h_ref[...] = ((v * rstd * g_ref[...]) * (1.0 + scale) + shift).astype(BF)

        @pl.when(t < nc)
        def _():
            go(c_ref, ss_ref[2:3, :], ss_ref[3:4, :])

        @pl.when(t >= nc)
        def _():
            go(x_ref, ss_ref[0:1, :], ss_ref[1:2, :])

    return pl.pallas_call(
        body, name="prenorm", grid=((l + n) // tm,), out_shape=jax.ShapeDtypeStruct((l + n, d), BF),
        in_specs=[pl.BlockSpec((tm, d), lambda t: (jnp.maximum(t - nc, 0), 0)),
                  pl.BlockSpec((tm, d), lambda t: (jnp.minimum(t, nc - 1), 0)),
                  pl.BlockSpec((1, d), lambda t: (0, 0)), pl.BlockSpec((4, d), lambda t: (0, 0))],
        out_specs=pl.BlockSpec((tm, d), lambda t: (t, 0)), compiler_params=_cparams("parallel"),
    )(x, ctx, g_pre, ss)


def _swap32(v):
    lane = lax.broadcasted_iota(jnp.int32, v.shape, 1)
    return jnp.where((lane % 64) < 32, pltpu.roll(v, 96, 1), pltpu.roll(v, 32, 1))


def _qk_prep(proj, gain, ctab, stab, *, row0, col0, width, nrows, tm, name):
    cw = min(512, width)
    rb0, cb0 = row0 // tm, col0 // cw
    assert row0 % tm == 0 and col0 % cw == 0 and width % cw == 0 and nrows % tm == 0

    def body(p_ref, g_ref, c_ref, s_ref, o_ref):
        for hd in range(cw // HEAD_DIM):
            cols = slice(hd * HEAD_DIM, (hd + 1) * HEAD_DIM)
            v = p_ref[:, cols].astype(F32)
            rstd = lax.rsqrt(jnp.mean(v * v, axis=-1, keepdims=True) + EPS)
            yv = v * rstd * g_ref[...]
            o_ref[:, cols] = (yv * c_ref[...] + _swap32(yv) * s_ref[...]).astype(BF)

    return pl.pallas_call(
        body, name=name, grid=(nrows // tm, width // cw), out_shape=jax.ShapeDtypeStruct((nrows, width), BF),
        in_specs=[pl.BlockSpec((tm, cw), lambda i, j: (i + rb0, j + cb0)), pl.BlockSpec((1, HEAD_DIM), lambda i, j: (0, 0)),
                  pl.BlockSpec((tm, HEAD_DIM), lambda i, j: (i, 0)), pl.BlockSpec((tm, HEAD_DIM), lambda i, j: (i, 0))],
        out_specs=pl.BlockSpec((tm, cw), lambda i, j: (i, j)), compiler_params=_cparams("parallel", "parallel"),
    )(proj, gain, ctab, stab)


def _qk_bwd(dy, proj, gain, ctab, stab, *, row0, col0, tm, name):
    nrows, width = dy.shape
    cw = min(512, width)
    rb0, cb0 = row0 // tm, col0 // cw

    def body(d_ref, p_ref, g_ref, c_ref, s_ref, o_ref, dg_ref):
        @pl.when(jnp.logical_and(pl.program_id(0) == 0, pl.program_id(1) == 0))
        def _():
            dg_ref[...] = jnp.zeros_like(dg_ref)

        dg = jnp.zeros((1, HEAD_DIM), F32)
        for hd in range(cw // HEAD_DIM):
            cols = slice(hd * HEAD_DIM, (hd + 1) * HEAD_DIM)
            v = p_ref[:, cols].astype(F32)
            rstd = lax.rsqrt(jnp.mean(v * v, axis=-1, keepdims=True) + EPS)
            nv = v * rstd
            d = d_ref[:, cols]
            dyu = d * c_ref[...] + _swap32(d * s_ref[...])
            dg = dg + jnp.sum(dyu * nv, axis=0, keepdims=True)
            dn = dyu * g_ref[...]
            o_ref[:, cols] = (rstd * (dn - nv * jnp.mean(dn * nv, axis=-1, keepdims=True))).astype(BF)
        dg_ref[0:1, :] += dg

    return pl.pallas_call(
        body, name=name, grid=(nrows // tm, width // cw),
        out_shape=(jax.ShapeDtypeStruct((nrows, width), BF), jax.ShapeDtypeStruct((8, HEAD_DIM), F32)),
        in_specs=[pl.BlockSpec((tm, cw), lambda i, j: (i, j)), pl.BlockSpec((tm, cw), lambda i, j: (i + rb0, j + cb0)),
                  pl.BlockSpec((1, HEAD_DIM), lambda i, j: (0, 0)),
                  pl.BlockSpec((tm, HEAD_DIM), lambda i, j: (i, 0)), pl.BlockSpec((tm, HEAD_DIM), lambda i, j: (i, 0))],
        out_specs=(pl.BlockSpec((tm, cw), lambda i, j: (i, j)), pl.BlockSpec((8, HEAD_DIM), lambda i, j: (0, 0))),
        compiler_params=_cparams("arbitrary", "arbitrary"),
    )(dy, proj, gain, ctab, stab)


def _flash_fwd(qr, k_all, vt_ctx, vt_lat, *, tq, scale):
    n, aw = qr.shape
    s_len, kvw = k_all.shape
    l = vt_ctx.shape[-1]
    n_j, tk = vt_lat.shape[1], vt_lat.shape[-1]
    kvh = kvw // HEAD_DIM
    n_i = n // tq
    gw = GQA_GROUP * HEAD_DIM
    c2 = scale * LOG2E

    def body(q_ref, k_ref, vc_ref, vl_ref, o_ref, lse_ref, acc_ref, m_ref, l_ref):
        acc_ref[...] = jnp.zeros_like(acc_ref)
        l_ref[...] = jnp.zeros_like(l_ref)
        m_ref[...] = jnp.full(m_ref.shape, -1e30, F32)

        def tile(kj, vtj):
            for g in range(GQA_GROUP):
                st = _dot(kj, q_ref[:, g * HEAD_DIM:(g + 1) * HEAD_DIM], 1, 1) * c2
                m_old = m_ref[g]
                m_new = jnp.maximum(m_old, jnp.max(st, axis=0, keepdims=True))
                alpha = jnp.exp2(m_old - m_new)
                pt = jnp.exp2(st - m_new)
                l_ref[g] = alpha * l_ref[g] + jnp.sum(pt, axis=0, keepdims=True)
                m_ref[g] = m_new
                acc_ref[g] = acc_ref[g] * alpha + _dot(vtj, pt.astype(BF), 1, 0)

        tile(k_ref[0:l, :], vc_ref[0, 0])

        def step(j, carry):
            tile(k_ref[pl.ds(pl.multiple_of(l + j * tk, min(l, tk)), tk), :], vl_ref[0, j])
            return carry

        lax.fori_loop(0, n_j, step, 0)
        for g in range(GQA_GROUP):
            o_ref[:, g * HEAD_DIM:(g + 1) * HEAD_DIM] = (acc_ref[g] / l_ref[g]).T.astype(BF)
            lse_ref[0, g, 0] = m_ref[g] + jnp.log(l_ref[g]) * LOG2E

    return pl.pallas_call(
        body, name="flash_fwd", grid=(kvh, n_i),
        out_shape=(jax.ShapeDtypeStruct((n, aw), BF), jax.ShapeDtypeStruct((kvh, GQA_GROUP, n_i, 1, tq), F32)),
        in_specs=[pl.BlockSpec((tq, gw), lambda h, i: (i, h)), pl.BlockSpec((s_len, HEAD_DIM), lambda h, i: (0, h)),
                  pl.BlockSpec((1, 1, HEAD_DIM, l), lambda h, i: (h, 0, 0, 0)),
                  pl.BlockSpec((1, n_j, HEAD_DIM, tk), lambda h, i: (h, 0, 0, 0))],
        out_specs=(pl.BlockSpec((tq, gw), lambda h, i: (i, h)),
                   pl.BlockSpec((1, GQA_GROUP, 1, 1, tq), lambda h, i: (h, 0, i, 0, 0))),
        scratch_shapes=[pltpu.VMEM((GQA_GROUP, HEAD_DIM, tq), F32), pltpu.VMEM((GQA_GROUP, 1, tq), F32),
                        pltpu.VMEM((GQA_GROUP, 1, tq), F32)],
        compiler_params=_cparams("parallel", "parallel"),
    )(qr, k_all, vt_ctx, vt_lat)


def _attn_delta(o, do, *, tq):
    n, aw = o.shape
    kvh = aw // (GQA_GROUP * HEAD_DIM)
    gw = GQA_GROUP * HEAD_DIM

    def body(o_ref, do_ref, d_ref):
        for g in range(GQA_GROUP):
            cols = slice(g * HEAD_DIM, (g + 1) * HEAD_DIM)
            prod = o_ref[:, cols].astype(F32) * do_ref[:, cols].astype(F32)
            d_ref[0, g, 0] = jnp.sum(prod.T, axis=0, keepdims=True)

    return pl.pallas_call(
        body, name="attn_delta", grid=(kvh, n // tq),
        out_shape=jax.ShapeDtypeStruct((kvh, GQA_GROUP, n // tq, 1, tq), F32),
        in_specs=[pl.BlockSpec((tq, gw), lambda h, i: (i, h)), pl.BlockSpec((tq, gw), lambda h, i: (i, h))],
        out_specs=pl.BlockSpec((1, GQA_GROUP, 1, 1, tq), lambda h, i: (h, 0, i, 0, 0)),
        compiler_params=_cparams("parallel", "parallel"),
    )(o, do)


def _flash_bwd(qr, do, k_all, v_all, lse, delta, *, tq, tk, scale):
    n, aw = qr.shape
    s_len, kvw = k_all.shape
    kvh = kvw // HEAD_DIM
    n_i, n_j = n // tq, s_len // tk
    gw = GQA_GROUP * HEAD_DIM

    c2 = scale * LOG2E

    def body(q_ref, do_ref, k_ref, v_ref, lse_ref, dl_ref, dq_ref, dk_ref, dv_ref):
        jj = pl.program_id(1)

        @pl.when(jj == 0)
        def _():
            dq_ref[...] = jnp.zeros_like(dq_ref)

        kj = k_ref[...]
        vj = v_ref[...]
        dk_ref[...] = jnp.zeros_like(dk_ref)
        dv_ref[...] = jnp.zeros_like(dv_ref)

        def step(i, carry):
            rows = pl.ds(pl.multiple_of(i * tq, tq), tq)
            dv_part = dk_part = None
            for g in range(GQA_GROUP):
                cols = slice(g * HEAD_DIM, (g + 1) * HEAD_DIM)
                qg = q_ref[rows, cols]
                dog = do_ref[rows, cols]
                pt = jnp.exp2(_dot(kj, qg, 1, 1) * c2 - lse_ref[0, g, i])
                dst = (pt * (_dot(vj, dog, 1, 1) - dl_ref[0, g, i])).astype(BF)
                dv_g = _dot(pt.astype(BF), dog, 1, 0)
                dk_g = _dot(dst, qg, 1, 0)
                dv_part = dv_g if dv_part is None else dv_part + dv_g
                dk_part = dk_g if dk_part is None else dk_part + dk_g
                dq_ref[rows, cols] += _dot(dst, kj, 0, 0)
            dv_ref[...] += dv_part
            dk_ref[...] += dk_part
            return carry

        lax.fori_loop(0, n_i, step, 0)
        dk_ref[...] = dk_ref[...] * scale

        @pl.when(jj == n_j - 1)
        def _():
            dq_ref[...] = dq_ref[...] * scale

    stat_spec = pl.BlockSpec((1, GQA_GROUP, n_i, 1, tq), lambda h, j: (h, 0, 0, 0, 0))
    return pl.pallas_call(
        body, name="flash_bwd", grid=(kvh, n_j),
        out_shape=(jax.ShapeDtypeStruct((n, aw), F32), jax.ShapeDtypeStruct((s_len, kvw), F32),
                   jax.ShapeDtypeStruct((s_len, kvw), F32)),
        in_specs=[pl.BlockSpec((n, gw), lambda h, j: (0, h)), pl.BlockSpec((n, gw), lambda h, j: (0, h)),
                  pl.BlockSpec((tk, HEAD_DIM), lambda h, j: (j, h)), pl.BlockSpec((tk, HEAD_DIM), lambda h, j: (j, h)),
                  stat_spec, stat_spec],
        out_specs=(pl.BlockSpec((n, gw), lambda h, j: (0, h)), pl.BlockSpec((tk, HEAD_DIM), lambda h, j: (j, h)),
                   pl.BlockSpec((tk, HEAD_DIM), lambda h, j: (j, h))),
        compiler_params=_cparams("parallel", "arbitrary"),
    )(qr, do, k_all, v_all, lse, delta)


def _pool_window(r, gi, l, n, tm):
    win = tm + 2 * POOL_HALO
    start = jnp.clip(l + r * tm - POOL_HALO, l, l + n - win)
    half = lax.shift_left(jnp.int32(1), gi)
    tok = r * tm + lax.broadcasted_iota(jnp.int32, (tm, win), 0)
    src = (start - l) + lax.broadcasted_iota(jnp.int32, (tm, win), 1)
    tok1 = r * tm + lax.broadcasted_iota(jnp.int32, (tm, 1), 0)
    cnt = (jnp.minimum(tok1 + half, n) - jnp.maximum(tok1 - half, 0)).astype(F32)
    return start, win, tok, src, half, cnt


def _pool_fwd(proj, pool_w, *, l, n, col0, tm):
    s_len = proj.shape[0]
    pg = pool_w.shape[-1]
    cb0 = col0 // pg
    assert col0 % pg == 0 and l % POOL_HALO == 0 and n >= tm + 2 * POOL_HALO

    def body(u_ref, w_ref, d_ref, po_ref):
        gi, r = pl.program_id(0), pl.program_id(1)
        start, win, tok, src, half, cnt = _pool_window(r, gi, l, n, tm)
        band = jnp.logical_and(src >= tok - half, src < tok + half).astype(BF)
        uw = u_ref[pl.ds(pl.multiple_of(start, POOL_HALO), win), :]
        ut = u_ref[pl.ds(pl.multiple_of(l + r * tm, POOL_HALO), tm), :].astype(F32)
        dv = (_dot(band, uw, 1, 0) / cnt - ut).astype(BF)
        d_ref[...] = dv
        po_ref[...] = _dot(dv, w_ref[0], 1, 0).astype(BF)

    return pl.pallas_call(
        body, name="pool_fwd", grid=(N_POOL_GROUPS, n // tm),
        out_shape=(jax.ShapeDtypeStruct((n, N_POOL_GROUPS * pg), BF), jax.ShapeDtypeStruct((n, N_POOL_GROUPS * pg), BF)),
        in_specs=[pl.BlockSpec((s_len, pg), lambda g, r: (0, cb0 + g)), pl.BlockSpec((1, pg, pg), lambda g, r: (g, 0, 0))],
        out_specs=(pl.BlockSpec((tm, pg), lambda g, r: (r, g)), pl.BlockSpec((tm, pg), lambda g, r: (r, g))),
        compiler_params=_cparams("parallel", "parallel"),
    )(proj, pool_w)


def _pool_bwd_map(dval, pool_w, *, n, col0, tm):
    pg = pool_w.shape[-1]
    cb0 = col0 // pg

    def body(d_ref, w_ref, o_ref):
        gi, r = pl.program_id(0), pl.program_id(1)
        half = lax.shift_left(jnp.int32(1), gi)
        tok1 = r * tm + lax.broadcasted_iota(jnp.int32, (tm, 1), 0)
        cnt = (jnp.minimum(tok1 + half, n) - jnp.maximum(tok1 - half, 0)).astype(F32)
        o_ref[...] = (_dot(d_ref[...], w_ref[0], 1, 1) / cnt).astype(BF)

    return pl.pallas_call(
        body, name="pool_bwd_map", grid=(N_POOL_GROUPS, n // tm),
        out_shape=jax.ShapeDtypeStruct((n, N_POOL_GROUPS * pg), BF),
        in_specs=[pl.BlockSpec((tm, pg), lambda g, r: (r, cb0 + g)), pl.BlockSpec((1, pg, pg), lambda g, r: (g, 0, 0))],
        out_specs=pl.BlockSpec((tm, pg), lambda g, r: (r, g)), compiler_params=_cparams("parallel", "parallel"),
    )(dval, pool_w)


def _pool_bwd_window(dds, *, n, tm):
    pg = dds.shape[1] // N_POOL_GROUPS

    def body(d_ref, o_ref):
        gi, r = pl.program_id(0), pl.program_id(1)
        start, win, tok, src, half, cnt = _pool_window(r, gi, 0, n, tm)
        band = jnp.logical_and(tok >= src - half, tok < src + half).astype(BF)
        dw = d_ref[pl.ds(pl.multiple_of(start, POOL_HALO), win), :]
        dt = d_ref[pl.ds(pl.multiple_of(r * tm, POOL_HALO), tm), :].astype(F32)
        o_ref[...] = (_dot(band, dw, 1, 0) - dt * cnt).astype(BF)

    return pl.pallas_call(
        body, name="pool_bwd_window", grid=(N_POOL_GROUPS, n // tm), out_shape=jax.ShapeDtypeStruct(dds.shape, BF),
        in_specs=[pl.BlockSpec((n, pg), lambda g, r: (0, g))], out_specs=pl.BlockSpec((tm, pg), lambda g, r: (r, g)),
        compiler_params=_cparams("parallel", "parallel"),
    )(dds)


def _pool_wgrad(dmat, dval, *, col0, tk):
    n, pw = dmat.shape
    pg = pw // N_POOL_GROUPS
    cb0 = col0 // pg
    nk = n // tk

    def body(a_ref, b_ref, o_ref, acc):
        k = pl.program_id(1)

        @pl.when(k == 0)
        def _():
            acc[...] = jnp.zeros_like(acc)

        acc[...] += _dot(a_ref[...], b_ref[...], 0, 0)

        @pl.when(k == nk - 1)
        def _():
            o_ref[0] = acc[...].astype(BF)

    return pl.pallas_call(
        body, name="pool_wgrad", grid=(N_POOL_GROUPS, nk), out_shape=jax.ShapeDtypeStruct((N_POOL_GROUPS, pg, pg), BF),
        in_specs=[pl.BlockSpec((tk, pg), lambda g, k: (k, g)), pl.BlockSpec((tk, pg), lambda g, k: (k, cb0 + g))],
        out_specs=pl.BlockSpec((1, pg, pg), lambda g, k: (g, 0, 0)), scratch_shapes=[pltpu.VMEM((pg, pg), F32)],
        compiler_params=_cparams("parallel", "arbitrary"),
    )(dmat, dval)


def _merge(attn_o, po_raw, proj, pool_scale, *, l, ga0, gp0, tm):
    n, aw = attn_o.shape
    bw = aw // 2
    rb0 = l // tm
    ga_b, gp_b = ga0 // bw, gp0 // bw
    assert ga0 % bw == 0 and gp0 % bw == 0 and l % tm == 0

    def body(a_ref, p_ref, g_ref, ps_ref, y_ref):
        cb = pl.program_id(1)
        sg, _ = _silu_and_grad(g_ref[...].astype(F32))

        @pl.when(cb < 2)
        def _():
            y_ref[...] = (a_ref[...].astype(F32) * sg).astype(BF)

        @pl.when(cb >= 2)
        def _():
            y_ref[...] = (p_ref[...].astype(F32) * ps_ref[...] * sg).astype(BF)

    return pl.pallas_call(
        body, name="merge", grid=(n // tm, 4), out_shape=jax.ShapeDtypeStruct((n, 2 * aw), BF),
        in_specs=[pl.BlockSpec((tm, bw), lambda i, cb: (i, jnp.minimum(cb, 1))),
                  pl.BlockSpec((tm, bw), lambda i, cb: (i, jnp.maximum(cb - 2, 0))),
                  pl.BlockSpec((tm, bw), lambda i, cb: (i + rb0, jnp.where(cb < 2, ga_b + cb, gp_b + cb - 2))),
                  pl.BlockSpec((1, bw), lambda i, cb: (0, jnp.maximum(cb - 2, 0)))],
        out_specs=pl.BlockSpec((tm, bw), lambda i, cb: (i, cb)), compiler_params=_cparams("parallel", "arbitrary"),
    )(attn_o, po_raw, proj, pool_scale)


def _merge_bwd(dy, attn_o, po_raw, proj, pool_scale, *, l, ga0, gp0, tm):
    n, aw = attn_o.shape
    bw = aw // 2
    rb0 = l // tm
    ga_b, gp_b = ga0 // bw, gp0 // bw

    def body(dy_ref, a_ref, p_ref, g_ref, ps_ref, dv_ref, dg_ref, dps_ref):
        cb, i = pl.program_id(0), pl.program_id(1)
        sg, sgrad = _silu_and_grad(g_ref[...].astype(F32))
        dyv = dy_ref[...].astype(F32)

        @pl.when(cb < 2)
        def _():
            dv_ref[...] = (dyv * sg).astype(BF)
            dg_ref[...] = (dyv * a_ref[...].astype(F32) * sgrad).astype(BF)

        @pl.when(cb >= 2)
        def _():
            @pl.when(i == 0)
            def _():
                dps_ref[...] = jnp.zeros_like(dps_ref)

            pr = p_ref[...].astype(F32)
            dpo = dyv * sg
            dv_ref[...] = (dpo * ps_ref[...]).astype(BF)
            dg_ref[...] = (dyv * (pr * ps_ref[...]) * sgrad).astype(BF)
            dps_ref[0:1, :] += jnp.sum(dpo * pr, axis=0, keepdims=True)

    blk = pl.BlockSpec((tm, bw), lambda cb, i: (i, cb))
    return pl.pallas_call(
        body, name="merge_bwd", grid=(4, n // tm),
        out_shape=(jax.ShapeDtypeStruct((n, 2 * aw), BF), jax.ShapeDtypeStruct((n, 2 * aw), BF),
                   jax.ShapeDtypeStruct((8, aw), F32)),
        in_specs=[blk, pl.BlockSpec((tm, bw), lambda cb, i: (i, jnp.minimum(cb, 1))),
                  pl.BlockSpec((tm, bw), lambda cb, i: (i, jnp.maximum(cb - 2, 0))),
                  pl.BlockSpec((tm, bw), lambda cb, i: (i + rb0, jnp.where(cb < 2, ga_b + cb, gp_b + cb - 2))),
                  pl.BlockSpec((1, bw), lambda cb, i: (0, jnp.maximum(cb - 2, 0)))],
        out_specs=(blk, blk, pl.BlockSpec((8, bw), lambda cb, i: (0, jnp.maximum(cb - 2, 0)))),
        compiler_params=_cparams("arbitrary", "arbitrary"),
    )(dy, attn_o, po_raw, proj, pool_scale)


def _post(x, out, target, gate, g_post, *, tm):
    n, d = x.shape

    def body(x_ref, o_ref, t_ref, gate_ref, gp_ref, dxn_ref, dout_ref, st_ref):
        @pl.when(pl.program_id(0) == 0)
        def _():
            st_ref[...] = jnp.zeros_like(st_ref)

        ov = o_ref[...]
        rstd = lax.rsqrt(jnp.mean(ov * ov, axis=-1, keepdims=True) + EPS)
        on = ov * rstd
        rn = on * gp_ref[...]
        err = (x_ref[...] + gate_ref[...] * rn) - t_ref[...]
        dxn = err / d
        dxn_ref[...] = dxn
        drn = dxn * gate_ref[...]
        don = drn * gp_ref[...]
        dout_ref[...] = (rstd * (don - on * jnp.mean(don * on, axis=-1, keepdims=True))).astype(BF)
        st_ref[0:1, :] += jnp.sum(dxn * rn, axis=0, keepdims=True)
        st_ref[1:2, :] += jnp.sum(drn * on, axis=0, keepdims=True)
        st_ref[2:3, :] += jnp.sum(err * err, axis=0, keepdims=True)

    row = pl.BlockSpec((tm, d), lambda i: (i, 0))
    vec = pl.BlockSpec((1, d), lambda i: (0, 0))
    return pl.pallas_call(
        body, name="post", grid=(n // tm,),
        out_shape=(jax.ShapeDtypeStruct((n, d), F32), jax.ShapeDtypeStruct((n, d), BF), jax.ShapeDtypeStruct((8, d), F32)),
        in_specs=[row, row, row, vec, vec], out_specs=(row, row, pl.BlockSpec((8, d), lambda i: (0, 0))),
        compiler_params=_cparams("arbitrary"),
    )(x, out, target, gate, g_post)


def _prenorm_bwd(x, ctx, dh, dxn, g_pre, ss, *, tm):
    n, d = x.shape
    l = ctx.shape[0]
    nc = l // tm

    def body(x_ref, c_ref, dh_ref, dxn_ref, g_ref, ss_ref, gx_ref, st_ref):
        t = pl.program_id(0)

        @pl.when(t == 0)
        def _():
            st_ref[...] = jnp.zeros_like(st_ref)

        def go(src, scale, row):
            v = src[...]
            dhv = dh_ref[...].astype(F32)
            rstd = lax.rsqrt(jnp.mean(v * v, axis=-1, keepdims=True) + EPS)
            xn = v * rstd
            st_ref[row:row + 1, :] += jnp.sum(dhv, axis=0, keepdims=True)
            st_ref[row + 1:row + 2, :] += jnp.sum(dhv * (xn * g_ref[...]), axis=0, keepdims=True)
            dg = dhv * (1.0 + scale)
            st_ref[4:5, :] += jnp.sum(dg * xn, axis=0, keepdims=True)
            dxn_ = dg * g_ref[...]
            return rstd * (dxn_ - xn * jnp.mean(dxn_ * xn, axis=-1, keepdims=True))

        @pl.when(t < nc)
        def _():
            go(c_ref, ss_ref[2:3, :], 2)

        @pl.when(t >= nc)
        def _():
            gx_ref[...] = dxn_ref[...] + go(x_ref, ss_ref[0:1, :], 0)

    lat = pl.BlockSpec((tm, d), lambda t: (jnp.maximum(t - nc, 0), 0))
    return pl.pallas_call(
        body, name="prenorm_bwd", grid=((l + n) // tm,),
        out_shape=(jax.ShapeDtypeStruct((n, d), F32), jax.ShapeDtypeStruct((8, d), F32)),
        in_specs=[lat, pl.BlockSpec((tm, d), lambda t: (jnp.minimum(t, nc - 1), 0)),
                  pl.BlockSpec((tm, d), lambda t: (t, 0)), lat,
                  pl.BlockSpec((1, d), lambda t: (0, 0)), pl.BlockSpec((4, d), lambda t: (0, 0))],
        out_specs=(lat, pl.BlockSpec((8, d), lambda t: (0, 0))), compiler_params=_cparams("arbitrary"),
    )(x, ctx, dh, dxn, g_pre, ss)


def _ada_fwd(craw, w_ada, b_loc, *, tn):
    d, wn = w_ada.shape

    def body(c_ref, w_ref, b_ref, o_ref):
        act, _ = _silu_and_grad(c_ref[...])
        o_ref[...] = _dot(act.astype(BF), w_ref[...].astype(BF), 1, 0) + b_ref[...]

    return pl.pallas_call(
        body, name="ada_fwd", grid=(wn // tn,), out_shape=jax.ShapeDtypeStruct((16, wn), F32),
        in_specs=[pl.BlockSpec((16, d), lambda j: (0, 0)), pl.BlockSpec((d, tn), lambda j: (0, j)),
                  pl.BlockSpec((1, tn), lambda j: (0, j))],
        out_specs=pl.BlockSpec((16, tn), lambda j: (0, j)), compiler_params=_cparams("parallel"),
    )(craw, w_ada, b_loc)


def _ada_bwd(craw_t, dm, w, m, v, *, tn):
    d, wn = w.shape

    def body(ct_ref, dm_ref, w_ref, m_ref, v_ref, g_ref, dl_ref, nm_ref, nv_ref, da_ref):
        @pl.when(pl.program_id(0) == 0)
        def _():
            da_ref[...] = jnp.zeros_like(da_ref)

        act, _ = _silu_and_grad(ct_ref[...])
        dmb = dm_ref[...].astype(BF)
        wv = w_ref[...]
        gv = _dot(act.astype(BF), dmb, 1, 0)
        da_ref[...] += _dot(dmb, wv.astype(BF), 1, 1)
        nm = ADAM_B1 * m_ref[...] + (1.0 - ADAM_B1) * gv
        nv = ADAM_B2 * v_ref[...] + (1.0 - ADAM_B2) * (gv * gv)
        m_hat = nm / (1.0 - ADAM_B1 ** ADAM_STEP)
        v_hat = nv / (1.0 - ADAM_B2 ** ADAM_STEP)
        g_ref[...] = gv
        dl_ref[...] = -ADAM_LR * (m_hat / (jnp.sqrt(v_hat) + ADAM_EPS) + ADAM_WD * wv)
        nm_ref[...] = nm
        nv_ref[...] = nv

    col = pl.BlockSpec((d, tn), lambda j: (0, j))
    return pl.pallas_call(
        body, name="ada_bwd", grid=(wn // tn,),
        out_shape=tuple([jax.ShapeDtypeStruct((d, wn), F32)] * 4) + (jax.ShapeDtypeStruct((16, d), F32),),
        in_specs=[pl.BlockSpec((d, 16), lambda j: (0, 0)), pl.BlockSpec((16, tn), lambda j: (0, j)), col, col, col],
        out_specs=(col, col, col, col, pl.BlockSpec((16, d), lambda j: (0, 0))),
        compiler_params=_cparams("arbitrary"),
    )(craw_t, dm, w, m, v)


def _reduce_small(gath, d3):
    t = gath.shape[-1]

    def body(g_ref, dm_ref, s_ref):
        tot = g_ref[0]
        for b in range(1, 8):
            tot = tot + g_ref[b]
        s_ref[...] = tot
        dm_ref[...] = jnp.zeros_like(dm_ref)
        for b in range(8):
            dm_ref[b:b + 1, :] = g_ref[b][:, 0:d3]
        dm_ref[8:9, :] = tot[:, d3:2 * d3]

    return pl.pallas_call(
        body, name="reduce_small", out_shape=(jax.ShapeDtypeStruct((16, d3), F32), jax.ShapeDtypeStruct((1, t), F32)),
        in_specs=[VMEM_SPEC], out_specs=(VMEM_SPEC, VMEM_SPEC),
    )(gath)


def _cctx_grad(parts, c_ctx):
    def body(p_ref, c_ref, o_ref):
        tot = (p_ref[0] + p_ref[1]) + (p_ref[2] + p_ref[3])
        _, sgrad = _silu_and_grad(c_ref[...])
        o_ref[...] = tot * sgrad

    return pl.pallas_call(
        body, name="cctx_grad", out_shape=jax.ShapeDtypeStruct(c_ctx.shape, F32),
        in_specs=[VMEM_SPEC, VMEM_SPEC], out_specs=VMEM_SPEC,
    )(parts, c_ctx)


def _rope_tables(n, l):
    rows = n // GRID_W
    row = jnp.repeat(jnp.arange(rows, dtype=F32), GRID_W)
    col = jnp.tile(jnp.arange(GRID_W, dtype=F32), rows)
    inv = ROPE_THETA ** (-jnp.arange(ROPE_PAIRS, dtype=F32) / ROPE_PAIRS)
    ang = jnp.concatenate([row[:, None] * inv, col[:, None] * inv], axis=-1)
    cos, sin = jnp.cos(ang), jnp.sin(ang)
    cr, cc, sr, sc = cos[:, :ROPE_PAIRS], cos[:, ROPE_PAIRS:], sin[:, :ROPE_PAIRS], sin[:, ROPE_PAIRS:]
    ctab = jnp.concatenate([cr, cr, cc, cc], axis=-1)
    stab = jnp.concatenate([-sr, sr, -sc, sc], axis=-1)
    ctab = jnp.concatenate([jnp.ones((l, HEAD_DIM), F32), ctab], axis=0)
    stab = jnp.concatenate([jnp.zeros((l, HEAD_DIM), F32), stab], axis=0)
    return ctab, stab


def kernel(x, c, ctx, c_ctx, w_ada, b_ada, norm_pre, norm_post, w_in, q_norm, k_norm, pool_w, pool_scale, w_out, loss_target, m_c_ctx, m_w_ada, m_b_ada, m_norm_pre, m_norm_post, m_w_in, m_q_norm, m_k_norm, m_pool_w, m_pool_scale, m_w_out, v_c_ctx, v_w_ada, v_b_ada, v_norm_pre, v_norm_post, v_w_in, v_q_norm, v_k_norm, v_pool_w, v_pool_scale, v_w_out):
    x2, ctx2, tgt = x[0], ctx[0], loss_target[0]
    n, d = x2.shape
    l = ctx2.shape[0]
    s_len = l + n
    aw = d // 2
    pw = d - aw
    n_heads = aw // HEAD_DIM
    kvw = (n_heads // GQA_GROUP) * HEAD_DIM
    pg = pw // N_POOL_GROUPS
    k0, v0, ga0 = aw, aw + kvw, aw + 2 * kvw
    up0, gp0 = ga0 + aw, ga0 + aw + pw
    in_w = gp0 + pw
    d3 = 3 * d
    ada_w = w_ada.shape[-1]
    px, py, pc = lax.axis_index("x"), lax.axis_index("y"), lax.axis_index("c")
    chip = 2 * px + py
    tr = min(256, l)
    tq = min(512, n)
    tk = min(256, l)
    ts = s_len // 8 if (s_len // 8) % 16 == 0 else tr
    scale = HEAD_DIM ** -0.5

    c_all = _allgather_small(c, chips_only=False, name="gather_c").reshape(8, d)
    craw = jnp.concatenate([c_all, c_ctx.reshape(1, d), jnp.zeros((7, d), F32)], axis=0)
    b_loc = lax.dynamic_slice(b_ada, (0, chip * ada_w), (1, ada_w))
    mod_part = _ada_fwd(craw, w_ada[0], b_loc, tn=min(512, ada_w))
    mod_all = _allgather_small(mod_part, chips_only=True, name="gather_mod")
    mod_all = jnp.transpose(mod_all, (1, 0, 2)).reshape(16, d3)
    me = 4 * px + 2 * py + pc
    mod_b = lax.dynamic_slice(mod_all, (me, 0), (1, d3))
    mod_c = mod_all[8:9]
    ss = jnp.concatenate([mod_b[:, d:2 * d], mod_b[:, 0:d], mod_c[:, d:2 * d], mod_c[:, 0:d]], axis=0)
    gate = mod_b[:, 2 * d:]

    w_in_f, w_out_f, pool_f = _allgather_weights(
        [_cast_into(w_in[0], 1, chip, name="cast_w_in"), _cast_into(w_out[0], 0, chip, name="cast_w_out"),
         _cast_into(pool_w[0], 1, chip, name="cast_pool_w")],
        [(1, 0), (0, 1), (1, 2)], name="gather_weights")

    ctab, stab = _rope_tables(n, l)
    h = _prenorm(x2, ctx2, norm_pre, ss, tm=tr)
    proj = _matmul(h, w_in_f, ca=1, cb=0, tm=ts, tn=min(512, in_w), tk=d, out_dtype=BF, name="in_proj")
    qr = _qk_prep(proj, q_norm, ctab[l:], stab[l:], row0=l, col0=0, width=aw, nrows=n, tm=tr, name="q_prep")
    kr = _qk_prep(proj, k_norm, ctab, stab, row0=0, col0=k0, width=kvw, nrows=s_len, tm=tr, name="k_prep")
    v_all = proj[:, v0:v0 + kvw]
    tkf = min(512, n)
    vt_ctx = jnp.transpose(v_all[:l].reshape(1, l, kvw // HEAD_DIM, HEAD_DIM), (2, 0, 3, 1))
    vt_lat = jnp.transpose(v_all[l:].reshape(n // tkf, tkf, kvw // HEAD_DIM, HEAD_DIM), (2, 0, 3, 1))
    attn_o, lse = _flash_fwd(qr, kr, vt_ctx, vt_lat, tq=tq, scale=scale)
    dmat, po_raw = _pool_fwd(proj, pool_f, l=l, n=n, col0=up0, tm=tr)
    yv = _merge(attn_o, po_raw, proj, pool_scale, l=l, ga0=ga0, gp0=gp0, tm=tr)
    out = _matmul(yv, w_out_f, ca=1, cb=0, tm=min(512, n), tn=min(512, d), tk=d, out_dtype=F32, name="out_proj")

    dxn, d_out, post_st = _post(x2, out, tgt, gate, norm_post, tm=min(128, n))
    loss = 0.5 * jnp.sum(post_st[2]) / d
    loss = lax.psum(loss, ("x", "y", "c"))
    g_wout = _matmul(yv, d_out, ca=0, cb=0, tm=min(512, d), tn=min(512, d), tk=n, out_dtype=BF, name="wgrad_out")
    dy = _matmul(d_out, w_out_f, ca=1, cb=1, tm=min(512, n), tn=min(512, d), tk=d, out_dtype=BF, name="dgrad_out")
    dval, dgate, dps = _merge_bwd(dy, attn_o, po_raw, proj, pool_scale, l=l, ga0=ga0, gp0=gp0, tm=tr)
    g_pool = _pool_wgrad(dmat, dval, col0=aw, tk=min(512, n))
    dds = _pool_bwd_map(dval, pool_f, n=n, col0=aw, tm=tr)
    du = _pool_bwd_window(dds, n=n, tm=tr)
    do = dval[:, :aw]
    delta = _attn_delta(attn_o, do, tq=tq)
    dq, dk, dv = _flash_bwd(qr, do, kr, v_all, lse, delta, tq=tq, tk=tk, scale=scale)
    dq_raw, dgq = _qk_bwd(dq, proj, q_norm, ctab[l:], stab[l:], row0=l, col0=0, tm=tr, name="q_bwd")
    dk_raw, dgk = _qk_bwd(dk, proj, k_norm, ctab, stab, row0=0, col0=k0, tm=tr, name="k_bwd")
    dvb = dv.astype(BF)
    dproj = jnp.concatenate([
        jnp.concatenate([jnp.zeros((l, aw), BF), dq_raw], axis=0), dk_raw, dvb,
        jnp.concatenate([jnp.zeros((l, aw + 2 * pw), BF),
                         jnp.concatenate([dgate[:, :aw], du, dgate[:, aw:]], axis=1)], axis=0)], axis=1)
    g_win = _matmul(h, dproj, ca=0, cb=0, tm=min(512, d), tn=min(512, in_w), tk=s_len, out_dtype=BF, name="wgrad_in")
    dh = _matmul(dproj, w_in_f, ca=1, cb=1, tm=ts, tn=min(512, d), tk=in_w // 2, out_dtype=BF, name="dgrad_in")
    grad_x, pre_st = _prenorm_bwd(x2, ctx2, dh, dxn, norm_pre, ss, tm=min(128, l))

    zero_d = jnp.zeros((1, d), F32)
    packed = jnp.concatenate([pre_st[0:1], pre_st[1:2], post_st[0:1], pre_st[2:3], pre_st[3:4], zero_d,
                              pre_st[4:5], post_st[1:2], dgq[0:1], dgk[0:1], dps[0:1]], axis=1)
    gath = _allgather_small(packed, chips_only=False, name="gather_small")
    dm, sums = _reduce_small(gath, d3)
    o1 = 2 * d3
    g_npre, g_npost = sums[:, o1:o1 + d], sums[:, o1 + d:o1 + 2 * d]
    g_q, g_k = sums[:, o1 + 2 * d:o1 + 2 * d + HEAD_DIM], sums[:, o1 + 2 * d + HEAD_DIM:o1 + 2 * d + 2 * HEAD_DIM]
    g_ps = sums[:, o1 + 2 * d + 2 * HEAD_DIM:]
    g_bada = sums[:, 0:d3] + sums[:, d3:2 * d3]
    dm_loc = lax.dynamic_slice(dm, (0, chip * ada_w), (16, ada_w))
    g_wada, dl_wada, nm_wada, nv_wada, dact = _ada_bwd(craw.T, dm_loc, w_ada[0], m_w_ada[0], v_w_ada[0], tn=128)
    cparts = _allgather_small(dact[8:9], chips_only=True, name="gather_cctx")
    g_cctx = _cctx_grad(cparts, c_ctx.reshape(1, d)).reshape(d)

    mats = [g_win, g_wout, g_pool]
    half_axes = [0, 1, 2]
    shard_axes = [1, 0, 1]
    recv_a = _half_to_sibling(mats, half_axes, name="reduce_pair")
    pair = [_add_own_half(mats[i], recv_a[i], half_axes[i], pc, name=f"pair_sum_{i}") for i in range(3)]
    recv_b = _pieces_to_owners(pair, shard_axes, name="reduce_chips")
    red = [_sum_own_block(pair[i], recv_b[3 * i:3 * i + 3], shard_axes[i], half_axes[i], chip, pc, name=f"chip_sum_{i}")
           for i in range(3)]
    g_win_s, g_wout_s, g_pool_s = _halves_join(red, half_axes, name="reduce_join")

    def upd(w, g, m, v, name):
        return _adamw(w, g.reshape(w.shape), m, v, name=name)

    grads = {"c_ctx": g_cctx, "b_ada": g_bada, "norm_pre": g_npre, "norm_post": g_npost, "w_in": g_win_s[None],
             "q_norm": g_q, "k_norm": g_k, "pool_w": g_pool_s[None], "pool_scale": g_ps, "w_out": g_wout_s[None]}
    res = {"w_ada": (g_wada[None], dl_wada[None], nm_wada[None], nv_wada[None])}
    given = {"c_ctx": (c_ctx, m_c_ctx, v_c_ctx), "b_ada": (b_ada, m_b_ada, v_b_ada),
             "norm_pre": (norm_pre, m_norm_pre, v_norm_pre), "norm_post": (norm_post, m_norm_post, v_norm_post),
             "w_in": (w_in, m_w_in, v_w_in), "q_norm": (q_norm, m_q_norm, v_q_norm), "k_norm": (k_norm, m_k_norm, v_k_norm),
             "pool_w": (pool_w, m_pool_w, v_pool_w), "pool_scale": (pool_scale, m_pool_scale, v_pool_scale),
             "w_out": (w_out, m_w_out, v_w_out)}
    for nme, (w, m, v) in given.items():
        g = grads[nme].reshape(w.shape)
        res[nme] = (g,) + upd(w, g, m, v, "adamw_" + nme)
    order = ["c_ctx", "w_ada", "b_ada", "norm_pre", "norm_post", "w_in", "q_norm", "k_norm", "pool_w", "pool_scale", "w_out"]
    return (loss, grad_x[None], *[res[k][0] for k in order], *[res[k][1] for k in order],
            *[res[k][2] for k in order], *[res[k][3] for k in order])
```

```python
import functools

import jax
import jax.numpy as jnp
from jax import lax
from jax.experimental import pallas as pl
from jax.experimental.pallas import tpu as pltpu

F32 = jnp.float32
BF = jnp.bfloat16
MESH = pl.DeviceIdType.MESH

HEAD_DIM = 128
GQA_GROUP = 4
GRID_W = 64
ROPE_PAIRS = HEAD_DIM // 4
ROPE_THETA = 10000.0
EPS = 1e-6
N_POOL_GROUPS = 4
POOL_HALO = 128
ADAM_LR = 0.001
ADAM_B1 = 0.9
ADAM_B2 = 0.999
ADAM_EPS = 1e-08
ADAM_WD = 0.01
ADAM_STEP = 10
LOG2E = 1.4426950408889634
MIB = 2 ** 20
VMEM_LIMIT = 48 * MIB
CHIP_MASKS = ((1, 0, 0), (0, 1, 0), (1, 1, 0))
ALL_MASKS = ((0, 0, 1), (0, 1, 0), (0, 1, 1), (1, 0, 0), (1, 0, 1), (1, 1, 0), (1, 1, 1))
HBM_SPEC = pl.BlockSpec(memory_space=pl.ANY)
VMEM_SPEC = pl.BlockSpec(memory_space=pltpu.VMEM)


def _cparams(*sem):
    return pltpu.CompilerParams(dimension_semantics=sem, vmem_limit_bytes=VMEM_LIMIT)


def _sigmoid(v):
    return 1.0 / (1.0 + jnp.exp(-v))


def _silu_and_grad(v):
    s = _sigmoid(v)
    return v * s, s * (1.0 + v * (1.0 - s))


def _dot(a, b, ca, cb):
    return lax.dot_general(a, b, (((ca,), (cb,)), ((), ())), preferred_element_type=F32)


def _block_rows(rows, width, itemsize=4, target=MIB):
    best = 8
    for t in range(8, rows + 1, 8):
        if rows % t == 0 and t * width * itemsize <= target:
            best = t
    return best if rows % 8 == 0 else rows


def _my_pos():
    return lax.axis_index("x"), lax.axis_index("y"), lax.axis_index("c")


def _flip(pos, mask):
    return tuple(jnp.bitwise_xor(p, m) if m else p for p, m in zip(pos, mask))


def _allgather_small(v, *, chips_only, name):
    r, w = v.shape
    masks = CHIP_MASKS if chips_only else ALL_MASKS
    nslot = 4 if chips_only else 8

    def slot(pos):
        return 2 * pos[0] + pos[1] if chips_only else 4 * pos[0] + 2 * pos[1] + pos[2]

    def body(v_ref, o_ref, send_sems, recv_sems, local_sem):
        me = _my_pos()

        def copy(k, block_of, to):
            return pltpu.make_async_remote_copy(
                src_ref=v_ref, dst_ref=o_ref.at[slot(block_of)], send_sem=send_sems.at[k], recv_sem=recv_sems.at[k],
                device_id=to, device_id_type=MESH)

        mine = pltpu.make_async_copy(v_ref, o_ref.at[slot(me)], local_sem)
        mine.start()
        sends = [copy(k, me, _flip(me, m)) for k, m in enumerate(masks)]
        for cp in sends:
            cp.start()
        for k, m in enumerate(masks):
            copy(k, _flip(me, m), me).wait_recv()
        for cp in sends:
            cp.wait_send()
        mine.wait()

    return pl.pallas_call(
        body, name=name, out_shape=jax.ShapeDtypeStruct((nslot, r, w), v.dtype),
        in_specs=[VMEM_SPEC], out_specs=VMEM_SPEC,
        scratch_shapes=[pltpu.SemaphoreType.DMA((len(masks),)), pltpu.SemaphoreType.DMA((len(masks),)),
                        pltpu.SemaphoreType.DMA],
    )(v)


def _sl(ref, axis, start, size):
    idx = [slice(None)] * len(ref.shape)
    idx[axis] = pl.ds(start, size)
    return ref.at[tuple(idx)]


class _Side:
    def __init__(self, ins, out_shapes, aliases, sem_shapes, start, finish, mid=None):
        self.ins, self.out_shapes, self.aliases, self.sem_shapes = list(ins), list(out_shapes), dict(aliases), list(sem_shapes)
        self.start, self.finish, self.mid = start, finish, mid


def _run_side(side, *, name):
    n_in, n_out = len(side.ins), len(side.out_shapes)

    def body(*refs):
        parts = refs[:n_in], refs[n_in:n_in + n_out], refs[n_in + n_out:]
        side.start(*parts)
        if side.mid is not None:
            side.mid(*parts)
        side.finish(*parts)

    return pl.pallas_call(
        body, name=name, out_shape=tuple(side.out_shapes), in_specs=[HBM_SPEC] * n_in, out_specs=tuple([HBM_SPEC] * n_out),
        input_output_aliases=side.aliases, scratch_shapes=side.sem_shapes,
    )(*side.ins)


def _side_at(grid):
    pids = [pl.program_id(a) for a in range(len(grid))]

    def at(values):
        cond = pids[0] == values[0]
        for p, v in zip(pids[1:], values[1:]):
            cond = jnp.logical_and(cond, p == v)
        return cond

    zeros = [0] * len(grid)
    return at(zeros), at([grid[0] // 2] + zeros[1:]), at([g - 1 for g in grid])


def _gather_side(shards, layouts):
    nm = len(shards)

    def copies(outs, sems):
        send_sems, recv_sems, fsend_sems, frecv_sems = sems
        x, y, c = _my_pos()
        me = (x, y, c)
        sib = (x, y, 1 - c)
        sends, landed, fwds, from_sib = [], [], [], []
        for m in range(nm):
            a_s, a_h = layouts[m]
            ns, nh = outs[m].shape[a_s] // 4, outs[m].shape[a_h] // 2

            def region(chip_pos, half, m=m, a_s=a_s, a_h=a_h, ns=ns, nh=nh):
                j = 2 * chip_pos[0] + chip_pos[1]
                return _sl(_sl(outs[m], a_s, j * ns, ns), a_h, half * nh, nh)

            for k, mask in enumerate(CHIP_MASKS):
                other = _flip(me, mask)

                def rc(ref, ssem, rsem, to):
                    return pltpu.make_async_remote_copy(src_ref=ref, dst_ref=ref, send_sem=ssem, recv_sem=rsem,
                                                        device_id=to, device_id_type=MESH)

                sends.append(rc(region(me, c), send_sems.at[m, k], recv_sems.at[m, k], other))
                landed.append(rc(region(other, c), send_sems.at[m, k], recv_sems.at[m, k], me))
                fwds.append(rc(region(other, c), fsend_sems.at[m, k], frecv_sems.at[m, k], sib))
                from_sib.append(rc(region(other, 1 - c), fsend_sems.at[m, k], frecv_sems.at[m, k], sib))
        return sends, landed, fwds, from_sib

    def start(ins, outs, sems):
        for cp in copies(outs, sems)[0]:
            cp.start()

    def mid(ins, outs, sems):
        _, landed, fwds, _ = copies(outs, sems)
        for arrived, fw in zip(landed, fwds):
            arrived.wait_recv()
            fw.start()

    def finish(ins, outs, sems):
        sends, _, fwds, from_sib = copies(outs, sems)
        for cp in from_sib:
            cp.wait_recv()
        for cp in sends + fwds:
            cp.wait_send()

    return _Side(shards, [jax.ShapeDtypeStruct(s.shape, s.dtype) for s in shards], {i: i for i in range(nm)},
                 [pltpu.SemaphoreType.DMA((nm, 3))] * 4, start, finish, mid)


def _pair_side(mats, half_axes):
    nm = len(mats)

    def copies(ins, outs, sems):
        x, y, c = _my_pos()
        cps = []
        for m in range(nm):
            nh = ins[m].shape[half_axes[m]] // 2
            cps.append(pltpu.make_async_remote_copy(
                src_ref=_sl(ins[m], half_axes[m], (1 - c) * nh, nh), dst_ref=outs[m],
                send_sem=sems[0].at[m], recv_sem=sems[1].at[m], device_id=(x, y, 1 - c), device_id_type=MESH))
        return cps

    def start(ins, outs, sems):
        for cp in copies(ins, outs, sems):
            cp.start()

    def finish(ins, outs, sems):
        for cp in copies(ins, outs, sems):
            cp.wait()

    out_shapes = []
    for s, a_h in zip(mats, half_axes):
        shp = list(s.shape)
        shp[a_h] //= 2
        out_shapes.append(jax.ShapeDtypeStruct(tuple(shp), s.dtype))
    return _Side(mats, out_shapes, {}, [pltpu.SemaphoreType.DMA((nm,))] * 2, start, finish)


def _chips_side(mats, shard_axes):
    nm = len(mats)

    def copies(ins, outs, sems):
        me = _my_pos()
        cps = []
        for m in range(nm):
            ns = ins[m].shape[shard_axes[m]] // 4
            for k, mask in enumerate(CHIP_MASKS):
                other = _flip(me, mask)
                cps.append(pltpu.make_async_remote_copy(
                    src_ref=_sl(ins[m], shard_axes[m], (2 * other[0] + other[1]) * ns, ns), dst_ref=outs[3 * m + k],
                    send_sem=sems[0].at[m, k], recv_sem=sems[1].at[m, k], device_id=other, device_id_type=MESH))
        return cps

    def start(ins, outs, sems):
        for cp in copies(ins, outs, sems):
            cp.start()

    def finish(ins, outs, sems):
        for cp in copies(ins, outs, sems):
            cp.wait()

    out_shapes = []
    for s, a_s in zip(mats, shard_axes):
        shp = list(s.shape)
        shp[a_s] //= 4
        out_shapes += [jax.ShapeDtypeStruct(tuple(shp), s.dtype)] * 3
    return _Side(mats, out_shapes, {}, [pltpu.SemaphoreType.DMA((nm, 3))] * 2, start, finish)


def _join_side(halves, half_axes):
    nm = len(halves)

    def copies(outs, sems):
        x, y, c = _my_pos()
        mine, theirs = [], []
        for m in range(nm):
            nh = outs[m].shape[half_axes[m]] // 2
            for half, lst in ((c, mine), (1 - c, theirs)):
                ref = _sl(outs[m], half_axes[m], half * nh, nh)
                lst.append(pltpu.make_async_remote_copy(
                    src_ref=ref, dst_ref=ref, send_sem=sems[0].at[m], recv_sem=sems[1].at[m],
                    device_id=(x, y, 1 - c), device_id_type=MESH))
        return mine, theirs

    def start(ins, outs, sems):
        for cp in copies(outs, sems)[0]:
            cp.start()

    def finish(ins, outs, sems):
        mine, theirs = copies(outs, sems)
        for cp in mine:
            cp.wait_send()
        for cp in theirs:
            cp.wait_recv()

    return _Side(halves, [jax.ShapeDtypeStruct(s.shape, s.dtype) for s in halves], {i: i for i in range(nm)},
                 [pltpu.SemaphoreType.DMA((nm,))] * 2, start, finish)


def _part_specs(shp, axis, slot, itemsize, target=MIB):
    if len(shp) == 2:
        rows, width = shp
        tm = _block_rows(rows, width, itemsize, target)
        nb = rows // tm
        own = pl.BlockSpec((tm, width), lambda i, p: (i, 0))
        if axis == 1:
            part = pl.BlockSpec((tm, width), lambda i, p: (i, p[slot]))
        else:
            part = pl.BlockSpec((tm, width), lambda i, p: (p[slot] * nb + i, 0))
        return (nb,), own, part
    assert len(shp) == 3 and axis in (1, 2)
    g, rows, width = shp
    own = pl.BlockSpec((1, rows, width), lambda i, p: (i, 0, 0))
    if axis == 1:
        part = pl.BlockSpec((1, rows, width), lambda i, p: (i, p[slot], 0))
    else:
        part = pl.BlockSpec((1, rows, width), lambda i, p: (i, 0, p[slot]))
    return (g,), own, part


def _cast_into(w, axis, p, *, name):
    grid, own, part = _part_specs(w.shape, axis, 0, 4, 2 * MIB)
    big = list(w.shape)
    big[axis] *= 4

    def body(p_ref, w_ref, o_ref):
        o_ref[...] = w_ref[...].astype(BF)

    return pl.pallas_call(
        body, name=name, out_shape=jax.ShapeDtypeStruct(tuple(big), BF),
        grid_spec=pltpu.PrefetchScalarGridSpec(num_scalar_prefetch=1, grid=grid, in_specs=[own], out_specs=part),
        compiler_params=_cparams("parallel"),
    )(p.reshape(1), w)


def _add_own_half(full, recv, half_axis, c, *, name):
    grid, own, part = _part_specs(recv.shape, half_axis, 0, 2)

    def body(c_ref, f_ref, r_ref, o_ref):
        o_ref[...] = (f_ref[...].astype(F32) + r_ref[...].astype(F32)).astype(BF)

    return pl.pallas_call(
        body, name=name, out_shape=jax.ShapeDtypeStruct(recv.shape, BF),
        grid_spec=pltpu.PrefetchScalarGridSpec(num_scalar_prefetch=1, grid=grid, in_specs=[part, own], out_specs=own),
        compiler_params=_cparams("parallel"),
    )(c.reshape(1), full, recv)


def _sum_own_block(mine_all, recvs, shard_axis, half_axis, j, c, *, name):
    shp = recvs[0].shape
    grid, own, mine = _part_specs(shp, shard_axis, 0, 4)
    _, _, place = _part_specs(shp, half_axis, 1, 4)
    big = list(shp)
    big[half_axis] *= 2

    def body(p_ref, a_ref, r0, r1, r2, o_ref):
        o_ref[...] = ((a_ref[...].astype(F32) + r0[...].astype(F32)) + r1[...].astype(F32)) + r2[...].astype(F32)

    return pl.pallas_call(
        body, name=name, out_shape=jax.ShapeDtypeStruct(tuple(big), F32),
        grid_spec=pltpu.PrefetchScalarGridSpec(
            num_scalar_prefetch=1, grid=grid, in_specs=[mine, own, own, own], out_specs=place),
        compiler_params=_cparams("parallel"),
    )(jnp.stack([j, c]), mine_all, *recvs)


def _adamw(w, g, m, v, *, name):
    shp = w.shape
    width = shp[-1] if len(shp) > 1 else shp[0]
    rows = 1
    for d in shp[:-1]:
        rows *= d
    if len(shp) == 1:
        rows = 1
    args = [a.reshape(rows, width) for a in (w, g, m, v)]
    tm = _block_rows(rows, width, 4, MIB // 2)

    def body(w_ref, g_ref, m_ref, v_ref, d_ref, nm_ref, nv_ref):
        gv = g_ref[...]
        nm = ADAM_B1 * m_ref[...] + (1.0 - ADAM_B1) * gv
        nv = ADAM_B2 * v_ref[...] + (1.0 - ADAM_B2) * (gv * gv)
        m_hat = nm / (1.0 - ADAM_B1 ** ADAM_STEP)
        v_hat = nv / (1.0 - ADAM_B2 ** ADAM_STEP)
        d_ref[...] = -ADAM_LR * (m_hat / (jnp.sqrt(v_hat) + ADAM_EPS) + ADAM_WD * w_ref[...])
        nm_ref[...] = nm
        nv_ref[...] = nv

    spec = pl.BlockSpec((tm, width), lambda i: (i, 0))
    outs = pl.pallas_call(
        body, name=name, grid=(rows // tm,), out_shape=tuple([jax.ShapeDtypeStruct((rows, width), F32)] * 3),
        in_specs=[spec] * 4, out_specs=tuple([spec] * 3), compiler_params=_cparams("parallel"),
    )(*args)
    return tuple(o.reshape(shp) for o in outs)


def _matmul(a, b, *, ca, cb, tm, tn, tk, out_dtype, name, b_resident=False, side=None):
    m, kdim = a.shape[1 - ca], a.shape[ca]
    n = b.shape[1 - cb]
    nk = kdim // tk
    assert m % tm == 0 and n % tn == 0 and kdim % tk == 0
    n_sin = len(side.ins) if side else 0
    n_sout = len(side.out_shapes) if side else 0
    if b_resident:
        gi = lambda p, q, k: (q, p, k)
        grid = (n // tn, m // tm, nk)
    else:
        gi = lambda p, q, k: (p, q, k)
        grid = (m // tm, n // tn, nk)

    def body(*refs):
        a_ref, b_ref = refs[0], refs[1]
        s_ins = refs[2:2 + n_sin]
        o_ref = refs[2 + n_sin]
        s_outs = refs[3 + n_sin:3 + n_sin + n_sout]
        rest = refs[3 + n_sin + n_sout:]
        acc, sems = (rest[:1], rest[1:]) if nk > 1 else ((), rest)
        if side:
            first, middle, last = _side_at(grid)
            pl.when(first)(lambda: side.start(s_ins, s_outs, sems))
            if side.mid is not None:
                pl.when(middle)(lambda: side.mid(s_ins, s_outs, sems))
        part = _dot(a_ref[...], b_ref[...], ca, cb)
        if nk == 1:
            o_ref[...] = part.astype(o_ref.dtype)
        else:
            k = pl.program_id(2)

            @pl.when(k == 0)
            def _():
                acc[0][...] = part

            @pl.when(jnp.logical_and(k > 0, k < nk - 1))
            def _():
                acc[0][...] += part

            @pl.when(k == nk - 1)
            def _():
                o_ref[...] = (acc[0][...] + part).astype(o_ref.dtype)
        if side:
            pl.when(last)(lambda: side.finish(s_ins, s_outs, sems))

    def a_map(p, q, k):
        i, _, kk = gi(p, q, k)
        return (i, kk) if ca == 1 else (kk, i)

    def b_map(p, q, k):
        _, j, kk = gi(p, q, k)
        return (kk, j) if cb == 0 else (j, kk)

    def o_map(p, q, k):
        i, j, _ = gi(p, q, k)
        return (i, j)

    a_spec = pl.BlockSpec((tm, tk) if ca == 1 else (tk, tm), a_map)
    b_spec = pl.BlockSpec((tk, tn) if cb == 0 else (tn, tk), b_map)
    o_shape = jax.ShapeDtypeStruct((m, n), out_dtype)
    acc_shapes = [pltpu.VMEM((tm, tn), F32)] if nk > 1 else []
    if not side:
        return pl.pallas_call(
            body, name=name, grid=grid, out_shape=o_shape,
            in_specs=[a_spec, b_spec], out_specs=pl.BlockSpec((tm, tn), o_map), scratch_shapes=acc_shapes,
            compiler_params=_cparams("parallel", "parallel", "arbitrary"),
        )(a, b)
    return pl.pallas_call(
        body, name=name, grid=grid, out_shape=(o_shape, *side.out_shapes),
        in_specs=[a_spec, b_spec] + [HBM_SPEC] * n_sin,
        out_specs=(pl.BlockSpec((tm, tn), o_map), *([HBM_SPEC] * n_sout)),
        input_output_aliases={2 + i: 1 + o for i, o in side.aliases.items()},
        scratch_shapes=acc_shapes + side.sem_shapes,
        compiler_params=_cparams("arbitrary", "arbitrary", "arbitrary"),
    )(a, b, *side.ins)


def _prenorm(x, ctx, g_pre, ss, *, tm):
    n, d = x.shape
    l = ctx.shape[0]
    nc = l // tm

    def body(x_ref, c_ref, g_ref, ss_ref, h_ref):
        t = pl.program_id(0)

        def go(src, scale, shift):
            v = src[...]
            rstd = lax.rsqrt(jnp.mean(v * v, axis=-1, keepdims=True) + EPS)
            h_ref[...] = ((v * rstd * g_ref[...]) * (1.0 + scale) + shift).astype(BF)

        @pl.when(t < nc)
        def _():
            go(c_ref, ss_ref[2:3, :], ss_ref[3:4, :])

        @pl.when(t >= nc)
        def _():
            go(x_ref, ss_ref[0:1, :], ss_ref[1:2, :])

    return pl.pallas_call(
        body, name="prenorm", grid=((l + n) // tm,), out_shape=jax.ShapeDtypeStruct((l + n, d), BF),
        in_specs=[pl.BlockSpec((tm, d), lambda t: (jnp.maximum(t - nc, 0), 0)),
                  pl.BlockSpec((tm, d), lambda t: (jnp.minimum(t, nc - 1), 0)),
                  pl.BlockSpec((1, d), lambda t: (0, 0)), pl.BlockSpec((4, d), lambda t: (0, 0))],
        out_specs=pl.BlockSpec((tm, d), lambda t: (t, 0)), compiler_params=_cparams("parallel"),
    )(x, ctx, g_pre, ss)


def _swap32(v):
    lane = lax.broadcasted_iota(jnp.int32, v.shape, 1)
    return jnp.where((lane % 64) < 32, pltpu.roll(v, 96, 1), pltpu.roll(v, 32, 1))


def _qk_prep(proj, gain, ctab, stab, *, row0, col0, width, nrows, tm, name):
    cw = min(512, width)
    rb0, cb0 = row0 // tm, col0 // cw
    assert row0 % tm == 0 and col0 % cw == 0 and width % cw == 0 and nrows % tm == 0

    def body(p_ref, g_ref, c_ref, s_ref, o_ref):
        for hd in range(cw // HEAD_DIM):
            cols = slice(hd * HEAD_DIM, (hd + 1) * HEAD_DIM)
            v = p_ref[:, cols].astype(F32)
            rstd = lax.rsqrt(jnp.mean(v * v, axis=-1, keepdims=True) + EPS)
            yv = v * rstd * g_ref[...]
            o_ref[:, cols] = (yv * c_ref[...] + _swap32(yv) * s_ref[...]).astype(BF)

    return pl.pallas_call(
        body, name=name, grid=(nrows // tm, width // cw), out_shape=jax.ShapeDtypeStruct((nrows, width), BF),
        in_specs=[pl.BlockSpec((tm, cw), lambda i, j: (i + rb0, j + cb0)), pl.BlockSpec((1, HEAD_DIM), lambda i, j: (0, 0)),
                  pl.BlockSpec((tm, HEAD_DIM), lambda i, j: (i, 0)), pl.BlockSpec((tm, HEAD_DIM), lambda i, j: (i, 0))],
        out_specs=pl.BlockSpec((tm, cw), lambda i, j: (i, j)), compiler_params=_cparams("parallel", "parallel"),
    )(proj, gain, ctab, stab)


def _qk_bwd(dy, proj, gain, ctab, stab, *, row0, col0, tm, name):
    nrows, width = dy.shape
    cw = min(512, width)
    rb0, cb0 = row0 // tm, col0 // cw

    def body(d_ref, p_ref, g_ref, c_ref, s_ref, o_ref, dg_ref):
        @pl.when(jnp.logical_and(pl.program_id(0) == 0, pl.program_id(1) == 0))
        def _():
            dg_ref[...] = jnp.zeros_like(dg_ref)

        dg = jnp.zeros((1, HEAD_DIM), F32)
        for hd in range(cw // HEAD_DIM):
            cols = slice(hd * HEAD_DIM, (hd + 1) * HEAD_DIM)
            v = p_ref[:, cols].astype(F32)
            rstd = lax.rsqrt(jnp.mean(v * v, axis=-1, keepdims=True) + EPS)
            nv = v * rstd
            d = d_ref[:, cols]
            dyu = d * c_ref[...] + _swap32(d * s_ref[...])
            dg = dg + jnp.sum(dyu * nv, axis=0, keepdims=True)
            dn = dyu * g_ref[...]
            o_ref[:, cols] = (rstd * (dn - nv * jnp.mean(dn * nv, axis=-1, keepdims=True))).astype(BF)
        dg_ref[0:1, :] += dg

    return pl.pallas_call(
        body, name=name, grid=(nrows // tm, width // cw),
        out_shape=(jax.ShapeDtypeStruct((nrows, width), BF), jax.ShapeDtypeStruct((8, HEAD_DIM), F32)),
        in_specs=[pl.BlockSpec((tm, cw), lambda i, j: (i, j)), pl.BlockSpec((tm, cw), lambda i, j: (i + rb0, j + cb0)),
                  pl.BlockSpec((1, HEAD_DIM), lambda i, j: (0, 0)),
                  pl.BlockSpec((tm, HEAD_DIM), lambda i, j: (i, 0)), pl.BlockSpec((tm, HEAD_DIM), lambda i, j: (i, 0))],
        out_specs=(pl.BlockSpec((tm, cw), lambda i, j: (i, j)), pl.BlockSpec((8, HEAD_DIM), lambda i, j: (0, 0))),
        compiler_params=_cparams("arbitrary", "arbitrary"),
    )(dy, proj, gain, ctab, stab)


def _flash_fwd(qr, k_all, vt_ctx, vt_lat, *, tq, scale):
    n, aw = qr.shape
    s_len, kvw = k_all.shape
    l = vt_ctx.shape[-1]
    n_j, tk = vt_lat.shape[1], vt_lat.shape[-1]
    kvh = kvw // HEAD_DIM
    n_i = n // tq
    gw = GQA_GROUP * HEAD_DIM
    c2 = scale * LOG2E

    def body(q_ref, k_ref, vc_ref, vl_ref, o_ref, lse_ref, acc_ref, m_ref, l_ref):
        acc_ref[...] = jnp.zeros_like(acc_ref)
        l_ref[...] = jnp.zeros_like(l_ref)
        m_ref[...] = jnp.full(m_ref.shape, -1e30, F32)

        def tile(kj, vtj):
            for g in range(GQA_GROUP):
                st = _dot(kj, q_ref[:, g * HEAD_DIM:(g + 1) * HEAD_DIM], 1, 1) * c2
                m_old = m_ref[g]
                m_new = jnp.maximum(m_old, jnp.max(st, axis=0, keepdims=True))
                alpha = jnp.exp2(m_old - m_new)
                pt = jnp.exp2(st - m_new)
                l_ref[g] = alpha * l_ref[g] + jnp.sum(pt, axis=0, keepdims=True)
                m_ref[g] = m_new
                acc_ref[g] = acc_ref[g] * alpha + _dot(vtj, pt.astype(BF), 1, 0)

        tile(k_ref[0:l, :], vc_ref[0, 0])

        def step(j, carry):
            tile(k_ref[pl.ds(pl.multiple_of(l + j * tk, min(l, tk)), tk), :], vl_ref[0, j])
            return carry

        lax.fori_loop(0, n_j, step, 0)
        for g in range(GQA_GROUP):
            o_ref[:, g * HEAD_DIM:(g + 1) * HEAD_DIM] = (acc_ref[g] / l_ref[g]).T.astype(BF)
            lse_ref[0, g, 0] = m_ref[g] + jnp.log(l_ref[g]) * LOG2E

    return pl.pallas_call(
        body, name="flash_fwd", grid=(kvh, n_i),
        out_shape=(jax.ShapeDtypeStruct((n, aw), BF), jax.ShapeDtypeStruct((kvh, GQA_GROUP, n_i, 1, tq), F32)),
        in_specs=[pl.BlockSpec((tq, gw), lambda h, i: (i, h)), pl.BlockSpec((s_len, HEAD_DIM), lambda h, i: (0, h)),
                  pl.BlockSpec((1, 1, HEAD_DIM, l), lambda h, i: (h, 0, 0, 0)),
                  pl.BlockSpec((1, n_j, HEAD_DIM, tk), lambda h, i: (h, 0, 0, 0))],
        out_specs=(pl.BlockSpec((tq, gw), lambda h, i: (i, h)),
                   pl.BlockSpec((1, GQA_GROUP, 1, 1, tq), lambda h, i: (h, 0, i, 0, 0))),
        scratch_shapes=[pltpu.VMEM((GQA_GROUP, HEAD_DIM, tq), F32), pltpu.VMEM((GQA_GROUP, 1, tq), F32),
                        pltpu.VMEM((GQA_GROUP, 1, tq), F32)],
        compiler_params=_cparams("parallel", "parallel"),
    )(qr, k_all, vt_ctx, vt_lat)


def _attn_delta(o, do, *, tq):
    n, aw = o.shape
    kvh = aw // (GQA_GROUP * HEAD_DIM)
    gw = GQA_GROUP * HEAD_DIM

    def body(o_ref, do_ref, d_ref):
        for g in range(GQA_GROUP):
            cols = slice(g * HEAD_DIM, (g + 1) * HEAD_DIM)
            prod = o_ref[:, cols].astype(F32) * do_ref[:, cols].astype(F32)
            d_ref[0, g, 0] = jnp.sum(prod.T, axis=0, keepdims=True)

    return pl.pallas_call(
        body, name="attn_delta", grid=(kvh, n // tq),
        out_shape=jax.ShapeDtypeStruct((kvh, GQA_GROUP, n // tq, 1, tq), F32),
        in_specs=[pl.BlockSpec((tq, gw), lambda h, i: (i, h)), pl.BlockSpec((tq, gw), lambda h, i: (i, h))],
        out_specs=pl.BlockSpec((1, GQA_GROUP, 1, 1, tq), lambda h, i: (h, 0, i, 0, 0)),
        compiler_params=_cparams("parallel", "parallel"),
    )(o, do)


def _flash_bwd(qr, do, k_all, v_all, lse, delta, *, tq, tk, scale, side=None):
    n, aw = qr.shape
    s_len, kvw = k_all.shape
    kvh = kvw // HEAD_DIM
    n_i, n_j = n // tq, s_len // tk
    gw = GQA_GROUP * HEAD_DIM
    n_sin = len(side.ins) if side else 0
    n_sout = len(side.out_shapes) if side else 0
    c2 = scale * LOG2E

    def body(*refs):
        q_ref, do_ref, k_ref, v_ref, lse_ref, dl_ref = refs[:6]
        s_ins = refs[6:6 + n_sin]
        dq_ref, dk_ref, dv_ref = refs[6 + n_sin:9 + n_sin]
        s_outs = refs[9 + n_sin:9 + n_sin + n_sout]
        sems = refs[9 + n_sin + n_sout:]
        jj = pl.program_id(1)
        if side:
            first, _, last = _side_at((kvh, n_j))
            pl.when(first)(lambda: side.start(s_ins, s_outs, sems))

        @pl.when(jj == 0)
        def _():
            dq_ref[...] = jnp.zeros_like(dq_ref)

        kj = k_ref[...]
        vj = v_ref[...]
        dk_ref[...] = jnp.zeros_like(dk_ref)
        dv_ref[...] = jnp.zeros_like(dv_ref)

        def step(i, carry):
            rows = pl.ds(pl.multiple_of(i * tq, tq), tq)
            dv_part = dk_part = None
            for g in range(GQA_GROUP):
                cols = slice(g * HEAD_DIM, (g + 1) * HEAD_DIM)
                qg = q_ref[rows, cols]
                dog = do_ref[rows, cols]
                pt = jnp.exp2(_dot(kj, qg, 1, 1) * c2 - lse_ref[0, g, i])
                dst = (pt * (_dot(vj, dog, 1, 1) - dl_ref[0, g, i])).astype(BF)
                dv_g = _dot(pt.astype(BF), dog, 1, 0)
                dk_g = _dot(dst, qg, 1, 0)
                dv_part = dv_g if dv_part is None else dv_part + dv_g
                dk_part = dk_g if dk_part is None else dk_part + dk_g
                dq_ref[rows, cols] += _dot(dst, kj, 0, 0)
            dv_ref[...] += dv_part
            dk_ref[...] += dk_part
            return carry

        lax.fori_loop(0, n_i, step, 0)
        dk_ref[...] = dk_ref[...] * scale

        @pl.when(jj == n_j - 1)
        def _():
            dq_ref[...] = dq_ref[...] * scale

        if side:
            pl.when(last)(lambda: side.finish(s_ins, s_outs, sems))

    stat_spec = pl.BlockSpec((1, GQA_GROUP, n_i, 1, tq), lambda h, j: (h, 0, 0, 0, 0))
    kv_spec = pl.BlockSpec((tk, HEAD_DIM), lambda h, j: (j, h))
    q_spec = pl.BlockSpec((n, gw), lambda h, j: (0, h))
    out_shape = (jax.ShapeDtypeStruct((n, aw), F32), jax.ShapeDtypeStruct((s_len, kvw), F32),
                 jax.ShapeDtypeStruct((s_len, kvw), F32))
    if not side:
        return pl.pallas_call(
            body, name="flash_bwd", grid=(kvh, n_j), out_shape=out_shape,
            in_specs=[q_spec, q_spec, kv_spec, kv_spec, stat_spec, stat_spec], out_specs=(q_spec, kv_spec, kv_spec),
            compiler_params=_cparams("parallel", "arbitrary"),
        )(qr, do, k_all, v_all, lse, delta)
    return pl.pallas_call(
        body, name="flash_bwd", grid=(kvh, n_j), out_shape=(*out_shape, *side.out_shapes),
        in_specs=[q_spec, q_spec, kv_spec, kv_spec, stat_spec, stat_spec] + [HBM_SPEC] * n_sin,
        out_specs=(q_spec, kv_spec, kv_spec, *([HBM_SPEC] * n_sout)),
        input_output_aliases={6 + i: 3 + o for i, o in side.aliases.items()},
        scratch_shapes=side.sem_shapes, compiler_params=_cparams("arbitrary", "arbitrary"),
    )(qr, do, k_all, v_all, lse, delta, *side.ins)


def _pool_window(r, gi, l, n, tm):
    win = tm + 2 * POOL_HALO
    start = jnp.clip(l + r * tm - POOL_HALO, l, l + n - win)
    half = lax.shift_left(jnp.int32(1), gi)
    tok = r * tm + lax.broadcasted_iota(jnp.int32, (tm, win), 0)
    src = (start - l) + lax.broadcasted_iota(jnp.int32, (tm, win), 1)
    tok1 = r * tm + lax.broadcasted_iota(jnp.int32, (tm, 1), 0)
    cnt = (jnp.minimum(tok1 + half, n) - jnp.maximum(tok1 - half, 0)).astype(F32)
    return start, win, tok, src, half, cnt


def _pool_fwd(proj, pool_w, *, l, n, col0, tm):
    s_len = proj.shape[0]
    pg = pool_w.shape[-1]
    cb0 = col0 // pg
    assert col0 % pg == 0 and l % POOL_HALO == 0 and n >= tm + 2 * POOL_HALO

    def body(u_ref, w_ref, d_ref, po_ref):
        gi, r = pl.program_id(0), pl.program_id(1)
        start, win, tok, src, half, cnt = _pool_window(r, gi, l, n, tm)
        band = jnp.logical_and(src >= tok - half, src < tok + half).astype(BF)
        uw = u_ref[pl.ds(pl.multiple_of(start, POOL_HALO), win), :]
        ut = u_ref[pl.ds(pl.multiple_of(l + r * tm, POOL_HALO), tm), :].astype(F32)
        dv = (_dot(band, uw, 1, 0) / cnt - ut).astype(BF)
        d_ref[...] = dv
        po_ref[...] = _dot(dv, w_ref[0], 1, 0).astype(BF)

    return pl.pallas_call(
        body, name="pool_fwd", grid=(N_POOL_GROUPS, n // tm),
        out_shape=(jax.ShapeDtypeStruct((n, N_POOL_GROUPS * pg), BF), jax.ShapeDtypeStruct((n, N_POOL_GROUPS * pg), BF)),
        in_specs=[pl.BlockSpec((s_len, pg), lambda g, r: (0, cb0 + g)), pl.BlockSpec((1, pg, pg), lambda g, r: (g, 0, 0))],
        out_specs=(pl.BlockSpec((tm, pg), lambda g, r: (r, g)), pl.BlockSpec((tm, pg), lambda g, r: (r, g))),
        compiler_params=_cparams("parallel", "parallel"),
    )(proj, pool_w)


def _pool_bwd_map(dval, pool_w, *, n, col0, tm):
    pg = pool_w.shape[-1]
    cb0 = col0 // pg

    def body(d_ref, w_ref, o_ref):
        gi, r = pl.program_id(0), pl.program_id(1)
        half = lax.shift_left(jnp.int32(1), gi)
        tok1 = r * tm + lax.broadcasted_iota(jnp.int32, (tm, 1), 0)
        cnt = (jnp.minimum(tok1 + half, n) - jnp.maximum(tok1 - half, 0)).astype(F32)
        o_ref[...] = (_dot(d_ref[...], w_ref[0], 1, 1) / cnt).astype(BF)

    return pl.pallas_call(
        body, name="pool_bwd_map", grid=(N_POOL_GROUPS, n // tm),
        out_shape=jax.ShapeDtypeStruct((n, N_POOL_GROUPS * pg), BF),
        in_specs=[pl.BlockSpec((tm, pg), lambda g, r: (r, cb0 + g)), pl.BlockSpec((1, pg, pg), lambda g, r: (g, 0, 0))],
        out_specs=pl.BlockSpec((tm, pg), lambda g, r: (r, g)), compiler_params=_cparams("parallel", "parallel"),
    )(dval, pool_w)


def _pool_bwd_window(dds, *, n, tm):
    pg = dds.shape[1] // N_POOL_GROUPS

    def body(d_ref, o_ref):
        gi, r = pl.program_id(0), pl.program_id(1)
        start, win, tok, src, half, cnt = _pool_window(r, gi, 0, n, tm)
        band = jnp.logical_and(tok >= src - half, tok < src + half).astype(BF)
        dw = d_ref[pl.ds(pl.multiple_of(start, POOL_HALO), win), :]
        dt = d_ref[pl.ds(pl.multiple_of(r * tm, POOL_HALO), tm), :].astype(F32)
        o_ref[...] = (_dot(band, dw, 1, 0) - dt * cnt).astype(BF)

    return pl.pallas_call(
        body, name="pool_bwd_window", grid=(N_POOL_GROUPS, n // tm), out_shape=jax.ShapeDtypeStruct(dds.shape, BF),
        in_specs=[pl.BlockSpec((n, pg), lambda g, r: (0, g))], out_specs=pl.BlockSpec((tm, pg), lambda g, r: (r, g)),
        compiler_params=_cparams("parallel", "parallel"),
    )(dds)


def _pool_wgrad(dmat, dval, *, col0, tk):
    n, pw = dmat.shape
    pg = pw // N_POOL_GROUPS
    cb0 = col0 // pg
    nk = n // tk

    def body(a_ref, b_ref, o_ref, acc):
        k = pl.program_id(1)

        @pl.when(k == 0)
        def _():
            acc[...] = jnp.zeros_like(acc)

        acc[...] += _dot(a_ref[...], b_ref[...], 0, 0)

        @pl.when(k == nk - 1)
        def _():
            o_ref[0] = acc[...].astype(BF)

    return pl.pallas_call(
        body, name="pool_wgrad", grid=(N_POOL_GROUPS, nk), out_shape=jax.ShapeDtypeStruct((N_POOL_GROUPS, pg, pg), BF),
        in_specs=[pl.BlockSpec((tk, pg), lambda g, k: (k, g)), pl.BlockSpec((tk, pg), lambda g, k: (k, cb0 + g))],
        out_specs=pl.BlockSpec((1, pg, pg), lambda g, k: (g, 0, 0)), scratch_shapes=[pltpu.VMEM((pg, pg), F32)],
        compiler_params=_cparams("parallel", "arbitrary"),
    )(dmat, dval)


def _merge(attn_o, po_raw, proj, pool_scale, *, l, ga0, gp0, tm):
    n, aw = attn_o.shape
    bw = aw // 2
    rb0 = l // tm
    ga_b, gp_b = ga0 // bw, gp0 // bw
    assert ga0 % bw == 0 and gp0 % bw == 0 and l % tm == 0

    def body(a_ref, p_ref, g_ref, ps_ref, y_ref):
        cb = pl.program_id(1)
        sg, _ = _silu_and_grad(g_ref[...].astype(F32))

        @pl.when(cb < 2)
        def _():
            y_ref[...] = (a_ref[...].astype(F32) * sg).astype(BF)

        @pl.when(cb >= 2)
        def _():
            y_ref[...] = (p_ref[...].astype(F32) * ps_ref[...] * sg).astype(BF)

    return pl.pallas_call(
        body, name="merge", grid=(n // tm, 4), out_shape=jax.ShapeDtypeStruct((n, 2 * aw), BF),
        in_specs=[pl.BlockSpec((tm, bw), lambda i, cb: (i, jnp.minimum(cb, 1))),
                  pl.BlockSpec((tm, bw), lambda i, cb: (i, jnp.maximum(cb - 2, 0))),
                  pl.BlockSpec((tm, bw), lambda i, cb: (i + rb0, jnp.where(cb < 2, ga_b + cb, gp_b + cb - 2))),
                  pl.BlockSpec((1, bw), lambda i, cb: (0, jnp.maximum(cb - 2, 0)))],
        out_specs=pl.BlockSpec((tm, bw), lambda i, cb: (i, cb)), compiler_params=_cparams("parallel", "arbitrary"),
    )(attn_o, po_raw, proj, pool_scale)


def _merge_bwd(dy, attn_o, po_raw, proj, pool_scale, *, l, ga0, gp0, tm):
    n, aw = attn_o.shape
    bw = aw // 2
    rb0 = l // tm
    ga_b, gp_b = ga0 // bw, gp0 // bw

    def body(dy_ref, a_ref, p_ref, g_ref, ps_ref, dv_ref, dg_ref, dps_ref):
        cb, i = pl.program_id(0), pl.program_id(1)
        sg, sgrad = _silu_and_grad(g_ref[...].astype(F32))
        dyv = dy_ref[...].astype(F32)

        @pl.when(cb < 2)
        def _():
            dv_ref[...] = (dyv * sg).astype(BF)
            dg_ref[...] = (dyv * a_ref[...].astype(F32) * sgrad).astype(BF)

        @pl.when(cb >= 2)
        def _():
            @pl.when(i == 0)
            def _():
                dps_ref[...] = jnp.zeros_like(dps_ref)

            pr = p_ref[...].astype(F32)
            dpo = dyv * sg
            dv_ref[...] = (dpo * ps_ref[...]).astype(BF)
            dg_ref[...] = (dyv * (pr * ps_ref[...]) * sgrad).astype(BF)
            dps_ref[0:1, :] += jnp.sum(dpo * pr, axis=0, keepdims=True)

    blk = pl.BlockSpec((tm, bw), lambda cb, i: (i, cb))
    return pl.pallas_call(
        body, name="merge_bwd", grid=(4, n // tm),
        out_shape=(jax.ShapeDtypeStruct((n, 2 * aw), BF), jax.ShapeDtypeStruct((n, 2 * aw), BF),
                   jax.ShapeDtypeStruct((8, aw), F32)),
        in_specs=[blk, pl.BlockSpec((tm, bw), lambda cb, i: (i, jnp.minimum(cb, 1))),
                  pl.BlockSpec((tm, bw), lambda cb, i: (i, jnp.maximum(cb - 2, 0))),
                  pl.BlockSpec((tm, bw), lambda cb, i: (i + rb0, jnp.where(cb < 2, ga_b + cb, gp_b + cb - 2))),
                  pl.BlockSpec((1, bw), lambda cb, i: (0, jnp.maximum(cb - 2, 0)))],
        out_specs=(blk, blk, pl.BlockSpec((8, bw), lambda cb, i: (0, jnp.maximum(cb - 2, 0)))),
        compiler_params=_cparams("arbitrary", "arbitrary"),
    )(dy, attn_o, po_raw, proj, pool_scale)


def _post(x, out, target, gate, g_post, *, tm):
    n, d = x.shape

    def body(x_ref, o_ref, t_ref, gate_ref, gp_ref, dxn_ref, dout_ref, st_ref):
        @pl.when(pl.program_id(0) == 0)
        def _():
            st_ref[...] = jnp.zeros_like(st_ref)

        ov = o_ref[...]
        rstd = lax.rsqrt(jnp.mean(ov * ov, axis=-1, keepdims=True) + EPS)
        on = ov * rstd
        rn = on * gp_ref[...]
        err = (x_ref[...] + gate_ref[...] * rn) - t_ref[...]
        dxn = err / d
        dxn_ref[...] = dxn
        drn = dxn * gate_ref[...]
        don = drn * gp_ref[...]
        dout_ref[...] = (rstd * (don - on * jnp.mean(don * on, axis=-1, keepdims=True))).astype(BF)
        st_ref[0:1, :] += jnp.sum(dxn * rn, axis=0, keepdims=True)
        st_ref[1:2, :] += jnp.sum(drn * on, axis=0, keepdims=True)
        st_ref[2:3, :] += jnp.sum(err * err, axis=0, keepdims=True)

    row = pl.BlockSpec((tm, d), lambda i: (i, 0))
    vec = pl.BlockSpec((1, d), lambda i: (0, 0))
    return pl.pallas_call(
        body, name="post", grid=(n // tm,),
        out_shape=(jax.ShapeDtypeStruct((n, d), F32), jax.ShapeDtypeStruct((n, d), BF), jax.ShapeDtypeStruct((8, d), F32)),
        in_specs=[row, row, row, vec, vec], out_specs=(row, row, pl.BlockSpec((8, d), lambda i: (0, 0))),
        compiler_params=_cparams("arbitrary"),
    )(x, out, target, gate, g_post)


def _prenorm_bwd(x, ctx, dh, dxn, g_pre, ss, *, tm):
    n, d = x.shape
    l = ctx.shape[0]
    nc = l // tm

    def body(x_ref, c_ref, dh_ref, dxn_ref, g_ref, ss_ref, gx_ref, st_ref):
        t = pl.program_id(0)

        @pl.when(t == 0)
        def _():
            st_ref[...] = jnp.zeros_like(st_ref)

        def go(src, scale, row):
            v = src[...]
            dhv = dh_ref[...].astype(F32)
            rstd = lax.rsqrt(jnp.mean(v * v, axis=-1, keepdims=True) + EPS)
            xn = v * rstd
            st_ref[row:row + 1, :] += jnp.sum(dhv, axis=0, keepdims=True)
            st_ref[row + 1:row + 2, :] += jnp.sum(dhv * (xn * g_ref[...]), axis=0, keepdims=True)
            dg = dhv * (1.0 + scale)
            st_ref[4:5, :] += jnp.sum(dg * xn, axis=0, keepdims=True)
            dxn_ = dg * g_ref[...]
            return rstd * (dxn_ - xn * jnp.mean(dxn_ * xn, axis=-1, keepdims=True))

        @pl.when(t < nc)
        def _():
            go(c_ref, ss_ref[2:3, :], 2)

        @pl.when(t >= nc)
        def _():
            gx_ref[...] = dxn_ref[...] + go(x_ref, ss_ref[0:1, :], 0)

    lat = pl.BlockSpec((tm, d), lambda t: (jnp.maximum(t - nc, 0), 0))
    return pl.pallas_call(
        body, name="prenorm_bwd", grid=((l + n) // tm,),
        out_shape=(jax.ShapeDtypeStruct((n, d), F32), jax.ShapeDtypeStruct((8, d), F32)),
        in_specs=[lat, pl.BlockSpec((tm, d), lambda t: (jnp.minimum(t, nc - 1), 0)),
                  pl.BlockSpec((tm, d), lambda t: (t, 0)), lat,
                  pl.BlockSpec((1, d), lambda t: (0, 0)), pl.BlockSpec((4, d), lambda t: (0, 0))],
        out_specs=(lat, pl.BlockSpec((8, d), lambda t: (0, 0))), compiler_params=_cparams("arbitrary"),
    )(x, ctx, dh, dxn, g_pre, ss)


def _ada_fwd(craw, w_ada, b_loc, *, tn):
    d, wn = w_ada.shape

    def body(c_ref, w_ref, b_ref, o_ref):
        act, _ = _silu_and_grad(c_ref[...])
        o_ref[...] = _dot(act.astype(BF), w_ref[...].astype(BF), 1, 0) + b_ref[...]

    return pl.pallas_call(
        body, name="ada_fwd", grid=(wn // tn,), out_shape=jax.ShapeDtypeStruct((16, wn), F32),
        in_specs=[pl.BlockSpec((16, d), lambda j: (0, 0)), pl.BlockSpec((d, tn), lambda j: (0, j)),
                  pl.BlockSpec((1, tn), lambda j: (0, j))],
        out_specs=pl.BlockSpec((16, tn), lambda j: (0, j)), compiler_params=_cparams("parallel"),
    )(craw, w_ada, b_loc)


def _ada_bwd(craw_t, dm, w, m, v, *, tn):
    d, wn = w.shape

    def body(ct_ref, dm_ref, w_ref, m_ref, v_ref, g_ref, dl_ref, nm_ref, nv_ref, da_ref):
        @pl.when(pl.program_id(0) == 0)
        def _():
            da_ref[...] = jnp.zeros_like(da_ref)

        act, _ = _silu_and_grad(ct_ref[...])
        dmb = dm_ref[...].astype(BF)
        wv = w_ref[...]
        gv = _dot(act.astype(BF), dmb, 1, 0)
        da_ref[...] += _dot(dmb, wv.astype(BF), 1, 1)
        nm = ADAM_B1 * m_ref[...] + (1.0 - ADAM_B1) * gv
        nv = ADAM_B2 * v_ref[...] + (1.0 - ADAM_B2) * (gv * gv)
        m_hat = nm / (1.0 - ADAM_B1 ** ADAM_STEP)
        v_hat = nv / (1.0 - ADAM_B2 ** ADAM_STEP)
        g_ref[...] = gv
        dl_ref[...] = -ADAM_LR * (m_hat / (jnp.sqrt(v_hat) + ADAM_EPS) + ADAM_WD * wv)
        nm_ref[...] = nm
        nv_ref[...] = nv

    col = pl.BlockSpec((d, tn), lambda j: (0, j))
    return pl.pallas_call(
        body, name="ada_bwd", grid=(wn // tn,),
        out_shape=tuple([jax.ShapeDtypeStruct((d, wn), F32)] * 4) + (jax.ShapeDtypeStruct((16, d), F32),),
        in_specs=[pl.BlockSpec((d, 16), lambda j: (0, 0)), pl.BlockSpec((16, tn), lambda j: (0, j)), col, col, col],
        out_specs=(col, col, col, col, pl.BlockSpec((16, d), lambda j: (0, 0))),
        compiler_params=_cparams("arbitrary"),
    )(craw_t, dm, w, m, v)


def _reduce_small(gath, d3):
    t = gath.shape[-1]

    def body(g_ref, dm_ref, s_ref):
        tot = g_ref[0]
        for b in range(1, 8):
            tot = tot + g_ref[b]
        s_ref[...] = tot
        dm_ref[...] = jnp.zeros_like(dm_ref)
        for b in range(8):
            dm_ref[b:b + 1, :] = g_ref[b][:, 0:d3]
        dm_ref[8:9, :] = tot[:, d3:2 * d3]

    return pl.pallas_call(
        body, name="reduce_small", out_shape=(jax.ShapeDtypeStruct((16, d3), F32), jax.ShapeDtypeStruct((1, t), F32)),
        in_specs=[VMEM_SPEC], out_specs=(VMEM_SPEC, VMEM_SPEC),
    )(gath)


def _cctx_grad(parts, c_ctx):
    def body(p_ref, c_ref, o_ref):
        tot = (p_ref[0] + p_ref[1]) + (p_ref[2] + p_ref[3])
        _, sgrad = _silu_and_grad(c_ref[...])
        o_ref[...] = tot * sgrad

    return pl.pallas_call(
        body, name="cctx_grad", out_shape=jax.ShapeDtypeStruct(c_ctx.shape, F32),
        in_specs=[VMEM_SPEC, VMEM_SPEC], out_specs=VMEM_SPEC,
    )(parts, c_ctx)


def _rope_tables(n, l):
    rows = n // GRID_W
    row = jnp.repeat(jnp.arange(rows, dtype=F32), GRID_W)
    col = jnp.tile(jnp.arange(GRID_W, dtype=F32), rows)
    inv = ROPE_THETA ** (-jnp.arange(ROPE_PAIRS, dtype=F32) / ROPE_PAIRS)
    ang = jnp.concatenate([row[:, None] * inv, col[:, None] * inv], axis=-1)
    cos, sin = jnp.cos(ang), jnp.sin(ang)
    cr, cc, sr, sc = cos[:, :ROPE_PAIRS], cos[:, ROPE_PAIRS:], sin[:, :ROPE_PAIRS], sin[:, ROPE_PAIRS:]
    ctab = jnp.concatenate([cr, cr, cc, cc], axis=-1)
    stab = jnp.concatenate([-sr, sr, -sc, sc], axis=-1)
    ctab = jnp.concatenate([jnp.ones((l, HEAD_DIM), F32), ctab], axis=0)
    stab = jnp.concatenate([jnp.zeros((l, HEAD_DIM), F32), stab], axis=0)
    return ctab, stab


def kernel(x, c, ctx, c_ctx, w_ada, b_ada, norm_pre, norm_post, w_in, q_norm, k_norm, pool_w, pool_scale, w_out, loss_target, m_c_ctx, m_w_ada, m_b_ada, m_norm_pre, m_norm_post, m_w_in, m_q_norm, m_k_norm, m_pool_w, m_pool_scale, m_w_out, v_c_ctx, v_w_ada, v_b_ada, v_norm_pre, v_norm_post, v_w_in, v_q_norm, v_k_norm, v_pool_w, v_pool_scale, v_w_out):
    x2, ctx2, tgt = x[0], ctx[0], loss_target[0]
    n, d = x2.shape
    l = ctx2.shape[0]
    s_len = l + n
    aw = d // 2
    pw = d - aw
    n_heads = aw // HEAD_DIM
    kvw = (n_heads // GQA_GROUP) * HEAD_DIM
    pg = pw // N_POOL_GROUPS
    k0, v0, ga0 = aw, aw + kvw, aw + 2 * kvw
    up0, gp0 = ga0 + aw, ga0 + aw + pw
    in_w = gp0 + pw
    d3 = 3 * d
    ada_w = w_ada.shape[-1]
    px, py, pc = lax.axis_index("x"), lax.axis_index("y"), lax.axis_index("c")
    chip = 2 * px + py
    tr = min(256, l)
    tq = min(512, n)
    tk = min(256, l)
    ts = s_len // 8 if (s_len // 8) % 16 == 0 else tr
    scale = HEAD_DIM ** -0.5

    c_all = _allgather_small(c, chips_only=False, name="gather_c").reshape(8, d)
    craw = jnp.concatenate([c_all, c_ctx.reshape(1, d), jnp.zeros((7, d), F32)], axis=0)
    b_loc = lax.dynamic_slice(b_ada, (0, chip * ada_w), (1, ada_w))
    mod_part = _ada_fwd(craw, w_ada[0], b_loc, tn=min(512, ada_w))
    mod_all = _allgather_small(mod_part, chips_only=True, name="gather_mod")
    mod_all = jnp.transpose(mod_all, (1, 0, 2)).reshape(16, d3)
    me = 4 * px + 2 * py + pc
    mod_b = lax.dynamic_slice(mod_all, (me, 0), (1, d3))
    mod_c = mod_all[8:9]
    ss = jnp.concatenate([mod_b[:, d:2 * d], mod_b[:, 0:d], mod_c[:, d:2 * d], mod_c[:, 0:d]], axis=0)
    gate = mod_b[:, 2 * d:]

    (w_in_f,) = _run_side(_gather_side([_cast_into(w_in[0], 1, chip, name="cast_w_in")], [(1, 0)]), name="gather_w_in")
    later_weights = _gather_side([_cast_into(w_out[0], 0, chip, name="cast_w_out"),
                                  _cast_into(pool_w[0], 1, chip, name="cast_pool_w")], [(0, 1), (1, 2)])

    ctab, stab = _rope_tables(n, l)
    h = _prenorm(x2, ctx2, norm_pre, ss, tm=tr)
    proj, w_out_f, pool_f = _matmul(h, w_in_f, ca=1, cb=0, tm=ts, tn=min(512, in_w), tk=d, out_dtype=BF, name="in_proj",
                                    side=later_weights)
    qr = _qk_prep(proj, q_norm, ctab[l:], stab[l:], row0=l, col0=0, width=aw, nrows=n, tm=tr, name="q_prep")
    kr = _qk_prep(proj, k_norm, ctab, stab, row0=0, col0=k0, width=kvw, nrows=s_len, tm=tr, name="k_prep")
    v_all = proj[:, v0:v0 + kvw]
    tkf = min(512, n)
    vt_ctx = jnp.transpose(v_all[:l].reshape(1, l, kvw // HEAD_DIM, HEAD_DIM), (2, 0, 3, 1))
    vt_lat = jnp.transpose(v_all[l:].reshape(n // tkf, tkf, kvw // HEAD_DIM, HEAD_DIM), (2, 0, 3, 1))
    attn_o, lse = _flash_fwd(qr, kr, vt_ctx, vt_lat, tq=tq, scale=scale)
    dmat, po_raw = _pool_fwd(proj, pool_f, l=l, n=n, col0=up0, tm=tr)
    yv = _merge(attn_o, po_raw, proj, pool_scale, l=l, ga0=ga0, gp0=gp0, tm=tr)
    out = _matmul(yv, w_out_f, ca=1, cb=0, tm=min(512, n), tn=min(512, d), tk=d, out_dtype=F32, name="out_proj")

    dxn, d_out, post_st = _post(x2, out, tgt, gate, norm_post, tm=min(128, n))
    loss = 0.5 * jnp.sum(post_st[2]) / d
    loss = lax.psum(loss, ("x", "y", "c"))
    g_wout = _matmul(yv, d_out, ca=0, cb=0, tm=min(512, d), tn=min(512, d), tk=n, out_dtype=BF, name="wgrad_out")
    dy, from_sib_wout = _matmul(d_out, w_out_f, ca=1, cb=1, tm=min(512, n), tn=min(512, d), tk=d, out_dtype=BF,
                                name="dgrad_out", side=_pair_side([g_wout], [1]))
    pair_wout = _add_own_half(g_wout, from_sib_wout, 1, pc, name="pair_sum_1")
    dval, dgate, dps = _merge_bwd(dy, attn_o, po_raw, proj, pool_scale, l=l, ga0=ga0, gp0=gp0, tm=tr)
    g_pool = _pool_wgrad(dmat, dval, col0=aw, tk=min(512, n))
    dds = _pool_bwd_map(dval, pool_f, n=n, col0=aw, tm=tr)
    du = _pool_bwd_window(dds, n=n, tm=tr)
    do = dval[:, :aw]
    delta = _attn_delta(attn_o, do, tq=tq)
    dq, dk, dv, *from_chips_wout = _flash_bwd(qr, do, kr, v_all, lse, delta, tq=tq, tk=tk, scale=scale,
                                              side=_chips_side([pair_wout], [0]))
    red_wout = _sum_own_block(pair_wout, from_chips_wout, 0, 1, chip, pc, name="chip_sum_1")
    dq_raw, dgq = _qk_bwd(dq, proj, q_norm, ctab[l:], stab[l:], row0=l, col0=0, tm=tr, name="q_bwd")
    dk_raw, dgk = _qk_bwd(dk, proj, k_norm, ctab, stab, row0=0, col0=k0, tm=tr, name="k_bwd")
    dvb = dv.astype(BF)
    dproj = jnp.concatenate([
        jnp.concatenate([jnp.zeros((l, aw), BF), dq_raw], axis=0), dk_raw, dvb,
        jnp.concatenate([jnp.zeros((l, aw + 2 * pw), BF),
                         jnp.concatenate([dgate[:, :aw], du, dgate[:, aw:]], axis=1)], axis=0)], axis=1)
    g_win = _matmul(h, dproj, ca=0, cb=0, tm=min(512, d), tn=min(512, in_w), tk=s_len, out_dtype=BF, name="wgrad_in")
    from_sib = _run_side(_pair_side([g_win, g_pool], [0, 2]), name="reduce_pair")
    pair_win = _add_own_half(g_win, from_sib[0], 0, pc, name="pair_sum_0")
    pair_pool = _add_own_half(g_pool, from_sib[1], 2, pc, name="pair_sum_2")
    dh, *from_chips = _matmul(dproj, w_in_f, ca=1, cb=1, tm=ts, tn=min(512, d), tk=in_w // 2, out_dtype=BF, name="dgrad_in",
                              side=_chips_side([pair_win, pair_pool], [1, 1]))
    red_win = _sum_own_block(pair_win, from_chips[0:3], 1, 0, chip, pc, name="chip_sum_0")
    red_pool = _sum_own_block(pair_pool, from_chips[3:6], 1, 2, chip, pc, name="chip_sum_2")
    grad_x, pre_st = _prenorm_bwd(x2, ctx2, dh, dxn, norm_pre, ss, tm=min(128, l))

    zero_d = jnp.zeros((1, d), F32)
    packed = jnp.concatenate([pre_st[0:1], pre_st[1:2], post_st[0:1], pre_st[2:3], pre_st[3:4], zero_d,
                              pre_st[4:5], post_st[1:2], dgq[0:1], dgk[0:1], dps[0:1]], axis=1)
    gath = _allgather_small(packed, chips_only=False, name="gather_small")
    dm, sums = _reduce_small(gath, d3)
    o1 = 2 * d3
    g_npre, g_npost = sums[:, o1:o1 + d], sums[:, o1 + d:o1 + 2 * d]
    g_q, g_k = sums[:, o1 + 2 * d:o1 + 2 * d + HEAD_DIM], sums[:, o1 + 2 * d + HEAD_DIM:o1 + 2 * d + 2 * HEAD_DIM]
    g_ps = sums[:, o1 + 2 * d + 2 * HEAD_DIM:]
    g_bada = sums[:, 0:d3] + sums[:, d3:2 * d3]
    dm_loc = lax.dynamic_slice(dm, (0, chip * ada_w), (16, ada_w))
    g_wada, dl_wada, nm_wada, nv_wada, dact = _ada_bwd(craw.T, dm_loc, w_ada[0], m_w_ada[0], v_w_ada[0], tn=128)
    cparts = _allgather_small(dact[8:9], chips_only=True, name="gather_cctx")
    g_cctx = _cctx_grad(cparts, c_ctx.reshape(1, d)).reshape(d)

    g_win_s, g_wout_s, g_pool_s = _run_side(_join_side([red_win, red_wout, red_pool], [0, 1, 2]), name="reduce_join")

    def upd(w, g, m, v, name):
        return _adamw(w, g.reshape(w.shape), m, v, name=name)

    grads = {"c_ctx": g_cctx, "b_ada": g_bada, "norm_pre": g_npre, "norm_post": g_npost, "w_in": g_win_s[None],
             "q_norm": g_q, "k_norm": g_k, "pool_w": g_pool_s[None], "pool_scale": g_ps, "w_out": g_wout_s[None]}
    res = {"w_ada": (g_wada[None], dl_wada[None], nm_wada[None], nv_wada[None])}
    given = {"c_ctx": (c_ctx, m_c_ctx, v_c_ctx), "b_ada": (b_ada, m_b_ada, v_b_ada),
             "norm_pre": (norm_pre, m_norm_pre, v_norm_pre), "norm_post": (norm_post, m_norm_post, v_norm_post),
             "w_in": (w_in, m_w_in, v_w_in), "q_norm": (q_norm, m_q_norm, v_q_norm), "k_norm": (k_norm, m_k_norm, v_k_norm),
             "pool_w": (pool_w, m_pool_w, v_pool_w), "pool_scale": (pool_scale, m_pool_scale, v_pool_scale),
             "w_out": (w_out, m_w_out, v_w_out)}
    for nme, (w, m, v) in given.items():
        g = grads[nme].reshape(w.shape)
        res[nme] = (g,) + upd(w, g, m, v, "adamw_" + nme)
    order = ["c_ctx", "w_ada", "b_ada", "norm_pre", "norm_post", "w_in", "q_norm", "k_norm", "pool_w", "pool_scale", "w_out"]
    return (loss, grad_x[None], *[res[k][0] for k in order], *[res[k][1] for k in order],
            *[res[k][2] for k in order], *[res[k][3] for k in order])
```

```python
import functools

import jax
import jax.numpy as jnp
from jax import lax
from jax.experimental import pallas as pl
from jax.experimental.pallas import tpu as pltpu

F32 = jnp.float32
BF = jnp.bfloat16
MESH = pl.DeviceIdType.MESH

HEAD_DIM = 128
GQA_GROUP = 4
GRID_W = 64
ROPE_PAIRS = HEAD_DIM // 4
ROPE_THETA = 10000.0
EPS = 1e-6
N_POOL_GROUPS = 4
POOL_HALO = 128
ADAM_LR = 0.001
ADAM_B1 = 0.9
ADAM_B2 = 0.999
ADAM_EPS = 1e-08
ADAM_WD = 0.01
ADAM_STEP = 10
LOG2E = 1.4426950408889634
MIB = 2 ** 20
VMEM_LIMIT = 48 * MIB
CHIP_MASKS = ((1, 0, 0), (0, 1, 0), (1, 1, 0))
ALL_MASKS = ((0, 0, 1), (0, 1, 0), (0, 1, 1), (1, 0, 0), (1, 0, 1), (1, 1, 0), (1, 1, 1))
HBM_SPEC = pl.BlockSpec(memory_space=pl.ANY)
VMEM_SPEC = pl.BlockSpec(memory_space=pltpu.VMEM)


def _cparams(*sem):
    return pltpu.CompilerParams(dimension_semantics=sem, vmem_limit_bytes=VMEM_LIMIT)


def _sigmoid(v):
    return 1.0 / (1.0 + jnp.exp(-v))


def _silu_and_grad(v):
    s = _sigmoid(v)
    return v * s, s * (1.0 + v * (1.0 - s))


def _dot(a, b, ca, cb):
    return lax.dot_general(a, b, (((ca,), (cb,)), ((), ())), preferred_element_type=F32)


def _block_rows(rows, width, itemsize=4, target=MIB):
    best = 8
    for t in range(8, rows + 1, 8):
        if rows % t == 0 and t * width * itemsize <= target:
            best = t
    return best if rows % 8 == 0 else rows


def _my_pos():
    return lax.axis_index("x"), lax.axis_index("y"), lax.axis_index("c")


def _flip(pos, mask):
    return tuple(jnp.bitwise_xor(p, m) if m else p for p, m in zip(pos, mask))


def _allgather_small(v, *, chips_only, name):
    r, w = v.shape
    masks = CHIP_MASKS if chips_only else ALL_MASKS
    nslot = 4 if chips_only else 8

    def slot(pos):
        return 2 * pos[0] + pos[1] if chips_only else 4 * pos[0] + 2 * pos[1] + pos[2]

    def body(v_ref, o_ref, send_sems, recv_sems, local_sem):
        me = _my_pos()

        def copy(k, block_of, to):
            return pltpu.make_async_remote_copy(
                src_ref=v_ref, dst_ref=o_ref.at[slot(block_of)], send_sem=send_sems.at[k], recv_sem=recv_sems.at[k],
                device_id=to, device_id_type=MESH)

        mine = pltpu.make_async_copy(v_ref, o_ref.at[slot(me)], local_sem)
        mine.start()
        sends = [copy(k, me, _flip(me, m)) for k, m in enumerate(masks)]
        for cp in sends:
            cp.start()
        for k, m in enumerate(masks):
            copy(k, _flip(me, m), me).wait_recv()
        for cp in sends:
            cp.wait_send()
        mine.wait()

    return pl.pallas_call(
        body, name=name, out_shape=jax.ShapeDtypeStruct((nslot, r, w), v.dtype),
        in_specs=[VMEM_SPEC], out_specs=VMEM_SPEC,
        scratch_shapes=[pltpu.SemaphoreType.DMA((len(masks),)), pltpu.SemaphoreType.DMA((len(masks),)),
                        pltpu.SemaphoreType.DMA],
    )(v)


def _sl(ref, axis, start, size):
    idx = [slice(None)] * len(ref.shape)
    idx[axis] = pl.ds(start, size)
    return ref.at[tuple(idx)]


class _Side:
    def __init__(self, ins, out_shapes, aliases, sem_shapes, start, finish, mid=None):
        self.ins, self.out_shapes, self.aliases, self.sem_shapes = list(ins), list(out_shapes), dict(aliases), list(sem_shapes)
        self.start, self.finish, self.mid = start, finish, mid


def _run_side(side, *, name):
    n_in, n_out = len(side.ins), len(side.out_shapes)

    def body(*refs):
        parts = refs[:n_in], refs[n_in:n_in + n_out], refs[n_in + n_out:]
        side.start(*parts)
        if side.mid is not None:
            side.mid(*parts)
        side.finish(*parts)

    return pl.pallas_call(
        body, name=name, out_shape=tuple(side.out_shapes), in_specs=[HBM_SPEC] * n_in, out_specs=tuple([HBM_SPEC] * n_out),
        input_output_aliases=side.aliases, scratch_shapes=side.sem_shapes,
    )(*side.ins)


def _side_at(grid):
    pids = [pl.program_id(a) for a in range(len(grid))]

    def at(values):
        cond = pids[0] == values[0]
        for p, v in zip(pids[1:], values[1:]):
            cond = jnp.logical_and(cond, p == v)
        return cond

    zeros = [0] * len(grid)
    return at(zeros), at([grid[0] // 2] + zeros[1:]), at([g - 1 for g in grid])


def _gather_side(shards, layouts):
    nm = len(shards)

    def copies(outs, sems):
        send_sems, recv_sems, fsend_sems, frecv_sems = sems
        x, y, c = _my_pos()
        me = (x, y, c)
        sib = (x, y, 1 - c)
        sends, landed, fwds, from_sib = [], [], [], []
        for m in range(nm):
            a_s, a_h = layouts[m]
            ns, nh = outs[m].shape[a_s] // 4, outs[m].shape[a_h] // 2

            def region(chip_pos, half, m=m, a_s=a_s, a_h=a_h, ns=ns, nh=nh):
                j = 2 * chip_pos[0] + chip_pos[1]
                return _sl(_sl(outs[m], a_s, j * ns, ns), a_h, half * nh, nh)

            for k, mask in enumerate(CHIP_MASKS):
                other = _flip(me, mask)

                def rc(ref, ssem, rsem, to):
                    return pltpu.make_async_remote_copy(src_ref=ref, dst_ref=ref, send_sem=ssem, recv_sem=rsem,
                                                        device_id=to, device_id_type=MESH)

                sends.append(rc(region(me, c), send_sems.at[m, k], recv_sems.at[m, k], other))
                landed.append(rc(region(other, c), send_sems.at[m, k], recv_sems.at[m, k], me))
                fwds.append(rc(region(other, c), fsend_sems.at[m, k], frecv_sems.at[m, k], sib))
                from_sib.append(rc(region(other, 1 - c), fsend_sems.at[m, k], frecv_sems.at[m, k], sib))
        return sends, landed, fwds, from_sib

    def start(ins, outs, sems):
        for cp in copies(outs, sems)[0]:
            cp.start()

    def mid(ins, outs, sems):
        _, landed, fwds, _ = copies(outs, sems)
        for arrived, fw in zip(landed, fwds):
            arrived.wait_recv()
            fw.start()

    def finish(ins, outs, sems):
        sends, _, fwds, from_sib = copies(outs, sems)
        for cp in from_sib:
            cp.wait_recv()
        for cp in sends + fwds:
            cp.wait_send()

    return _Side(shards, [jax.ShapeDtypeStruct(s.shape, s.dtype) for s in shards], {i: i for i in range(nm)},
                 [pltpu.SemaphoreType.DMA((nm, 3))] * 4, start, finish, mid)


def _gather_parts_side(w_own, mask_ids):
    nt = len(mask_ids)
    nh = w_own.shape[0] // 2

    def copies(ins, outs, sems):
        send_sems, recv_sems, fsend_sems, frecv_sems = sems
        x, y, c = _my_pos()
        me, sib = (x, y, c), (x, y, 1 - c)
        sends, landed, fwds, from_sib = [], [], [], []
        for t, k in enumerate(mask_ids):
            mine = _sl(outs[t], 0, c * nh, nh)
            theirs = _sl(outs[t], 0, (1 - c) * nh, nh)

            def rc(src, dst, ssem, rsem, to):
                return pltpu.make_async_remote_copy(src_ref=src, dst_ref=dst, send_sem=ssem, recv_sem=rsem,
                                                    device_id=to, device_id_type=MESH)

            sends.append(rc(_sl(ins[0], 0, c * nh, nh), mine, send_sems.at[t], recv_sems.at[t], _flip(me, CHIP_MASKS[k])))
            landed.append(rc(mine, mine, send_sems.at[t], recv_sems.at[t], me))
            fwds.append(rc(mine, mine, fsend_sems.at[t], frecv_sems.at[t], sib))
            from_sib.append(rc(theirs, theirs, fsend_sems.at[t], frecv_sems.at[t], sib))
        return sends, landed, fwds, from_sib

    def start(ins, outs, sems):
        for cp in copies(ins, outs, sems)[0]:
            cp.start()

    def finish(ins, outs, sems):
        sends, landed, fwds, from_sib = copies(ins, outs, sems)
        for arrived, fw in zip(landed, fwds):
            arrived.wait_recv()
            fw.start()
        for cp in from_sib:
            cp.wait_recv()
        for cp in sends + fwds:
            cp.wait_send()

    return _Side([w_own], [jax.ShapeDtypeStruct(w_own.shape, w_own.dtype)] * nt, {},
                 [pltpu.SemaphoreType.DMA((nt,))] * 4, start, finish)


def _pair_side(mats, half_axes):
    nm = len(mats)

    def copies(ins, outs, sems):
        x, y, c = _my_pos()
        cps = []
        for m in range(nm):
            nh = ins[m].shape[half_axes[m]] // 2
            cps.append(pltpu.make_async_remote_copy(
                src_ref=_sl(ins[m], half_axes[m], (1 - c) * nh, nh), dst_ref=outs[m],
                send_sem=sems[0].at[m], recv_sem=sems[1].at[m], device_id=(x, y, 1 - c), device_id_type=MESH))
        return cps

    def start(ins, outs, sems):
        for cp in copies(ins, outs, sems):
            cp.start()

    def finish(ins, outs, sems):
        for cp in copies(ins, outs, sems):
            cp.wait()

    out_shapes = []
    for s, a_h in zip(mats, half_axes):
        shp = list(s.shape)
        shp[a_h] //= 2
        out_shapes.append(jax.ShapeDtypeStruct(tuple(shp), s.dtype))
    return _Side(mats, out_shapes, {}, [pltpu.SemaphoreType.DMA((nm,))] * 2, start, finish)


def _chips_side(mats, shard_axes):
    nm = len(mats)

    def copies(ins, outs, sems):
        me = _my_pos()
        cps = []
        for m in range(nm):
            ns = ins[m].shape[shard_axes[m]] // 4
            for k, mask in enumerate(CHIP_MASKS):
                other = _flip(me, mask)
                cps.append(pltpu.make_async_remote_copy(
                    src_ref=_sl(ins[m], shard_axes[m], (2 * other[0] + other[1]) * ns, ns), dst_ref=outs[3 * m + k],
                    send_sem=sems[0].at[m, k], recv_sem=sems[1].at[m, k], device_id=other, device_id_type=MESH))
        return cps

    def start(ins, outs, sems):
        for cp in copies(ins, outs, sems):
            cp.start()

    def finish(ins, outs, sems):
        for cp in copies(ins, outs, sems):
            cp.wait()

    out_shapes = []
    for s, a_s in zip(mats, shard_axes):
        shp = list(s.shape)
        shp[a_s] //= 4
        out_shapes += [jax.ShapeDtypeStruct(tuple(shp), s.dtype)] * 3
    return _Side(mats, out_shapes, {}, [pltpu.SemaphoreType.DMA((nm, 3))] * 2, start, finish)


def _join_side(halves, half_axes):
    nm = len(halves)

    def copies(outs, sems):
        x, y, c = _my_pos()
        mine, theirs = [], []
        for m in range(nm):
            nh = outs[m].shape[half_axes[m]] // 2
            for half, lst in ((c, mine), (1 - c, theirs)):
                ref = _sl(outs[m], half_axes[m], half * nh, nh)
                lst.append(pltpu.make_async_remote_copy(
                    src_ref=ref, dst_ref=ref, send_sem=sems[0].at[m], recv_sem=sems[1].at[m],
                    device_id=(x, y, 1 - c), device_id_type=MESH))
        return mine, theirs

    def start(ins, outs, sems):
        for cp in copies(outs, sems)[0]:
            cp.start()

    def finish(ins, outs, sems):
        mine, theirs = copies(outs, sems)
        for cp in mine:
            cp.wait_send()
        for cp in theirs:
            cp.wait_recv()

    return _Side(halves, [jax.ShapeDtypeStruct(s.shape, s.dtype) for s in halves], {i: i for i in range(nm)},
                 [pltpu.SemaphoreType.DMA((nm,))] * 2, start, finish)


def _part_specs(shp, axis, slot, itemsize, target=MIB):
    if len(shp) == 2:
        rows, width = shp
        tm = _block_rows(rows, width, itemsize, target)
        nb = rows // tm
        own = pl.BlockSpec((tm, width), lambda i, p: (i, 0))
        if axis == 1:
            part = pl.BlockSpec((tm, width), lambda i, p: (i, p[slot]))
        else:
            part = pl.BlockSpec((tm, width), lambda i, p: (p[slot] * nb + i, 0))
        return (nb,), own, part
    assert len(shp) == 3 and axis in (1, 2)
    g, rows, width = shp
    own = pl.BlockSpec((1, rows, width), lambda i, p: (i, 0, 0))
    if axis == 1:
        part = pl.BlockSpec((1, rows, width), lambda i, p: (i, p[slot], 0))
    else:
        part = pl.BlockSpec((1, rows, width), lambda i, p: (i, 0, p[slot]))
    return (g,), own, part


def _cast_into(w, axis, p, *, name):
    grid, own, part = _part_specs(w.shape, axis, 0, 4, 2 * MIB)
    big = list(w.shape)
    big[axis] *= 4

    def body(p_ref, w_ref, o_ref):
        o_ref[...] = w_ref[...].astype(BF)

    return pl.pallas_call(
        body, name=name, out_shape=jax.ShapeDtypeStruct(tuple(big), BF),
        grid_spec=pltpu.PrefetchScalarGridSpec(num_scalar_prefetch=1, grid=grid, in_specs=[own], out_specs=part),
        compiler_params=_cparams("parallel"),
    )(p.reshape(1), w)


def _cast_bf16(w, *, name):
    rows, width = w.shape
    tm = _block_rows(rows, width, 4, 2 * MIB)

    def body(w_ref, o_ref):
        o_ref[...] = w_ref[...].astype(BF)

    spec = pl.BlockSpec((tm, width), lambda i: (i, 0))
    return pl.pallas_call(body, name=name, grid=(rows // tm,), out_shape=jax.ShapeDtypeStruct(w.shape, BF),
                          in_specs=[spec], out_specs=spec, compiler_params=_cparams("parallel"))(w)


def _add_own_half(full, recv, half_axis, c, *, name):
    grid, own, part = _part_specs(recv.shape, half_axis, 0, 2, 4 * MIB)

    def body(c_ref, f_ref, r_ref, o_ref):
        o_ref[...] = (f_ref[...].astype(F32) + r_ref[...].astype(F32)).astype(BF)

    return pl.pallas_call(
        body, name=name, out_shape=jax.ShapeDtypeStruct(recv.shape, BF),
        grid_spec=pltpu.PrefetchScalarGridSpec(num_scalar_prefetch=1, grid=grid, in_specs=[part, own], out_specs=own),
        compiler_params=_cparams("parallel"),
    )(c.reshape(1), full, recv)


def _sum_own_block(mine_all, recvs, shard_axis, half_axis, j, c, *, name):
    shp = recvs[0].shape
    grid, own, mine = _part_specs(shp, shard_axis, 0, 4, 2 * MIB)
    _, _, place = _part_specs(shp, half_axis, 1, 4, 2 * MIB)
    big = list(shp)
    big[half_axis] *= 2

    def body(p_ref, a_ref, r0, r1, r2, o_ref):
        o_ref[...] = ((a_ref[...].astype(F32) + r0[...].astype(F32)) + r1[...].astype(F32)) + r2[...].astype(F32)

    return pl.pallas_call(
        body, name=name, out_shape=jax.ShapeDtypeStruct(tuple(big), F32),
        grid_spec=pltpu.PrefetchScalarGridSpec(
            num_scalar_prefetch=1, grid=grid, in_specs=[mine, own, own, own], out_specs=place),
        compiler_params=_cparams("parallel"),
    )(jnp.stack([j, c]), mine_all, *recvs)


def _adamw(w, g, m, v, *, name):
    shp = w.shape
    width = shp[-1] if len(shp) > 1 else shp[0]
    rows = 1
    for d in shp[:-1]:
        rows *= d
    if len(shp) == 1:
        rows = 1
    args = [a.reshape(rows, width) for a in (w, g, m, v)]
    tm = _block_rows(rows, width, 4, 2 * MIB)

    def body(w_ref, g_ref, m_ref, v_ref, d_ref, nm_ref, nv_ref):
        gv = g_ref[...]
        nm = ADAM_B1 * m_ref[...] + (1.0 - ADAM_B1) * gv
        nv = ADAM_B2 * v_ref[...] + (1.0 - ADAM_B2) * (gv * gv)
        m_hat = nm / (1.0 - ADAM_B1 ** ADAM_STEP)
        v_hat = nv / (1.0 - ADAM_B2 ** ADAM_STEP)
        d_ref[...] = -ADAM_LR * (m_hat / (jnp.sqrt(v_hat) + ADAM_EPS) + ADAM_WD * w_ref[...])
        nm_ref[...] = nm
        nv_ref[...] = nv

    spec = pl.BlockSpec((tm, width), lambda i: (i, 0))
    outs = pl.pallas_call(
        body, name=name, grid=(rows // tm,), out_shape=tuple([jax.ShapeDtypeStruct((rows, width), F32)] * 3),
        in_specs=[spec] * 4, out_specs=tuple([spec] * 3), compiler_params=_cparams("parallel"),
    )(*args)
    return tuple(o.reshape(shp) for o in outs)


def _matmul(a, b, *, ca, cb, tm, tn, tk, out_dtype, name, b_resident=False, side=None):
    m, kdim = a.shape[1 - ca], a.shape[ca]
    n = b.shape[1 - cb]
    nk = kdim // tk
    assert m % tm == 0 and n % tn == 0 and kdim % tk == 0
    n_sin = len(side.ins) if side else 0
    n_sout = len(side.out_shapes) if side else 0
    if b_resident:
        gi = lambda p, q, k: (q, p, k)
        grid = (n // tn, m // tm, nk)
    else:
        gi = lambda p, q, k: (p, q, k)
        grid = (m // tm, n // tn, nk)

    def body(*refs):
        a_ref, b_ref = refs[0], refs[1]
        s_ins = refs[2:2 + n_sin]
        o_ref = refs[2 + n_sin]
        s_outs = refs[3 + n_sin:3 + n_sin + n_sout]
        rest = refs[3 + n_sin + n_sout:]
        acc, sems = (rest[:1], rest[1:]) if nk > 1 else ((), rest)
        if side:
            first, middle, last = _side_at(grid)
            pl.when(first)(lambda: side.start(s_ins, s_outs, sems))
            if side.mid is not None:
                pl.when(middle)(lambda: side.mid(s_ins, s_outs, sems))
        part = _dot(a_ref[...], b_ref[...], ca, cb)
        if nk == 1:
            o_ref[...] = part.astype(o_ref.dtype)
        else:
            k = pl.program_id(2)

            @pl.when(k == 0)
            def _():
                acc[0][...] = part

            @pl.when(jnp.logical_and(k > 0, k < nk - 1))
            def _():
                acc[0][...] += part

            @pl.when(k == nk - 1)
            def _():
                o_ref[...] = (acc[0][...] + part).astype(o_ref.dtype)
        if side:
            pl.when(last)(lambda: side.finish(s_ins, s_outs, sems))

    def a_map(p, q, k):
        i, _, kk = gi(p, q, k)
        return (i, kk) if ca == 1 else (kk, i)

    def b_map(p, q, k):
        _, j, kk = gi(p, q, k)
        return (kk, j) if cb == 0 else (j, kk)

    def o_map(p, q, k):
        i, j, _ = gi(p, q, k)
        return (i, j)

    a_spec = pl.BlockSpec((tm, tk) if ca == 1 else (tk, tm), a_map)
    b_spec = pl.BlockSpec((tk, tn) if cb == 0 else (tn, tk), b_map)
    o_shape = jax.ShapeDtypeStruct((m, n), out_dtype)
    acc_shapes = [pltpu.VMEM((tm, tn), F32)] if nk > 1 else []
    if not side:
        return pl.pallas_call(
            body, name=name, grid=grid, out_shape=o_shape,
            in_specs=[a_spec, b_spec], out_specs=pl.BlockSpec((tm, tn), o_map), scratch_shapes=acc_shapes,
            compiler_params=_cparams("parallel", "parallel", "arbitrary"),
        )(a, b)
    return pl.pallas_call(
        body, name=name, grid=grid, out_shape=(o_shape, *side.out_shapes),
        in_specs=[a_spec, b_spec] + [HBM_SPEC] * n_sin,
        out_specs=(pl.BlockSpec((tm, tn), o_map), *([HBM_SPEC] * n_sout)),
        input_output_aliases={2 + i: 1 + o for i, o in side.aliases.items()},
        scratch_shapes=acc_shapes + side.sem_shapes,
        compiler_params=_cparams("arbitrary", "arbitrary", "arbitrary"),
    )(a, b, *side.ins)


def _in_proj_parts(h, w_parts, ids, prev, *, width, tm, tn, name, side=None):
    s_len, d = h.shape
    nb = w_parts[0].shape[1] // tn
    n_p = len(w_parts)
    n_prev = 0 if prev is None else 1
    n_sin = len(side.ins) if side else 0
    n_sout = len(side.out_shapes) if side else 0
    grid = (n_p, s_len // tm, nb)
    base = 1 + n_p + n_prev

    def body(ids_ref, *refs):
        h_ref, w_refs = refs[0], refs[1:1 + n_p]
        s_ins = refs[base:base + n_sin]
        o_ref = refs[base + n_sin]
        s_outs = refs[base + n_sin + 1:base + n_sin + 1 + n_sout]
        sems = refs[base + n_sin + 1 + n_sout:]
        if side:
            first, _, last = _side_at(grid)
            pl.when(first)(lambda: side.start(s_ins, s_outs, sems))
        for s in range(n_p):
            @pl.when(pl.program_id(0) == s)
            def _(s=s):
                o_ref[...] = _dot(h_ref[...], w_refs[s][...], 1, 0).astype(o_ref.dtype)
        if side:
            pl.when(last)(lambda: side.finish(s_ins, s_outs, sems))

    in_specs = [pl.BlockSpec((tm, d), lambda p, i, j, ids_ref: (i, 0))]
    for s in range(n_p):
        in_specs.append(pl.BlockSpec((d, tn), lambda p, i, j, ids_ref, s=s: (0, jnp.where(p == s, j, 0))))
    in_specs += [HBM_SPEC] * (n_prev + n_sin)
    o_spec = pl.BlockSpec((tm, tn), lambda p, i, j, ids_ref: (i, ids_ref[p] * nb + j))
    aliases = {2 + n_p: 0} if prev is not None else {}
    if side:
        aliases.update({1 + base + i: 1 + o for i, o in side.aliases.items()})
    outs = pl.pallas_call(
        body, name=name, out_shape=(jax.ShapeDtypeStruct((s_len, width), BF), *(side.out_shapes if side else [])),
        grid_spec=pltpu.PrefetchScalarGridSpec(
            num_scalar_prefetch=1, grid=grid, in_specs=in_specs, out_specs=(o_spec, *([HBM_SPEC] * n_sout)),
            scratch_shapes=side.sem_shapes if side else []),
        input_output_aliases=aliases, compiler_params=_cparams("arbitrary", "arbitrary", "arbitrary"),
    )(ids, h, *w_parts, *([prev] if prev is not None else []), *(side.ins if side else []))
    return outs


def _dgrad_in_parts(dproj, w_parts, ids, *, tm, tn, name, side=None):
    s_len = dproj.shape[0]
    d, ws = w_parts[0].shape
    n_p = len(w_parts)
    n_sin = len(side.ins) if side else 0
    n_sout = len(side.out_shapes) if side else 0
    grid = (s_len // tm, d // tn)

    def body(ids_ref, *refs):
        a_refs, w_refs = refs[:n_p], refs[n_p:2 * n_p]
        s_ins = refs[2 * n_p:2 * n_p + n_sin]
        o_ref = refs[2 * n_p + n_sin]
        s_outs = refs[2 * n_p + n_sin + 1:2 * n_p + n_sin + 1 + n_sout]
        sems = refs[2 * n_p + n_sin + 1 + n_sout:]
        if side:
            first, _, last = _side_at(grid)
            pl.when(first)(lambda: side.start(s_ins, s_outs, sems))
        tot = _dot(a_refs[0][...], w_refs[0][...], 1, 1)
        for s in range(1, n_p):
            tot = tot + _dot(a_refs[s][...], w_refs[s][...], 1, 1)
        o_ref[...] = tot.astype(o_ref.dtype)
        if side:
            pl.when(last)(lambda: side.finish(s_ins, s_outs, sems))

    in_specs = [pl.BlockSpec((tm, ws), lambda i, j, ids_ref, s=s: (i, ids_ref[s])) for s in range(n_p)]
    in_specs += [pl.BlockSpec((tn, ws), lambda i, j, ids_ref: (j, 0))] * n_p
    in_specs += [HBM_SPEC] * n_sin
    aliases = {1 + 2 * n_p + i: 1 + o for i, o in side.aliases.items()} if side else {}
    return pl.pallas_call(
        body, name=name, out_shape=(jax.ShapeDtypeStruct((s_len, d), BF), *(side.out_shapes if side else [])),
        grid_spec=pltpu.PrefetchScalarGridSpec(
            num_scalar_prefetch=1, grid=grid, in_specs=in_specs,
            out_specs=(pl.BlockSpec((tm, tn), lambda i, j, ids_ref: (i, j)), *([HBM_SPEC] * n_sout)),
            scratch_shapes=side.sem_shapes if side else []),
        input_output_aliases=aliases, compiler_params=_cparams("arbitrary", "arbitrary"),
    )(ids, *([dproj] * n_p), *w_parts, *(side.ins if side else []))


def _prenorm(x, ctx, g_pre, ss, *, tm):
    n, d = x.shape
    l = ctx.shape[0]
    nc = l // tm

    def body(x_ref, c_ref, g_ref, ss_ref, h_ref):
        t = pl.program_id(0)

        def go(src, scale, shift):
            v = src[...]
            rstd = lax.rsqrt(jnp.mean(v * v, axis=-1, keepdims=True) + EPS)
            h_ref[...] = ((v * rstd * g_ref[...]) * (1.0 + scale) + shift).astype(BF)

        @pl.when(t < nc)
        def _():
            go(c_ref, ss_ref[2:3, :], ss_ref[3:4, :])

        @pl.when(t >= nc)
        def _():
            go(x_ref, ss_ref[0:1, :], ss_ref[1:2, :])

    return pl.pallas_call(
        body, name="prenorm", grid=((l + n) // tm,), out_shape=jax.ShapeDtypeStruct((l + n, d), BF),
        in_specs=[pl.BlockSpec((tm, d), lambda t: (jnp.maximum(t - nc, 0), 0)),
                  pl.BlockSpec((tm, d), lambda t: (jnp.minimum(t, nc - 1), 0)),
                  pl.BlockSpec((1, d), lambda t: (0, 0)), pl.BlockSpec((4, d), lambda t: (0, 0))],
        out_specs=pl.BlockSpec((tm, d), lambda t: (t, 0)), compiler_params=_cparams("parallel"),
    )(x, ctx, g_pre, ss)


def _swap32(v):
    lane = lax.broadcasted_iota(jnp.int32, v.shape, 1)
    return jnp.where((lane % 64) < 32, pltpu.roll(v, 96, 1), pltpu.roll(v, 32, 1))


def _qk_prep(proj, gain, ctab, stab, *, row0, col0, width, nrows, tm, name):
    cw = min(512, width)
    rb0, cb0 = row0 // tm, col0 // cw
    assert row0 % tm == 0 and col0 % cw == 0 and width % cw == 0 and nrows % tm == 0

    def body(p_ref, g_ref, c_ref, s_ref, o_ref):
        for hd in range(cw // HEAD_DIM):
            cols = slice(hd * HEAD_DIM, (hd + 1) * HEAD_DIM)
            v = p_ref[:, cols].astype(F32)
            rstd = lax.rsqrt(jnp.mean(v * v, axis=-1, keepdims=True) + EPS)
            yv = v * rstd * g_ref[...]
            o_ref[:, cols] = (yv * c_ref[...] + _swap32(yv) * s_ref[...]).astype(BF)

    return pl.pallas_call(
        body, name=name, grid=(nrows // tm, width // cw), out_shape=jax.ShapeDtypeStruct((nrows, width), BF),
        in_specs=[pl.BlockSpec((tm, cw), lambda i, j: (i + rb0, j + cb0)), pl.BlockSpec((1, HEAD_DIM), lambda i, j: (0, 0)),
                  pl.BlockSpec((tm, HEAD_DIM), lambda i, j: (i, 0)), pl.BlockSpec((tm, HEAD_DIM), lambda i, j: (i, 0))],
        out_specs=pl.BlockSpec((tm, cw), lambda i, j: (i, j)), compiler_params=_cparams("parallel", "parallel"),
    )(proj, gain, ctab, stab)


def _qk_bwd(dy, proj, gain, ctab, stab, *, row0, col0, tm, name):
    nrows, width = dy.shape
    cw = min(512, width)
    rb0, cb0 = row0 // tm, col0 // cw

    def body(d_ref, p_ref, g_ref, c_ref, s_ref, o_ref, dg_ref):
        @pl.when(jnp.logical_and(pl.program_id(0) == 0, pl.program_id(1) == 0))
        def _():
            dg_ref[...] = jnp.zeros_like(dg_ref)

        dg = jnp.zeros((1, HEAD_DIM), F32)
        for hd in range(cw // HEAD_DIM):
            cols = slice(hd * HEAD_DIM, (hd + 1) * HEAD_DIM)
            v = p_ref[:, cols].astype(F32)
            rstd = lax.rsqrt(jnp.mean(v * v, axis=-1, keepdims=True) + EPS)
            nv = v * rstd
            d = d_ref[:, cols]
            dyu = d * c_ref[...] + _swap32(d * s_ref[...])
            dg = dg + jnp.sum(dyu * nv, axis=0, keepdims=True)
            dn = dyu * g_ref[...]
            o_ref[:, cols] = (rstd * (dn - nv * jnp.mean(dn * nv, axis=-1, keepdims=True))).astype(BF)
        dg_ref[0:1, :] += dg

    return pl.pallas_call(
        body, name=name, grid=(nrows // tm, width // cw),
        out_shape=(jax.ShapeDtypeStruct((nrows, width), BF), jax.ShapeDtypeStruct((8, HEAD_DIM), F32)),
        in_specs=[pl.BlockSpec((tm, cw), lambda i, j: (i, j)), pl.BlockSpec((tm, cw), lambda i, j: (i + rb0, j + cb0)),
                  pl.BlockSpec((1, HEAD_DIM), lambda i, j: (0, 0)),
                  pl.BlockSpec((tm, HEAD_DIM), lambda i, j: (i, 0)), pl.BlockSpec((tm, HEAD_DIM), lambda i, j: (i, 0))],
        out_specs=(pl.BlockSpec((tm, cw), lambda i, j: (i, j)), pl.BlockSpec((8, HEAD_DIM), lambda i, j: (0, 0))),
        compiler_params=_cparams("arbitrary", "arbitrary"),
    )(dy, proj, gain, ctab, stab)


def _flash_fwd(qr, k_all, vt_ctx, vt_lat, *, tq, scale, side=None):
    n, aw = qr.shape
    s_len, kvw = k_all.shape
    l = vt_ctx.shape[-1]
    n_j, tk = vt_lat.shape[1], vt_lat.shape[-1]
    kvh = kvw // HEAD_DIM
    n_i = n // tq
    gw = GQA_GROUP * HEAD_DIM
    c2 = scale * LOG2E
    n_sin = len(side.ins) if side else 0
    n_sout = len(side.out_shapes) if side else 0

    def body(*refs):
        q_ref, k_ref, vc_ref, vl_ref = refs[:4]
        s_ins = refs[4:4 + n_sin]
        o_ref, lse_ref = refs[4 + n_sin:6 + n_sin]
        s_outs = refs[6 + n_sin:6 + n_sin + n_sout]
        acc_ref, m_ref, l_ref = refs[6 + n_sin + n_sout:9 + n_sin + n_sout]
        sems = refs[9 + n_sin + n_sout:]
        if side:
            first, middle, last = _side_at((kvh, n_i))
            pl.when(first)(lambda: side.start(s_ins, s_outs, sems))
            if side.mid is not None:
                pl.when(middle)(lambda: side.mid(s_ins, s_outs, sems))
        acc_ref[...] = jnp.zeros_like(acc_ref)
        l_ref[...] = jnp.zeros_like(l_ref)
        m_ref[...] = jnp.full(m_ref.shape, -1e30, F32)

        def tile(kj, vtj):
            for g in range(GQA_GROUP):
                st = _dot(kj, q_ref[:, g * HEAD_DIM:(g + 1) * HEAD_DIM], 1, 1) * c2
                m_old = m_ref[g]
                m_new = jnp.maximum(m_old, jnp.max(st, axis=0, keepdims=True))
                alpha = jnp.exp2(m_old - m_new)
                pt = jnp.exp2(st - m_new)
                l_ref[g] = alpha * l_ref[g] + jnp.sum(pt, axis=0, keepdims=True)
                m_ref[g] = m_new
                acc_ref[g] = acc_ref[g] * alpha + _dot(vtj, pt.astype(BF), 1, 0)

        tile(k_ref[0:l, :], vc_ref[0, 0])

        def step(j, carry):
            tile(k_ref[pl.ds(pl.multiple_of(l + j * tk, min(l, tk)), tk), :], vl_ref[0, j])
            return carry

        lax.fori_loop(0, n_j, step, 0)
        for g in range(GQA_GROUP):
            o_ref[:, g * HEAD_DIM:(g + 1) * HEAD_DIM] = (acc_ref[g] / l_ref[g]).T.astype(BF)
            lse_ref[0, g, 0] = m_ref[g] + jnp.log(l_ref[g]) * LOG2E
        if side:
            pl.when(last)(lambda: side.finish(s_ins, s_outs, sems))

    return pl.pallas_call(
        body, name="flash_fwd", grid=(kvh, n_i),
        out_shape=(jax.ShapeDtypeStruct((n, aw), BF), jax.ShapeDtypeStruct((kvh, GQA_GROUP, n_i, 1, tq), F32),
                   *(side.out_shapes if side else [])),
        in_specs=[pl.BlockSpec((tq, gw), lambda h, i: (i, h)), pl.BlockSpec((s_len, HEAD_DIM), lambda h, i: (0, h)),
                  pl.BlockSpec((1, 1, HEAD_DIM, l), lambda h, i: (h, 0, 0, 0)),
                  pl.BlockSpec((1, n_j, HEAD_DIM, tk), lambda h, i: (h, 0, 0, 0))] + [HBM_SPEC] * n_sin,
        out_specs=(pl.BlockSpec((tq, gw), lambda h, i: (i, h)),
                   pl.BlockSpec((1, GQA_GROUP, 1, 1, tq), lambda h, i: (h, 0, i, 0, 0)), *([HBM_SPEC] * n_sout)),
        input_output_aliases={4 + i: 2 + o for i, o in side.aliases.items()} if side else {},
        scratch_shapes=[pltpu.VMEM((GQA_GROUP, HEAD_DIM, tq), F32), pltpu.VMEM((GQA_GROUP, 1, tq), F32),
                        pltpu.VMEM((GQA_GROUP, 1, tq), F32)] + (side.sem_shapes if side else []),
        compiler_params=_cparams("arbitrary", "arbitrary") if side else _cparams("parallel", "parallel"),
    )(qr, k_all, vt_ctx, vt_lat, *(side.ins if side else []))


def _attn_delta(o, do, *, tq):
    n, aw = o.shape
    kvh = aw // (GQA_GROUP * HEAD_DIM)
    gw = GQA_GROUP * HEAD_DIM

    def body(o_ref, do_ref, d_ref):
        for g in range(GQA_GROUP):
            cols = slice(g * HEAD_DIM, (g + 1) * HEAD_DIM)
            prod = o_ref[:, cols].astype(F32) * do_ref[:, cols].astype(F32)
            d_ref[0, g, 0] = jnp.sum(prod.T, axis=0, keepdims=True)

    return pl.pallas_call(
        body, name="attn_delta", grid=(kvh, n // tq),
        out_shape=jax.ShapeDtypeStruct((kvh, GQA_GROUP, n // tq, 1, tq), F32),
        in_specs=[pl.BlockSpec((tq, gw), lambda h, i: (i, h)), pl.BlockSpec((tq, gw), lambda h, i: (i, h))],
        out_specs=pl.BlockSpec((1, GQA_GROUP, 1, 1, tq), lambda h, i: (h, 0, i, 0, 0)),
        compiler_params=_cparams("parallel", "parallel"),
    )(o, do)


def _flash_bwd(qr, do, k_all, v_all, lse, delta, *, tq, tk, scale, side=None):
    n, aw = qr.shape
    s_len, kvw = k_all.shape
    kvh = kvw // HEAD_DIM
    n_i, n_j = n // tq, s_len // tk
    gw = GQA_GROUP * HEAD_DIM
    n_sin = len(side.ins) if side else 0
    n_sout = len(side.out_shapes) if side else 0
    c2 = scale * LOG2E

    def body(*refs):
        q_ref, do_ref, k_ref, v_ref, lse_ref, dl_ref = refs[:6]
        s_ins = refs[6:6 + n_sin]
        dq_ref, dk_ref, dv_ref = refs[6 + n_sin:9 + n_sin]
        s_outs = refs[9 + n_sin:9 + n_sin + n_sout]
        sems = refs[9 + n_sin + n_sout:]
        jj = pl.program_id(1)
        if side:
            first, _, last = _side_at((kvh, n_j))
            pl.when(first)(lambda: side.start(s_ins, s_outs, sems))

        @pl.when(jj == 0)
        def _():
            dq_ref[...] = jnp.zeros_like(dq_ref)

        kj = k_ref[...]
        vj = v_ref[...]
        dk_ref[...] = jnp.zeros_like(dk_ref)
        dv_ref[...] = jnp.zeros_like(dv_ref)

        def step(i, carry):
            rows = pl.ds(pl.multiple_of(i * tq, tq), tq)
            dv_part = dk_part = None
            for g in range(GQA_GROUP):
                cols = slice(g * HEAD_DIM, (g + 1) * HEAD_DIM)
                qg = q_ref[rows, cols]
                dog = do_ref[rows, cols]
                pt = jnp.exp2(_dot(kj, qg, 1, 1) * c2 - lse_ref[0, g, i])
                dst = (pt * (_dot(vj, dog, 1, 1) - dl_ref[0, g, i])).astype(BF)
                dv_g = _dot(pt.astype(BF), dog, 1, 0)
                dk_g = _dot(dst, qg, 1, 0)
                dv_part = dv_g if dv_part is None else dv_part + dv_g
                dk_part = dk_g if dk_part is None else dk_part + dk_g
                dq_ref[rows, cols] += _dot(dst, kj, 0, 0)
            dv_ref[...] += dv_part
            dk_ref[...] += dk_part
            return carry

        lax.fori_loop(0, n_i, step, 0)
        dk_ref[...] = dk_ref[...] * scale

        @pl.when(jj == n_j - 1)
        def _():
            dq_ref[...] = dq_ref[...] * scale

        if side:
            pl.when(last)(lambda: side.finish(s_ins, s_outs, sems))

    stat_spec = pl.BlockSpec((1, GQA_GROUP, n_i, 1, tq), lambda h, j: (h, 0, 0, 0, 0))
    kv_spec = pl.BlockSpec((tk, HEAD_DIM), lambda h, j: (j, h))
    q_spec = pl.BlockSpec((n, gw), lambda h, j: (0, h))
    out_shape = (jax.ShapeDtypeStruct((n, aw), F32), jax.ShapeDtypeStruct((s_len, kvw), F32),
                 jax.ShapeDtypeStruct((s_len, kvw), F32))
    if not side:
        return pl.pallas_call(
            body, name="flash_bwd", grid=(kvh, n_j), out_shape=out_shape,
            in_specs=[q_spec, q_spec, kv_spec, kv_spec, stat_spec, stat_spec], out_specs=(q_spec, kv_spec, kv_spec),
            compiler_params=_cparams("parallel", "arbitrary"),
        )(qr, do, k_all, v_all, lse, delta)
    return pl.pallas_call(
        body, name="flash_bwd", grid=(kvh, n_j), out_shape=(*out_shape, *side.out_shapes),
        in_specs=[q_spec, q_spec, kv_spec, kv_spec, stat_spec, stat_spec] + [HBM_SPEC] * n_sin,
        out_specs=(q_spec, kv_spec, kv_spec, *([HBM_SPEC] * n_sout)),
        input_output_aliases={6 + i: 3 + o for i, o in side.aliases.items()},
        scratch_shapes=side.sem_shapes, compiler_params=_cparams("arbitrary", "arbitrary"),
    )(qr, do, k_all, v_all, lse, delta, *side.ins)


def _pool_window(r, gi, l, n, tm):
    win = tm + 2 * POOL_HALO
    start = jnp.clip(l + r * tm - POOL_HALO, l, l + n - win)
    half = lax.shift_left(jnp.int32(1), gi)
    tok = r * tm + lax.broadcasted_iota(jnp.int32, (tm, win), 0)
    src = (start - l) + lax.broadcasted_iota(jnp.int32, (tm, win), 1)
    tok1 = r * tm + lax.broadcasted_iota(jnp.int32, (tm, 1), 0)
    cnt = (jnp.minimum(tok1 + half, n) - jnp.maximum(tok1 - half, 0)).astype(F32)
    return start, win, tok, src, half, cnt


def _pool_fwd(proj, pool_w, *, l, n, col0, tm):
    s_len = proj.shape[0]
    pg = pool_w.shape[-1]
    cb0 = col0 // pg
    assert col0 % pg == 0 and l % POOL_HALO == 0 and n >= tm + 2 * POOL_HALO

    def body(u_ref, w_ref, d_ref, po_ref):
        gi, r = pl.program_id(0), pl.program_id(1)
        start, win, tok, src, half, cnt = _pool_window(r, gi, l, n, tm)
        band = jnp.logical_and(src >= tok - half, src < tok + half).astype(BF)
        uw = u_ref[pl.ds(pl.multiple_of(start, POOL_HALO), win), :]
        ut = u_ref[pl.ds(pl.multiple_of(l + r * tm, POOL_HALO), tm), :].astype(F32)
        dv = (_dot(band, uw, 1, 0) / cnt - ut).astype(BF)
        d_ref[...] = dv
        po_ref[...] = _dot(dv, w_ref[0], 1, 0).astype(BF)

    return pl.pallas_call(
        body, name="pool_fwd", grid=(N_POOL_GROUPS, n // tm),
        out_shape=(jax.ShapeDtypeStruct((n, N_POOL_GROUPS * pg), BF), jax.ShapeDtypeStruct((n, N_POOL_GROUPS * pg), BF)),
        in_specs=[pl.BlockSpec((s_len, pg), lambda g, r: (0, cb0 + g)), pl.BlockSpec((1, pg, pg), lambda g, r: (g, 0, 0))],
        out_specs=(pl.BlockSpec((tm, pg), lambda g, r: (r, g)), pl.BlockSpec((tm, pg), lambda g, r: (r, g))),
        compiler_params=_cparams("parallel", "parallel"),
    )(proj, pool_w)


def _pool_bwd_map(dval, pool_w, *, n, col0, tm):
    pg = pool_w.shape[-1]
    cb0 = col0 // pg

    def body(d_ref, w_ref, o_ref):
        gi, r = pl.program_id(0), pl.program_id(1)
        half = lax.shift_left(jnp.int32(1), gi)
        tok1 = r * tm + lax.broadcasted_iota(jnp.int32, (tm, 1), 0)
        cnt = (jnp.minimum(tok1 + half, n) - jnp.maximum(tok1 - half, 0)).astype(F32)
        o_ref[...] = (_dot(d_ref[...], w_ref[0], 1, 1) / cnt).astype(BF)

    return pl.pallas_call(
        body, name="pool_bwd_map", grid=(N_POOL_GROUPS, n // tm),
        out_shape=jax.ShapeDtypeStruct((n, N_POOL_GROUPS * pg), BF),
        in_specs=[pl.BlockSpec((tm, pg), lambda g, r: (r, cb0 + g)), pl.BlockSpec((1, pg, pg), lambda g, r: (g, 0, 0))],
        out_specs=pl.BlockSpec((tm, pg), lambda g, r: (r, g)), compiler_params=_cparams("parallel", "parallel"),
    )(dval, pool_w)


def _pool_bwd_window(dds, *, n, tm):
    pg = dds.shape[1] // N_POOL_GROUPS

    def body(d_ref, o_ref):
        gi, r = pl.program_id(0), pl.program_id(1)
        start, win, tok, src, half, cnt = _pool_window(r, gi, 0, n, tm)
        band = jnp.logical_and(tok >= src - half, tok < src + half).astype(BF)
        dw = d_ref[pl.ds(pl.multiple_of(start, POOL_HALO), win), :]
        dt = d_ref[pl.ds(pl.multiple_of(r * tm, POOL_HALO), tm), :].astype(F32)
        o_ref[...] = (_dot(band, dw, 1, 0) - dt * cnt).astype(BF)

    return pl.pallas_call(
        body, name="pool_bwd_window", grid=(N_POOL_GROUPS, n // tm), out_shape=jax.ShapeDtypeStruct(dds.shape, BF),
        in_specs=[pl.BlockSpec((n, pg), lambda g, r: (0, g))], out_specs=pl.BlockSpec((tm, pg), lambda g, r: (r, g)),
        compiler_params=_cparams("parallel", "parallel"),
    )(dds)


def _pool_wgrad(dmat, dval, *, col0, tk):
    n, pw = dmat.shape
    pg = pw // N_POOL_GROUPS
    cb0 = col0 // pg
    nk = n // tk

    def body(a_ref, b_ref, o_ref, acc):
        k = pl.program_id(1)

        @pl.when(k == 0)
        def _():
            acc[...] = jnp.zeros_like(acc)

        acc[...] += _dot(a_ref[...], b_ref[...], 0, 0)

        @pl.when(k == nk - 1)
        def _():
            o_ref[0] = acc[...].astype(BF)

    return pl.pallas_call(
        body, name="pool_wgrad", grid=(N_POOL_GROUPS, nk), out_shape=jax.ShapeDtypeStruct((N_POOL_GROUPS, pg, pg), BF),
        in_specs=[pl.BlockSpec((tk, pg), lambda g, k: (k, g)), pl.BlockSpec((tk, pg), lambda g, k: (k, cb0 + g))],
        out_specs=pl.BlockSpec((1, pg, pg), lambda g, k: (g, 0, 0)), scratch_shapes=[pltpu.VMEM((pg, pg), F32)],
        compiler_params=_cparams("parallel", "arbitrary"),
    )(dmat, dval)


def _merge(attn_o, po_raw, proj, pool_scale, *, l, ga0, gp0, tm):
    n, aw = attn_o.shape
    bw = aw // 2
    rb0 = l // tm
    ga_b, gp_b = ga0 // bw, gp0 // bw
    assert ga0 % bw == 0 and gp0 % bw == 0 and l % tm == 0

    def body(a_ref, p_ref, g_ref, ps_ref, y_ref):
        cb = pl.program_id(1)
        sg, _ = _silu_and_grad(g_ref[...].astype(F32))

        @pl.when(cb < 2)
        def _():
            y_ref[...] = (a_ref[...].astype(F32) * sg).astype(BF)

        @pl.when(cb >= 2)
        def _():
            y_ref[...] = (p_ref[...].astype(F32) * ps_ref[...] * sg).astype(BF)

    return pl.pallas_call(
        body, name="merge", grid=(n // tm, 4), out_shape=jax.ShapeDtypeStruct((n, 2 * aw), BF),
        in_specs=[pl.BlockSpec((tm, bw), lambda i, cb: (i, jnp.minimum(cb, 1))),
                  pl.BlockSpec((tm, bw), lambda i, cb: (i, jnp.maximum(cb - 2, 0))),
                  pl.BlockSpec((tm, bw), lambda i, cb: (i + rb0, jnp.where(cb < 2, ga_b + cb, gp_b + cb - 2))),
                  pl.BlockSpec((1, bw), lambda i, cb: (0, jnp.maximum(cb - 2, 0)))],
        out_specs=pl.BlockSpec((tm, bw), lambda i, cb: (i, cb)), compiler_params=_cparams("parallel", "arbitrary"),
    )(attn_o, po_raw, proj, pool_scale)


def _merge_bwd(dy, attn_o, po_raw, proj, pool_scale, *, l, ga0, gp0, tm):
    n, aw = attn_o.shape
    bw = aw // 2
    rb0 = l // tm
    ga_b, gp_b = ga0 // bw, gp0 // bw

    def body(dy_ref, a_ref, p_ref, g_ref, ps_ref, dv_ref, dg_ref, dps_ref):
        cb, i = pl.program_id(0), pl.program_id(1)
        sg, sgrad = _silu_and_grad(g_ref[...].astype(F32))
        dyv = dy_ref[...].astype(F32)

        @pl.when(cb < 2)
        def _():
            dv_ref[...] = (dyv * sg).astype(BF)
            dg_ref[...] = (dyv * a_ref[...].astype(F32) * sgrad).astype(BF)

        @pl.when(cb >= 2)
        def _():
            @pl.when(i == 0)
            def _():
                dps_ref[...] = jnp.zeros_like(dps_ref)

            pr = p_ref[...].astype(F32)
            dpo = dyv * sg
            dv_ref[...] = (dpo * ps_ref[...]).astype(BF)
            dg_ref[...] = (dyv * (pr * ps_ref[...]) * sgrad).astype(BF)
            dps_ref[0:1, :] += jnp.sum(dpo * pr, axis=0, keepdims=True)

    blk = pl.BlockSpec((tm, bw), lambda cb, i: (i, cb))
    return pl.pallas_call(
        body, name="merge_bwd", grid=(4, n // tm),
        out_shape=(jax.ShapeDtypeStruct((n, 2 * aw), BF), jax.ShapeDtypeStruct((n, 2 * aw), BF),
                   jax.ShapeDtypeStruct((8, aw), F32)),
        in_specs=[blk, pl.BlockSpec((tm, bw), lambda cb, i: (i, jnp.minimum(cb, 1))),
                  pl.BlockSpec((tm, bw), lambda cb, i: (i, jnp.maximum(cb - 2, 0))),
                  pl.BlockSpec((tm, bw), lambda cb, i: (i + rb0, jnp.where(cb < 2, ga_b + cb, gp_b + cb - 2))),
                  pl.BlockSpec((1, bw), lambda cb, i: (0, jnp.maximum(cb - 2, 0)))],
        out_specs=(blk, blk, pl.BlockSpec((8, bw), lambda cb, i: (0, jnp.maximum(cb - 2, 0)))),
        compiler_params=_cparams("arbitrary", "arbitrary"),
    )(dy, attn_o, po_raw, proj, pool_scale)


def _post(x, out, target, gate, g_post, *, tm):
    n, d = x.shape

    def body(x_ref, o_ref, t_ref, gate_ref, gp_ref, dxn_ref, dout_ref, st_ref):
        @pl.when(pl.program_id(0) == 0)
        def _():
            st_ref[...] = jnp.zeros_like(st_ref)

        ov = o_ref[...]
        rstd = lax.rsqrt(jnp.mean(ov * ov, axis=-1, keepdims=True) + EPS)
        on = ov * rstd
        rn = on * gp_ref[...]
        err = (x_ref[...] + gate_ref[...] * rn) - t_ref[...]
        dxn = err / d
        dxn_ref[...] = dxn
        drn = dxn * gate_ref[...]
        don = drn * gp_ref[...]
        dout_ref[...] = (rstd * (don - on * jnp.mean(don * on, axis=-1, keepdims=True))).astype(BF)
        st_ref[0:1, :] += jnp.sum(dxn * rn, axis=0, keepdims=True)
        st_ref[1:2, :] += jnp.sum(drn * on, axis=0, keepdims=True)
        st_ref[2:3, :] += jnp.sum(err * err, axis=0, keepdims=True)

    row = pl.BlockSpec((tm, d), lambda i: (i, 0))
    vec = pl.BlockSpec((1, d), lambda i: (0, 0))
    return pl.pallas_call(
        body, name="post", grid=(n // tm,),
        out_shape=(jax.ShapeDtypeStruct((n, d), F32), jax.ShapeDtypeStruct((n, d), BF), jax.ShapeDtypeStruct((8, d), F32)),
        in_specs=[row, row, row, vec, vec], out_specs=(row, row, pl.BlockSpec((8, d), lambda i: (0, 0))),
        compiler_params=_cparams("arbitrary"),
    )(x, out, target, gate, g_post)


def _prenorm_bwd(x, ctx, dh, dxn, g_pre, ss, *, tm):
    n, d = x.shape
    l = ctx.shape[0]
    nc = l // tm

    def body(x_ref, c_ref, dh_ref, dxn_ref, g_ref, ss_ref, gx_ref, st_ref):
        t = pl.program_id(0)

        @pl.when(t == 0)
        def _():
            st_ref[...] = jnp.zeros_like(st_ref)

        def go(src, scale, row):
            v = src[...]
            dhv = dh_ref[...].astype(F32)
            rstd = lax.rsqrt(jnp.mean(v * v, axis=-1, keepdims=True) + EPS)
            xn = v * rstd
            st_ref[row:row + 1, :] += jnp.sum(dhv, axis=0, keepdims=True)
            st_ref[row + 1:row + 2, :] += jnp.sum(dhv * (xn * g_ref[...]), axis=0, keepdims=True)
            dg = dhv * (1.0 + scale)
            st_ref[4:5, :] += jnp.sum(dg * xn, axis=0, keepdims=True)
            dxn_ = dg * g_ref[...]
            return rstd * (dxn_ - xn * jnp.mean(dxn_ * xn, axis=-1, keepdims=True))

        @pl.when(t < nc)
        def _():
            go(c_ref, ss_ref[2:3, :], 2)

        @pl.when(t >= nc)
        def _():
            gx_ref[...] = dxn_ref[...] + go(x_ref, ss_ref[0:1, :], 0)

    lat = pl.BlockSpec((tm, d), lambda t: (jnp.maximum(t - nc, 0), 0))
    return pl.pallas_call(
        body, name="prenorm_bwd", grid=((l + n) // tm,),
        out_shape=(jax.ShapeDtypeStruct((n, d), F32), jax.ShapeDtypeStruct((8, d), F32)),
        in_specs=[lat, pl.BlockSpec((tm, d), lambda t: (jnp.minimum(t, nc - 1), 0)),
                  pl.BlockSpec((tm, d), lambda t: (t, 0)), lat,
                  pl.BlockSpec((1, d), lambda t: (0, 0)), pl.BlockSpec((4, d), lambda t: (0, 0))],
        out_specs=(lat, pl.BlockSpec((8, d), lambda t: (0, 0))), compiler_params=_cparams("arbitrary"),
    )(x, ctx, dh, dxn, g_pre, ss)


def _ada_fwd(craw, w_ada, b_loc, *, tn):
    d, wn = w_ada.shape

    def body(c_ref, w_ref, b_ref, o_ref):
        act, _ = _silu_and_grad(c_ref[...])
        o_ref[...] = _dot(act.astype(BF), w_ref[...].astype(BF), 1, 0) + b_ref[...]

    return pl.pallas_call(
        body, name="ada_fwd", grid=(wn // tn,), out_shape=jax.ShapeDtypeStruct((16, wn), F32),
        in_specs=[pl.BlockSpec((16, d), lambda j: (0, 0)), pl.BlockSpec((d, tn), lambda j: (0, j)),
                  pl.BlockSpec((1, tn), lambda j: (0, j))],
        out_specs=pl.BlockSpec((16, tn), lambda j: (0, j)), compiler_params=_cparams("parallel"),
    )(craw, w_ada, b_loc)


def _ada_bwd(craw_t, dm, w, m, v, *, tn):
    d, wn = w.shape

    def body(ct_ref, dm_ref, w_ref, m_ref, v_ref, g_ref, dl_ref, nm_ref, nv_ref, da_ref):
        @pl.when(pl.program_id(0) == 0)
        def _():
            da_ref[...] = jnp.zeros_like(da_ref)

        act, _ = _silu_and_grad(ct_ref[...])
        dmb = dm_ref[...].astype(BF)
        wv = w_ref[...]
        gv = _dot(act.astype(BF), dmb, 1, 0)
        da_ref[...] += _dot(dmb, wv.astype(BF), 1, 1)
        nm = ADAM_B1 * m_ref[...] + (1.0 - ADAM_B1) * gv
        nv = ADAM_B2 * v_ref[...] + (1.0 - ADAM_B2) * (gv * gv)
        m_hat = nm / (1.0 - ADAM_B1 ** ADAM_STEP)
        v_hat = nv / (1.0 - ADAM_B2 ** ADAM_STEP)
        g_ref[...] = gv
        dl_ref[...] = -ADAM_LR * (m_hat / (jnp.sqrt(v_hat) + ADAM_EPS) + ADAM_WD * wv)
        nm_ref[...] = nm
        nv_ref[...] = nv

    col = pl.BlockSpec((d, tn), lambda j: (0, j))
    return pl.pallas_call(
        body, name="ada_bwd", grid=(wn // tn,),
        out_shape=tuple([jax.ShapeDtypeStruct((d, wn), F32)] * 4) + (jax.ShapeDtypeStruct((16, d), F32),),
        in_specs=[pl.BlockSpec((d, 16), lambda j: (0, 0)), pl.BlockSpec((16, tn), lambda j: (0, j)), col, col, col],
        out_specs=(col, col, col, col, pl.BlockSpec((16, d), lambda j: (0, 0))),
        compiler_params=_cparams("arbitrary"),
    )(craw_t, dm, w, m, v)


def _reduce_small(gath, d3):
    t = gath.shape[-1]

    def body(g_ref, dm_ref, s_ref):
        tot = g_ref[0]
        for b in range(1, 8):
            tot = tot + g_ref[b]
        s_ref[...] = tot
        dm_ref[...] = jnp.zeros_like(dm_ref)
        for b in range(8):
            dm_ref[b:b + 1, :] = g_ref[b][:, 0:d3]
        dm_ref[8:9, :] = tot[:, d3:2 * d3]

    return pl.pallas_call(
        body, name="reduce_small", out_shape=(jax.ShapeDtypeStruct((16, d3), F32), jax.ShapeDtypeStruct((1, t), F32)),
        in_specs=[VMEM_SPEC], out_specs=(VMEM_SPEC, VMEM_SPEC),
    )(gath)


def _cctx_grad(parts, c_ctx):
    def body(p_ref, c_ref, o_ref):
        tot = (p_ref[0] + p_ref[1]) + (p_ref[2] + p_ref[3])
        _, sgrad = _silu_and_grad(c_ref[...])
        o_ref[...] = tot * sgrad

    return pl.pallas_call(
        body, name="cctx_grad", out_shape=jax.ShapeDtypeStruct(c_ctx.shape, F32),
        in_specs=[VMEM_SPEC, VMEM_SPEC], out_specs=VMEM_SPEC,
    )(parts, c_ctx)


def _rope_tables(n, l):
    rows = n // GRID_W
    row = jnp.repeat(jnp.arange(rows, dtype=F32), GRID_W)
    col = jnp.tile(jnp.arange(GRID_W, dtype=F32), rows)
    inv = ROPE_THETA ** (-jnp.arange(ROPE_PAIRS, dtype=F32) / ROPE_PAIRS)
    ang = jnp.concatenate([row[:, None] * inv, col[:, None] * inv], axis=-1)
    cos, sin = jnp.cos(ang), jnp.sin(ang)
    cr, cc, sr, sc = cos[:, :ROPE_PAIRS], cos[:, ROPE_PAIRS:], sin[:, :ROPE_PAIRS], sin[:, ROPE_PAIRS:]
    ctab = jnp.concatenate([cr, cr, cc, cc], axis=-1)
    stab = jnp.concatenate([-sr, sr, -sc, sc], axis=-1)
    ctab = jnp.concatenate([jnp.ones((l, HEAD_DIM), F32), ctab], axis=0)
    stab = jnp.concatenate([jnp.zeros((l, HEAD_DIM), F32), stab], axis=0)
    return ctab, stab


def kernel(x, c, ctx, c_ctx, w_ada, b_ada, norm_pre, norm_post, w_in, q_norm, k_norm, pool_w, pool_scale, w_out, loss_target, m_c_ctx, m_w_ada, m_b_ada, m_norm_pre, m_norm_post, m_w_in, m_q_norm, m_k_norm, m_pool_w, m_pool_scale, m_w_out, v_c_ctx, v_w_ada, v_b_ada, v_norm_pre, v_norm_post, v_w_in, v_q_norm, v_k_norm, v_pool_w, v_pool_scale, v_w_out):
    x2, ctx2, tgt = x[0], ctx[0], loss_target[0]
    n, d = x2.shape
    l = ctx2.shape[0]
    s_len = l + n
    aw = d // 2
    pw = d - aw
    n_heads = aw // HEAD_DIM
    kvw = (n_heads // GQA_GROUP) * HEAD_DIM
    pg = pw // N_POOL_GROUPS
    k0, v0, ga0 = aw, aw + kvw, aw + 2 * kvw
    up0, gp0 = ga0 + aw, ga0 + aw + pw
    in_w = gp0 + pw
    d3 = 3 * d
    ada_w = w_ada.shape[-1]
    px, py, pc = lax.axis_index("x"), lax.axis_index("y"), lax.axis_index("c")
    chip = 2 * px + py
    tr = min(256, l)
    tq = min(512, n)
    tk = min(256, l)
    ts = s_len // 8 if (s_len // 8) % 16 == 0 else tr
    scale = HEAD_DIM ** -0.5

    c_all = _allgather_small(c, chips_only=False, name="gather_c").reshape(8, d)
    craw = jnp.concatenate([c_all, c_ctx.reshape(1, d), jnp.zeros((7, d), F32)], axis=0)
    b_loc = lax.dynamic_slice(b_ada, (0, chip * ada_w), (1, ada_w))
    mod_part = _ada_fwd(craw, w_ada[0], b_loc, tn=min(512, ada_w))
    mod_all = _allgather_small(mod_part, chips_only=True, name="gather_mod")
    mod_all = jnp.transpose(mod_all, (1, 0, 2)).reshape(16, d3)
    me = 4 * px + 2 * py + pc
    mod_b = lax.dynamic_slice(mod_all, (me, 0), (1, d3))
    mod_c = mod_all[8:9]
    ss = jnp.concatenate([mod_b[:, d:2 * d], mod_b[:, 0:d], mod_c[:, d:2 * d], mod_c[:, 0:d]], axis=0)
    gate = mod_b[:, 2 * d:]

    w_own = _cast_bf16(w_in[0], name="cast_w_in")
    later_weights = _gather_side([_cast_into(w_out[0], 0, chip, name="cast_w_out"),
                                  _cast_into(pool_w[0], 1, chip, name="cast_pool_w")], [(0, 1), (1, 2)])
    ids = jnp.stack([chip, jnp.bitwise_xor(chip, 2), jnp.bitwise_xor(chip, 1), jnp.bitwise_xor(chip, 3)])
    ws = in_w // 4
    tn_p = ws // 3 if (ws // 3) % HEAD_DIM == 0 else ws

    ctab, stab = _rope_tables(n, l)
    h = _prenorm(x2, ctx2, norm_pre, ss, tm=tr)
    proj, w_x, w_y = _in_proj_parts(h, [w_own], ids[0:1], None, width=in_w, tm=ts, tn=tn_p, name="in_proj_own",
                                    side=_gather_parts_side(w_own, (0, 1)))
    proj, w_d = _in_proj_parts(h, [w_x, w_y], ids[1:3], proj, width=in_w, tm=ts, tn=tn_p, name="in_proj_xy",
                               side=_gather_parts_side(w_own, (2,)))
    (proj,) = _in_proj_parts(h, [w_d], ids[3:4], proj, width=in_w, tm=ts, tn=tn_p, name="in_proj_diag")
    w_parts = [w_own, w_x, w_y, w_d]
    qr = _qk_prep(proj, q_norm, ctab[l:], stab[l:], row0=l, col0=0, width=aw, nrows=n, tm=tr, name="q_prep")
    kr = _qk_prep(proj, k_norm, ctab, stab, row0=0, col0=k0, width=kvw, nrows=s_len, tm=tr, name="k_prep")
    v_all = proj[:, v0:v0 + kvw]
    tkf = min(512, n)
    vt_ctx = jnp.transpose(v_all[:l].reshape(1, l, kvw // HEAD_DIM, HEAD_DIM), (2, 0, 3, 1))
    vt_lat = jnp.transpose(v_all[l:].reshape(n // tkf, tkf, kvw // HEAD_DIM, HEAD_DIM), (2, 0, 3, 1))
    attn_o, lse, w_out_f, pool_f = _flash_fwd(qr, kr, vt_ctx, vt_lat, tq=tq, scale=scale, side=later_weights)
    dmat, po_raw = _pool_fwd(proj, pool_f, l=l, n=n, col0=up0, tm=tr)
    yv = _merge(attn_o, po_raw, proj, pool_scale, l=l, ga0=ga0, gp0=gp0, tm=tr)
    out = _matmul(yv, w_out_f, ca=1, cb=0, tm=min(512, n), tn=min(512, d), tk=d, out_dtype=F32, name="out_proj")

    dxn, d_out, post_st = _post(x2, out, tgt, gate, norm_post, tm=min(128, n))
    loss = 0.5 * jnp.sum(post_st[2]) / d
    loss = lax.psum(loss, ("x", "y", "c"))
    g_wout = _matmul(yv, d_out, ca=0, cb=0, tm=min(512, d), tn=min(512, d), tk=n, out_dtype=BF, name="wgrad_out")
    dy, from_sib_wout = _matmul(d_out, w_out_f, ca=1, cb=1, tm=min(512, n), tn=min(512, d), tk=d, out_dtype=BF,
                                name="dgrad_out", side=_pair_side([g_wout], [1]))
    pair_wout = _add_own_half(g_wout, from_sib_wout, 1, pc, name="pair_sum_1")
    dval, dgate, dps = _merge_bwd(dy, attn_o, po_raw, proj, pool_scale, l=l, ga0=ga0, gp0=gp0, tm=tr)
    g_pool = _pool_wgrad(dmat, dval, col0=aw, tk=min(512, n))
    dds = _pool_bwd_map(dval, pool_f, n=n, col0=aw, tm=tr)
    du = _pool_bwd_window(dds, n=n, tm=tr)
    do = dval[:, :aw]
    delta = _attn_delta(attn_o, do, tq=tq)
    dq, dk, dv, *from_chips_wout = _flash_bwd(qr, do, kr, v_all, lse, delta, tq=tq, tk=tk, scale=scale,
                                              side=_chips_side([pair_wout], [0]))
    red_wout = _sum_own_block(pair_wout, from_chips_wout, 0, 1, chip, pc, name="chip_sum_1")
    dq_raw, dgq = _qk_bwd(dq, proj, q_norm, ctab[l:], stab[l:], row0=l, col0=0, tm=tr, name="q_bwd")
    dk_raw, dgk = _qk_bwd(dk, proj, k_norm, ctab, stab, row0=0, col0=k0, tm=tr, name="k_bwd")
    dvb = dv.astype(BF)
    dproj = jnp.concatenate([
        jnp.concatenate([jnp.zeros((l, aw), BF), dq_raw], axis=0), dk_raw, dvb,
        jnp.concatenate([jnp.zeros((l, aw + 2 * pw), BF),
                         jnp.concatenate([dgate[:, :aw], du, dgate[:, aw:]], axis=1)], axis=0)], axis=1)
    g_win = _matmul(h, dproj, ca=0, cb=0, tm=min(512, d), tn=min(512, in_w), tk=s_len, out_dtype=BF, name="wgrad_in")
    from_sib = _run_side(_pair_side([g_win, g_pool], [0, 2]), name="reduce_pair")
    pair_win = _add_own_half(g_win, from_sib[0], 0, pc, name="pair_sum_0")
    pair_pool = _add_own_half(g_pool, from_sib[1], 2, pc, name="pair_sum_2")
    dh, *from_chips = _dgrad_in_parts(dproj, w_parts, ids, tm=ts, tn=min(512, d), name="dgrad_in",
                                      side=_chips_side([pair_win, pair_pool], [1, 1]))
    red_win = _sum_own_block(pair_win, from_chips[0:3], 1, 0, chip, pc, name="chip_sum_0")
    red_pool = _sum_own_block(pair_pool, from_chips[3:6], 1, 2, chip, pc, name="chip_sum_2")
    grad_x, pre_st = _prenorm_bwd(x2, ctx2, dh, dxn, norm_pre, ss, tm=min(128, l))

    zero_d = jnp.zeros((1, d), F32)
    packed = jnp.concatenate([pre_st[0:1], pre_st[1:2], post_st[0:1], pre_st[2:3], pre_st[3:4], zero_d,
                              pre_st[4:5], post_st[1:2], dgq[0:1], dgk[0:1], dps[0:1]], axis=1)
    gath = _allgather_small(packed, chips_only=False, name="gather_small")
    dm, sums = _reduce_small(gath, d3)
    o1 = 2 * d3
    g_npre, g_npost = sums[:, o1:o1 + d], sums[:, o1 + d:o1 + 2 * d]
    g_q, g_k = sums[:, o1 + 2 * d:o1 + 2 * d + HEAD_DIM], sums[:, o1 + 2 * d + HEAD_DIM:o1 + 2 * d + 2 * HEAD_DIM]
    g_ps = sums[:, o1 + 2 * d + 2 * HEAD_DIM:]
    g_bada = sums[:, 0:d3] + sums[:, d3:2 * d3]
    dm_loc = lax.dynamic_slice(dm, (0, chip * ada_w), (16, ada_w))
    g_wada, dl_wada, nm_wada, nv_wada, dact = _ada_bwd(craw.T, dm_loc, w_ada[0], m_w_ada[0], v_w_ada[0], tn=128)
    cparts = _allgather_small(dact[8:9], chips_only=True, name="gather_cctx")
    g_cctx = _cctx_grad(cparts, c_ctx.reshape(1, d)).reshape(d)

    g_win_s, g_wout_s, g_pool_s = _run_side(_join_side([red_win, red_wout, red_pool], [0, 1, 2]), name="reduce_join")

    def upd(w, g, m, v, name):
        return _adamw(w, g.reshape(w.shape), m, v, name=name)

    grads = {"c_ctx": g_cctx, "b_ada": g_bada, "norm_pre": g_npre, "norm_post": g_npost, "w_in": g_win_s[None],
             "q_norm": g_q, "k_norm": g_k, "pool_w": g_pool_s[None], "pool_scale": g_ps, "w_out": g_wout_s[None]}
    res = {"w_ada": (g_wada[None], dl_wada[None], nm_wada[None], nv_wada[None])}
    given = {"c_ctx": (c_ctx, m_c_ctx, v_c_ctx), "b_ada": (b_ada, m_b_ada, v_b_ada),
             "norm_pre": (norm_pre, m_norm_pre, v_norm_pre), "norm_post": (norm_post, m_norm_post, v_norm_post),
             "w_in": (w_in, m_w_in, v_w_in), "q_norm": (q_norm, m_q_norm, v_q_norm), "k_norm": (k_norm, m_k_norm, v_k_norm),
             "pool_w": (pool_w, m_pool_w, v_pool_w), "pool_scale": (pool_scale, m_pool_scale, v_pool_scale),
             "w_out": (w_out, m_w_out, v_w_out)}
    for nme, (w, m, v) in given.items():
        g = grads[nme].reshape(w.shape)
        res[nme] = (g,) + upd(w, g, m, v, "adamw_" + nme)
    order = ["c_ctx", "w_ada", "b_ada", "norm_pre", "norm_post", "w_in", "q_norm", "k_norm", "pool_w", "pool_scale", "w_out"]
    return (loss, grad_x[None], *[res[k][0] for k in order], *[res[k][1] for k in order],
            *[res[k][2] for k in order], *[res[k][3] for k in order])
```

```python
import functools

import jax
import jax.numpy as jnp
from jax import lax
from jax.experimental import pallas as pl
from jax.experimental.pallas import tpu as pltpu

F32 = jnp.float32
BF = jnp.bfloat16
MESH = pl.DeviceIdType.MESH

HEAD_DIM = 128
GQA_GROUP = 4
GRID_W = 64
ROPE_PAIRS = HEAD_DIM // 4
ROPE_THETA = 10000.0
EPS = 1e-6
N_POOL_GROUPS = 4
POOL_HALO = 128
ADAM_LR = 0.001
ADAM_B1 = 0.9
ADAM_B2 = 0.999
ADAM_EPS = 1e-08
ADAM_WD = 0.01
ADAM_STEP = 10
LOG2E = 1.4426950408889634
MIB = 2 ** 20
VMEM_LIMIT = 48 * MIB
CHIP_MASKS = ((1, 0, 0), (0, 1, 0), (1, 1, 0))
ALL_MASKS = ((0, 0, 1), (0, 1, 0), (0, 1, 1), (1, 0, 0), (1, 0, 1), (1, 1, 0), (1, 1, 1))
HBM_SPEC = pl.BlockSpec(memory_space=pl.ANY)
VMEM_SPEC = pl.BlockSpec(memory_space=pltpu.VMEM)


def _cparams(*sem):
    return pltpu.CompilerParams(dimension_semantics=sem, vmem_limit_bytes=VMEM_LIMIT)


def _sigmoid(v):
    return 1.0 / (1.0 + jnp.exp(-v))


def _silu_and_grad(v):
    s = _sigmoid(v)
    return v * s, s * (1.0 + v * (1.0 - s))


def _dot(a, b, ca, cb):
    return lax.dot_general(a, b, (((ca,), (cb,)), ((), ())), preferred_element_type=F32)


def _block_rows(rows, width, itemsize=4, target=MIB):
    best = 8
    for t in range(8, rows + 1, 8):
        if rows % t == 0 and t * width * itemsize <= target:
            best = t
    return best if rows % 8 == 0 else rows


def _my_pos():
    return lax.axis_index("x"), lax.axis_index("y"), lax.axis_index("c")


def _flip(pos, mask):
    return tuple(jnp.bitwise_xor(p, m) if m else p for p, m in zip(pos, mask))


def _allgather_small(v, *, chips_only, name):
    r, w = v.shape
    masks = CHIP_MASKS if chips_only else ALL_MASKS
    nslot = 4 if chips_only else 8

    def slot(pos):
        return 2 * pos[0] + pos[1] if chips_only else 4 * pos[0] + 2 * pos[1] + pos[2]

    def body(v_ref, o_ref, send_sems, recv_sems, local_sem):
        me = _my_pos()

        def copy(k, block_of, to):
            return pltpu.make_async_remote_copy(
                src_ref=v_ref, dst_ref=o_ref.at[slot(block_of)], send_sem=send_sems.at[k], recv_sem=recv_sems.at[k],
                device_id=to, device_id_type=MESH)

        mine = pltpu.make_async_copy(v_ref, o_ref.at[slot(me)], local_sem)
        mine.start()
        sends = [copy(k, me, _flip(me, m)) for k, m in enumerate(masks)]
        for cp in sends:
            cp.start()
        for k, m in enumerate(masks):
            copy(k, _flip(me, m), me).wait_recv()
        for cp in sends:
            cp.wait_send()
        mine.wait()

    return pl.pallas_call(
        body, name=name, out_shape=jax.ShapeDtypeStruct((nslot, r, w), v.dtype),
        in_specs=[VMEM_SPEC], out_specs=VMEM_SPEC,
        scratch_shapes=[pltpu.SemaphoreType.DMA((len(masks),)), pltpu.SemaphoreType.DMA((len(masks),)),
                        pltpu.SemaphoreType.DMA],
    )(v)


def _sl(ref, axis, start, size):
    idx = [slice(None)] * len(ref.shape)
    idx[axis] = pl.ds(start, size)
    return ref.at[tuple(idx)]


class _Side:
    def __init__(self, ins, out_shapes, aliases, sem_shapes, start, finish, mid=None):
        self.ins, self.out_shapes, self.aliases, self.sem_shapes = list(ins), list(out_shapes), dict(aliases), list(sem_shapes)
        self.start, self.finish, self.mid = start, finish, mid


def _run_side(side, *, name):
    n_in, n_out = len(side.ins), len(side.out_shapes)

    def body(*refs):
        parts = refs[:n_in], refs[n_in:n_in + n_out], refs[n_in + n_out:]
        side.start(*parts)
        if side.mid is not None:
            side.mid(*parts)
        side.finish(*parts)

    return pl.pallas_call(
        body, name=name, out_shape=tuple(side.out_shapes), in_specs=[HBM_SPEC] * n_in, out_specs=tuple([HBM_SPEC] * n_out),
        input_output_aliases=side.aliases, scratch_shapes=side.sem_shapes,
    )(*side.ins)


def _side_at(grid):
    pids = [pl.program_id(a) for a in range(len(grid))]

    def at(values):
        cond = pids[0] == values[0]
        for p, v in zip(pids[1:], values[1:]):
            cond = jnp.logical_and(cond, p == v)
        return cond

    zeros = [0] * len(grid)
    return at(zeros), at([grid[0] // 2] + zeros[1:]), at([g - 1 for g in grid])


def _gather_side(shards, layouts):
    nm = len(shards)

    def copies(outs, sems):
        send_sems, recv_sems, fsend_sems, frecv_sems = sems
        x, y, c = _my_pos()
        me = (x, y, c)
        sib = (x, y, 1 - c)
        sends, landed, fwds, from_sib = [], [], [], []
        for m in range(nm):
            a_s, a_h = layouts[m]
            ns, nh = outs[m].shape[a_s] // 4, outs[m].shape[a_h] // 2

            def region(chip_pos, half, m=m, a_s=a_s, a_h=a_h, ns=ns, nh=nh):
                j = 2 * chip_pos[0] + chip_pos[1]
                return _sl(_sl(outs[m], a_s, j * ns, ns), a_h, half * nh, nh)

            for k, mask in enumerate(CHIP_MASKS):
                other = _flip(me, mask)

                def rc(ref, ssem, rsem, to):
                    return pltpu.make_async_remote_copy(src_ref=ref, dst_ref=ref, send_sem=ssem, recv_sem=rsem,
                                                        device_id=to, device_id_type=MESH)

                sends.append(rc(region(me, c), send_sems.at[m, k], recv_sems.at[m, k], other))
                landed.append(rc(region(other, c), send_sems.at[m, k], recv_sems.at[m, k], me))
                fwds.append(rc(region(other, c), fsend_sems.at[m, k], frecv_sems.at[m, k], sib))
                from_sib.append(rc(region(other, 1 - c), fsend_sems.at[m, k], frecv_sems.at[m, k], sib))
        return sends, landed, fwds, from_sib

    def start(ins, outs, sems):
        for cp in copies(outs, sems)[0]:
            cp.start()

    def mid(ins, outs, sems):
        _, landed, fwds, _ = copies(outs, sems)
        for arrived, fw in zip(landed, fwds):
            arrived.wait_recv()
            fw.start()

    def finish(ins, outs, sems):
        sends, _, fwds, from_sib = copies(outs, sems)
        for cp in from_sib:
            cp.wait_recv()
        for cp in sends + fwds:
            cp.wait_send()

    return _Side(shards, [jax.ShapeDtypeStruct(s.shape, s.dtype) for s in shards], {i: i for i in range(nm)},
                 [pltpu.SemaphoreType.DMA((nm, 3))] * 4, start, finish, mid)


def _gather_parts_side(w_own, mask_ids):
    nt = len(mask_ids)
    nh = w_own.shape[0] // 2

    def copies(ins, outs, sems):
        send_sems, recv_sems, fsend_sems, frecv_sems = sems
        x, y, c = _my_pos()
        me, sib = (x, y, c), (x, y, 1 - c)
        sends, landed, fwds, from_sib = [], [], [], []
        for t, k in enumerate(mask_ids):
            mine = _sl(outs[t], 0, c * nh, nh)
            theirs = _sl(outs[t], 0, (1 - c) * nh, nh)

            def rc(src, dst, ssem, rsem, to):
                return pltpu.make_async_remote_copy(src_ref=src, dst_ref=dst, send_sem=ssem, recv_sem=rsem,
                                                    device_id=to, device_id_type=MESH)

            sends.append(rc(_sl(ins[0], 0, c * nh, nh), mine, send_sems.at[t], recv_sems.at[t], _flip(me, CHIP_MASKS[k])))
            landed.append(rc(mine, mine, send_sems.at[t], recv_sems.at[t], me))
            fwds.append(rc(mine, mine, fsend_sems.at[t], frecv_sems.at[t], sib))
            from_sib.append(rc(theirs, theirs, fsend_sems.at[t], frecv_sems.at[t], sib))
        return sends, landed, fwds, from_sib

    def start(ins, outs, sems):
        for cp in copies(ins, outs, sems)[0]:
            cp.start()

    def finish(ins, outs, sems):
        sends, landed, fwds, from_sib = copies(ins, outs, sems)
        for arrived, fw in zip(landed, fwds):
            arrived.wait_recv()
            fw.start()
        for cp in from_sib:
            cp.wait_recv()
        for cp in sends + fwds:
            cp.wait_send()

    return _Side([w_own], [jax.ShapeDtypeStruct(w_own.shape, w_own.dtype)] * nt, {},
                 [pltpu.SemaphoreType.DMA((nt,))] * 4, start, finish)


def _pair_side(mats, half_axes):
    nm = len(mats)

    def copies(ins, outs, sems):
        x, y, c = _my_pos()
        cps = []
        for m in range(nm):
            nh = ins[m].shape[half_axes[m]] // 2
            cps.append(pltpu.make_async_remote_copy(
                src_ref=_sl(ins[m], half_axes[m], (1 - c) * nh, nh), dst_ref=outs[m],
                send_sem=sems[0].at[m], recv_sem=sems[1].at[m], device_id=(x, y, 1 - c), device_id_type=MESH))
        return cps

    def start(ins, outs, sems):
        for cp in copies(ins, outs, sems):
            cp.start()

    def finish(ins, outs, sems):
        for cp in copies(ins, outs, sems):
            cp.wait()

    out_shapes = []
    for s, a_h in zip(mats, half_axes):
        shp = list(s.shape)
        shp[a_h] //= 2
        out_shapes.append(jax.ShapeDtypeStruct(tuple(shp), s.dtype))
    return _Side(mats, out_shapes, {}, [pltpu.SemaphoreType.DMA((nm,))] * 2, start, finish)


def _chips_side(mats, shard_axes):
    nm = len(mats)

    def copies(ins, outs, sems):
        me = _my_pos()
        cps = []
        for m in range(nm):
            ns = ins[m].shape[shard_axes[m]] // 4
            for k, mask in enumerate(CHIP_MASKS):
                other = _flip(me, mask)
                cps.append(pltpu.make_async_remote_copy(
                    src_ref=_sl(ins[m], shard_axes[m], (2 * other[0] + other[1]) * ns, ns), dst_ref=outs[3 * m + k],
                    send_sem=sems[0].at[m, k], recv_sem=sems[1].at[m, k], device_id=other, device_id_type=MESH))
        return cps

    def start(ins, outs, sems):
        for cp in copies(ins, outs, sems):
            cp.start()

    def finish(ins, outs, sems):
        for cp in copies(ins, outs, sems):
            cp.wait()

    out_shapes = []
    for s, a_s in zip(mats, shard_axes):
        shp = list(s.shape)
        shp[a_s] //= 4
        out_shapes += [jax.ShapeDtypeStruct(tuple(shp), s.dtype)] * 3
    return _Side(mats, out_shapes, {}, [pltpu.SemaphoreType.DMA((nm, 3))] * 2, start, finish)


def _join_side(halves, half_axes):
    nm = len(halves)

    def copies(outs, sems):
        x, y, c = _my_pos()
        mine, theirs = [], []
        for m in range(nm):
            nh = outs[m].shape[half_axes[m]] // 2
            for half, lst in ((c, mine), (1 - c, theirs)):
                ref = _sl(outs[m], half_axes[m], half * nh, nh)
                lst.append(pltpu.make_async_remote_copy(
                    src_ref=ref, dst_ref=ref, send_sem=sems[0].at[m], recv_sem=sems[1].at[m],
                    device_id=(x, y, 1 - c), device_id_type=MESH))
        return mine, theirs

    def start(ins, outs, sems):
        for cp in copies(outs, sems)[0]:
            cp.start()

    def finish(ins, outs, sems):
        mine, theirs = copies(outs, sems)
        for cp in mine:
            cp.wait_send()
        for cp in theirs:
            cp.wait_recv()

    return _Side(halves, [jax.ShapeDtypeStruct(s.shape, s.dtype) for s in halves], {i: i for i in range(nm)},
                 [pltpu.SemaphoreType.DMA((nm,))] * 2, start, finish)


def _part_specs(shp, axis, slot, itemsize, target=MIB):
    if len(shp) == 2:
        rows, width = shp
        tm = _block_rows(rows, width, itemsize, target)
        nb = rows // tm
        own = pl.BlockSpec((tm, width), lambda i, p: (i, 0))
        if axis == 1:
            part = pl.BlockSpec((tm, width), lambda i, p: (i, p[slot]))
        else:
            part = pl.BlockSpec((tm, width), lambda i, p: (p[slot] * nb + i, 0))
        return (nb,), own, part
    assert len(shp) == 3 and axis in (1, 2)
    g, rows, width = shp
    own = pl.BlockSpec((1, rows, width), lambda i, p: (i, 0, 0))
    if axis == 1:
        part = pl.BlockSpec((1, rows, width), lambda i, p: (i, p[slot], 0))
    else:
        part = pl.BlockSpec((1, rows, width), lambda i, p: (i, 0, p[slot]))
    return (g,), own, part


def _cast_into(w, axis, p, *, name):
    grid, own, part = _part_specs(w.shape, axis, 0, 4, 2 * MIB)
    big = list(w.shape)
    big[axis] *= 4

    def body(p_ref, w_ref, o_ref):
        o_ref[...] = w_ref[...].astype(BF)

    return pl.pallas_call(
        body, name=name, out_shape=jax.ShapeDtypeStruct(tuple(big), BF),
        grid_spec=pltpu.PrefetchScalarGridSpec(num_scalar_prefetch=1, grid=grid, in_specs=[own], out_specs=part),
        compiler_params=_cparams("parallel"),
    )(p.reshape(1), w)


def _cast_bf16(w, *, name):
    rows, width = w.shape
    tm = _block_rows(rows, width, 4, 2 * MIB)

    def body(w_ref, o_ref):
        o_ref[...] = w_ref[...].astype(BF)

    spec = pl.BlockSpec((tm, width), lambda i: (i, 0))
    return pl.pallas_call(body, name=name, grid=(rows // tm,), out_shape=jax.ShapeDtypeStruct(w.shape, BF),
                          in_specs=[spec], out_specs=spec, compiler_params=_cparams("parallel"))(w)


def _add_own_half(full, recv, half_axis, c, *, name):
    grid, own, part = _part_specs(recv.shape, half_axis, 0, 2, 4 * MIB)

    def body(c_ref, f_ref, r_ref, o_ref):
        o_ref[...] = (f_ref[...].astype(F32) + r_ref[...].astype(F32)).astype(BF)

    return pl.pallas_call(
        body, name=name, out_shape=jax.ShapeDtypeStruct(recv.shape, BF),
        grid_spec=pltpu.PrefetchScalarGridSpec(num_scalar_prefetch=1, grid=grid, in_specs=[part, own], out_specs=own),
        compiler_params=_cparams("parallel"),
    )(c.reshape(1), full, recv)


def _sum_own_block(mine_all, recvs, shard_axis, half_axis, j, c, *, name):
    shp = recvs[0].shape
    grid, own, mine = _part_specs(shp, shard_axis, 0, 4, 2 * MIB)
    _, _, place = _part_specs(shp, half_axis, 1, 4, 2 * MIB)
    big = list(shp)
    big[half_axis] *= 2

    def body(p_ref, a_ref, r0, r1, r2, o_ref):
        o_ref[...] = ((a_ref[...].astype(F32) + r0[...].astype(F32)) + r1[...].astype(F32)) + r2[...].astype(F32)

    return pl.pallas_call(
        body, name=name, out_shape=jax.ShapeDtypeStruct(tuple(big), F32),
        grid_spec=pltpu.PrefetchScalarGridSpec(
            num_scalar_prefetch=1, grid=grid, in_specs=[mine, own, own, own], out_specs=place),
        compiler_params=_cparams("parallel"),
    )(jnp.stack([j, c]), mine_all, *recvs)


def _adamw(w, g, m, v, *, name):
    shp = w.shape
    width = shp[-1] if len(shp) > 1 else shp[0]
    rows = 1
    for d in shp[:-1]:
        rows *= d
    if len(shp) == 1:
        rows = 1
    args = [a.reshape(rows, width) for a in (w, g, m, v)]
    tm = _block_rows(rows, width, 4, 2 * MIB)

    def body(w_ref, g_ref, m_ref, v_ref, d_ref, nm_ref, nv_ref):
        gv = g_ref[...]
        nm = ADAM_B1 * m_ref[...] + (1.0 - ADAM_B1) * gv
        nv = ADAM_B2 * v_ref[...] + (1.0 - ADAM_B2) * (gv * gv)
        m_hat = nm / (1.0 - ADAM_B1 ** ADAM_STEP)
        v_hat = nv / (1.0 - ADAM_B2 ** ADAM_STEP)
        d_ref[...] = -ADAM_LR * (m_hat / (jnp.sqrt(v_hat) + ADAM_EPS) + ADAM_WD * w_ref[...])
        nm_ref[...] = nm
        nv_ref[...] = nv

    spec = pl.BlockSpec((tm, width), lambda i: (i, 0))
    outs = pl.pallas_call(
        body, name=name, grid=(rows // tm,), out_shape=tuple([jax.ShapeDtypeStruct((rows, width), F32)] * 3),
        in_specs=[spec] * 4, out_specs=tuple([spec] * 3), compiler_params=_cparams("parallel"),
    )(*args)
    return tuple(o.reshape(shp) for o in outs)


def _matmul(a, b, *, ca, cb, tm, tn, tk, out_dtype, name, b_resident=False, side=None):
    m, kdim = a.shape[1 - ca], a.shape[ca]
    n = b.shape[1 - cb]
    nk = kdim // tk
    assert m % tm == 0 and n % tn == 0 and kdim % tk == 0
    n_sin = len(side.ins) if side else 0
    n_sout = len(side.out_shapes) if side else 0
    if b_resident:
        gi = lambda p, q, k: (q, p, k)
        grid = (n // tn, m // tm, nk)
    else:
        gi = lambda p, q, k: (p, q, k)
        grid = (m // tm, n // tn, nk)

    def body(*refs):
        a_ref, b_ref = refs[0], refs[1]
        s_ins = refs[2:2 + n_sin]
        o_ref = refs[2 + n_sin]
        s_outs = refs[3 + n_sin:3 + n_sin + n_sout]
        rest = refs[3 + n_sin + n_sout:]
        acc, sems = (rest[:1], rest[1:]) if nk > 1 else ((), rest)
        if side:
            first, middle, last = _side_at(grid)
            pl.when(first)(lambda: side.start(s_ins, s_outs, sems))
            if side.mid is not None:
                pl.when(middle)(lambda: side.mid(s_ins, s_outs, sems))
        part = _dot(a_ref[...], b_ref[...], ca, cb)
        if nk == 1:
            o_ref[...] = part.astype(o_ref.dtype)
        else:
            k = pl.program_id(2)

            @pl.when(k == 0)
            def _():
                acc[0][...] = part

            @pl.when(jnp.logical_and(k > 0, k < nk - 1))
            def _():
                acc[0][...] += part

            @pl.when(k == nk - 1)
            def _():
                o_ref[...] = (acc[0][...] + part).astype(o_ref.dtype)
        if side:
            pl.when(last)(lambda: side.finish(s_ins, s_outs, sems))

    def a_map(p, q, k):
        i, _, kk = gi(p, q, k)
        return (i, kk) if ca == 1 else (kk, i)

    def b_map(p, q, k):
        _, j, kk = gi(p, q, k)
        return (kk, j) if cb == 0 else (j, kk)

    def o_map(p, q, k):
        i, j, _ = gi(p, q, k)
        return (i, j)

    a_spec = pl.BlockSpec((tm, tk) if ca == 1 else (tk, tm), a_map)
    b_spec = pl.BlockSpec((tk, tn) if cb == 0 else (tn, tk), b_map)
    o_shape = jax.ShapeDtypeStruct((m, n), out_dtype)
    acc_shapes = [pltpu.VMEM((tm, tn), F32)] if nk > 1 else []
    if not side:
        return pl.pallas_call(
            body, name=name, grid=grid, out_shape=o_shape,
            in_specs=[a_spec, b_spec], out_specs=pl.BlockSpec((tm, tn), o_map), scratch_shapes=acc_shapes,
            compiler_params=_cparams("parallel", "parallel", "arbitrary"),
        )(a, b)
    return pl.pallas_call(
        body, name=name, grid=grid, out_shape=(o_shape, *side.out_shapes),
        in_specs=[a_spec, b_spec] + [HBM_SPEC] * n_sin,
        out_specs=(pl.BlockSpec((tm, tn), o_map), *([HBM_SPEC] * n_sout)),
        input_output_aliases={2 + i: 1 + o for i, o in side.aliases.items()},
        scratch_shapes=acc_shapes + side.sem_shapes,
        compiler_params=_cparams("arbitrary", "arbitrary", "arbitrary"),
    )(a, b, *side.ins)


def _in_proj_parts(h, w_parts, ids, prev, *, width, tm, tn, name, side=None):
    s_len, d = h.shape
    nb = w_parts[0].shape[1] // tn
    n_p = len(w_parts)
    n_prev = 0 if prev is None else 1
    n_sin = len(side.ins) if side else 0
    n_sout = len(side.out_shapes) if side else 0
    grid = (n_p, s_len // tm, nb)
    base = 1 + n_p + n_prev

    def body(ids_ref, *refs):
        h_ref, w_refs = refs[0], refs[1:1 + n_p]
        s_ins = refs[base:base + n_sin]
        o_ref = refs[base + n_sin]
        s_outs = refs[base + n_sin + 1:base + n_sin + 1 + n_sout]
        sems = refs[base + n_sin + 1 + n_sout:]
        if side:
            first, _, last = _side_at(grid)
            pl.when(first)(lambda: side.start(s_ins, s_outs, sems))
        for s in range(n_p):
            @pl.when(pl.program_id(0) == s)
            def _(s=s):
                o_ref[...] = _dot(h_ref[...], w_refs[s][...], 1, 0).astype(o_ref.dtype)
        if side:
            pl.when(last)(lambda: side.finish(s_ins, s_outs, sems))

    in_specs = [pl.BlockSpec((tm, d), lambda p, i, j, ids_ref: (i, 0))]
    for s in range(n_p):
        in_specs.append(pl.BlockSpec((d, tn), lambda p, i, j, ids_ref, s=s: (0, jnp.where(p == s, j, 0))))
    in_specs += [HBM_SPEC] * (n_prev + n_sin)
    o_spec = pl.BlockSpec((tm, tn), lambda p, i, j, ids_ref: (i, ids_ref[p] * nb + j))
    aliases = {2 + n_p: 0} if prev is not None else {}
    if side:
        aliases.update({1 + base + i: 1 + o for i, o in side.aliases.items()})
    outs = pl.pallas_call(
        body, name=name, out_shape=(jax.ShapeDtypeStruct((s_len, width), BF), *(side.out_shapes if side else [])),
        grid_spec=pltpu.PrefetchScalarGridSpec(
            num_scalar_prefetch=1, grid=grid, in_specs=in_specs, out_specs=(o_spec, *([HBM_SPEC] * n_sout)),
            scratch_shapes=side.sem_shapes if side else []),
        input_output_aliases=aliases, compiler_params=_cparams("arbitrary", "arbitrary", "arbitrary"),
    )(ids, h, *w_parts, *([prev] if prev is not None else []), *(side.ins if side else []))
    return outs


def _dgrad_in_parts(dproj, w_parts, ids, *, tm, tn, name, side=None):
    s_len = dproj.shape[0]
    d, ws = w_parts[0].shape
    n_p = len(w_parts)
    n_sin = len(side.ins) if side else 0
    n_sout = len(side.out_shapes) if side else 0
    grid = (s_len // tm, d // tn)

    def body(ids_ref, *refs):
        a_refs, w_refs = refs[:n_p], refs[n_p:2 * n_p]
        s_ins = refs[2 * n_p:2 * n_p + n_sin]
        o_ref = refs[2 * n_p + n_sin]
        s_outs = refs[2 * n_p + n_sin + 1:2 * n_p + n_sin + 1 + n_sout]
        sems = refs[2 * n_p + n_sin + 1 + n_sout:]
        if side:
            first, _, last = _side_at(grid)
            pl.when(first)(lambda: side.start(s_ins, s_outs, sems))
        tot = _dot(a_refs[0][...], w_refs[0][...], 1, 1)
        for s in range(1, n_p):
            tot = tot + _dot(a_refs[s][...], w_refs[s][...], 1, 1)
        o_ref[...] = tot.astype(o_ref.dtype)
        if side:
            pl.when(last)(lambda: side.finish(s_ins, s_outs, sems))

    in_specs = [pl.BlockSpec((tm, ws), lambda i, j, ids_ref, s=s: (i, ids_ref[s])) for s in range(n_p)]
    in_specs += [pl.BlockSpec((tn, ws), lambda i, j, ids_ref: (j, 0))] * n_p
    in_specs += [HBM_SPEC] * n_sin
    aliases = {1 + 2 * n_p + i: 1 + o for i, o in side.aliases.items()} if side else {}
    return pl.pallas_call(
        body, name=name, out_shape=(jax.ShapeDtypeStruct((s_len, d), BF), *(side.out_shapes if side else [])),
        grid_spec=pltpu.PrefetchScalarGridSpec(
            num_scalar_prefetch=1, grid=grid, in_specs=in_specs,
            out_specs=(pl.BlockSpec((tm, tn), lambda i, j, ids_ref: (i, j)), *([HBM_SPEC] * n_sout)),
            scratch_shapes=side.sem_shapes if side else []),
        input_output_aliases=aliases, compiler_params=_cparams("arbitrary", "arbitrary"),
    )(ids, *([dproj] * n_p), *w_parts, *(side.ins if side else []))


def _prenorm(x, ctx, g_pre, ss, *, tm):
    n, d = x.shape
    l = ctx.shape[0]
    nc = l // tm

    def body(x_ref, c_ref, g_ref, ss_ref, h_ref):
        t = pl.program_id(0)

        def go(src, scale, shift):
            v = src[...]
            rstd = lax.rsqrt(jnp.mean(v * v, axis=-1, keepdims=True) + EPS)
            h_ref[...] = ((v * rstd * g_ref[...]) * (1.0 + scale) + shift).astype(BF)

        @pl.when(t < nc)
        def _():
            go(c_ref, ss_ref[2:3, :], ss_ref[3:4, :])

        @pl.when(t >= nc)
        def _():
            go(x_ref, ss_ref[0:1, :], ss_ref[1:2, :])

    return pl.pallas_call(
        body, name="prenorm", grid=((l + n) // tm,), out_shape=jax.ShapeDtypeStruct((l + n, d), BF),
        in_specs=[pl.BlockSpec((tm, d), lambda t: (jnp.maximum(t - nc, 0), 0)),
                  pl.BlockSpec((tm, d), lambda t: (jnp.minimum(t, nc - 1), 0)),
                  pl.BlockSpec((1, d), lambda t: (0, 0)), pl.BlockSpec((4, d), lambda t: (0, 0))],
        out_specs=pl.BlockSpec((tm, d), lambda t: (t, 0)), compiler_params=_cparams("parallel"),
    )(x, ctx, g_pre, ss)


def _swap32(v):
    lane = lax.broadcasted_iota(jnp.int32, v.shape, 1)
    return jnp.where((lane % 64) < 32, pltpu.roll(v, 96, 1), pltpu.roll(v, 32, 1))


def _qk_prep(proj, gain, ctab, stab, *, row0, col0, width, nrows, tm, name):
    cw = min(512, width)
    rb0, cb0 = row0 // tm, col0 // cw
    assert row0 % tm == 0 and col0 % cw == 0 and width % cw == 0 and nrows % tm == 0

    def body(p_ref, g_ref, c_ref, s_ref, o_ref):
        for hd in range(cw // HEAD_DIM):
            cols = slice(hd * HEAD_DIM, (hd + 1) * HEAD_DIM)
            v = p_ref[:, cols].astype(F32)
            rstd = lax.rsqrt(jnp.mean(v * v, axis=-1, keepdims=True) + EPS)
            yv = v * rstd * g_ref[...]
            o_ref[:, cols] = (yv * c_ref[...] + _swap32(yv) * s_ref[...]).astype(BF)

    return pl.pallas_call(
        body, name=name, grid=(nrows // tm, width // cw), out_shape=jax.ShapeDtypeStruct((nrows, width), BF),
        in_specs=[pl.BlockSpec((tm, cw), lambda i, j: (i + rb0, j + cb0)), pl.BlockSpec((1, HEAD_DIM), lambda i, j: (0, 0)),
                  pl.BlockSpec((tm, HEAD_DIM), lambda i, j: (i, 0)), pl.BlockSpec((tm, HEAD_DIM), lambda i, j: (i, 0))],
        out_specs=pl.BlockSpec((tm, cw), lambda i, j: (i, j)), compiler_params=_cparams("parallel", "parallel"),
    )(proj, gain, ctab, stab)


def _zero_rows(rows, shape, *, name):
    def body(o_ref):
        o_ref[...] = jnp.zeros_like(o_ref)

    return pl.pallas_call(body, name=name, grid=(1,), out_shape=jax.ShapeDtypeStruct(shape, BF),
                          out_specs=pl.BlockSpec((rows, shape[1]), lambda i: (0, 0)))()


def _qk_bwd(dy, proj, gain, ctab, stab, into, *, row0, col0, tm, name):
    nrows, width = dy.shape
    cw = min(512, width)
    rb0, cb0 = row0 // tm, col0 // cw

    def body(d_ref, p_ref, g_ref, c_ref, s_ref, into_ref, o_ref, dg_ref):
        @pl.when(jnp.logical_and(pl.program_id(0) == 0, pl.program_id(1) == 0))
        def _():
            dg_ref[...] = jnp.zeros_like(dg_ref)

        dg = jnp.zeros((1, HEAD_DIM), F32)
        for hd in range(cw // HEAD_DIM):
            cols = slice(hd * HEAD_DIM, (hd + 1) * HEAD_DIM)
            v = p_ref[:, cols].astype(F32)
            rstd = lax.rsqrt(jnp.mean(v * v, axis=-1, keepdims=True) + EPS)
            nv = v * rstd
            d = d_ref[:, cols]
            dyu = d * c_ref[...] + _swap32(d * s_ref[...])
            dg = dg + jnp.sum(dyu * nv, axis=0, keepdims=True)
            dn = dyu * g_ref[...]
            o_ref[:, cols] = (rstd * (dn - nv * jnp.mean(dn * nv, axis=-1, keepdims=True))).astype(BF)
        dg_ref[0:1, :] += dg

    placed = pl.BlockSpec((tm, cw), lambda i, j: (i + rb0, j + cb0))
    return pl.pallas_call(
        body, name=name, grid=(nrows // tm, width // cw),
        out_shape=(jax.ShapeDtypeStruct(into.shape, BF), jax.ShapeDtypeStruct((8, HEAD_DIM), F32)),
        in_specs=[pl.BlockSpec((tm, cw), lambda i, j: (i, j)), placed, pl.BlockSpec((1, HEAD_DIM), lambda i, j: (0, 0)),
                  pl.BlockSpec((tm, HEAD_DIM), lambda i, j: (i, 0)), pl.BlockSpec((tm, HEAD_DIM), lambda i, j: (i, 0)), HBM_SPEC],
        out_specs=(placed, pl.BlockSpec((8, HEAD_DIM), lambda i, j: (0, 0))),
        input_output_aliases={5: 0}, compiler_params=_cparams("arbitrary", "arbitrary"),
    )(dy, proj, gain, ctab, stab, into)


def _flash_fwd(qr, k_all, vt_ctx, vt_lat, *, tq, scale, side=None):
    n, aw = qr.shape
    s_len, kvw = k_all.shape
    l = vt_ctx.shape[-1]
    n_j, tk = vt_lat.shape[1], vt_lat.shape[-1]
    kvh = kvw // HEAD_DIM
    n_i = n // tq
    gw = GQA_GROUP * HEAD_DIM
    c2 = scale * LOG2E
    n_sin = len(side.ins) if side else 0
    n_sout = len(side.out_shapes) if side else 0

    def body(*refs):
        q_ref, k_ref, vc_ref, vl_ref = refs[:4]
        s_ins = refs[4:4 + n_sin]
        o_ref, lse_ref = refs[4 + n_sin:6 + n_sin]
        s_outs = refs[6 + n_sin:6 + n_sin + n_sout]
        acc_ref, m_ref, l_ref = refs[6 + n_sin + n_sout:9 + n_sin + n_sout]
        sems = refs[9 + n_sin + n_sout:]
        if side:
            first, middle, last = _side_at((kvh, n_i))
            pl.when(first)(lambda: side.start(s_ins, s_outs, sems))
            if side.mid is not None:
                pl.when(middle)(lambda: side.mid(s_ins, s_outs, sems))
        acc_ref[...] = jnp.zeros_like(acc_ref)
        l_ref[...] = jnp.zeros_like(l_ref)
        m_ref[...] = jnp.full(m_ref.shape, -1e30, F32)

        def tile(kj, vtj):
            for g in range(GQA_GROUP):
                st = _dot(kj, q_ref[:, g * HEAD_DIM:(g + 1) * HEAD_DIM], 1, 1) * c2
                m_old = m_ref[g]
                m_new = jnp.maximum(m_old, jnp.max(st, axis=0, keepdims=True))
                alpha = jnp.exp2(m_old - m_new)
                pt = jnp.exp2(st - m_new)
                l_ref[g] = alpha * l_ref[g] + jnp.sum(pt, axis=0, keepdims=True)
                m_ref[g] = m_new
                acc_ref[g] = acc_ref[g] * alpha + _dot(vtj, pt.astype(BF), 1, 0)

        tile(k_ref[0:l, :], vc_ref[0, 0])

        def step(j, carry):
            tile(k_ref[pl.ds(pl.multiple_of(l + j * tk, min(l, tk)), tk), :], vl_ref[0, j])
            return carry

        lax.fori_loop(0, n_j, step, 0)
        for g in range(GQA_GROUP):
            o_ref[:, g * HEAD_DIM:(g + 1) * HEAD_DIM] = (acc_ref[g] / l_ref[g]).T.astype(BF)
            lse_ref[0, g, 0] = m_ref[g] + jnp.log(l_ref[g]) * LOG2E
        if side:
            pl.when(last)(lambda: side.finish(s_ins, s_outs, sems))

    return pl.pallas_call(
        body, name="flash_fwd", grid=(kvh, n_i),
        out_shape=(jax.ShapeDtypeStruct((n, aw), BF), jax.ShapeDtypeStruct((kvh, GQA_GROUP, n_i, 1, tq), F32),
                   *(side.out_shapes if side else [])),
        in_specs=[pl.BlockSpec((tq, gw), lambda h, i: (i, h)), pl.BlockSpec((s_len, HEAD_DIM), lambda h, i: (0, h)),
                  pl.BlockSpec((1, 1, HEAD_DIM, l), lambda h, i: (h, 0, 0, 0)),
                  pl.BlockSpec((1, n_j, HEAD_DIM, tk), lambda h, i: (h, 0, 0, 0))] + [HBM_SPEC] * n_sin,
        out_specs=(pl.BlockSpec((tq, gw), lambda h, i: (i, h)),
                   pl.BlockSpec((1, GQA_GROUP, 1, 1, tq), lambda h, i: (h, 0, i, 0, 0)), *([HBM_SPEC] * n_sout)),
        input_output_aliases={4 + i: 2 + o for i, o in side.aliases.items()} if side else {},
        scratch_shapes=[pltpu.VMEM((GQA_GROUP, HEAD_DIM, tq), F32), pltpu.VMEM((GQA_GROUP, 1, tq), F32),
                        pltpu.VMEM((GQA_GROUP, 1, tq), F32)] + (side.sem_shapes if side else []),
        compiler_params=_cparams("arbitrary", "arbitrary") if side else _cparams("parallel", "parallel"),
    )(qr, k_all, vt_ctx, vt_lat, *(side.ins if side else []))


def _attn_delta(o, do, *, tq):
    n, aw = o.shape
    kvh = aw // (GQA_GROUP * HEAD_DIM)
    gw = GQA_GROUP * HEAD_DIM

    def body(o_ref, do_ref, d_ref):
        for g in range(GQA_GROUP):
            cols = slice(g * HEAD_DIM, (g + 1) * HEAD_DIM)
            prod = o_ref[:, cols].astype(F32) * do_ref[:, cols].astype(F32)
            d_ref[0, g, 0] = jnp.sum(prod.T, axis=0, keepdims=True)

    return pl.pallas_call(
        body, name="attn_delta", grid=(kvh, n // tq),
        out_shape=jax.ShapeDtypeStruct((kvh, GQA_GROUP, n // tq, 1, tq), F32),
        in_specs=[pl.BlockSpec((tq, gw), lambda h, i: (i, h)), pl.BlockSpec((tq, gw), lambda h, i: (i, h))],
        out_specs=pl.BlockSpec((1, GQA_GROUP, 1, 1, tq), lambda h, i: (h, 0, i, 0, 0)),
        compiler_params=_cparams("parallel", "parallel"),
    )(o, do)


def _flash_bwd(qr, do, k_all, proj, lse, delta, into, *, v_col0, tq, tk, scale, side=None):
    n, aw = qr.shape
    s_len, kvw = k_all.shape
    kvh = kvw // HEAD_DIM
    n_i, n_j = n // tq, s_len // tk
    gw = GQA_GROUP * HEAD_DIM
    vb0 = v_col0 // HEAD_DIM
    n_sin = len(side.ins) if side else 0
    n_sout = len(side.out_shapes) if side else 0
    c2 = scale * LOG2E

    def body(*refs):
        q_ref, do_ref, k_ref, v_ref, lse_ref, dl_ref = refs[:6]
        s_ins = refs[7:7 + n_sin]
        dq_ref, dk_ref, dvo_ref = refs[7 + n_sin:10 + n_sin]
        s_outs = refs[10 + n_sin:10 + n_sin + n_sout]
        dv_ref = refs[10 + n_sin + n_sout]
        sems = refs[11 + n_sin + n_sout:]
        jj = pl.program_id(1)
        if side:
            first, _, last = _side_at((kvh, n_j))
            pl.when(first)(lambda: side.start(s_ins, s_outs, sems))

        @pl.when(jj == 0)
        def _():
            dq_ref[...] = jnp.zeros_like(dq_ref)

        kj = k_ref[...]
        vj = v_ref[...]
        dk_ref[...] = jnp.zeros_like(dk_ref)
        dv_ref[...] = jnp.zeros_like(dv_ref)

        def step(i, carry):
            rows = pl.ds(pl.multiple_of(i * tq, tq), tq)
            dv_part = dk_part = None
            for g in range(GQA_GROUP):
                cols = slice(g * HEAD_DIM, (g + 1) * HEAD_DIM)
                qg = q_ref[rows, cols]
                dog = do_ref[rows, cols]
                pt = jnp.exp2(_dot(kj, qg, 1, 1) * c2 - lse_ref[0, g, i])
                dst = (pt * (_dot(vj, dog, 1, 1) - dl_ref[0, g, i])).astype(BF)
                dv_g = _dot(pt.astype(BF), dog, 1, 0)
                dk_g = _dot(dst, qg, 1, 0)
                dv_part = dv_g if dv_part is None else dv_part + dv_g
                dk_part = dk_g if dk_part is None else dk_part + dk_g
                dq_ref[rows, cols] += _dot(dst, kj, 0, 0)
            dv_ref[...] += dv_part
            dk_ref[...] += dk_part
            return carry

        lax.fori_loop(0, n_i, step, 0)
        dk_ref[...] = dk_ref[...] * scale
        dvo_ref[...] = dv_ref[...].astype(BF)

        @pl.when(jj == n_j - 1)
        def _():
            dq_ref[...] = dq_ref[...] * scale

        if side:
            pl.when(last)(lambda: side.finish(s_ins, s_outs, sems))

    stat_spec = pl.BlockSpec((1, GQA_GROUP, n_i, 1, tq), lambda h, j: (h, 0, 0, 0, 0))
    kv_spec = pl.BlockSpec((tk, HEAD_DIM), lambda h, j: (j, h))
    v_spec = pl.BlockSpec((tk, HEAD_DIM), lambda h, j: (j, vb0 + h))
    q_spec = pl.BlockSpec((n, gw), lambda h, j: (0, h))
    aliases = {6: 2}
    if side:
        aliases.update({7 + i: 3 + o for i, o in side.aliases.items()})
    return pl.pallas_call(
        body, name="flash_bwd", grid=(kvh, n_j),
        out_shape=(jax.ShapeDtypeStruct((n, aw), F32), jax.ShapeDtypeStruct((s_len, kvw), F32),
                   jax.ShapeDtypeStruct(into.shape, BF), *(side.out_shapes if side else [])),
        in_specs=[q_spec, q_spec, kv_spec, v_spec, stat_spec, stat_spec, HBM_SPEC] + [HBM_SPEC] * n_sin,
        out_specs=(q_spec, kv_spec, v_spec, *([HBM_SPEC] * n_sout)),
        input_output_aliases=aliases,
        scratch_shapes=[pltpu.VMEM((tk, HEAD_DIM), F32)] + (side.sem_shapes if side else []),
        compiler_params=_cparams("arbitrary", "arbitrary"),
    )(qr, do, k_all, proj, lse, delta, into, *(side.ins if side else []))


def _pool_window(r, gi, l, n, tm):
    win = tm + 2 * POOL_HALO
    start = jnp.clip(l + r * tm - POOL_HALO, l, l + n - win)
    half = lax.shift_left(jnp.int32(1), gi)
    tok = r * tm + lax.broadcasted_iota(jnp.int32, (tm, win), 0)
    src = (start - l) + lax.broadcasted_iota(jnp.int32, (tm, win), 1)
    tok1 = r * tm + lax.broadcasted_iota(jnp.int32, (tm, 1), 0)
    cnt = (jnp.minimum(tok1 + half, n) - jnp.maximum(tok1 - half, 0)).astype(F32)
    return start, win, tok, src, half, cnt


def _pool_fwd(proj, pool_w, *, l, n, col0, tm):
    s_len = proj.shape[0]
    pg = pool_w.shape[-1]
    cb0 = col0 // pg
    assert col0 % pg == 0 and l % POOL_HALO == 0 and n >= tm + 2 * POOL_HALO

    def body(u_ref, w_ref, d_ref, po_ref):
        gi, r = pl.program_id(0), pl.program_id(1)
        start, win, tok, src, half, cnt = _pool_window(r, gi, l, n, tm)
        band = jnp.logical_and(src >= tok - half, src < tok + half).astype(BF)
        uw = u_ref[pl.ds(pl.multiple_of(start, POOL_HALO), win), :]
        ut = u_ref[pl.ds(pl.multiple_of(l + r * tm, POOL_HALO), tm), :].astype(F32)
        dv = (_dot(band, uw, 1, 0) / cnt - ut).astype(BF)
        d_ref[...] = dv
        po_ref[...] = _dot(dv, w_ref[0], 1, 0).astype(BF)

    return pl.pallas_call(
        body, name="pool_fwd", grid=(N_POOL_GROUPS, n // tm),
        out_shape=(jax.ShapeDtypeStruct((n, N_POOL_GROUPS * pg), BF), jax.ShapeDtypeStruct((n, N_POOL_GROUPS * pg), BF)),
        in_specs=[pl.BlockSpec((s_len, pg), lambda g, r: (0, cb0 + g)), pl.BlockSpec((1, pg, pg), lambda g, r: (g, 0, 0))],
        out_specs=(pl.BlockSpec((tm, pg), lambda g, r: (r, g)), pl.BlockSpec((tm, pg), lambda g, r: (r, g))),
        compiler_params=_cparams("parallel", "parallel"),
    )(proj, pool_w)


def _pool_bwd_map(dval, pool_w, *, n, col0, tm):
    pg = pool_w.shape[-1]
    cb0 = col0 // pg

    def body(d_ref, w_ref, o_ref):
        gi, r = pl.program_id(0), pl.program_id(1)
        half = lax.shift_left(jnp.int32(1), gi)
        tok1 = r * tm + lax.broadcasted_iota(jnp.int32, (tm, 1), 0)
        cnt = (jnp.minimum(tok1 + half, n) - jnp.maximum(tok1 - half, 0)).astype(F32)
        o_ref[...] = (_dot(d_ref[...], w_ref[0], 1, 1) / cnt).astype(BF)

    return pl.pallas_call(
        body, name="pool_bwd_map", grid=(N_POOL_GROUPS, n // tm),
        out_shape=jax.ShapeDtypeStruct((n, N_POOL_GROUPS * pg), BF),
        in_specs=[pl.BlockSpec((tm, pg), lambda g, r: (r, cb0 + g)), pl.BlockSpec((1, pg, pg), lambda g, r: (g, 0, 0))],
        out_specs=pl.BlockSpec((tm, pg), lambda g, r: (r, g)), compiler_params=_cparams("parallel", "parallel"),
    )(dval, pool_w)


def _pool_bwd_window(dds, into, *, n, tm, row0, col0):
    pg = dds.shape[1] // N_POOL_GROUPS
    rb0, cb0 = row0 // tm, col0 // pg
    assert row0 % tm == 0 and col0 % pg == 0

    def body(d_ref, into_ref, o_ref):
        gi, r = pl.program_id(0), pl.program_id(1)
        start, win, tok, src, half, cnt = _pool_window(r, gi, 0, n, tm)
        band = jnp.logical_and(tok >= src - half, tok < src + half).astype(BF)
        dw = d_ref[pl.ds(pl.multiple_of(start, POOL_HALO), win), :]
        dt = d_ref[pl.ds(pl.multiple_of(r * tm, POOL_HALO), tm), :].astype(F32)
        o_ref[...] = (_dot(band, dw, 1, 0) - dt * cnt).astype(BF)

    return pl.pallas_call(
        body, name="pool_bwd_window", grid=(N_POOL_GROUPS, n // tm), out_shape=jax.ShapeDtypeStruct(into.shape, BF),
        in_specs=[pl.BlockSpec((n, pg), lambda g, r: (0, g)), HBM_SPEC],
        out_specs=pl.BlockSpec((tm, pg), lambda g, r: (r + rb0, g + cb0)),
        input_output_aliases={1: 0}, compiler_params=_cparams("parallel", "parallel"),
    )(dds, into)


def _pool_wgrad(dmat, dval, *, col0, tk):
    n, pw = dmat.shape
    pg = pw // N_POOL_GROUPS
    cb0 = col0 // pg
    nk = n // tk

    def body(a_ref, b_ref, o_ref, acc):
        k = pl.program_id(1)

        @pl.when(k == 0)
        def _():
            acc[...] = jnp.zeros_like(acc)

        acc[...] += _dot(a_ref[...], b_ref[...], 0, 0)

        @pl.when(k == nk - 1)
        def _():
            o_ref[0] = acc[...].astype(BF)

    return pl.pallas_call(
        body, name="pool_wgrad", grid=(N_POOL_GROUPS, nk), out_shape=jax.ShapeDtypeStruct((N_POOL_GROUPS, pg, pg), BF),
        in_specs=[pl.BlockSpec((tk, pg), lambda g, k: (k, g)), pl.BlockSpec((tk, pg), lambda g, k: (k, cb0 + g))],
        out_specs=pl.BlockSpec((1, pg, pg), lambda g, k: (g, 0, 0)), scratch_shapes=[pltpu.VMEM((pg, pg), F32)],
        compiler_params=_cparams("parallel", "arbitrary"),
    )(dmat, dval)


def _merge(attn_o, po_raw, proj, pool_scale, *, l, ga0, gp0, tm):
    n, aw = attn_o.shape
    bw = aw // 2
    rb0 = l // tm
    ga_b, gp_b = ga0 // bw, gp0 // bw
    assert ga0 % bw == 0 and gp0 % bw == 0 and l % tm == 0

    def body(a_ref, p_ref, g_ref, ps_ref, y_ref):
        cb = pl.program_id(1)
        sg, _ = _silu_and_grad(g_ref[...].astype(F32))

        @pl.when(cb < 2)
        def _():
            y_ref[...] = (a_ref[...].astype(F32) * sg).astype(BF)

        @pl.when(cb >= 2)
        def _():
            y_ref[...] = (p_ref[...].astype(F32) * ps_ref[...] * sg).astype(BF)

    return pl.pallas_call(
        body, name="merge", grid=(n // tm, 4), out_shape=jax.ShapeDtypeStruct((n, 2 * aw), BF),
        in_specs=[pl.BlockSpec((tm, bw), lambda i, cb: (i, jnp.minimum(cb, 1))),
                  pl.BlockSpec((tm, bw), lambda i, cb: (i, jnp.maximum(cb - 2, 0))),
                  pl.BlockSpec((tm, bw), lambda i, cb: (i + rb0, jnp.where(cb < 2, ga_b + cb, gp_b + cb - 2))),
                  pl.BlockSpec((1, bw), lambda i, cb: (0, jnp.maximum(cb - 2, 0)))],
        out_specs=pl.BlockSpec((tm, bw), lambda i, cb: (i, cb)), compiler_params=_cparams("parallel", "arbitrary"),
    )(attn_o, po_raw, proj, pool_scale)


def _merge_bwd(dy, attn_o, po_raw, proj, pool_scale, into, *, l, ga0, gp0, tm):
    n, aw = attn_o.shape
    bw = aw // 2
    rb0 = l // tm
    ga_b, gp_b = ga0 // bw, gp0 // bw

    def body(dy_ref, a_ref, p_ref, g_ref, ps_ref, into_ref, dv_ref, dg_ref, dps_ref):
        cb, i = pl.program_id(0), pl.program_id(1)
        sg, sgrad = _silu_and_grad(g_ref[...].astype(F32))
        dyv = dy_ref[...].astype(F32)

        @pl.when(cb < 2)
        def _():
            dv_ref[...] = (dyv * sg).astype(BF)
            dg_ref[...] = (dyv * a_ref[...].astype(F32) * sgrad).astype(BF)

        @pl.when(cb >= 2)
        def _():
            @pl.when(i == 0)
            def _():
                dps_ref[...] = jnp.zeros_like(dps_ref)

            pr = p_ref[...].astype(F32)
            dpo = dyv * sg
            dv_ref[...] = (dpo * ps_ref[...]).astype(BF)
            dg_ref[...] = (dyv * (pr * ps_ref[...]) * sgrad).astype(BF)
            dps_ref[0:1, :] += jnp.sum(dpo * pr, axis=0, keepdims=True)

    blk = pl.BlockSpec((tm, bw), lambda cb, i: (i, cb))
    gate_blk = pl.BlockSpec((tm, bw), lambda cb, i: (i + rb0, jnp.where(cb < 2, ga_b + cb, gp_b + cb - 2)))
    return pl.pallas_call(
        body, name="merge_bwd", grid=(4, n // tm),
        out_shape=(jax.ShapeDtypeStruct((n, 2 * aw), BF), jax.ShapeDtypeStruct(into.shape, BF),
                   jax.ShapeDtypeStruct((8, aw), F32)),
        in_specs=[blk, pl.BlockSpec((tm, bw), lambda cb, i: (i, jnp.minimum(cb, 1))),
                  pl.BlockSpec((tm, bw), lambda cb, i: (i, jnp.maximum(cb - 2, 0))), gate_blk,
                  pl.BlockSpec((1, bw), lambda cb, i: (0, jnp.maximum(cb - 2, 0))), HBM_SPEC],
        out_specs=(blk, gate_blk, pl.BlockSpec((8, bw), lambda cb, i: (0, jnp.maximum(cb - 2, 0)))),
        input_output_aliases={5: 1}, compiler_params=_cparams("arbitrary", "arbitrary"),
    )(dy, attn_o, po_raw, proj, pool_scale, into)


def _post(x, out, target, gate, g_post, *, tm):
    n, d = x.shape

    def body(x_ref, o_ref, t_ref, gate_ref, gp_ref, dxn_ref, dout_ref, st_ref):
        @pl.when(pl.program_id(0) == 0)
        def _():
            st_ref[...] = jnp.zeros_like(st_ref)

        ov = o_ref[...]
        rstd = lax.rsqrt(jnp.mean(ov * ov, axis=-1, keepdims=True) + EPS)
        on = ov * rstd
        rn = on * gp_ref[...]
        err = (x_ref[...] + gate_ref[...] * rn) - t_ref[...]
        dxn = err / d
        dxn_ref[...] = dxn
        drn = dxn * gate_ref[...]
        don = drn * gp_ref[...]
        dout_ref[...] = (rstd * (don - on * jnp.mean(don * on, axis=-1, keepdims=True))).astype(BF)
        st_ref[0:1, :] += jnp.sum(dxn * rn, axis=0, keepdims=True)
        st_ref[1:2, :] += jnp.sum(drn * on, axis=0, keepdims=True)
        st_ref[2:3, :] += jnp.sum(err * err, axis=0, keepdims=True)

    row = pl.BlockSpec((tm, d), lambda i: (i, 0))
    vec = pl.BlockSpec((1, d), lambda i: (0, 0))
    return pl.pallas_call(
        body, name="post", grid=(n // tm,),
        out_shape=(jax.ShapeDtypeStruct((n, d), F32), jax.ShapeDtypeStruct((n, d), BF), jax.ShapeDtypeStruct((8, d), F32)),
        in_specs=[row, row, row, vec, vec], out_specs=(row, row, pl.BlockSpec((8, d), lambda i: (0, 0))),
        compiler_params=_cparams("arbitrary"),
    )(x, out, target, gate, g_post)


def _prenorm_bwd(x, ctx, dh, dxn, g_pre, ss, *, tm):
    n, d = x.shape
    l = ctx.shape[0]
    nc = l // tm

    def body(x_ref, c_ref, dh_ref, dxn_ref, g_ref, ss_ref, gx_ref, st_ref):
        t = pl.program_id(0)

        @pl.when(t == 0)
        def _():
            st_ref[...] = jnp.zeros_like(st_ref)

        def go(src, scale, row):
            v = src[...]
            dhv = dh_ref[...].astype(F32)
            rstd = lax.rsqrt(jnp.mean(v * v, axis=-1, keepdims=True) + EPS)
            xn = v * rstd
            st_ref[row:row + 1, :] += jnp.sum(dhv, axis=0, keepdims=True)
            st_ref[row + 1:row + 2, :] += jnp.sum(dhv * (xn * g_ref[...]), axis=0, keepdims=True)
            dg = dhv * (1.0 + scale)
            st_ref[4:5, :] += jnp.sum(dg * xn, axis=0, keepdims=True)
            dxn_ = dg * g_ref[...]
            return rstd * (dxn_ - xn * jnp.mean(dxn_ * xn, axis=-1, keepdims=True))

        @pl.when(t < nc)
        def _():
            go(c_ref, ss_ref[2:3, :], 2)

        @pl.when(t >= nc)
        def _():
            gx_ref[...] = dxn_ref[...] + go(x_ref, ss_ref[0:1, :], 0)

    lat = pl.BlockSpec((tm, d), lambda t: (jnp.maximum(t - nc, 0), 0))
    return pl.pallas_call(
        body, name="prenorm_bwd", grid=((l + n) // tm,),
        out_shape=(jax.ShapeDtypeStruct((n, d), F32), jax.ShapeDtypeStruct((8, d), F32)),
        in_specs=[lat, pl.BlockSpec((tm, d), lambda t: (jnp.minimum(t, nc - 1), 0)),
                  pl.BlockSpec((tm, d), lambda t: (t, 0)), lat,
                  pl.BlockSpec((1, d), lambda t: (0, 0)), pl.BlockSpec((4, d), lambda t: (0, 0))],
        out_specs=(lat, pl.BlockSpec((8, d), lambda t: (0, 0))), compiler_params=_cparams("arbitrary"),
    )(x, ctx, dh, dxn, g_pre, ss)


def _ada_fwd(craw, w_ada, b_loc, *, tn):
    d, wn = w_ada.shape

    def body(c_ref, w_ref, b_ref, o_ref):
        act, _ = _silu_and_grad(c_ref[...])
        o_ref[...] = _dot(act.astype(BF), w_ref[...].astype(BF), 1, 0) + b_ref[...]

    return pl.pallas_call(
        body, name="ada_fwd", grid=(wn // tn,), out_shape=jax.ShapeDtypeStruct((16, wn), F32),
        in_specs=[pl.BlockSpec((16, d), lambda j: (0, 0)), pl.BlockSpec((d, tn), lambda j: (0, j)),
                  pl.BlockSpec((1, tn), lambda j: (0, j))],
        out_specs=pl.BlockSpec((16, tn), lambda j: (0, j)), compiler_params=_cparams("parallel"),
    )(craw, w_ada, b_loc)


def _ada_bwd(craw_t, dm, w, m, v, *, tn):
    d, wn = w.shape

    def body(ct_ref, dm_ref, w_ref, m_ref, v_ref, g_ref, dl_ref, nm_ref, nv_ref, da_ref):
        @pl.when(pl.program_id(0) == 0)
        def _():
            da_ref[...] = jnp.zeros_like(da_ref)

        act, _ = _silu_and_grad(ct_ref[...])
        dmb = dm_ref[...].astype(BF)
        wv = w_ref[...]
        gv = _dot(act.astype(BF), dmb, 1, 0)
        da_ref[...] += _dot(dmb, wv.astype(BF), 1, 1)
        nm = ADAM_B1 * m_ref[...] + (1.0 - ADAM_B1) * gv
        nv = ADAM_B2 * v_ref[...] + (1.0 - ADAM_B2) * (gv * gv)
        m_hat = nm / (1.0 - ADAM_B1 ** ADAM_STEP)
        v_hat = nv / (1.0 - ADAM_B2 ** ADAM_STEP)
        g_ref[...] = gv
        dl_ref[...] = -ADAM_LR * (m_hat / (jnp.sqrt(v_hat) + ADAM_EPS) + ADAM_WD * wv)
        nm_ref[...] = nm
        nv_ref[...] = nv

    col = pl.BlockSpec((d, tn), lambda j: (0, j))
    return pl.pallas_call(
        body, name="ada_bwd", grid=(wn // tn,),
        out_shape=tuple([jax.ShapeDtypeStruct((d, wn), F32)] * 4) + (jax.ShapeDtypeStruct((16, d), F32),),
        in_specs=[pl.BlockSpec((d, 16), lambda j: (0, 0)), pl.BlockSpec((16, tn), lambda j: (0, j)), col, col, col],
        out_specs=(col, col, col, col, pl.BlockSpec((16, d), lambda j: (0, 0))),
        compiler_params=_cparams("arbitrary"),
    )(craw_t, dm, w, m, v)


def _reduce_small(gath, d3):
    t = gath.shape[-1]

    def body(g_ref, dm_ref, s_ref):
        tot = g_ref[0]
        for b in range(1, 8):
            tot = tot + g_ref[b]
        s_ref[...] = tot
        dm_ref[...] = jnp.zeros_like(dm_ref)
        for b in range(8):
            dm_ref[b:b + 1, :] = g_ref[b][:, 0:d3]
        dm_ref[8:9, :] = tot[:, d3:2 * d3]

    return pl.pallas_call(
        body, name="reduce_small", out_shape=(jax.ShapeDtypeStruct((16, d3), F32), jax.ShapeDtypeStruct((1, t), F32)),
        in_specs=[VMEM_SPEC], out_specs=(VMEM_SPEC, VMEM_SPEC),
    )(gath)


def _cctx_grad(parts, c_ctx):
    def body(p_ref, c_ref, o_ref):
        tot = (p_ref[0] + p_ref[1]) + (p_ref[2] + p_ref[3])
        _, sgrad = _silu_and_grad(c_ref[...])
        o_ref[...] = tot * sgrad

    return pl.pallas_call(
        body, name="cctx_grad", out_shape=jax.ShapeDtypeStruct(c_ctx.shape, F32),
        in_specs=[VMEM_SPEC, VMEM_SPEC], out_specs=VMEM_SPEC,
    )(parts, c_ctx)


def _rope_tables(n, l):
    rows = n // GRID_W
    row = jnp.repeat(jnp.arange(rows, dtype=F32), GRID_W)
    col = jnp.tile(jnp.arange(GRID_W, dtype=F32), rows)
    inv = ROPE_THETA ** (-jnp.arange(ROPE_PAIRS, dtype=F32) / ROPE_PAIRS)
    ang = jnp.concatenate([row[:, None] * inv, col[:, None] * inv], axis=-1)
    cos, sin = jnp.cos(ang), jnp.sin(ang)
    cr, cc, sr, sc = cos[:, :ROPE_PAIRS], cos[:, ROPE_PAIRS:], sin[:, :ROPE_PAIRS], sin[:, ROPE_PAIRS:]
    ctab = jnp.concatenate([cr, cr, cc, cc], axis=-1)
    stab = jnp.concatenate([-sr, sr, -sc, sc], axis=-1)
    ctab = jnp.concatenate([jnp.ones((l, HEAD_DIM), F32), ctab], axis=0)
    stab = jnp.concatenate([jnp.zeros((l, HEAD_DIM), F32), stab], axis=0)
    return ctab, stab


def kernel(x, c, ctx, c_ctx, w_ada, b_ada, norm_pre, norm_post, w_in, q_norm, k_norm, pool_w, pool_scale, w_out, loss_target, m_c_ctx, m_w_ada, m_b_ada, m_norm_pre, m_norm_post, m_w_in, m_q_norm, m_k_norm, m_pool_w, m_pool_scale, m_w_out, v_c_ctx, v_w_ada, v_b_ada, v_norm_pre, v_norm_post, v_w_in, v_q_norm, v_k_norm, v_pool_w, v_pool_scale, v_w_out):
    x2, ctx2, tgt = x[0], ctx[0], loss_target[0]
    n, d = x2.shape
    l = ctx2.shape[0]
    s_len = l + n
    aw = d // 2
    pw = d - aw
    n_heads = aw // HEAD_DIM
    kvw = (n_heads // GQA_GROUP) * HEAD_DIM
    pg = pw // N_POOL_GROUPS
    k0, v0, ga0 = aw, aw + kvw, aw + 2 * kvw
    up0, gp0 = ga0 + aw, ga0 + aw + pw
    in_w = gp0 + pw
    d3 = 3 * d
    ada_w = w_ada.shape[-1]
    px, py, pc = lax.axis_index("x"), lax.axis_index("y"), lax.axis_index("c")
    chip = 2 * px + py
    tr = min(256, l)
    tq = min(512, n)
    tk = min(256, l)
    ts = s_len // 8 if (s_len // 8) % 16 == 0 else tr
    scale = HEAD_DIM ** -0.5

    c_all = _allgather_small(c, chips_only=False, name="gather_c").reshape(8, d)
    craw = jnp.concatenate([c_all, c_ctx.reshape(1, d), jnp.zeros((7, d), F32)], axis=0)
    b_loc = lax.dynamic_slice(b_ada, (0, chip * ada_w), (1, ada_w))
    mod_part = _ada_fwd(craw, w_ada[0], b_loc, tn=min(512, ada_w))
    mod_all = _allgather_small(mod_part, chips_only=True, name="gather_mod")
    mod_all = jnp.transpose(mod_all, (1, 0, 2)).reshape(16, d3)
    me = 4 * px + 2 * py + pc
    mod_b = lax.dynamic_slice(mod_all, (me, 0), (1, d3))
    mod_c = mod_all[8:9]
    ss = jnp.concatenate([mod_b[:, d:2 * d], mod_b[:, 0:d], mod_c[:, d:2 * d], mod_c[:, 0:d]], axis=0)
    gate = mod_b[:, 2 * d:]

    w_own = _cast_bf16(w_in[0], name="cast_w_in")
    later_weights = _gather_side([_cast_into(w_out[0], 0, chip, name="cast_w_out"),
                                  _cast_into(pool_w[0], 1, chip, name="cast_pool_w")], [(0, 1), (1, 2)])
    ids = jnp.stack([chip, jnp.bitwise_xor(chip, 2), jnp.bitwise_xor(chip, 1), jnp.bitwise_xor(chip, 3)])
    ws = in_w // 4
    tn_p = ws // 3 if (ws // 3) % HEAD_DIM == 0 else ws

    ctab, stab = _rope_tables(n, l)
    h = _prenorm(x2, ctx2, norm_pre, ss, tm=tr)
    proj, w_x, w_y = _in_proj_parts(h, [w_own], ids[0:1], None, width=in_w, tm=ts, tn=tn_p, name="in_proj_own",
                                    side=_gather_parts_side(w_own, (0, 1)))
    proj, w_d = _in_proj_parts(h, [w_x, w_y], ids[1:3], proj, width=in_w, tm=ts, tn=tn_p, name="in_proj_xy",
                               side=_gather_parts_side(w_own, (2,)))
    (proj,) = _in_proj_parts(h, [w_d], ids[3:4], proj, width=in_w, tm=ts, tn=tn_p, name="in_proj_diag")
    w_parts = [w_own, w_x, w_y, w_d]
    qr = _qk_prep(proj, q_norm, ctab[l:], stab[l:], row0=l, col0=0, width=aw, nrows=n, tm=tr, name="q_prep")
    kr = _qk_prep(proj, k_norm, ctab, stab, row0=0, col0=k0, width=kvw, nrows=s_len, tm=tr, name="k_prep")
    v_all = proj[:, v0:v0 + kvw]
    tkf = min(2048, n)
    vt_ctx = jnp.transpose(v_all[:l].reshape(1, l, kvw // HEAD_DIM, HEAD_DIM), (2, 0, 3, 1))
    vt_lat = jnp.transpose(v_all[l:].reshape(n // tkf, tkf, kvw // HEAD_DIM, HEAD_DIM), (2, 0, 3, 1))
    attn_o, lse, w_out_f, pool_f = _flash_fwd(qr, kr, vt_ctx, vt_lat, tq=tq, scale=scale, side=later_weights)
    dmat, po_raw = _pool_fwd(proj, pool_f, l=l, n=n, col0=up0, tm=tr)
    yv = _merge(attn_o, po_raw, proj, pool_scale, l=l, ga0=ga0, gp0=gp0, tm=tr)
    out = _matmul(yv, w_out_f, ca=1, cb=0, tm=min(512, n), tn=min(512, d), tk=d, out_dtype=F32, name="out_proj")

    dxn, d_out, post_st = _post(x2, out, tgt, gate, norm_post, tm=min(128, n))
    loss = 0.5 * jnp.sum(post_st[2]) / d
    loss = lax.psum(loss, ("x", "y", "c"))
    g_wout = _matmul(yv, d_out, ca=0, cb=0, tm=min(512, d), tn=min(512, d), tk=n, out_dtype=BF, name="wgrad_out")
    dy, from_sib_wout = _matmul(d_out, w_out_f, ca=1, cb=1, tm=min(512, n), tn=min(512, d), tk=d, out_dtype=BF,
                                name="dgrad_out", side=_pair_side([g_wout], [1]))
    pair_wout = _add_own_half(g_wout, from_sib_wout, 1, pc, name="pair_sum_1")
    dproj = _zero_rows(l, (s_len, in_w), name="dproj_init")
    dval, dproj, dps = _merge_bwd(dy, attn_o, po_raw, proj, pool_scale, dproj, l=l, ga0=ga0, gp0=gp0, tm=tr)
    g_pool = _pool_wgrad(dmat, dval, col0=aw, tk=min(512, n))
    dds = _pool_bwd_map(dval, pool_f, n=n, col0=aw, tm=tr)
    dproj = _pool_bwd_window(dds, dproj, n=n, tm=tr, row0=l, col0=up0)
    delta = _attn_delta(attn_o, dval, tq=tq)
    tqb = min(2048, n)
    stat_shape = (kvw // HEAD_DIM, GQA_GROUP, n // tqb, 1, tqb)
    dq, dk, dproj, *from_chips_wout = _flash_bwd(
        qr, dval, kr, proj, lse.reshape(stat_shape), delta.reshape(stat_shape), dproj, v_col0=v0, tq=tqb, tk=tk,
        scale=scale, side=_chips_side([pair_wout], [0]))
    red_wout = _sum_own_block(pair_wout, from_chips_wout, 0, 1, chip, pc, name="chip_sum_1")
    dproj, dgq = _qk_bwd(dq, proj, q_norm, ctab[l:], stab[l:], dproj, row0=l, col0=0, tm=tr, name="q_bwd")
    dproj, dgk = _qk_bwd(dk, proj, k_norm, ctab, stab, dproj, row0=0, col0=k0, tm=tr, name="k_bwd")
    g_win = _matmul(h, dproj, ca=0, cb=0, tm=min(512, d), tn=min(512, in_w), tk=s_len, out_dtype=BF, name="wgrad_in")
    from_sib = _run_side(_pair_side([g_win, g_pool], [0, 2]), name="reduce_pair")
    pair_win = _add_own_half(g_win, from_sib[0], 0, pc, name="pair_sum_0")
    pair_pool = _add_own_half(g_pool, from_sib[1], 2, pc, name="pair_sum_2")
    dh, *from_chips = _dgrad_in_parts(dproj, w_parts, ids, tm=ts, tn=min(512, d), name="dgrad_in",
                                      side=_chips_side([pair_win, pair_pool], [1, 1]))
    red_win = _sum_own_block(pair_win, from_chips[0:3], 1, 0, chip, pc, name="chip_sum_0")
    red_pool = _sum_own_block(pair_pool, from_chips[3:6], 1, 2, chip, pc, name="chip_sum_2")
    grad_x, pre_st = _prenorm_bwd(x2, ctx2, dh, dxn, norm_pre, ss, tm=min(128, l))

    zero_d = jnp.zeros((1, d), F32)
    packed = jnp.concatenate([pre_st[0:1], pre_st[1:2], post_st[0:1], pre_st[2:3], pre_st[3:4], zero_d,
                              pre_st[4:5], post_st[1:2], dgq[0:1], dgk[0:1], dps[0:1]], axis=1)
    gath = _allgather_small(packed, chips_only=False, name="gather_small")
    dm, sums = _reduce_small(gath, d3)
    o1 = 2 * d3
    g_npre, g_npost = sums[:, o1:o1 + d], sums[:, o1 + d:o1 + 2 * d]
    g_q, g_k = sums[:, o1 + 2 * d:o1 + 2 * d + HEAD_DIM], sums[:, o1 + 2 * d + HEAD_DIM:o1 + 2 * d + 2 * HEAD_DIM]
    g_ps = sums[:, o1 + 2 * d + 2 * HEAD_DIM:]
    g_bada = sums[:, 0:d3] + sums[:, d3:2 * d3]
    dm_loc = lax.dynamic_slice(dm, (0, chip * ada_w), (16, ada_w))
    g_wada, dl_wada, nm_wada, nv_wada, dact = _ada_bwd(craw.T, dm_loc, w_ada[0], m_w_ada[0], v_w_ada[0], tn=128)
    cparts = _allgather_small(dact[8:9], chips_only=True, name="gather_cctx")
    g_cctx = _cctx_grad(cparts, c_ctx.reshape(1, d)).reshape(d)

    g_win_s, g_wout_s, g_pool_s = _run_side(_join_side([red_win, red_wout, red_pool], [0, 1, 2]), name="reduce_join")

    def upd(w, g, m, v, name):
        return _adamw(w, g.reshape(w.shape), m, v, name=name)

    grads = {"c_ctx": g_cctx, "b_ada": g_bada, "norm_pre": g_npre, "norm_post": g_npost, "w_in": g_win_s[None],
             "q_norm": g_q, "k_norm": g_k, "pool_w": g_pool_s[None], "pool_scale": g_ps, "w_out": g_wout_s[None]}
    res = {"w_ada": (g_wada[None], dl_wada[None], nm_wada[None], nv_wada[None])}
    given = {"c_ctx": (c_ctx, m_c_ctx, v_c_ctx), "b_ada": (b_ada, m_b_ada, v_b_ada),
             "norm_pre": (norm_pre, m_norm_pre, v_norm_pre), "norm_post": (norm_post, m_norm_post, v_norm_post),
             "w_in": (w_in, m_w_in, v_w_in), "q_norm": (q_norm, m_q_norm, v_q_norm), "k_norm": (k_norm, m_k_norm, v_k_norm),
             "pool_w": (pool_w, m_pool_w, v_pool_w), "pool_scale": (pool_scale, m_pool_scale, v_pool_scale),
             "w_out": (w_out, m_w_out, v_w_out)}
    for nme, (w, m, v) in given.items():
        g = grads[nme].reshape(w.shape)
        res[nme] = (g,) + upd(w, g, m, v, "adamw_" + nme)
    order = ["c_ctx", "w_ada", "b_ada", "norm_pre", "norm_post", "w_in", "q_norm", "k_norm", "pool_w", "pool_scale", "w_out"]
    return (loss, grad_x[None], *[res[k][0] for k in order], *[res[k][1] for k in order],
            *[res[k][2] for k in order], *[res[k][3] for k in order])
```

```python
import functools

import jax
import jax.numpy as jnp
from jax import lax
from jax.experimental import pallas as pl
from jax.experimental.pallas import tpu as pltpu

F32 = jnp.float32
BF = jnp.bfloat16
MESH = pl.DeviceIdType.MESH

HEAD_DIM = 128
GQA_GROUP = 4
GRID_W = 64
ROPE_PAIRS = HEAD_DIM // 4
ROPE_THETA = 10000.0
EPS = 1e-6
N_POOL_GROUPS = 4
POOL_HALO = 128
ADAM_LR = 0.001
ADAM_B1 = 0.9
ADAM_B2 = 0.999
ADAM_EPS = 1e-08
ADAM_WD = 0.01
ADAM_STEP = 10
LOG2E = 1.4426950408889634
MIB = 2 ** 20
VMEM_LIMIT = 48 * MIB
CHIP_MASKS = ((1, 0, 0), (0, 1, 0), (1, 1, 0))
ALL_MASKS = ((0, 0, 1), (0, 1, 0), (0, 1, 1), (1, 0, 0), (1, 0, 1), (1, 1, 0), (1, 1, 1))
HBM_SPEC = pl.BlockSpec(memory_space=pl.ANY)
VMEM_SPEC = pl.BlockSpec(memory_space=pltpu.VMEM)


def _cparams(*sem):
    return pltpu.CompilerParams(dimension_semantics=sem, vmem_limit_bytes=VMEM_LIMIT)


def _sigmoid(v):
    return 1.0 / (1.0 + jnp.exp(-v))


def _silu_and_grad(v):
    s = _sigmoid(v)
    return v * s, s * (1.0 + v * (1.0 - s))


def _dot(a, b, ca, cb):
    return lax.dot_general(a, b, (((ca,), (cb,)), ((), ())), preferred_element_type=F32)


def _block_rows(rows, width, itemsize=4, target=MIB):
    best = 8
    for t in range(8, rows + 1, 8):
        if rows % t == 0 and t * width * itemsize <= target:
            best = t
    return best if rows % 8 == 0 else rows


def _my_pos():
    return lax.axis_index("x"), lax.axis_index("y"), lax.axis_index("c")


def _flip(pos, mask):
    return tuple(jnp.bitwise_xor(p, m) if m else p for p, m in zip(pos, mask))


def _allgather_small(v, *, chips_only, name):
    r, w = v.shape
    masks = CHIP_MASKS if chips_only else ALL_MASKS
    nslot = 4 if chips_only else 8

    def slot(pos):
        return 2 * pos[0] + pos[1] if chips_only else 4 * pos[0] + 2 * pos[1] + pos[2]

    def body(v_ref, o_ref, send_sems, recv_sems, local_sem):
        me = _my_pos()

        def copy(k, block_of, to):
            return pltpu.make_async_remote_copy(
                src_ref=v_ref, dst_ref=o_ref.at[slot(block_of)], send_sem=send_sems.at[k], recv_sem=recv_sems.at[k],
                device_id=to, device_id_type=MESH)

        mine = pltpu.make_async_copy(v_ref, o_ref.at[slot(me)], local_sem)
        mine.start()
        sends = [copy(k, me, _flip(me, m)) for k, m in enumerate(masks)]
        for cp in sends:
            cp.start()
        for k, m in enumerate(masks):
            copy(k, _flip(me, m), me).wait_recv()
        for cp in sends:
            cp.wait_send()
        mine.wait()

    return pl.pallas_call(
        body, name=name, out_shape=jax.ShapeDtypeStruct((nslot, r, w), v.dtype),
        in_specs=[VMEM_SPEC], out_specs=VMEM_SPEC,
        scratch_shapes=[pltpu.SemaphoreType.DMA((len(masks),)), pltpu.SemaphoreType.DMA((len(masks),)),
                        pltpu.SemaphoreType.DMA],
    )(v)


def _sl(ref, axis, start, size):
    idx = [slice(None)] * len(ref.shape)
    idx[axis] = pl.ds(start, size)
    return ref.at[tuple(idx)]


class _Side:
    def __init__(self, ins, out_shapes, aliases, sem_shapes, start, finish, mid=None):
        self.ins, self.out_shapes, self.aliases, self.sem_shapes = list(ins), list(out_shapes), dict(aliases), list(sem_shapes)
        self.start, self.finish, self.mid = start, finish, mid


def _run_side(side, *, name):
    n_in, n_out = len(side.ins), len(side.out_shapes)

    def body(*refs):
        parts = refs[:n_in], refs[n_in:n_in + n_out], refs[n_in + n_out:]
        side.start(*parts)
        if side.mid is not None:
            side.mid(*parts)
        side.finish(*parts)

    return pl.pallas_call(
        body, name=name, out_shape=tuple(side.out_shapes), in_specs=[HBM_SPEC] * n_in, out_specs=tuple([HBM_SPEC] * n_out),
        input_output_aliases=side.aliases, scratch_shapes=side.sem_shapes,
    )(*side.ins)


def _side_at(grid):
    pids = [pl.program_id(a) for a in range(len(grid))]

    def at(values):
        cond = pids[0] == values[0]
        for p, v in zip(pids[1:], values[1:]):
            cond = jnp.logical_and(cond, p == v)
        return cond

    zeros = [0] * len(grid)
    return at(zeros), at([grid[0] // 2] + zeros[1:]), at([g - 1 for g in grid])


def _gather_side(shards, layouts):
    nm = len(shards)

    def copies(outs, sems):
        send_sems, recv_sems, fsend_sems, frecv_sems = sems
        x, y, c = _my_pos()
        me = (x, y, c)
        sib = (x, y, 1 - c)
        sends, landed, fwds, from_sib = [], [], [], []
        for m in range(nm):
            a_s, a_h = layouts[m]
            ns, nh = outs[m].shape[a_s] // 4, outs[m].shape[a_h] // 2

            def region(chip_pos, half, m=m, a_s=a_s, a_h=a_h, ns=ns, nh=nh):
                j = 2 * chip_pos[0] + chip_pos[1]
                return _sl(_sl(outs[m], a_s, j * ns, ns), a_h, half * nh, nh)

            for k, mask in enumerate(CHIP_MASKS):
                other = _flip(me, mask)

                def rc(ref, ssem, rsem, to):
                    return pltpu.make_async_remote_copy(src_ref=ref, dst_ref=ref, send_sem=ssem, recv_sem=rsem,
                                                        device_id=to, device_id_type=MESH)

                sends.append(rc(region(me, c), send_sems.at[m, k], recv_sems.at[m, k], other))
                landed.append(rc(region(other, c), send_sems.at[m, k], recv_sems.at[m, k], me))
                fwds.append(rc(region(other, c), fsend_sems.at[m, k], frecv_sems.at[m, k], sib))
                from_sib.append(rc(region(other, 1 - c), fsend_sems.at[m, k], frecv_sems.at[m, k], sib))
        return sends, landed, fwds, from_sib

    def start(ins, outs, sems):
        for cp in copies(outs, sems)[0]:
            cp.start()

    def mid(ins, outs, sems):
        _, landed, fwds, _ = copies(outs, sems)
        for arrived, fw in zip(landed, fwds):
            arrived.wait_recv()
            fw.start()

    def finish(ins, outs, sems):
        sends, _, fwds, from_sib = copies(outs, sems)
        for cp in from_sib:
            cp.wait_recv()
        for cp in sends + fwds:
            cp.wait_send()

    return _Side(shards, [jax.ShapeDtypeStruct(s.shape, s.dtype) for s in shards], {i: i for i in range(nm)},
                 [pltpu.SemaphoreType.DMA((nm, 3))] * 4, start, finish, mid)


def _both(s1, s2):
    n1, m1, k1 = len(s1.ins), len(s1.out_shapes), len(s1.sem_shapes)

    def split(fn1, fn2):
        def run(ins, outs, sems):
            fn1(ins[:n1], outs[:m1], sems[:k1])
            fn2(ins[n1:], outs[m1:], sems[k1:])
        return run

    aliases = dict(s1.aliases)
    aliases.update({n1 + i: m1 + o for i, o in s2.aliases.items()})
    return _Side(s1.ins + s2.ins, s1.out_shapes + s2.out_shapes, aliases, s1.sem_shapes + s2.sem_shapes,
                 split(s1.start, s2.start), split(s1.finish, s2.finish))


def _swap_side(mats):
    nm = len(mats)

    def copies(ins, outs, sems):
        x, y, c = _my_pos()
        return [pltpu.make_async_remote_copy(src_ref=ins[m], dst_ref=outs[m], send_sem=sems[0].at[m], recv_sem=sems[1].at[m],
                                             device_id=(x, y, 1 - c), device_id_type=MESH) for m in range(nm)]

    def start(ins, outs, sems):
        for cp in copies(ins, outs, sems):
            cp.start()

    def finish(ins, outs, sems):
        for cp in copies(ins, outs, sems):
            cp.wait()

    return _Side(mats, [jax.ShapeDtypeStruct(s.shape, s.dtype) for s in mats], {},
                 [pltpu.SemaphoreType.DMA((nm,))] * 2, start, finish)


def _gather_parts_side(w_own, mask_ids):
    nt = len(mask_ids)
    nh = w_own.shape[0] // 2

    def copies(ins, outs, sems):
        send_sems, recv_sems, fsend_sems, frecv_sems = sems
        x, y, c = _my_pos()
        me, sib = (x, y, c), (x, y, 1 - c)
        sends, landed, fwds, from_sib = [], [], [], []
        for t, k in enumerate(mask_ids):
            mine = _sl(outs[t], 0, c * nh, nh)
            theirs = _sl(outs[t], 0, (1 - c) * nh, nh)

            def rc(src, dst, ssem, rsem, to):
                return pltpu.make_async_remote_copy(src_ref=src, dst_ref=dst, send_sem=ssem, recv_sem=rsem,
                                                    device_id=to, device_id_type=MESH)

            sends.append(rc(_sl(ins[0], 0, c * nh, nh), mine, send_sems.at[t], recv_sems.at[t], _flip(me, CHIP_MASKS[k])))
            landed.append(rc(mine, mine, send_sems.at[t], recv_sems.at[t], me))
            fwds.append(rc(mine, mine, fsend_sems.at[t], frecv_sems.at[t], sib))
            from_sib.append(rc(theirs, theirs, fsend_sems.at[t], frecv_sems.at[t], sib))
        return sends, landed, fwds, from_sib

    def start(ins, outs, sems):
        for cp in copies(ins, outs, sems)[0]:
            cp.start()

    def finish(ins, outs, sems):
        sends, landed, fwds, from_sib = copies(ins, outs, sems)
        for arrived, fw in zip(landed, fwds):
            arrived.wait_recv()
            fw.start()
        for cp in from_sib:
            cp.wait_recv()
        for cp in sends + fwds:
            cp.wait_send()

    return _Side([w_own], [jax.ShapeDtypeStruct(w_own.shape, w_own.dtype)] * nt, {},
                 [pltpu.SemaphoreType.DMA((nt,))] * 4, start, finish)


def _pair_side(mats, half_axes):
    nm = len(mats)

    def copies(ins, outs, sems):
        x, y, c = _my_pos()
        cps = []
        for m in range(nm):
            nh = ins[m].shape[half_axes[m]] // 2
            cps.append(pltpu.make_async_remote_copy(
                src_ref=_sl(ins[m], half_axes[m], (1 - c) * nh, nh), dst_ref=outs[m],
                send_sem=sems[0].at[m], recv_sem=sems[1].at[m], device_id=(x, y, 1 - c), device_id_type=MESH))
        return cps

    def start(ins, outs, sems):
        for cp in copies(ins, outs, sems):
            cp.start()

    def finish(ins, outs, sems):
        for cp in copies(ins, outs, sems):
            cp.wait()

    out_shapes = []
    for s, a_h in zip(mats, half_axes):
        shp = list(s.shape)
        shp[a_h] //= 2
        out_shapes.append(jax.ShapeDtypeStruct(tuple(shp), s.dtype))
    return _Side(mats, out_shapes, {}, [pltpu.SemaphoreType.DMA((nm,))] * 2, start, finish)


def _chips_side(mats, shard_axes):
    nm = len(mats)

    def copies(ins, outs, sems):
        me = _my_pos()
        cps = []
        for m in range(nm):
            ns = ins[m].shape[shard_axes[m]] // 4
            for k, mask in enumerate(CHIP_MASKS):
                other = _flip(me, mask)
                cps.append(pltpu.make_async_remote_copy(
                    src_ref=_sl(ins[m], shard_axes[m], (2 * other[0] + other[1]) * ns, ns), dst_ref=outs[3 * m + k],
                    send_sem=sems[0].at[m, k], recv_sem=sems[1].at[m, k], device_id=other, device_id_type=MESH))
        return cps

    def start(ins, outs, sems):
        for cp in copies(ins, outs, sems):
            cp.start()

    def finish(ins, outs, sems):
        for cp in copies(ins, outs, sems):
            cp.wait()

    out_shapes = []
    for s, a_s in zip(mats, shard_axes):
        shp = list(s.shape)
        shp[a_s] //= 4
        out_shapes += [jax.ShapeDtypeStruct(tuple(shp), s.dtype)] * 3
    return _Side(mats, out_shapes, {}, [pltpu.SemaphoreType.DMA((nm, 3))] * 2, start, finish)


def _join_side(halves, half_axes):
    nm = len(halves)

    def copies(outs, sems):
        x, y, c = _my_pos()
        mine, theirs = [], []
        for m in range(nm):
            nh = outs[m].shape[half_axes[m]] // 2
            for half, lst in ((c, mine), (1 - c, theirs)):
                ref = _sl(outs[m], half_axes[m], half * nh, nh)
                lst.append(pltpu.make_async_remote_copy(
                    src_ref=ref, dst_ref=ref, send_sem=sems[0].at[m], recv_sem=sems[1].at[m],
                    device_id=(x, y, 1 - c), device_id_type=MESH))
        return mine, theirs

    def start(ins, outs, sems):
        for cp in copies(outs, sems)[0]:
            cp.start()

    def finish(ins, outs, sems):
        mine, theirs = copies(outs, sems)
        for cp in mine:
            cp.wait_send()
        for cp in theirs:
            cp.wait_recv()

    return _Side(halves, [jax.ShapeDtypeStruct(s.shape, s.dtype) for s in halves], {i: i for i in range(nm)},
                 [pltpu.SemaphoreType.DMA((nm,))] * 2, start, finish)


def _part_specs(shp, axis, slot, itemsize, target=MIB):
    if len(shp) == 2:
        rows, width = shp
        tm = _block_rows(rows, width, itemsize, target)
        nb = rows // tm
        own = pl.BlockSpec((tm, width), lambda i, p: (i, 0))
        if axis == 1:
            part = pl.BlockSpec((tm, width), lambda i, p: (i, p[slot]))
        else:
            part = pl.BlockSpec((tm, width), lambda i, p: (p[slot] * nb + i, 0))
        return (nb,), own, part
    assert len(shp) == 3 and axis in (1, 2)
    g, rows, width = shp
    own = pl.BlockSpec((1, rows, width), lambda i, p: (i, 0, 0))
    if axis == 1:
        part = pl.BlockSpec((1, rows, width), lambda i, p: (i, p[slot], 0))
    else:
        part = pl.BlockSpec((1, rows, width), lambda i, p: (i, 0, p[slot]))
    return (g,), own, part


def _cast_into(w, axis, p, *, name):
    grid, own, part = _part_specs(w.shape, axis, 0, 4, 2 * MIB)
    big = list(w.shape)
    big[axis] *= 4

    def body(p_ref, w_ref, o_ref):
        o_ref[...] = w_ref[...].astype(BF)

    return pl.pallas_call(
        body, name=name, out_shape=jax.ShapeDtypeStruct(tuple(big), BF),
        grid_spec=pltpu.PrefetchScalarGridSpec(num_scalar_prefetch=1, grid=grid, in_specs=[own], out_specs=part),
        compiler_params=_cparams("parallel"),
    )(p.reshape(1), w)


def _cast_bf16(w, *, name):
    rows, width = w.shape
    tm = _block_rows(rows, width, 4, 2 * MIB)

    def body(w_ref, o_ref):
        o_ref[...] = w_ref[...].astype(BF)

    spec = pl.BlockSpec((tm, width), lambda i: (i, 0))
    return pl.pallas_call(body, name=name, grid=(rows // tm,), out_shape=jax.ShapeDtypeStruct(w.shape, BF),
                          in_specs=[spec], out_specs=spec, compiler_params=_cparams("parallel"))(w)


def _add2(a, b, *, name):
    rows, width = a.shape
    tm = _block_rows(rows, width, 2, 4 * MIB)

    def body(a_ref, b_ref, o_ref):
        o_ref[...] = (a_ref[...].astype(F32) + b_ref[...].astype(F32)).astype(BF)

    spec = pl.BlockSpec((tm, width), lambda i: (i, 0))
    return pl.pallas_call(body, name=name, grid=(rows // tm,), out_shape=jax.ShapeDtypeStruct(a.shape, BF),
                          in_specs=[spec, spec], out_specs=spec, compiler_params=_cparams("parallel"))(a, b)


def _add_own_half(full, recv, half_axis, c, *, name):
    grid, own, part = _part_specs(recv.shape, half_axis, 0, 2, 4 * MIB)

    def body(c_ref, f_ref, r_ref, o_ref):
        o_ref[...] = (f_ref[...].astype(F32) + r_ref[...].astype(F32)).astype(BF)

    return pl.pallas_call(
        body, name=name, out_shape=jax.ShapeDtypeStruct(recv.shape, BF),
        grid_spec=pltpu.PrefetchScalarGridSpec(num_scalar_prefetch=1, grid=grid, in_specs=[part, own], out_specs=own),
        compiler_params=_cparams("parallel"),
    )(c.reshape(1), full, recv)


def _sum_own_block(mine_all, recvs, shard_axis, half_axis, j, c, *, name):
    shp = recvs[0].shape
    grid, own, mine = _part_specs(shp, shard_axis, 0, 4, 2 * MIB)
    _, _, place = _part_specs(shp, half_axis, 1, 4, 2 * MIB)
    big = list(shp)
    big[half_axis] *= 2

    def body(p_ref, a_ref, r0, r1, r2, o_ref):
        o_ref[...] = ((a_ref[...].astype(F32) + r0[...].astype(F32)) + r1[...].astype(F32)) + r2[...].astype(F32)

    return pl.pallas_call(
        body, name=name, out_shape=jax.ShapeDtypeStruct(tuple(big), F32),
        grid_spec=pltpu.PrefetchScalarGridSpec(
            num_scalar_prefetch=1, grid=grid, in_specs=[mine, own, own, own], out_specs=place),
        compiler_params=_cparams("parallel"),
    )(jnp.stack([j, c]), mine_all, *recvs)


def _adamw(w, g, m, v, *, name):
    shp = w.shape
    width = shp[-1] if len(shp) > 1 else shp[0]
    rows = 1
    for d in shp[:-1]:
        rows *= d
    if len(shp) == 1:
        rows = 1
    args = [a.reshape(rows, width) for a in (w, g, m, v)]
    tm = _block_rows(rows, width, 4, 2 * MIB)

    def body(w_ref, g_ref, m_ref, v_ref, d_ref, nm_ref, nv_ref):
        gv = g_ref[...]
        nm = ADAM_B1 * m_ref[...] + (1.0 - ADAM_B1) * gv
        nv = ADAM_B2 * v_ref[...] + (1.0 - ADAM_B2) * (gv * gv)
        m_hat = nm / (1.0 - ADAM_B1 ** ADAM_STEP)
        v_hat = nv / (1.0 - ADAM_B2 ** ADAM_STEP)
        d_ref[...] = -ADAM_LR * (m_hat / (jnp.sqrt(v_hat) + ADAM_EPS) + ADAM_WD * w_ref[...])
        nm_ref[...] = nm
        nv_ref[...] = nv

    spec = pl.BlockSpec((tm, width), lambda i: (i, 0))
    outs = pl.pallas_call(
        body, name=name, grid=(rows // tm,), out_shape=tuple([jax.ShapeDtypeStruct((rows, width), F32)] * 3),
        in_specs=[spec] * 4, out_specs=tuple([spec] * 3), compiler_params=_cparams("parallel"),
    )(*args)
    return tuple(o.reshape(shp) for o in outs)


def _matmul(a, b, *, ca, cb, tm, tn, tk, out_dtype, name, b_resident=False, side=None):
    m, kdim = a.shape[1 - ca], a.shape[ca]
    n = b.shape[1 - cb]
    nk = kdim // tk
    assert m % tm == 0 and n % tn == 0 and kdim % tk == 0
    n_sin = len(side.ins) if side else 0
    n_sout = len(side.out_shapes) if side else 0
    if b_resident:
        gi = lambda p, q, k: (q, p, k)
        grid = (n // tn, m // tm, nk)
    else:
        gi = lambda p, q, k: (p, q, k)
        grid = (m // tm, n // tn, nk)

    def body(*refs):
        a_ref, b_ref = refs[0], refs[1]
        s_ins = refs[2:2 + n_sin]
        o_ref = refs[2 + n_sin]
        s_outs = refs[3 + n_sin:3 + n_sin + n_sout]
        rest = refs[3 + n_sin + n_sout:]
        acc, sems = (rest[:1], rest[1:]) if nk > 1 else ((), rest)
        if side:
            first, middle, last = _side_at(grid)
            pl.when(first)(lambda: side.start(s_ins, s_outs, sems))
            if side.mid is not None:
                pl.when(middle)(lambda: side.mid(s_ins, s_outs, sems))
        part = _dot(a_ref[...], b_ref[...], ca, cb)
        if nk == 1:
            o_ref[...] = part.astype(o_ref.dtype)
        else:
            k = pl.program_id(2)

            @pl.when(k == 0)
            def _():
                acc[0][...] = part

            @pl.when(jnp.logical_and(k > 0, k < nk - 1))
            def _():
                acc[0][...] += part

            @pl.when(k == nk - 1)
            def _():
                o_ref[...] = (acc[0][...] + part).astype(o_ref.dtype)
        if side:
            pl.when(last)(lambda: side.finish(s_ins, s_outs, sems))

    def a_map(p, q, k):
        i, _, kk = gi(p, q, k)
        return (i, kk) if ca == 1 else (kk, i)

    def b_map(p, q, k):
        _, j, kk = gi(p, q, k)
        return (kk, j) if cb == 0 else (j, kk)

    def o_map(p, q, k):
        i, j, _ = gi(p, q, k)
        return (i, j)

    a_spec = pl.BlockSpec((tm, tk) if ca == 1 else (tk, tm), a_map)
    b_spec = pl.BlockSpec((tk, tn) if cb == 0 else (tn, tk), b_map)
    o_shape = jax.ShapeDtypeStruct((m, n), out_dtype)
    acc_shapes = [pltpu.VMEM((tm, tn), F32)] if nk > 1 else []
    if not side:
        return pl.pallas_call(
            body, name=name, grid=grid, out_shape=o_shape,
            in_specs=[a_spec, b_spec], out_specs=pl.BlockSpec((tm, tn), o_map), scratch_shapes=acc_shapes,
            compiler_params=_cparams("parallel", "parallel", "arbitrary"),
        )(a, b)
    return pl.pallas_call(
        body, name=name, grid=grid, out_shape=(o_shape, *side.out_shapes),
        in_specs=[a_spec, b_spec] + [HBM_SPEC] * n_sin,
        out_specs=(pl.BlockSpec((tm, tn), o_map), *([HBM_SPEC] * n_sout)),
        input_output_aliases={2 + i: 1 + o for i, o in side.aliases.items()},
        scratch_shapes=acc_shapes + side.sem_shapes,
        compiler_params=_cparams("arbitrary", "arbitrary", "arbitrary"),
    )(a, b, *side.ins)


def _wgrad_rows(a, b, half, *, tm, tn, name, side=None):
    kdim, m = a.shape
    n = b.shape[1]
    nb = (m // 2) // tm
    grid = (nb, n // tn)
    n_sin = len(side.ins) if side else 0
    n_sout = len(side.out_shapes) if side else 0

    def body(p_ref, a_ref, b_ref, *refs):
        s_ins, o_ref = refs[:n_sin], refs[n_sin]
        s_outs, sems = refs[n_sin + 1:n_sin + 1 + n_sout], refs[n_sin + 1 + n_sout:]
        if side:
            first, _, last = _side_at(grid)
            pl.when(first)(lambda: side.start(s_ins, s_outs, sems))
        o_ref[...] = _dot(a_ref[...], b_ref[...], 0, 0).astype(o_ref.dtype)
        if side:
            pl.when(last)(lambda: side.finish(s_ins, s_outs, sems))

    outs = pl.pallas_call(
        body, name=name, out_shape=(jax.ShapeDtypeStruct((m // 2, n), BF), *(side.out_shapes if side else [])),
        grid_spec=pltpu.PrefetchScalarGridSpec(
            num_scalar_prefetch=1, grid=grid,
            in_specs=[pl.BlockSpec((kdim, tm), lambda i, j, p: (0, p[0] * nb + i)),
                      pl.BlockSpec((kdim, tn), lambda i, j, p: (0, j))] + [HBM_SPEC] * n_sin,
            out_specs=(pl.BlockSpec((tm, tn), lambda i, j, p: (i, j)), *([HBM_SPEC] * n_sout)),
            scratch_shapes=side.sem_shapes if side else []),
        input_output_aliases={3 + i: 1 + o for i, o in side.aliases.items()} if side else {},
        compiler_params=_cparams("arbitrary", "arbitrary"),
    )(half.reshape(1), a, b, *(side.ins if side else []))
    return outs if side else outs[0]


def _in_proj_parts(h, w_parts, ids, prev, *, width, tm, tn, name, side=None):
    s_len, d = h.shape
    nb = w_parts[0].shape[1] // tn
    n_p = len(w_parts)
    n_prev = 0 if prev is None else 1
    n_sin = len(side.ins) if side else 0
    n_sout = len(side.out_shapes) if side else 0
    grid = (n_p, s_len // tm, nb)
    base = 1 + n_p + n_prev

    def body(ids_ref, *refs):
        h_ref, w_refs = refs[0], refs[1:1 + n_p]
        s_ins = refs[base:base + n_sin]
        o_ref = refs[base + n_sin]
        s_outs = refs[base + n_sin + 1:base + n_sin + 1 + n_sout]
        sems = refs[base + n_sin + 1 + n_sout:]
        if side:
            first, _, last = _side_at(grid)
            pl.when(first)(lambda: side.start(s_ins, s_outs, sems))
        for s in range(n_p):
            @pl.when(pl.program_id(0) == s)
            def _(s=s):
                o_ref[...] = _dot(h_ref[...], w_refs[s][...], 1, 0).astype(o_ref.dtype)
        if side:
            pl.when(last)(lambda: side.finish(s_ins, s_outs, sems))

    in_specs = [pl.BlockSpec((tm, d), lambda p, i, j, ids_ref: (i, 0))]
    for s in range(n_p):
        in_specs.append(pl.BlockSpec((d, tn), lambda p, i, j, ids_ref, s=s: (0, jnp.where(p == s, j, 0))))
    in_specs += [HBM_SPEC] * (n_prev + n_sin)
    o_spec = pl.BlockSpec((tm, tn), lambda p, i, j, ids_ref: (i, ids_ref[p] * nb + j))
    aliases = {2 + n_p: 0} if prev is not None else {}
    if side:
        aliases.update({1 + base + i: 1 + o for i, o in side.aliases.items()})
    outs = pl.pallas_call(
        body, name=name, out_shape=(jax.ShapeDtypeStruct((s_len, width), BF), *(side.out_shapes if side else [])),
        grid_spec=pltpu.PrefetchScalarGridSpec(
            num_scalar_prefetch=1, grid=grid, in_specs=in_specs, out_specs=(o_spec, *([HBM_SPEC] * n_sout)),
            scratch_shapes=side.sem_shapes if side else []),
        input_output_aliases=aliases, compiler_params=_cparams("arbitrary", "arbitrary", "arbitrary"),
    )(ids, h, *w_parts, *([prev] if prev is not None else []), *(side.ins if side else []))
    return outs


def _dgrad_in_parts(dproj, w_parts, ids, *, tm, tn, name, side=None):
    s_len = dproj.shape[0]
    d, ws = w_parts[0].shape
    n_p = len(w_parts)
    n_sin = len(side.ins) if side else 0
    n_sout = len(side.out_shapes) if side else 0
    grid = (s_len // tm, d // tn)

    def body(ids_ref, *refs):
        a_refs, w_refs = refs[:n_p], refs[n_p:2 * n_p]
        s_ins = refs[2 * n_p:2 * n_p + n_sin]
        o_ref = refs[2 * n_p + n_sin]
        s_outs = refs[2 * n_p + n_sin + 1:2 * n_p + n_sin + 1 + n_sout]
        sems = refs[2 * n_p + n_sin + 1 + n_sout:]
        if side:
            first, _, last = _side_at(grid)
            pl.when(first)(lambda: side.start(s_ins, s_outs, sems))
        tot = _dot(a_refs[0][...], w_refs[0][...], 1, 1)
        for s in range(1, n_p):
            tot = tot + _dot(a_refs[s][...], w_refs[s][...], 1, 1)
        o_ref[...] = tot.astype(o_ref.dtype)
        if side:
            pl.when(last)(lambda: side.finish(s_ins, s_outs, sems))

    in_specs = [pl.BlockSpec((tm, ws), lambda i, j, ids_ref, s=s: (i, ids_ref[s])) for s in range(n_p)]
    in_specs += [pl.BlockSpec((tn, ws), lambda i, j, ids_ref: (j, 0))] * n_p
    in_specs += [HBM_SPEC] * n_sin
    aliases = {1 + 2 * n_p + i: 1 + o for i, o in side.aliases.items()} if side else {}
    return pl.pallas_call(
        body, name=name, out_shape=(jax.ShapeDtypeStruct((s_len, d), BF), *(side.out_shapes if side else [])),
        grid_spec=pltpu.PrefetchScalarGridSpec(
            num_scalar_prefetch=1, grid=grid, in_specs=in_specs,
            out_specs=(pl.BlockSpec((tm, tn), lambda i, j, ids_ref: (i, j)), *([HBM_SPEC] * n_sout)),
            scratch_shapes=side.sem_shapes if side else []),
        input_output_aliases=aliases, compiler_params=_cparams("arbitrary", "arbitrary"),
    )(ids, *([dproj] * n_p), *w_parts, *(side.ins if side else []))


def _prenorm(x, ctx, g_pre, ss, *, tm):
    n, d = x.shape
    l = ctx.shape[0]
    nc = l // tm

    def body(x_ref, c_ref, g_ref, ss_ref, h_ref):
        t = pl.program_id(0)

        def go(src, scale, shift):
            v = src[...]
            rstd = lax.rsqrt(jnp.mean(v * v, axis=-1, keepdims=True) + EPS)
            h_ref[...] = ((v * rstd * g_ref[...]) * (1.0 + scale) + shift).astype(BF)

        @pl.when(t < nc)
        def _():
            go(c_ref, ss_ref[2:3, :], ss_ref[3:4, :])

        @pl.when(t >= nc)
        def _():
            go(x_ref, ss_ref[0:1, :], ss_ref[1:2, :])

    return pl.pallas_call(
        body, name="prenorm", grid=((l + n) // tm,), out_shape=jax.ShapeDtypeStruct((l + n, d), BF),
        in_specs=[pl.BlockSpec((tm, d), lambda t: (jnp.maximum(t - nc, 0), 0)),
                  pl.BlockSpec((tm, d), lambda t: (jnp.minimum(t, nc - 1), 0)),
                  pl.BlockSpec((1, d), lambda t: (0, 0)), pl.BlockSpec((4, d), lambda t: (0, 0))],
        out_specs=pl.BlockSpec((tm, d), lambda t: (t, 0)), compiler_params=_cparams("parallel"),
    )(x, ctx, g_pre, ss)


def _swap32(v):
    lane = lax.broadcasted_iota(jnp.int32, v.shape, 1)
    return jnp.where((lane % 64) < 32, pltpu.roll(v, 96, 1), pltpu.roll(v, 32, 1))


def _qk_prep(proj, gain, ctab, stab, *, row0, col0, width, nrows, tm, name, out_scale=1.0):
    cw = min(512, width)
    rb0, cb0 = row0 // tm, col0 // cw
    assert row0 % tm == 0 and col0 % cw == 0 and width % cw == 0 and nrows % tm == 0

    def body(p_ref, g_ref, c_ref, s_ref, o_ref):
        for hd in range(cw // HEAD_DIM):
            cols = slice(hd * HEAD_DIM, (hd + 1) * HEAD_DIM)
            v = p_ref[:, cols].astype(F32)
            rstd = lax.rsqrt(jnp.mean(v * v, axis=-1, keepdims=True) + EPS)
            yv = v * rstd * g_ref[...]
            roped = yv * c_ref[...] + _swap32(yv) * s_ref[...]
            o_ref[:, cols] = (roped if out_scale == 1.0 else roped * out_scale).astype(BF)

    return pl.pallas_call(
        body, name=name, grid=(nrows // tm, width // cw), out_shape=jax.ShapeDtypeStruct((nrows, width), BF),
        in_specs=[pl.BlockSpec((tm, cw), lambda i, j: (i + rb0, j + cb0)), pl.BlockSpec((1, HEAD_DIM), lambda i, j: (0, 0)),
                  pl.BlockSpec((tm, HEAD_DIM), lambda i, j: (i, 0)), pl.BlockSpec((tm, HEAD_DIM), lambda i, j: (i, 0))],
        out_specs=pl.BlockSpec((tm, cw), lambda i, j: (i, j)), compiler_params=_cparams("parallel", "parallel"),
    )(proj, gain, ctab, stab)


def _zero_rows(rows, shape, *, name):
    def body(o_ref):
        o_ref[...] = jnp.zeros_like(o_ref)

    return pl.pallas_call(body, name=name, grid=(1,), out_shape=jax.ShapeDtypeStruct(shape, BF),
                          out_specs=pl.BlockSpec((rows, shape[1]), lambda i: (0, 0)))()


def _qk_bwd(dy, proj, gain, ctab, stab, into, *, row0, col0, tm, name, transposed=False):
    if transposed:
        tq = dy.shape[-1]
        nrows, width = dy.shape[1] * tq, dy.shape[0] * GQA_GROUP * HEAD_DIM
        per = tq // tm
        dy_spec = pl.BlockSpec((1, 1, GQA_GROUP, HEAD_DIM, tm), lambda i, j: (j, i // per, 0, 0, i % per))
    else:
        nrows, width = dy.shape
    cw = min(512, width)
    rb0, cb0 = row0 // tm, col0 // cw
    if transposed:
        assert cw == GQA_GROUP * HEAD_DIM and tq % tm == 0
    else:
        dy_spec = pl.BlockSpec((tm, cw), lambda i, j: (i, j))

    def body(d_ref, p_ref, g_ref, c_ref, s_ref, into_ref, o_ref, dg_ref):
        @pl.when(jnp.logical_and(pl.program_id(0) == 0, pl.program_id(1) == 0))
        def _():
            dg_ref[...] = jnp.zeros_like(dg_ref)

        dg = jnp.zeros((1, HEAD_DIM), F32)
        for hd in range(cw // HEAD_DIM):
            cols = slice(hd * HEAD_DIM, (hd + 1) * HEAD_DIM)
            v = p_ref[:, cols].astype(F32)
            rstd = lax.rsqrt(jnp.mean(v * v, axis=-1, keepdims=True) + EPS)
            nv = v * rstd
            d = d_ref[0, 0, hd].T if transposed else d_ref[:, cols]
            dyu = d * c_ref[...] + _swap32(d * s_ref[...])
            dg = dg + jnp.sum(dyu * nv, axis=0, keepdims=True)
            dn = dyu * g_ref[...]
            o_ref[:, cols] = (rstd * (dn - nv * jnp.mean(dn * nv, axis=-1, keepdims=True))).astype(BF)
        dg_ref[0:1, :] += dg

    placed = pl.BlockSpec((tm, cw), lambda i, j: (i + rb0, j + cb0))
    return pl.pallas_call(
        body, name=name, grid=(nrows // tm, width // cw),
        out_shape=(jax.ShapeDtypeStruct(into.shape, BF), jax.ShapeDtypeStruct((8, HEAD_DIM), F32)),
        in_specs=[dy_spec, placed, pl.BlockSpec((1, HEAD_DIM), lambda i, j: (0, 0)),
                  pl.BlockSpec((tm, HEAD_DIM), lambda i, j: (i, 0)), pl.BlockSpec((tm, HEAD_DIM), lambda i, j: (i, 0)), HBM_SPEC],
        out_specs=(placed, pl.BlockSpec((8, HEAD_DIM), lambda i, j: (0, 0))),
        input_output_aliases={5: 0}, compiler_params=_cparams("arbitrary", "arbitrary"),
    )(dy, proj, gain, ctab, stab, into)


def _flash_fwd(qr, k_all, vt_ctx, vt_lat, *, tq, side=None):
    n, aw = qr.shape
    s_len, kvw = k_all.shape
    l = vt_ctx.shape[-1]
    n_j, tk = vt_lat.shape[1], vt_lat.shape[-1]
    kvh = kvw // HEAD_DIM
    n_i = n // tq
    gw = GQA_GROUP * HEAD_DIM
    n_sin = len(side.ins) if side else 0
    n_sout = len(side.out_shapes) if side else 0

    def body(*refs):
        q_ref, k_ref, vc_ref, vl_ref = refs[:4]
        s_ins = refs[4:4 + n_sin]
        o_ref, lse_ref = refs[4 + n_sin:6 + n_sin]
        s_outs = refs[6 + n_sin:6 + n_sin + n_sout]
        acc_ref, m_ref, l_ref = refs[6 + n_sin + n_sout:9 + n_sin + n_sout]
        sems = refs[9 + n_sin + n_sout:]
        if side:
            first, middle, last = _side_at((kvh, n_i))
            pl.when(first)(lambda: side.start(s_ins, s_outs, sems))
            if side.mid is not None:
                pl.when(middle)(lambda: side.mid(s_ins, s_outs, sems))
        acc_ref[...] = jnp.zeros_like(acc_ref)
        l_ref[...] = jnp.zeros_like(l_ref)
        m_ref[...] = jnp.full(m_ref.shape, -1e30, F32)

        def tile(kj, vtj):
            for g in range(GQA_GROUP):
                st = _dot(kj, q_ref[:, g * HEAD_DIM:(g + 1) * HEAD_DIM], 1, 1)
                m_old = m_ref[g]
                m_new = jnp.maximum(m_old, jnp.max(st, axis=0, keepdims=True))
                alpha = jnp.exp2(m_old - m_new)
                pt = jnp.exp2(st - m_new)
                l_ref[g] = alpha * l_ref[g] + jnp.sum(pt, axis=0, keepdims=True)
                m_ref[g] = m_new
                acc_ref[g] = acc_ref[g] * alpha + _dot(vtj, pt.astype(BF), 1, 0)

        tile(k_ref[0:l, :], vc_ref[0, 0])

        def step(j, carry):
            tile(k_ref[pl.ds(pl.multiple_of(l + j * tk, min(l, tk)), tk), :], vl_ref[0, j])
            return carry

        lax.fori_loop(0, n_j, step, 0)
        for g in range(GQA_GROUP):
            o_ref[:, g * HEAD_DIM:(g + 1) * HEAD_DIM] = (acc_ref[g] / l_ref[g]).T.astype(BF)
            lse_ref[0, g, 0] = m_ref[g] + jnp.log(l_ref[g]) * LOG2E
        if side:
            pl.when(last)(lambda: side.finish(s_ins, s_outs, sems))

    return pl.pallas_call(
        body, name="flash_fwd", grid=(kvh, n_i),
        out_shape=(jax.ShapeDtypeStruct((n, aw), BF), jax.ShapeDtypeStruct((kvh, GQA_GROUP, n_i, 1, tq), F32),
                   *(side.out_shapes if side else [])),
        in_specs=[pl.BlockSpec((tq, gw), lambda h, i: (i, h)), pl.BlockSpec((s_len, HEAD_DIM), lambda h, i: (0, h)),
                  pl.BlockSpec((1, 1, HEAD_DIM, l), lambda h, i: (h, 0, 0, 0)),
                  pl.BlockSpec((1, n_j, HEAD_DIM, tk), lambda h, i: (h, 0, 0, 0))] + [HBM_SPEC] * n_sin,
        out_specs=(pl.BlockSpec((tq, gw), lambda h, i: (i, h)),
                   pl.BlockSpec((1, GQA_GROUP, 1, 1, tq), lambda h, i: (h, 0, i, 0, 0)), *([HBM_SPEC] * n_sout)),
        input_output_aliases={4 + i: 2 + o for i, o in side.aliases.items()} if side else {},
        scratch_shapes=[pltpu.VMEM((GQA_GROUP, HEAD_DIM, tq), F32), pltpu.VMEM((GQA_GROUP, 1, tq), F32),
                        pltpu.VMEM((GQA_GROUP, 1, tq), F32)] + (side.sem_shapes if side else []),
        compiler_params=_cparams("arbitrary", "arbitrary") if side else _cparams("parallel", "parallel"),
    )(qr, k_all, vt_ctx, vt_lat, *(side.ins if side else []))


def _attn_delta(o, do, *, tq):
    n, aw = o.shape
    kvh = aw // (GQA_GROUP * HEAD_DIM)
    gw = GQA_GROUP * HEAD_DIM

    def body(o_ref, do_ref, d_ref):
        for g in range(GQA_GROUP):
            cols = slice(g * HEAD_DIM, (g + 1) * HEAD_DIM)
            prod = o_ref[:, cols].astype(F32) * do_ref[:, cols].astype(F32)
            d_ref[0, g, 0] = jnp.sum(prod.T, axis=0, keepdims=True)

    return pl.pallas_call(
        body, name="attn_delta", grid=(kvh, n // tq),
        out_shape=jax.ShapeDtypeStruct((kvh, GQA_GROUP, n // tq, 1, tq), F32),
        in_specs=[pl.BlockSpec((tq, gw), lambda h, i: (i, h)), pl.BlockSpec((tq, gw), lambda h, i: (i, h))],
        out_specs=pl.BlockSpec((1, GQA_GROUP, 1, 1, tq), lambda h, i: (h, 0, i, 0, 0)),
        compiler_params=_cparams("parallel", "parallel"),
    )(o, do)


def _flash_bwd(qr, do, k_all, proj, lse, delta, into, *, v_col0, tq, tk, scale, side=None):
    n, aw = qr.shape
    s_len, kvw = k_all.shape
    kvh = kvw // HEAD_DIM
    n_i, n_j = n // tq, s_len // tk
    gw = GQA_GROUP * HEAD_DIM
    vb0 = v_col0 // HEAD_DIM
    n_sin = len(side.ins) if side else 0
    n_sout = len(side.out_shapes) if side else 0

    def body(*refs):
        q_ref, do_ref, k_ref, v_ref, lse_ref, dl_ref = refs[:6]
        s_ins = refs[7:7 + n_sin]
        dq_ref, dk_ref, dvo_ref = refs[7 + n_sin:10 + n_sin]
        s_outs = refs[10 + n_sin:10 + n_sin + n_sout]
        dv_ref = refs[10 + n_sin + n_sout]
        sems = refs[11 + n_sin + n_sout:]
        jj = pl.program_id(1)
        if side:
            first, _, last = _side_at((kvh, n_j))
            pl.when(first)(lambda: side.start(s_ins, s_outs, sems))

        @pl.when(jj == 0)
        def _():
            dq_ref[...] = jnp.zeros_like(dq_ref)

        kj = k_ref[...]
        vj = v_ref[...]
        kjt = kj.astype(F32).T.astype(BF)
        dk_ref[...] = jnp.zeros_like(dk_ref)
        dv_ref[...] = jnp.zeros_like(dv_ref)

        def step(i, carry):
            rows = pl.ds(pl.multiple_of(i * tq, tq), tq)
            dv_part = dk_part = None
            for g in range(GQA_GROUP):
                cols = slice(g * HEAD_DIM, (g + 1) * HEAD_DIM)
                qg = q_ref[rows, cols]
                dog = do_ref[rows, cols]
                pt = jnp.exp2(_dot(kj, qg, 1, 1) - lse_ref[0, g, i])
                dst = (pt * (_dot(vj, dog, 1, 1) - dl_ref[0, g, i])).astype(BF)
                dv_g = _dot(pt.astype(BF), dog, 1, 0)
                dk_g = _dot(dst, qg, 1, 0)
                dv_part = dv_g if dv_part is None else dv_part + dv_g
                dk_part = dk_g if dk_part is None else dk_part + dk_g
                dq_ref[0, i, g] += _dot(kjt, dst, 1, 0)
            dv_ref[...] += dv_part
            dk_ref[...] += dk_part
            return carry

        lax.fori_loop(0, n_i, step, 0)
        dk_ref[...] = dk_ref[...] * (1.0 / LOG2E)
        dvo_ref[...] = dv_ref[...].astype(BF)

        @pl.when(jj == n_j - 1)
        def _():
            dq_ref[...] = dq_ref[...] * scale

        if side:
            pl.when(last)(lambda: side.finish(s_ins, s_outs, sems))

    stat_spec = pl.BlockSpec((1, GQA_GROUP, n_i, 1, tq), lambda h, j: (h, 0, 0, 0, 0))
    kv_spec = pl.BlockSpec((tk, HEAD_DIM), lambda h, j: (j, h))
    v_spec = pl.BlockSpec((tk, HEAD_DIM), lambda h, j: (j, vb0 + h))
    q_spec = pl.BlockSpec((n, gw), lambda h, j: (0, h))
    dq_spec = pl.BlockSpec((1, n_i, GQA_GROUP, HEAD_DIM, tq), lambda h, j: (h, 0, 0, 0, 0))
    aliases = {6: 2}
    if side:
        aliases.update({7 + i: 3 + o for i, o in side.aliases.items()})
    return pl.pallas_call(
        body, name="flash_bwd", grid=(kvh, n_j),
        out_shape=(jax.ShapeDtypeStruct((kvh, n_i, GQA_GROUP, HEAD_DIM, tq), F32), jax.ShapeDtypeStruct((s_len, kvw), F32),
                   jax.ShapeDtypeStruct(into.shape, BF), *(side.out_shapes if side else [])),
        in_specs=[q_spec, q_spec, kv_spec, v_spec, stat_spec, stat_spec, HBM_SPEC] + [HBM_SPEC] * n_sin,
        out_specs=(dq_spec, kv_spec, v_spec, *([HBM_SPEC] * n_sout)),
        input_output_aliases=aliases,
        scratch_shapes=[pltpu.VMEM((tk, HEAD_DIM), F32)] + (side.sem_shapes if side else []),
        compiler_params=_cparams("arbitrary", "arbitrary"),
    )(qr, do, k_all, proj, lse, delta, into, *(side.ins if side else []))


def _pool_window(r, gi, l, n, tm):
    win = tm + 2 * POOL_HALO
    start = jnp.clip(l + r * tm - POOL_HALO, l, l + n - win)
    half = lax.shift_left(jnp.int32(1), gi)
    tok = r * tm + lax.broadcasted_iota(jnp.int32, (tm, win), 0)
    src = (start - l) + lax.broadcasted_iota(jnp.int32, (tm, win), 1)
    tok1 = r * tm + lax.broadcasted_iota(jnp.int32, (tm, 1), 0)
    cnt = (jnp.minimum(tok1 + half, n) - jnp.maximum(tok1 - half, 0)).astype(F32)
    return start, win, tok, src, half, cnt


def _pool_fwd(proj, pool_w, *, l, n, col0, tm):
    s_len = proj.shape[0]
    pg = pool_w.shape[-1]
    cb0 = col0 // pg
    assert col0 % pg == 0 and l % POOL_HALO == 0 and n >= tm + 2 * POOL_HALO

    def body(u_ref, w_ref, d_ref, po_ref):
        gi, r = pl.program_id(0), pl.program_id(1)
        start, win, tok, src, half, cnt = _pool_window(r, gi, l, n, tm)
        band = jnp.logical_and(src >= tok - half, src < tok + half).astype(BF)
        uw = u_ref[pl.ds(pl.multiple_of(start, POOL_HALO), win), :]
        ut = u_ref[pl.ds(pl.multiple_of(l + r * tm, POOL_HALO), tm), :].astype(F32)
        dv = (_dot(band, uw, 1, 0) / cnt - ut).astype(BF)
        d_ref[...] = dv
        po_ref[...] = _dot(dv, w_ref[0], 1, 0).astype(BF)

    return pl.pallas_call(
        body, name="pool_fwd", grid=(N_POOL_GROUPS, n // tm),
        out_shape=(jax.ShapeDtypeStruct((n, N_POOL_GROUPS * pg), BF), jax.ShapeDtypeStruct((n, N_POOL_GROUPS * pg), BF)),
        in_specs=[pl.BlockSpec((s_len, pg), lambda g, r: (0, cb0 + g)), pl.BlockSpec((1, pg, pg), lambda g, r: (g, 0, 0))],
        out_specs=(pl.BlockSpec((tm, pg), lambda g, r: (r, g)), pl.BlockSpec((tm, pg), lambda g, r: (r, g))),
        compiler_params=_cparams("parallel", "parallel"),
    )(proj, pool_w)


def _pool_bwd_map(dval, pool_w, *, n, col0, tm):
    pg = pool_w.shape[-1]
    cb0 = col0 // pg

    def body(d_ref, w_ref, o_ref):
        gi, r = pl.program_id(0), pl.program_id(1)
        half = lax.shift_left(jnp.int32(1), gi)
        tok1 = r * tm + lax.broadcasted_iota(jnp.int32, (tm, 1), 0)
        cnt = (jnp.minimum(tok1 + half, n) - jnp.maximum(tok1 - half, 0)).astype(F32)
        o_ref[...] = (_dot(d_ref[...], w_ref[0], 1, 1) / cnt).astype(BF)

    return pl.pallas_call(
        body, name="pool_bwd_map", grid=(N_POOL_GROUPS, n // tm),
        out_shape=jax.ShapeDtypeStruct((n, N_POOL_GROUPS * pg), BF),
        in_specs=[pl.BlockSpec((tm, pg), lambda g, r: (r, cb0 + g)), pl.BlockSpec((1, pg, pg), lambda g, r: (g, 0, 0))],
        out_specs=pl.BlockSpec((tm, pg), lambda g, r: (r, g)), compiler_params=_cparams("parallel", "parallel"),
    )(dval, pool_w)


def _pool_bwd_window(dds, into, *, n, tm, row0, col0):
    pg = dds.shape[1] // N_POOL_GROUPS
    rb0, cb0 = row0 // tm, col0 // pg
    assert row0 % tm == 0 and col0 % pg == 0

    def body(d_ref, into_ref, o_ref):
        gi, r = pl.program_id(0), pl.program_id(1)
        start, win, tok, src, half, cnt = _pool_window(r, gi, 0, n, tm)
        band = jnp.logical_and(tok >= src - half, tok < src + half).astype(BF)
        dw = d_ref[pl.ds(pl.multiple_of(start, POOL_HALO), win), :]
        dt = d_ref[pl.ds(pl.multiple_of(r * tm, POOL_HALO), tm), :].astype(F32)
        o_ref[...] = (_dot(band, dw, 1, 0) - dt * cnt).astype(BF)

    return pl.pallas_call(
        body, name="pool_bwd_window", grid=(N_POOL_GROUPS, n // tm), out_shape=jax.ShapeDtypeStruct(into.shape, BF),
        in_specs=[pl.BlockSpec((n, pg), lambda g, r: (0, g)), HBM_SPEC],
        out_specs=pl.BlockSpec((tm, pg), lambda g, r: (r + rb0, g + cb0)),
        input_output_aliases={1: 0}, compiler_params=_cparams("parallel", "parallel"),
    )(dds, into)


def _pool_wgrad(dmat, dval, *, col0, tk):
    n, pw = dmat.shape
    pg = pw // N_POOL_GROUPS
    cb0 = col0 // pg
    nk = n // tk

    def body(a_ref, b_ref, o_ref, acc):
        k = pl.program_id(1)

        @pl.when(k == 0)
        def _():
            acc[...] = jnp.zeros_like(acc)

        acc[...] += _dot(a_ref[...], b_ref[...], 0, 0)

        @pl.when(k == nk - 1)
        def _():
            o_ref[0] = acc[...].astype(BF)

    return pl.pallas_call(
        body, name="pool_wgrad", grid=(N_POOL_GROUPS, nk), out_shape=jax.ShapeDtypeStruct((N_POOL_GROUPS, pg, pg), BF),
        in_specs=[pl.BlockSpec((tk, pg), lambda g, k: (k, g)), pl.BlockSpec((tk, pg), lambda g, k: (k, cb0 + g))],
        out_specs=pl.BlockSpec((1, pg, pg), lambda g, k: (g, 0, 0)), scratch_shapes=[pltpu.VMEM((pg, pg), F32)],
        compiler_params=_cparams("parallel", "arbitrary"),
    )(dmat, dval)


def _merge(attn_o, po_raw, proj, pool_scale, *, l, ga0, gp0, tm):
    n, aw = attn_o.shape
    bw = aw // 2
    rb0 = l // tm
    ga_b, gp_b = ga0 // bw, gp0 // bw
    assert ga0 % bw == 0 and gp0 % bw == 0 and l % tm == 0

    def body(a_ref, p_ref, g_ref, ps_ref, y_ref):
        cb = pl.program_id(1)
        sg, _ = _silu_and_grad(g_ref[...].astype(F32))

        @pl.when(cb < 2)
        def _():
            y_ref[...] = (a_ref[...].astype(F32) * sg).astype(BF)

        @pl.when(cb >= 2)
        def _():
            y_ref[...] = (p_ref[...].astype(F32) * ps_ref[...] * sg).astype(BF)

    return pl.pallas_call(
        body, name="merge", grid=(n // tm, 4), out_shape=jax.ShapeDtypeStruct((n, 2 * aw), BF),
        in_specs=[pl.BlockSpec((tm, bw), lambda i, cb: (i, jnp.minimum(cb, 1))),
                  pl.BlockSpec((tm, bw), lambda i, cb: (i, jnp.maximum(cb - 2, 0))),
                  pl.BlockSpec((tm, bw), lambda i, cb: (i + rb0, jnp.where(cb < 2, ga_b + cb, gp_b + cb - 2))),
                  pl.BlockSpec((1, bw), lambda i, cb: (0, jnp.maximum(cb - 2, 0)))],
        out_specs=pl.BlockSpec((tm, bw), lambda i, cb: (i, cb)), compiler_params=_cparams("parallel", "arbitrary"),
    )(attn_o, po_raw, proj, pool_scale)


def _merge_bwd(dy, attn_o, po_raw, proj, pool_scale, into, *, l, ga0, gp0, tm):
    n, aw = attn_o.shape
    bw = aw // 2
    rb0 = l // tm
    ga_b, gp_b = ga0 // bw, gp0 // bw

    def body(dy_ref, a_ref, p_ref, g_ref, ps_ref, into_ref, dv_ref, dg_ref, dps_ref):
        cb, i = pl.program_id(0), pl.program_id(1)
        sg, sgrad = _silu_and_grad(g_ref[...].astype(F32))
        dyv = dy_ref[...].astype(F32)

        @pl.when(cb < 2)
        def _():
            dv_ref[...] = (dyv * sg).astype(BF)
            dg_ref[...] = (dyv * a_ref[...].astype(F32) * sgrad).astype(BF)

        @pl.when(cb >= 2)
        def _():
            @pl.when(i == 0)
            def _():
                dps_ref[...] = jnp.zeros_like(dps_ref)

            pr = p_ref[...].astype(F32)
            dpo = dyv * sg
            dv_ref[...] = (dpo * ps_ref[...]).astype(BF)
            dg_ref[...] = (dyv * (pr * ps_ref[...]) * sgrad).astype(BF)
            dps_ref[0:1, :] += jnp.sum(dpo * pr, axis=0, keepdims=True)

    blk = pl.BlockSpec((tm, bw), lambda cb, i: (i, cb))
    gate_blk = pl.BlockSpec((tm, bw), lambda cb, i: (i + rb0, jnp.where(cb < 2, ga_b + cb, gp_b + cb - 2)))
    return pl.pallas_call(
        body, name="merge_bwd", grid=(4, n // tm),
        out_shape=(jax.ShapeDtypeStruct((n, 2 * aw), BF), jax.ShapeDtypeStruct(into.shape, BF),
                   jax.ShapeDtypeStruct((8, aw), F32)),
        in_specs=[blk, pl.BlockSpec((tm, bw), lambda cb, i: (i, jnp.minimum(cb, 1))),
                  pl.BlockSpec((tm, bw), lambda cb, i: (i, jnp.maximum(cb - 2, 0))), gate_blk,
                  pl.BlockSpec((1, bw), lambda cb, i: (0, jnp.maximum(cb - 2, 0))), HBM_SPEC],
        out_specs=(blk, gate_blk, pl.BlockSpec((8, bw), lambda cb, i: (0, jnp.maximum(cb - 2, 0)))),
        input_output_aliases={5: 1}, compiler_params=_cparams("arbitrary", "arbitrary"),
    )(dy, attn_o, po_raw, proj, pool_scale, into)


def _post(x, out, target, gate, g_post, *, tm):
    n, d = x.shape

    def body(x_ref, o_ref, t_ref, gate_ref, gp_ref, dxn_ref, dout_ref, st_ref):
        @pl.when(pl.program_id(0) == 0)
        def _():
            st_ref[...] = jnp.zeros_like(st_ref)

        ov = o_ref[...]
        rstd = lax.rsqrt(jnp.mean(ov * ov, axis=-1, keepdims=True) + EPS)
        on = ov * rstd
        rn = on * gp_ref[...]
        err = (x_ref[...] + gate_ref[...] * rn) - t_ref[...]
        dxn = err / d
        dxn_ref[...] = dxn
        drn = dxn * gate_ref[...]
        don = drn * gp_ref[...]
        dout_ref[...] = (rstd * (don - on * jnp.mean(don * on, axis=-1, keepdims=True))).astype(BF)
        st_ref[0:1, :] += jnp.sum(dxn * rn, axis=0, keepdims=True)
        st_ref[1:2, :] += jnp.sum(drn * on, axis=0, keepdims=True)
        st_ref[2:3, :] += jnp.sum(err * err, axis=0, keepdims=True)

    row = pl.BlockSpec((tm, d), lambda i: (i, 0))
    vec = pl.BlockSpec((1, d), lambda i: (0, 0))
    return pl.pallas_call(
        body, name="post", grid=(n // tm,),
        out_shape=(jax.ShapeDtypeStruct((n, d), F32), jax.ShapeDtypeStruct((n, d), BF), jax.ShapeDtypeStruct((8, d), F32)),
        in_specs=[row, row, row, vec, vec], out_specs=(row, row, pl.BlockSpec((8, d), lambda i: (0, 0))),
        compiler_params=_cparams("arbitrary"),
    )(x, out, target, gate, g_post)


def _prenorm_bwd(x, ctx, dh, dxn, g_pre, ss, *, tm):
    n, d = x.shape
    l = ctx.shape[0]
    nc = l // tm

    def body(x_ref, c_ref, dh_ref, dxn_ref, g_ref, ss_ref, gx_ref, st_ref):
        t = pl.program_id(0)

        @pl.when(t == 0)
        def _():
            st_ref[...] = jnp.zeros_like(st_ref)

        def go(src, scale, row):
            v = src[...]
            dhv = dh_ref[...].astype(F32)
            rstd = lax.rsqrt(jnp.mean(v * v, axis=-1, keepdims=True) + EPS)
            xn = v * rstd
            st_ref[row:row + 1, :] += jnp.sum(dhv, axis=0, keepdims=True)
            st_ref[row + 1:row + 2, :] += jnp.sum(dhv * (xn * g_ref[...]), axis=0, keepdims=True)
            dg = dhv * (1.0 + scale)
            st_ref[4:5, :] += jnp.sum(dg * xn, axis=0, keepdims=True)
            dxn_ = dg * g_ref[...]
            return rstd * (dxn_ - xn * jnp.mean(dxn_ * xn, axis=-1, keepdims=True))

        @pl.when(t < nc)
        def _():
            go(c_ref, ss_ref[2:3, :], 2)

        @pl.when(t >= nc)
        def _():
            gx_ref[...] = dxn_ref[...] + go(x_ref, ss_ref[0:1, :], 0)

    lat = pl.BlockSpec((tm, d), lambda t: (jnp.maximum(t - nc, 0), 0))
    return pl.pallas_call(
        body, name="prenorm_bwd", grid=((l + n) // tm,),
        out_shape=(jax.ShapeDtypeStruct((n, d), F32), jax.ShapeDtypeStruct((8, d), F32)),
        in_specs=[lat, pl.BlockSpec((tm, d), lambda t: (jnp.minimum(t, nc - 1), 0)),
                  pl.BlockSpec((tm, d), lambda t: (t, 0)), lat,
                  pl.BlockSpec((1, d), lambda t: (0, 0)), pl.BlockSpec((4, d), lambda t: (0, 0))],
        out_specs=(lat, pl.BlockSpec((8, d), lambda t: (0, 0))), compiler_params=_cparams("arbitrary"),
    )(x, ctx, dh, dxn, g_pre, ss)


def _ada_fwd(craw, w_ada, b_loc, *, tn):
    d, wn = w_ada.shape

    def body(c_ref, w_ref, b_ref, o_ref):
        act, _ = _silu_and_grad(c_ref[...])
        o_ref[...] = _dot(act.astype(BF), w_ref[...].astype(BF), 1, 0) + b_ref[...]

    return pl.pallas_call(
        body, name="ada_fwd", grid=(wn // tn,), out_shape=jax.ShapeDtypeStruct((16, wn), F32),
        in_specs=[pl.BlockSpec((16, d), lambda j: (0, 0)), pl.BlockSpec((d, tn), lambda j: (0, j)),
                  pl.BlockSpec((1, tn), lambda j: (0, j))],
        out_specs=pl.BlockSpec((16, tn), lambda j: (0, j)), compiler_params=_cparams("parallel"),
    )(craw, w_ada, b_loc)


def _ada_bwd(craw_t, dm, w, m, v, *, tm):
    d, wn = w.shape

    def body(ct_ref, dm_ref, w_ref, m_ref, v_ref, g_ref, dl_ref, nm_ref, nv_ref, da_ref):
        act, _ = _silu_and_grad(ct_ref[...])
        dmb = dm_ref[...].astype(BF)
        wv = w_ref[...]
        gv = _dot(act.astype(BF), dmb, 1, 0)
        da_ref[...] = _dot(dmb, wv.astype(BF), 1, 1)
        nm = ADAM_B1 * m_ref[...] + (1.0 - ADAM_B1) * gv
        nv = ADAM_B2 * v_ref[...] + (1.0 - ADAM_B2) * (gv * gv)
        m_hat = nm / (1.0 - ADAM_B1 ** ADAM_STEP)
        v_hat = nv / (1.0 - ADAM_B2 ** ADAM_STEP)
        g_ref[...] = gv
        dl_ref[...] = -ADAM_LR * (m_hat / (jnp.sqrt(v_hat) + ADAM_EPS) + ADAM_WD * wv)
        nm_ref[...] = nm
        nv_ref[...] = nv

    row = pl.BlockSpec((tm, wn), lambda i: (i, 0))
    return pl.pallas_call(
        body, name="ada_bwd", grid=(d // tm,),
        out_shape=tuple([jax.ShapeDtypeStruct((d, wn), F32)] * 4) + (jax.ShapeDtypeStruct((16, d), F32),),
        in_specs=[pl.BlockSpec((tm, 16), lambda i: (i, 0)), pl.BlockSpec((16, wn), lambda i: (0, 0)), row, row, row],
        out_specs=(row, row, row, row, pl.BlockSpec((16, tm), lambda i: (0, i))),
        compiler_params=_cparams("parallel"),
    )(craw_t, dm, w, m, v)


def _reduce_small(gath, d3):
    t = gath.shape[-1]

    def body(g_ref, dm_ref, s_ref):
        tot = g_ref[0]
        for b in range(1, 8):
            tot = tot + g_ref[b]
        s_ref[...] = tot
        dm_ref[...] = jnp.zeros_like(dm_ref)
        for b in range(8):
            dm_ref[b:b + 1, :] = g_ref[b][:, 0:d3]
        dm_ref[8:9, :] = tot[:, d3:2 * d3]

    return pl.pallas_call(
        body, name="reduce_small", out_shape=(jax.ShapeDtypeStruct((16, d3), F32), jax.ShapeDtypeStruct((1, t), F32)),
        in_specs=[VMEM_SPEC], out_specs=(VMEM_SPEC, VMEM_SPEC),
    )(gath)


def _cctx_grad(parts, c_ctx):
    def body(p_ref, c_ref, o_ref):
        tot = (p_ref[0] + p_ref[1]) + (p_ref[2] + p_ref[3])
        _, sgrad = _silu_and_grad(c_ref[...])
        o_ref[...] = tot * sgrad

    return pl.pallas_call(
        body, name="cctx_grad", out_shape=jax.ShapeDtypeStruct(c_ctx.shape, F32),
        in_specs=[VMEM_SPEC, VMEM_SPEC], out_specs=VMEM_SPEC,
    )(parts, c_ctx)


def _rope_tables(n, l):
    rows = n // GRID_W
    row = jnp.repeat(jnp.arange(rows, dtype=F32), GRID_W)
    col = jnp.tile(jnp.arange(GRID_W, dtype=F32), rows)
    inv = ROPE_THETA ** (-jnp.arange(ROPE_PAIRS, dtype=F32) / ROPE_PAIRS)
    ang = jnp.concatenate([row[:, None] * inv, col[:, None] * inv], axis=-1)
    cos, sin = jnp.cos(ang), jnp.sin(ang)
    cr, cc, sr, sc = cos[:, :ROPE_PAIRS], cos[:, ROPE_PAIRS:], sin[:, :ROPE_PAIRS], sin[:, ROPE_PAIRS:]
    ctab = jnp.concatenate([cr, cr, cc, cc], axis=-1)
    stab = jnp.concatenate([-sr, sr, -sc, sc], axis=-1)
    ctab = jnp.concatenate([jnp.ones((l, HEAD_DIM), F32), ctab], axis=0)
    stab = jnp.concatenate([jnp.zeros((l, HEAD_DIM), F32), stab], axis=0)
    return ctab, stab


def kernel(x, c, ctx, c_ctx, w_ada, b_ada, norm_pre, norm_post, w_in, q_norm, k_norm, pool_w, pool_scale, w_out, loss_target, m_c_ctx, m_w_ada, m_b_ada, m_norm_pre, m_norm_post, m_w_in, m_q_norm, m_k_norm, m_pool_w, m_pool_scale, m_w_out, v_c_ctx, v_w_ada, v_b_ada, v_norm_pre, v_norm_post, v_w_in, v_q_norm, v_k_norm, v_pool_w, v_pool_scale, v_w_out):
    x2, ctx2, tgt = x[0], ctx[0], loss_target[0]
    n, d = x2.shape
    l = ctx2.shape[0]
    s_len = l + n
    aw = d // 2
    pw = d - aw
    n_heads = aw // HEAD_DIM
    kvw = (n_heads // GQA_GROUP) * HEAD_DIM
    pg = pw // N_POOL_GROUPS
    k0, v0, ga0 = aw, aw + kvw, aw + 2 * kvw
    up0, gp0 = ga0 + aw, ga0 + aw + pw
    in_w = gp0 + pw
    d3 = 3 * d
    ada_w = w_ada.shape[-1]
    px, py, pc = lax.axis_index("x"), lax.axis_index("y"), lax.axis_index("c")
    chip = 2 * px + py
    tr = min(256, l)
    tq = min(512, n)
    tk = min(256, l)
    ts = s_len // 8 if (s_len // 8) % 16 == 0 else tr
    scale = HEAD_DIM ** -0.5

    c_all = _allgather_small(c, chips_only=False, name="gather_c").reshape(8, d)
    craw = jnp.concatenate([c_all, c_ctx.reshape(1, d), jnp.zeros((7, d), F32)], axis=0)
    b_loc = lax.dynamic_slice(b_ada, (0, chip * ada_w), (1, ada_w))
    mod_part = _ada_fwd(craw, w_ada[0], b_loc, tn=min(512, ada_w))
    mod_all = _allgather_small(mod_part, chips_only=True, name="gather_mod")
    mod_all = jnp.transpose(mod_all, (1, 0, 2)).reshape(16, d3)
    me = 4 * px + 2 * py + pc
    mod_b = lax.dynamic_slice(mod_all, (me, 0), (1, d3))
    mod_c = mod_all[8:9]
    ss = jnp.concatenate([mod_b[:, d:2 * d], mod_b[:, 0:d], mod_c[:, d:2 * d], mod_c[:, 0:d]], axis=0)
    gate = mod_b[:, 2 * d:]

    w_own = _cast_bf16(w_in[0], name="cast_w_in")
    later_weights = _gather_side([_cast_into(w_out[0], 0, chip, name="cast_w_out"),
                                  _cast_into(pool_w[0], 1, chip, name="cast_pool_w")], [(0, 1), (1, 2)])
    ids = jnp.stack([chip, jnp.bitwise_xor(chip, 2), jnp.bitwise_xor(chip, 1), jnp.bitwise_xor(chip, 3)])
    ws = in_w // 4
    tn_p = ws // 3 if (ws // 3) % HEAD_DIM == 0 else ws

    ctab, stab = _rope_tables(n, l)
    h = _prenorm(x2, ctx2, norm_pre, ss, tm=tr)
    proj, w_x, w_y = _in_proj_parts(h, [w_own], ids[0:1], None, width=in_w, tm=ts, tn=tn_p, name="in_proj_own",
                                    side=_gather_parts_side(w_own, (0, 1)))
    proj, w_d = _in_proj_parts(h, [w_x, w_y], ids[1:3], proj, width=in_w, tm=ts, tn=tn_p, name="in_proj_xy",
                               side=_gather_parts_side(w_own, (2,)))
    (proj,) = _in_proj_parts(h, [w_d], ids[3:4], proj, width=in_w, tm=ts, tn=tn_p, name="in_proj_diag")
    w_parts = [w_own, w_x, w_y, w_d]
    qr = _qk_prep(proj, q_norm, ctab[l:], stab[l:], row0=l, col0=0, width=aw, nrows=n, tm=tr, name="q_prep",
                  out_scale=scale * LOG2E)
    kr = _qk_prep(proj, k_norm, ctab, stab, row0=0, col0=k0, width=kvw, nrows=s_len, tm=tr, name="k_prep")
    v_all = proj[:, v0:v0 + kvw]
    tkf = min(2048, n)
    vt_ctx = jnp.transpose(v_all[:l].reshape(1, l, kvw // HEAD_DIM, HEAD_DIM), (2, 0, 3, 1))
    vt_lat = jnp.transpose(v_all[l:].reshape(n // tkf, tkf, kvw // HEAD_DIM, HEAD_DIM), (2, 0, 3, 1))
    attn_o, lse, w_out_f, pool_f = _flash_fwd(qr, kr, vt_ctx, vt_lat, tq=tq, side=later_weights)
    dmat, po_raw = _pool_fwd(proj, pool_f, l=l, n=n, col0=up0, tm=tr)
    yv = _merge(attn_o, po_raw, proj, pool_scale, l=l, ga0=ga0, gp0=gp0, tm=tr)
    out = _matmul(yv, w_out_f, ca=1, cb=0, tm=min(512, n), tn=min(512, d), tk=d, out_dtype=F32, name="out_proj")

    dxn, d_out, post_st = _post(x2, out, tgt, gate, norm_post, tm=min(128, n))
    loss = 0.5 * jnp.sum(post_st[2]) / d
    loss = lax.psum(loss, ("x", "y", "c"))
    g_wout = _matmul(yv, d_out, ca=0, cb=0, tm=min(512, d), tn=min(512, d), tk=n, out_dtype=BF, name="wgrad_out")
    dy, from_sib_wout = _matmul(d_out, w_out_f, ca=1, cb=1, tm=min(512, n), tn=min(512, d), tk=d, out_dtype=BF,
                                name="dgrad_out", side=_pair_side([g_wout], [1]))
    pair_wout = _add_own_half(g_wout, from_sib_wout, 1, pc, name="pair_sum_1")
    dproj = _zero_rows(l, (s_len, in_w), name="dproj_init")
    dval, dproj, dps = _merge_bwd(dy, attn_o, po_raw, proj, pool_scale, dproj, l=l, ga0=ga0, gp0=gp0, tm=tr)
    g_pool = _pool_wgrad(dmat, dval, col0=aw, tk=min(512, n))
    dds = _pool_bwd_map(dval, pool_f, n=n, col0=aw, tm=tr)
    dproj = _pool_bwd_window(dds, dproj, n=n, tm=tr, row0=l, col0=up0)
    delta = _attn_delta(attn_o, dval, tq=tq)
    tqb = min(2048, n)
    stat_shape = (kvw // HEAD_DIM, GQA_GROUP, n // tqb, 1, tqb)
    dq, dk, dproj, *from_chips_wout = _flash_bwd(
        qr, dval, kr, proj, lse.reshape(stat_shape), delta.reshape(stat_shape), dproj, v_col0=v0, tq=tqb, tk=tk,
        scale=scale, side=_chips_side([pair_wout], [0]))
    red_wout = _sum_own_block(pair_wout, from_chips_wout, 0, 1, chip, pc, name="chip_sum_1")
    dproj, dgq = _qk_bwd(dq, proj, q_norm, ctab[l:], stab[l:], dproj, row0=l, col0=0, tm=tr, name="q_bwd", transposed=True)
    dproj, dgk = _qk_bwd(dk, proj, k_norm, ctab, stab, dproj, row0=0, col0=k0, tm=tr, name="k_bwd")
    g_send = _wgrad_rows(h, dproj, 1 - pc, tm=min(512, d // 2), tn=min(512, in_w), name="wgrad_in_send")
    g_keep, from_sib_win, from_sib_pool = _wgrad_rows(
        h, dproj, pc, tm=min(512, d // 2), tn=min(512, in_w), name="wgrad_in_keep",
        side=_both(_swap_side([g_send]), _pair_side([g_pool], [2])))
    pair_win = _add2(g_keep, from_sib_win, name="pair_sum_0")
    pair_pool = _add_own_half(g_pool, from_sib_pool, 2, pc, name="pair_sum_2")
    dh, *from_chips = _dgrad_in_parts(dproj, w_parts, ids, tm=ts, tn=min(512, d), name="dgrad_in",
                                      side=_chips_side([pair_win, pair_pool], [1, 1]))
    red_win = _sum_own_block(pair_win, from_chips[0:3], 1, 0, chip, pc, name="chip_sum_0")
    red_pool = _sum_own_block(pair_pool, from_chips[3:6], 1, 2, chip, pc, name="chip_sum_2")
    grad_x, pre_st = _prenorm_bwd(x2, ctx2, dh, dxn, norm_pre, ss, tm=min(128, l))

    zero_d = jnp.zeros((1, d), F32)
    packed = jnp.concatenate([pre_st[0:1], pre_st[1:2], post_st[0:1], pre_st[2:3], pre_st[3:4], zero_d,
                              pre_st[4:5], post_st[1:2], dgq[0:1], dgk[0:1], dps[0:1]], axis=1)
    gath = _allgather_small(packed, chips_only=False, name="gather_small")
    dm, sums = _reduce_small(gath, d3)
    o1 = 2 * d3
    g_npre, g_npost = sums[:, o1:o1 + d], sums[:, o1 + d:o1 + 2 * d]
    g_q, g_k = sums[:, o1 + 2 * d:o1 + 2 * d + HEAD_DIM], sums[:, o1 + 2 * d + HEAD_DIM:o1 + 2 * d + 2 * HEAD_DIM]
    g_ps = sums[:, o1 + 2 * d + 2 * HEAD_DIM:]
    g_bada = sums[:, 0:d3] + sums[:, d3:2 * d3]
    dm_loc = lax.dynamic_slice(dm, (0, chip * ada_w), (16, ada_w))
    g_wada, dl_wada, nm_wada, nv_wada, dact = _ada_bwd(craw.T, dm_loc, w_ada[0], m_w_ada[0], v_w_ada[0], tm=128)
    cparts = _allgather_small(dact[8:9], chips_only=True, name="gather_cctx")
    g_cctx = _cctx_grad(cparts, c_ctx.reshape(1, d)).reshape(d)

    g_win_s, g_wout_s, g_pool_s = _run_side(_join_side([red_win, red_wout, red_pool], [0, 1, 2]), name="reduce_join")

    def upd(w, g, m, v, name):
        return _adamw(w, g.reshape(w.shape), m, v, name=name)

    grads = {"c_ctx": g_cctx, "b_ada": g_bada, "norm_pre": g_npre, "norm_post": g_npost, "w_in": g_win_s[None],
             "q_norm": g_q, "k_norm": g_k, "pool_w": g_pool_s[None], "pool_scale": g_ps, "w_out": g_wout_s[None]}
    res = {"w_ada": (g_wada[None], dl_wada[None], nm_wada[None], nv_wada[None])}
    given = {"c_ctx": (c_ctx, m_c_ctx, v_c_ctx), "b_ada": (b_ada, m_b_ada, v_b_ada),
             "norm_pre": (norm_pre, m_norm_pre, v_norm_pre), "norm_post": (norm_post, m_norm_post, v_norm_post),
             "w_in": (w_in, m_w_in, v_w_in), "q_norm": (q_norm, m_q_norm, v_q_norm), "k_norm": (k_norm, m_k_norm, v_k_norm),
             "pool_w": (pool_w, m_pool_w, v_pool_w), "pool_scale": (pool_scale, m_pool_scale, v_pool_scale),
             "w_out": (w_out, m_w_out, v_w_out)}
    for nme, (w, m, v) in given.items():
        g = grads[nme].reshape(w.shape)
        res[nme] = (g,) + upd(w, g, m, v, "adamw_" + nme)
    order = ["c_ctx", "w_ada", "b_ada", "norm_pre", "norm_post", "w_in", "q_norm", "k_norm", "pool_w", "pool_scale", "w_out"]
    return (loss, grad_x[None], *[res[k][0] for k in order], *[res[k][1] for k in order],
            *[res[k][2] for k in order], *[res[k][3] for k in order])
```

```python
import functools

import jax
import jax.numpy as jnp
from jax import lax
from jax.experimental import pallas as pl
from jax.experimental.pallas import tpu as pltpu

F32 = jnp.float32
BF = jnp.bfloat16
MESH = pl.DeviceIdType.MESH

HEAD_DIM = 128
GQA_GROUP = 4
GRID_W = 64
ROPE_PAIRS = HEAD_DIM // 4
ROPE_THETA = 10000.0
EPS = 1e-6
N_POOL_GROUPS = 4
POOL_HALO = 128
ADAM_LR = 0.001
ADAM_B1 = 0.9
ADAM_B2 = 0.999
ADAM_EPS = 1e-08
ADAM_WD = 0.01
ADAM_STEP = 10
LOG2E = 1.4426950408889634
MIB = 2 ** 20
VMEM_LIMIT = 48 * MIB
CHIP_MASKS = ((1, 0, 0), (0, 1, 0), (1, 1, 0))
ALL_MASKS = ((0, 0, 1), (0, 1, 0), (0, 1, 1), (1, 0, 0), (1, 0, 1), (1, 1, 0), (1, 1, 1))
HBM_SPEC = pl.BlockSpec(memory_space=pl.ANY)
VMEM_SPEC = pl.BlockSpec(memory_space=pltpu.VMEM)


def _cparams(*sem):
    return pltpu.CompilerParams(dimension_semantics=sem, vmem_limit_bytes=VMEM_LIMIT)


def _sigmoid(v):
    return 0.5 * jnp.tanh(0.5 * v) + 0.5


def _silu_and_grad(v):
    s = _sigmoid(v)
    return v * s, s * (1.0 + v * (1.0 - s))


def _dot(a, b, ca, cb):
    return lax.dot_general(a, b, (((ca,), (cb,)), ((), ())), preferred_element_type=F32)


def _block_rows(rows, width, itemsize=4, target=MIB):
    best = 8
    for t in range(8, rows + 1, 8):
        if rows % t == 0 and t * width * itemsize <= target:
            best = t
    return best if rows % 8 == 0 else rows


def _my_pos():
    return lax.axis_index("x"), lax.axis_index("y"), lax.axis_index("c")


def _flip(pos, mask):
    return tuple(jnp.bitwise_xor(p, m) if m else p for p, m in zip(pos, mask))


def _allgather_small(v, *, chips_only, name):
    r, w = v.shape
    masks = CHIP_MASKS if chips_only else ALL_MASKS
    nslot = 4 if chips_only else 8

    def slot(pos):
        return 2 * pos[0] + pos[1] if chips_only else 4 * pos[0] + 2 * pos[1] + pos[2]

    def body(v_ref, o_ref, send_sems, recv_sems, local_sem):
        me = _my_pos()

        def copy(k, block_of, to):
            return pltpu.make_async_remote_copy(
                src_ref=v_ref, dst_ref=o_ref.at[slot(block_of)], send_sem=send_sems.at[k], recv_sem=recv_sems.at[k],
                device_id=to, device_id_type=MESH)

        mine = pltpu.make_async_copy(v_ref, o_ref.at[slot(me)], local_sem)
        mine.start()
        sends = [copy(k, me, _flip(me, m)) for k, m in enumerate(masks)]
        for cp in sends:
            cp.start()
        for k, m in enumerate(masks):
            copy(k, _flip(me, m), me).wait_recv()
        for cp in sends:
            cp.wait_send()
        mine.wait()

    return pl.pallas_call(
        body, name=name, out_shape=jax.ShapeDtypeStruct((nslot, r, w), v.dtype),
        in_specs=[VMEM_SPEC], out_specs=VMEM_SPEC,
        scratch_shapes=[pltpu.SemaphoreType.DMA((len(masks),)), pltpu.SemaphoreType.DMA((len(masks),)),
                        pltpu.SemaphoreType.DMA],
    )(v)


def _sl(ref, axis, start, size):
    idx = [slice(None)] * len(ref.shape)
    idx[axis] = pl.ds(start, size)
    return ref.at[tuple(idx)]


class _Side:
    def __init__(self, ins, out_shapes, aliases, sem_shapes, start, finish, mid=None):
        self.ins, self.out_shapes, self.aliases, self.sem_shapes = list(ins), list(out_shapes), dict(aliases), list(sem_shapes)
        self.start, self.finish, self.mid = start, finish, mid


def _run_side(side, *, name):
    n_in, n_out = len(side.ins), len(side.out_shapes)

    def body(*refs):
        parts = refs[:n_in], refs[n_in:n_in + n_out], refs[n_in + n_out:]
        side.start(*parts)
        if side.mid is not None:
            side.mid(*parts)
        side.finish(*parts)

    return pl.pallas_call(
        body, name=name, out_shape=tuple(side.out_shapes), in_specs=[HBM_SPEC] * n_in, out_specs=tuple([HBM_SPEC] * n_out),
        input_output_aliases=side.aliases, scratch_shapes=side.sem_shapes,
    )(*side.ins)


def _side_at(grid):
    pids = [pl.program_id(a) for a in range(len(grid))]

    def at(values):
        cond = pids[0] == values[0]
        for p, v in zip(pids[1:], values[1:]):
            cond = jnp.logical_and(cond, p == v)
        return cond

    zeros = [0] * len(grid)
    return at(zeros), at([grid[0] // 2] + zeros[1:]), at([g - 1 for g in grid])


def _gather_side(shards, layouts):
    nm = len(shards)

    def copies(outs, sems):
        send_sems, recv_sems, fsend_sems, frecv_sems = sems
        x, y, c = _my_pos()
        me = (x, y, c)
        sib = (x, y, 1 - c)
        sends, landed, fwds, from_sib = [], [], [], []
        for m in range(nm):
            a_s, a_h = layouts[m]
            ns, nh = outs[m].shape[a_s] // 4, outs[m].shape[a_h] // 2

            def region(chip_pos, half, m=m, a_s=a_s, a_h=a_h, ns=ns, nh=nh):
                j = 2 * chip_pos[0] + chip_pos[1]
                return _sl(_sl(outs[m], a_s, j * ns, ns), a_h, half * nh, nh)

            for k, mask in enumerate(CHIP_MASKS):
                other = _flip(me, mask)

                def rc(ref, ssem, rsem, to):
                    return pltpu.make_async_remote_copy(src_ref=ref, dst_ref=ref, send_sem=ssem, recv_sem=rsem,
                                                        device_id=to, device_id_type=MESH)

                sends.append(rc(region(me, c), send_sems.at[m, k], recv_sems.at[m, k], other))
                landed.append(rc(region(other, c), send_sems.at[m, k], recv_sems.at[m, k], me))
                fwds.append(rc(region(other, c), fsend_sems.at[m, k], frecv_sems.at[m, k], sib))
                from_sib.append(rc(region(other, 1 - c), fsend_sems.at[m, k], frecv_sems.at[m, k], sib))
        return sends, landed, fwds, from_sib

    def start(ins, outs, sems):
        for cp in copies(outs, sems)[0]:
            cp.start()

    def mid(ins, outs, sems):
        _, landed, fwds, _ = copies(outs, sems)
        for arrived, fw in zip(landed, fwds):
            arrived.wait_recv()
            fw.start()

    def finish(ins, outs, sems):
        sends, _, fwds, from_sib = copies(outs, sems)
        for cp in from_sib:
            cp.wait_recv()
        for cp in sends + fwds:
            cp.wait_send()

    return _Side(shards, [jax.ShapeDtypeStruct(s.shape, s.dtype) for s in shards], {i: i for i in range(nm)},
                 [pltpu.SemaphoreType.DMA((nm, 3))] * 4, start, finish, mid)


def _both(s1, s2):
    n1, m1, k1 = len(s1.ins), len(s1.out_shapes), len(s1.sem_shapes)

    def split(fn1, fn2):
        def run(ins, outs, sems):
            fn1(ins[:n1], outs[:m1], sems[:k1])
            fn2(ins[n1:], outs[m1:], sems[k1:])
        return run

    aliases = dict(s1.aliases)
    aliases.update({n1 + i: m1 + o for i, o in s2.aliases.items()})
    return _Side(s1.ins + s2.ins, s1.out_shapes + s2.out_shapes, aliases, s1.sem_shapes + s2.sem_shapes,
                 split(s1.start, s2.start), split(s1.finish, s2.finish))


def _swap_side(mats):
    nm = len(mats)

    def copies(ins, outs, sems):
        x, y, c = _my_pos()
        return [pltpu.make_async_remote_copy(src_ref=ins[m], dst_ref=outs[m], send_sem=sems[0].at[m], recv_sem=sems[1].at[m],
                                             device_id=(x, y, 1 - c), device_id_type=MESH) for m in range(nm)]

    def start(ins, outs, sems):
        for cp in copies(ins, outs, sems):
            cp.start()

    def finish(ins, outs, sems):
        for cp in copies(ins, outs, sems):
            cp.wait()

    return _Side(mats, [jax.ShapeDtypeStruct(s.shape, s.dtype) for s in mats], {},
                 [pltpu.SemaphoreType.DMA((nm,))] * 2, start, finish)


def _gather_parts_side(w_own, mask_ids):
    nt = len(mask_ids)
    nh = w_own.shape[0] // 2

    def copies(ins, outs, sems):
        send_sems, recv_sems, fsend_sems, frecv_sems = sems
        x, y, c = _my_pos()
        me, sib = (x, y, c), (x, y, 1 - c)
        sends, landed, fwds, from_sib = [], [], [], []
        for t, k in enumerate(mask_ids):
            mine = _sl(outs[t], 0, c * nh, nh)
            theirs = _sl(outs[t], 0, (1 - c) * nh, nh)

            def rc(src, dst, ssem, rsem, to):
                return pltpu.make_async_remote_copy(src_ref=src, dst_ref=dst, send_sem=ssem, recv_sem=rsem,
                                                    device_id=to, device_id_type=MESH)

            sends.append(rc(_sl(ins[0], 0, c * nh, nh), mine, send_sems.at[t], recv_sems.at[t], _flip(me, CHIP_MASKS[k])))
            landed.append(rc(mine, mine, send_sems.at[t], recv_sems.at[t], me))
            fwds.append(rc(mine, mine, fsend_sems.at[t], frecv_sems.at[t], sib))
            from_sib.append(rc(theirs, theirs, fsend_sems.at[t], frecv_sems.at[t], sib))
        return sends, landed, fwds, from_sib

    def start(ins, outs, sems):
        for cp in copies(ins, outs, sems)[0]:
            cp.start()

    def finish(ins, outs, sems):
        sends, landed, fwds, from_sib = copies(ins, outs, sems)
        for arrived, fw in zip(landed, fwds):
            arrived.wait_recv()
            fw.start()
        for cp in from_sib:
            cp.wait_recv()
        for cp in sends + fwds:
            cp.wait_send()

    return _Side([w_own], [jax.ShapeDtypeStruct(w_own.shape, w_own.dtype)] * nt, {},
                 [pltpu.SemaphoreType.DMA((nt,))] * 4, start, finish)


def _pair_side(mats, half_axes):
    nm = len(mats)

    def copies(ins, outs, sems):
        x, y, c = _my_pos()
        cps = []
        for m in range(nm):
            nh = ins[m].shape[half_axes[m]] // 2
            cps.append(pltpu.make_async_remote_copy(
                src_ref=_sl(ins[m], half_axes[m], (1 - c) * nh, nh), dst_ref=outs[m],
                send_sem=sems[0].at[m], recv_sem=sems[1].at[m], device_id=(x, y, 1 - c), device_id_type=MESH))
        return cps

    def start(ins, outs, sems):
        for cp in copies(ins, outs, sems):
            cp.start()

    def finish(ins, outs, sems):
        for cp in copies(ins, outs, sems):
            cp.wait()

    out_shapes = []
    for s, a_h in zip(mats, half_axes):
        shp = list(s.shape)
        shp[a_h] //= 2
        out_shapes.append(jax.ShapeDtypeStruct(tuple(shp), s.dtype))
    return _Side(mats, out_shapes, {}, [pltpu.SemaphoreType.DMA((nm,))] * 2, start, finish)


def _chips_side(mats, shard_axes):
    nm = len(mats)

    def copies(ins, outs, sems):
        me = _my_pos()
        cps = []
        for m in range(nm):
            ns = ins[m].shape[shard_axes[m]] // 4
            for k, mask in enumerate(CHIP_MASKS):
                other = _flip(me, mask)
                cps.append(pltpu.make_async_remote_copy(
                    src_ref=_sl(ins[m], shard_axes[m], (2 * other[0] + other[1]) * ns, ns), dst_ref=outs[3 * m + k],
                    send_sem=sems[0].at[m, k], recv_sem=sems[1].at[m, k], device_id=other, device_id_type=MESH))
        return cps

    def start(ins, outs, sems):
        for cp in copies(ins, outs, sems):
            cp.start()

    def finish(ins, outs, sems):
        for cp in copies(ins, outs, sems):
            cp.wait()

    out_shapes = []
    for s, a_s in zip(mats, shard_axes):
        shp = list(s.shape)
        shp[a_s] //= 4
        out_shapes += [jax.ShapeDtypeStruct(tuple(shp), s.dtype)] * 3
    return _Side(mats, out_shapes, {}, [pltpu.SemaphoreType.DMA((nm, 3))] * 2, start, finish)


def _join_side(halves, half_axes):
    nm = len(halves)

    def copies(outs, sems):
        x, y, c = _my_pos()
        mine, theirs = [], []
        for m in range(nm):
            nh = outs[m].shape[half_axes[m]] // 2
            for half, lst in ((c, mine), (1 - c, theirs)):
                ref = _sl(outs[m], half_axes[m], half * nh, nh)
                lst.append(pltpu.make_async_remote_copy(
                    src_ref=ref, dst_ref=ref, send_sem=sems[0].at[m], recv_sem=sems[1].at[m],
                    device_id=(x, y, 1 - c), device_id_type=MESH))
        return mine, theirs

    def start(ins, outs, sems):
        for cp in copies(outs, sems)[0]:
            cp.start()

    def finish(ins, outs, sems):
        mine, theirs = copies(outs, sems)
        for cp in mine:
            cp.wait_send()
        for cp in theirs:
            cp.wait_recv()

    return _Side(halves, [jax.ShapeDtypeStruct(s.shape, s.dtype) for s in halves], {i: i for i in range(nm)},
                 [pltpu.SemaphoreType.DMA((nm,))] * 2, start, finish)


def _part_specs(shp, axis, slot, itemsize, target=MIB):
    if len(shp) == 2:
        rows, width = shp
        tm = _block_rows(rows, width, itemsize, target)
        nb = rows // tm
        own = pl.BlockSpec((tm, width), lambda i, p: (i, 0))
        if axis == 1:
            part = pl.BlockSpec((tm, width), lambda i, p: (i, p[slot]))
        else:
            part = pl.BlockSpec((tm, width), lambda i, p: (p[slot] * nb + i, 0))
        return (nb,), own, part
    assert len(shp) == 3 and axis in (1, 2)
    g, rows, width = shp
    own = pl.BlockSpec((1, rows, width), lambda i, p: (i, 0, 0))
    if axis == 1:
        part = pl.BlockSpec((1, rows, width), lambda i, p: (i, p[slot], 0))
    else:
        part = pl.BlockSpec((1, rows, width), lambda i, p: (i, 0, p[slot]))
    return (g,), own, part


def _cast_into(w, axis, p, *, name):
    grid, own, part = _part_specs(w.shape, axis, 0, 4, 2 * MIB)
    big = list(w.shape)
    big[axis] *= 4

    def body(p_ref, w_ref, o_ref):
        o_ref[...] = w_ref[...].astype(BF)

    return pl.pallas_call(
        body, name=name, out_shape=jax.ShapeDtypeStruct(tuple(big), BF),
        grid_spec=pltpu.PrefetchScalarGridSpec(num_scalar_prefetch=1, grid=grid, in_specs=[own], out_specs=part),
        compiler_params=_cparams("parallel"),
    )(p.reshape(1), w)


def _cast_bf16(w, *, name):
    rows, width = w.shape
    tm = _block_rows(rows, width, 4, 2 * MIB)

    def body(w_ref, o_ref):
        o_ref[...] = w_ref[...].astype(BF)

    spec = pl.BlockSpec((tm, width), lambda i: (i, 0))
    return pl.pallas_call(body, name=name, grid=(rows // tm,), out_shape=jax.ShapeDtypeStruct(w.shape, BF),
                          in_specs=[spec], out_specs=spec, compiler_params=_cparams("parallel"))(w)


def _add2(a, b, *, name):
    rows, width = a.shape
    tm = _block_rows(rows, width, 2, 4 * MIB)

    def body(a_ref, b_ref, o_ref):
        o_ref[...] = (a_ref[...].astype(F32) + b_ref[...].astype(F32)).astype(BF)

    spec = pl.BlockSpec((tm, width), lambda i: (i, 0))
    return pl.pallas_call(body, name=name, grid=(rows // tm,), out_shape=jax.ShapeDtypeStruct(a.shape, BF),
                          in_specs=[spec, spec], out_specs=spec, compiler_params=_cparams("parallel"))(a, b)


def _add_own_half(full, recv, half_axis, c, *, name):
    grid, own, part = _part_specs(recv.shape, half_axis, 0, 2, 4 * MIB)

    def body(c_ref, f_ref, r_ref, o_ref):
        o_ref[...] = (f_ref[...].astype(F32) + r_ref[...].astype(F32)).astype(BF)

    return pl.pallas_call(
        body, name=name, out_shape=jax.ShapeDtypeStruct(recv.shape, BF),
        grid_spec=pltpu.PrefetchScalarGridSpec(num_scalar_prefetch=1, grid=grid, in_specs=[part, own], out_specs=own),
        compiler_params=_cparams("parallel"),
    )(c.reshape(1), full, recv)


def _sum_own_block(mine_all, recvs, shard_axis, half_axis, j, c, *, name):
    shp = recvs[0].shape
    grid, own, mine = _part_specs(shp, shard_axis, 0, 4, 2 * MIB)
    _, _, place = _part_specs(shp, half_axis, 1, 4, 2 * MIB)
    big = list(shp)
    big[half_axis] *= 2

    def body(p_ref, a_ref, r0, r1, r2, o_ref):
        o_ref[...] = ((a_ref[...].astype(F32) + r0[...].astype(F32)) + r1[...].astype(F32)) + r2[...].astype(F32)

    return pl.pallas_call(
        body, name=name, out_shape=jax.ShapeDtypeStruct(tuple(big), F32),
        grid_spec=pltpu.PrefetchScalarGridSpec(
            num_scalar_prefetch=1, grid=grid, in_specs=[mine, own, own, own], out_specs=place),
        compiler_params=_cparams("parallel"),
    )(jnp.stack([j, c]), mine_all, *recvs)


def _adamw(w, g, m, v, *, name):
    shp = w.shape
    width = shp[-1] if len(shp) > 1 else shp[0]
    rows = 1
    for d in shp[:-1]:
        rows *= d
    if len(shp) == 1:
        rows = 1
    args = [a.reshape(rows, width) for a in (w, g, m, v)]
    tm = _block_rows(rows, width, 4, 2 * MIB)

    def body(w_ref, g_ref, m_ref, v_ref, d_ref, nm_ref, nv_ref):
        gv = g_ref[...]
        nm = ADAM_B1 * m_ref[...] + (1.0 - ADAM_B1) * gv
        nv = ADAM_B2 * v_ref[...] + (1.0 - ADAM_B2) * (gv * gv)
        m_hat = nm / (1.0 - ADAM_B1 ** ADAM_STEP)
        v_hat = nv / (1.0 - ADAM_B2 ** ADAM_STEP)
        d_ref[...] = -ADAM_LR * (m_hat / (jnp.sqrt(v_hat) + ADAM_EPS) + ADAM_WD * w_ref[...])
        nm_ref[...] = nm
        nv_ref[...] = nv

    spec = pl.BlockSpec((tm, width), lambda i: (i, 0))
    outs = pl.pallas_call(
        body, name=name, grid=(rows // tm,), out_shape=tuple([jax.ShapeDtypeStruct((rows, width), F32)] * 3),
        in_specs=[spec] * 4, out_specs=tuple([spec] * 3), compiler_params=_cparams("parallel"),
    )(*args)
    return tuple(o.reshape(shp) for o in outs)


def _matmul(a, b, *, ca, cb, tm, tn, tk, out_dtype, name, b_resident=False, side=None):
    m, kdim = a.shape[1 - ca], a.shape[ca]
    n = b.shape[1 - cb]
    nk = kdim // tk
    assert m % tm == 0 and n % tn == 0 and kdim % tk == 0
    n_sin = len(side.ins) if side else 0
    n_sout = len(side.out_shapes) if side else 0
    if b_resident:
        gi = lambda p, q, k: (q, p, k)
        grid = (n // tn, m // tm, nk)
    else:
        gi = lambda p, q, k: (p, q, k)
        grid = (m // tm, n // tn, nk)

    def body(*refs):
        a_ref, b_ref = refs[0], refs[1]
        s_ins = refs[2:2 + n_sin]
        o_ref = refs[2 + n_sin]
        s_outs = refs[3 + n_sin:3 + n_sin + n_sout]
        rest = refs[3 + n_sin + n_sout:]
        acc, sems = (rest[:1], rest[1:]) if nk > 1 else ((), rest)
        if side:
            first, middle, last = _side_at(grid)
            pl.when(first)(lambda: side.start(s_ins, s_outs, sems))
            if side.mid is not None:
                pl.when(middle)(lambda: side.mid(s_ins, s_outs, sems))
        part = _dot(a_ref[...], b_ref[...], ca, cb)
        if nk == 1:
            o_ref[...] = part.astype(o_ref.dtype)
        else:
            k = pl.program_id(2)

            @pl.when(k == 0)
            def _():
                acc[0][...] = part

            @pl.when(jnp.logical_and(k > 0, k < nk - 1))
            def _():
                acc[0][...] += part

            @pl.when(k == nk - 1)
            def _():
                o_ref[...] = (acc[0][...] + part).astype(o_ref.dtype)
        if side:
            pl.when(last)(lambda: side.finish(s_ins, s_outs, sems))

    def a_map(p, q, k):
        i, _, kk = gi(p, q, k)
        return (i, kk) if ca == 1 else (kk, i)

    def b_map(p, q, k):
        _, j, kk = gi(p, q, k)
        return (kk, j) if cb == 0 else (j, kk)

    def o_map(p, q, k):
        i, j, _ = gi(p, q, k)
        return (i, j)

    a_spec = pl.BlockSpec((tm, tk) if ca == 1 else (tk, tm), a_map)
    b_spec = pl.BlockSpec((tk, tn) if cb == 0 else (tn, tk), b_map)
    o_shape = jax.ShapeDtypeStruct((m, n), out_dtype)
    acc_shapes = [pltpu.VMEM((tm, tn), F32)] if nk > 1 else []
    if not side:
        return pl.pallas_call(
            body, name=name, grid=grid, out_shape=o_shape,
            in_specs=[a_spec, b_spec], out_specs=pl.BlockSpec((tm, tn), o_map), scratch_shapes=acc_shapes,
            compiler_params=_cparams("parallel", "parallel", "arbitrary"),
        )(a, b)
    return pl.pallas_call(
        body, name=name, grid=grid, out_shape=(o_shape, *side.out_shapes),
        in_specs=[a_spec, b_spec] + [HBM_SPEC] * n_sin,
        out_specs=(pl.BlockSpec((tm, tn), o_map), *([HBM_SPEC] * n_sout)),
        input_output_aliases={2 + i: 1 + o for i, o in side.aliases.items()},
        scratch_shapes=acc_shapes + side.sem_shapes,
        compiler_params=_cparams("arbitrary", "arbitrary", "arbitrary"),
    )(a, b, *side.ins)


def _wgrad_rows(a, b, half, *, tm, tn, name, side=None):
    kdim, m = a.shape
    n = b.shape[1]
    nb = (m // 2) // tm
    grid = (nb, n // tn)
    n_sin = len(side.ins) if side else 0
    n_sout = len(side.out_shapes) if side else 0

    def body(p_ref, a_ref, b_ref, *refs):
        s_ins, o_ref = refs[:n_sin], refs[n_sin]
        s_outs, sems = refs[n_sin + 1:n_sin + 1 + n_sout], refs[n_sin + 1 + n_sout:]
        if side:
            first, _, last = _side_at(grid)
            pl.when(first)(lambda: side.start(s_ins, s_outs, sems))
        o_ref[...] = _dot(a_ref[...], b_ref[...], 0, 0).astype(o_ref.dtype)
        if side:
            pl.when(last)(lambda: side.finish(s_ins, s_outs, sems))

    outs = pl.pallas_call(
        body, name=name, out_shape=(jax.ShapeDtypeStruct((m // 2, n), BF), *(side.out_shapes if side else [])),
        grid_spec=pltpu.PrefetchScalarGridSpec(
            num_scalar_prefetch=1, grid=grid,
            in_specs=[pl.BlockSpec((kdim, tm), lambda i, j, p: (0, p[0] * nb + i)),
                      pl.BlockSpec((kdim, tn), lambda i, j, p: (0, j))] + [HBM_SPEC] * n_sin,
            out_specs=(pl.BlockSpec((tm, tn), lambda i, j, p: (i, j)), *([HBM_SPEC] * n_sout)),
            scratch_shapes=side.sem_shapes if side else []),
        input_output_aliases={3 + i: 1 + o for i, o in side.aliases.items()} if side else {},
        compiler_params=_cparams("arbitrary", "arbitrary"),
    )(half.reshape(1), a, b, *(side.ins if side else []))
    return outs if side else outs[0]


def _in_proj_parts(h, w_parts, ids, prev, *, width, tm, tn, name, side=None):
    s_len, d = h.shape
    nb = w_parts[0].shape[1] // tn
    n_p = len(w_parts)
    n_prev = 0 if prev is None else 1
    n_sin = len(side.ins) if side else 0
    n_sout = len(side.out_shapes) if side else 0
    grid = (n_p, s_len // tm, nb)
    base = 1 + n_p + n_prev

    def body(ids_ref, *refs):
        h_ref, w_refs = refs[0], refs[1:1 + n_p]
        s_ins = refs[base:base + n_sin]
        o_ref = refs[base + n_sin]
        s_outs = refs[base + n_sin + 1:base + n_sin + 1 + n_sout]
        sems = refs[base + n_sin + 1 + n_sout:]
        if side:
            first, _, last = _side_at(grid)
            pl.when(first)(lambda: side.start(s_ins, s_outs, sems))
        for s in range(n_p):
            @pl.when(pl.program_id(0) == s)
            def _(s=s):
                o_ref[...] = _dot(h_ref[...], w_refs[s][...], 1, 0).astype(o_ref.dtype)
        if side:
            pl.when(last)(lambda: side.finish(s_ins, s_outs, sems))

    in_specs = [pl.BlockSpec((tm, d), lambda p, i, j, ids_ref: (i, 0))]
    for s in range(n_p):
        in_specs.append(pl.BlockSpec((d, tn), lambda p, i, j, ids_ref, s=s: (0, jnp.where(p == s, j, 0))))
    in_specs += [HBM_SPEC] * (n_prev + n_sin)
    o_spec = pl.BlockSpec((tm, tn), lambda p, i, j, ids_ref: (i, ids_ref[p] * nb + j))
    aliases = {2 + n_p: 0} if prev is not None else {}
    if side:
        aliases.update({1 + base + i: 1 + o for i, o in side.aliases.items()})
    outs = pl.pallas_call(
        body, name=name, out_shape=(jax.ShapeDtypeStruct((s_len, width), BF), *(side.out_shapes if side else [])),
        grid_spec=pltpu.PrefetchScalarGridSpec(
            num_scalar_prefetch=1, grid=grid, in_specs=in_specs, out_specs=(o_spec, *([HBM_SPEC] * n_sout)),
            scratch_shapes=side.sem_shapes if side else []),
        input_output_aliases=aliases, compiler_params=_cparams("arbitrary", "arbitrary", "arbitrary"),
    )(ids, h, *w_parts, *([prev] if prev is not None else []), *(side.ins if side else []))
    return outs


def _in_proj_own(x, ctx, g_pre, ss, w_own, ids, *, width, tm, tn, side):
    n, d = x.shape
    l = ctx.shape[0]
    nc = l // tm
    nb = w_own.shape[1] // tn
    grid = ((l + n) // tm, nb)
    n_sin, n_sout = len(side.ins), len(side.out_shapes)

    def body(ids_ref, x_ref, c_ref, g_ref, ss_ref, w_ref, *refs):
        s_ins = refs[:n_sin]
        o_ref, h_ref = refs[n_sin:n_sin + 2]
        s_outs, sems = refs[n_sin + 2:n_sin + 2 + n_sout], refs[n_sin + 2 + n_sout:]
        first, _, last = _side_at(grid)
        pl.when(first)(lambda: side.start(s_ins, s_outs, sems))
        t, j = pl.program_id(0), pl.program_id(1)

        def norm(src, scale, shift):
            v = src[...]
            rstd = lax.rsqrt(jnp.mean(v * v, axis=-1, keepdims=True) + EPS)
            h_ref[...] = ((v * rstd * g_ref[...]) * (1.0 + scale) + shift).astype(BF)

        @pl.when(jnp.logical_and(j == 0, t < nc))
        def _():
            norm(c_ref, ss_ref[2:3, :], ss_ref[3:4, :])

        @pl.when(jnp.logical_and(j == 0, t >= nc))
        def _():
            norm(x_ref, ss_ref[0:1, :], ss_ref[1:2, :])

        o_ref[...] = _dot(h_ref[...], w_ref[...], 1, 0).astype(o_ref.dtype)
        pl.when(last)(lambda: side.finish(s_ins, s_outs, sems))

    return pl.pallas_call(
        body, name="in_proj_own",
        out_shape=(jax.ShapeDtypeStruct((l + n, width), BF), jax.ShapeDtypeStruct((l + n, d), BF), *side.out_shapes),
        grid_spec=pltpu.PrefetchScalarGridSpec(
            num_scalar_prefetch=1, grid=grid,
            in_specs=[pl.BlockSpec((tm, d), lambda t, j, ids_ref: (jnp.maximum(t - nc, 0), 0)),
                      pl.BlockSpec((tm, d), lambda t, j, ids_ref: (jnp.minimum(t, nc - 1), 0)),
                      pl.BlockSpec((1, d), lambda t, j, ids_ref: (0, 0)), pl.BlockSpec((4, d), lambda t, j, ids_ref: (0, 0)),
                      pl.BlockSpec((d, tn), lambda t, j, ids_ref: (0, j))] + [HBM_SPEC] * n_sin,
            out_specs=(pl.BlockSpec((tm, tn), lambda t, j, ids_ref: (t, ids_ref[0] * nb + j)),
                       pl.BlockSpec((tm, d), lambda t, j, ids_ref: (t, 0)), *([HBM_SPEC] * n_sout)),
            scratch_shapes=side.sem_shapes),
        input_output_aliases={6 + i: 2 + o for i, o in side.aliases.items()},
        compiler_params=_cparams("arbitrary", "arbitrary"),
    )(ids, x, ctx, g_pre, ss, w_own, *side.ins)


def _dgrad_in_parts(dproj, w_parts, ids, *, tm, tn, name, side=None):
    s_len = dproj.shape[0]
    d, ws = w_parts[0].shape
    n_p = len(w_parts)
    n_sin = len(side.ins) if side else 0
    n_sout = len(side.out_shapes) if side else 0
    grid = (s_len // tm, d // tn)

    def body(ids_ref, *refs):
        a_refs, w_refs = refs[:n_p], refs[n_p:2 * n_p]
        s_ins = refs[2 * n_p:2 * n_p + n_sin]
        o_ref = refs[2 * n_p + n_sin]
        s_outs = refs[2 * n_p + n_sin + 1:2 * n_p + n_sin + 1 + n_sout]
        sems = refs[2 * n_p + n_sin + 1 + n_sout:]
        if side:
            first, _, last = _side_at(grid)
            pl.when(first)(lambda: side.start(s_ins, s_outs, sems))
        tot = _dot(a_refs[0][...], w_refs[0][...], 1, 1)
        for s in range(1, n_p):
            tot = tot + _dot(a_refs[s][...], w_refs[s][...], 1, 1)
        o_ref[...] = tot.astype(o_ref.dtype)
        if side:
            pl.when(last)(lambda: side.finish(s_ins, s_outs, sems))

    in_specs = [pl.BlockSpec((tm, ws), lambda i, j, ids_ref, s=s: (i, ids_ref[s])) for s in range(n_p)]
    in_specs += [pl.BlockSpec((tn, ws), lambda i, j, ids_ref: (j, 0))] * n_p
    in_specs += [HBM_SPEC] * n_sin
    aliases = {1 + 2 * n_p + i: 1 + o for i, o in side.aliases.items()} if side else {}
    return pl.pallas_call(
        body, name=name, out_shape=(jax.ShapeDtypeStruct((s_len, d), BF), *(side.out_shapes if side else [])),
        grid_spec=pltpu.PrefetchScalarGridSpec(
            num_scalar_prefetch=1, grid=grid, in_specs=in_specs,
            out_specs=(pl.BlockSpec((tm, tn), lambda i, j, ids_ref: (i, j)), *([HBM_SPEC] * n_sout)),
            scratch_shapes=side.sem_shapes if side else []),
        input_output_aliases=aliases, compiler_params=_cparams("arbitrary", "arbitrary"),
    )(ids, *([dproj] * n_p), *w_parts, *(side.ins if side else []))


def _swap32(v):
    lane = lax.broadcasted_iota(jnp.int32, v.shape, 1)
    return jnp.where((lane % 64) < 32, pltpu.roll(v, 96, 1), pltpu.roll(v, 32, 1))


def _qk_prep(proj, gain, ctab, stab, *, row0, col0, width, nrows, tm, name, out_scale=1.0):
    cw = min(512, width)
    rb0, cb0 = row0 // tm, col0 // cw
    assert row0 % tm == 0 and col0 % cw == 0 and width % cw == 0 and nrows % tm == 0

    def body(p_ref, g_ref, c_ref, s_ref, o_ref):
        for hd in range(cw // HEAD_DIM):
            cols = slice(hd * HEAD_DIM, (hd + 1) * HEAD_DIM)
            v = p_ref[:, cols].astype(F32)
            rstd = lax.rsqrt(jnp.mean(v * v, axis=-1, keepdims=True) + EPS)
            yv = v * rstd * g_ref[...]
            roped = yv * c_ref[...] + _swap32(yv) * s_ref[...]
            o_ref[:, cols] = (roped if out_scale == 1.0 else roped * out_scale).astype(BF)

    return pl.pallas_call(
        body, name=name, grid=(nrows // tm, width // cw), out_shape=jax.ShapeDtypeStruct((nrows, width), BF),
        in_specs=[pl.BlockSpec((tm, cw), lambda i, j: (i + rb0, j + cb0)), pl.BlockSpec((1, HEAD_DIM), lambda i, j: (0, 0)),
                  pl.BlockSpec((tm, HEAD_DIM), lambda i, j: (i, 0)), pl.BlockSpec((tm, HEAD_DIM), lambda i, j: (i, 0))],
        out_specs=pl.BlockSpec((tm, cw), lambda i, j: (i, j)), compiler_params=_cparams("parallel", "parallel"),
    )(proj, gain, ctab, stab)


def _zero_rows(rows, shape, *, name):
    def body(o_ref):
        o_ref[...] = jnp.zeros_like(o_ref)

    return pl.pallas_call(body, name=name, grid=(1,), out_shape=jax.ShapeDtypeStruct(shape, BF),
                          out_specs=pl.BlockSpec((rows, shape[1]), lambda i: (0, 0)))()


def _qk_bwd(dy, proj, gain, ctab, stab, into, *, row0, col0, tm, name, transposed=False):
    if transposed:
        tq = dy.shape[-1]
        nrows, width = dy.shape[1] * tq, dy.shape[0] * GQA_GROUP * HEAD_DIM
        per = tq // tm
        dy_spec = pl.BlockSpec((1, 1, GQA_GROUP, HEAD_DIM, tm), lambda i, j: (j, i // per, 0, 0, i % per))
    else:
        nrows, width = dy.shape
    cw = min(512, width)
    rb0, cb0 = row0 // tm, col0 // cw
    if transposed:
        assert cw == GQA_GROUP * HEAD_DIM and tq % tm == 0
    else:
        dy_spec = pl.BlockSpec((tm, cw), lambda i, j: (i, j))

    def body(d_ref, p_ref, g_ref, c_ref, s_ref, into_ref, o_ref, dg_ref):
        @pl.when(jnp.logical_and(pl.program_id(0) == 0, pl.program_id(1) == 0))
        def _():
            dg_ref[...] = jnp.zeros_like(dg_ref)

        dg = jnp.zeros((1, HEAD_DIM), F32)
        for hd in range(cw // HEAD_DIM):
            cols = slice(hd * HEAD_DIM, (hd + 1) * HEAD_DIM)
            v = p_ref[:, cols].astype(F32)
            rstd = lax.rsqrt(jnp.mean(v * v, axis=-1, keepdims=True) + EPS)
            nv = v * rstd
            d = d_ref[0, 0, hd].T if transposed else d_ref[:, cols]
            dyu = d * c_ref[...] + _swap32(d * s_ref[...])
            dg = dg + jnp.sum(dyu * nv, axis=0, keepdims=True)
            dn = dyu * g_ref[...]
            o_ref[:, cols] = (rstd * (dn - nv * jnp.mean(dn * nv, axis=-1, keepdims=True))).astype(BF)
        dg_ref[0:1, :] += dg

    placed = pl.BlockSpec((tm, cw), lambda i, j: (i + rb0, j + cb0))
    return pl.pallas_call(
        body, name=name, grid=(nrows // tm, width // cw),
        out_shape=(jax.ShapeDtypeStruct(into.shape, BF), jax.ShapeDtypeStruct((8, HEAD_DIM), F32)),
        in_specs=[dy_spec, placed, pl.BlockSpec((1, HEAD_DIM), lambda i, j: (0, 0)),
                  pl.BlockSpec((tm, HEAD_DIM), lambda i, j: (i, 0)), pl.BlockSpec((tm, HEAD_DIM), lambda i, j: (i, 0)), HBM_SPEC],
        out_specs=(placed, pl.BlockSpec((8, HEAD_DIM), lambda i, j: (0, 0))),
        input_output_aliases={5: 0}, compiler_params=_cparams("arbitrary", "arbitrary"),
    )(dy, proj, gain, ctab, stab, into)


def _flash_fwd(qr, k_all, vt_ctx, vt_lat, *, tq, side=None):
    n, aw = qr.shape
    s_len, kvw = k_all.shape
    l = vt_ctx.shape[-1]
    n_j, tk = vt_lat.shape[1], vt_lat.shape[-1]
    kvh = kvw // HEAD_DIM
    n_i = n // tq
    gw = GQA_GROUP * HEAD_DIM
    n_sin = len(side.ins) if side else 0
    n_sout = len(side.out_shapes) if side else 0

    def body(*refs):
        q_ref, k_ref, vc_ref, vl_ref = refs[:4]
        s_ins = refs[4:4 + n_sin]
        o_ref, lse_ref = refs[4 + n_sin:6 + n_sin]
        s_outs = refs[6 + n_sin:6 + n_sin + n_sout]
        acc_ref, m_ref, l_ref = refs[6 + n_sin + n_sout:9 + n_sin + n_sout]
        sems = refs[9 + n_sin + n_sout:]
        if side:
            first, middle, last = _side_at((kvh, n_i))
            pl.when(first)(lambda: side.start(s_ins, s_outs, sems))
            if side.mid is not None:
                pl.when(middle)(lambda: side.mid(s_ins, s_outs, sems))
        acc_ref[...] = jnp.zeros_like(acc_ref)
        l_ref[...] = jnp.zeros_like(l_ref)
        m_ref[...] = jnp.full(m_ref.shape, -1e30, F32)

        def tile(kj, vtj):
            for g in range(GQA_GROUP):
                st = _dot(kj, q_ref[:, g * HEAD_DIM:(g + 1) * HEAD_DIM], 1, 1)
                m_old = m_ref[g]
                m_new = jnp.maximum(m_old, jnp.max(st, axis=0, keepdims=True))
                alpha = jnp.exp2(m_old - m_new)
                pt = jnp.exp2(st - m_new)
                l_ref[g] = alpha * l_ref[g] + jnp.sum(pt, axis=0, keepdims=True)
                m_ref[g] = m_new
                acc_ref[g] = acc_ref[g] * alpha + _dot(vtj, pt.astype(BF), 1, 0)

        tile(k_ref[0:l, :], vc_ref[0, 0])

        def step(j, carry):
            tile(k_ref[pl.ds(pl.multiple_of(l + j * tk, min(l, tk)), tk), :], vl_ref[0, j])
            return carry

        lax.fori_loop(0, n_j, step, 0)
        for g in range(GQA_GROUP):
            o_ref[:, g * HEAD_DIM:(g + 1) * HEAD_DIM] = (acc_ref[g] / l_ref[g]).T.astype(BF)
            lse_ref[0, g, 0] = m_ref[g] + jnp.log(l_ref[g]) * LOG2E
        if side:
            pl.when(last)(lambda: side.finish(s_ins, s_outs, sems))

    return pl.pallas_call(
        body, name="flash_fwd", grid=(kvh, n_i),
        out_shape=(jax.ShapeDtypeStruct((n, aw), BF), jax.ShapeDtypeStruct((kvh, GQA_GROUP, n_i, 1, tq), F32),
                   *(side.out_shapes if side else [])),
        in_specs=[pl.BlockSpec((tq, gw), lambda h, i: (i, h)), pl.BlockSpec((s_len, HEAD_DIM), lambda h, i: (0, h)),
                  pl.BlockSpec((1, 1, HEAD_DIM, l), lambda h, i: (h, 0, 0, 0)),
                  pl.BlockSpec((1, n_j, HEAD_DIM, tk), lambda h, i: (h, 0, 0, 0))] + [HBM_SPEC] * n_sin,
        out_specs=(pl.BlockSpec((tq, gw), lambda h, i: (i, h)),
                   pl.BlockSpec((1, GQA_GROUP, 1, 1, tq), lambda h, i: (h, 0, i, 0, 0)), *([HBM_SPEC] * n_sout)),
        input_output_aliases={4 + i: 2 + o for i, o in side.aliases.items()} if side else {},
        scratch_shapes=[pltpu.VMEM((GQA_GROUP, HEAD_DIM, tq), F32), pltpu.VMEM((GQA_GROUP, 1, tq), F32),
                        pltpu.VMEM((GQA_GROUP, 1, tq), F32)] + (side.sem_shapes if side else []),
        compiler_params=_cparams("arbitrary", "arbitrary") if side else _cparams("parallel", "parallel"),
    )(qr, k_all, vt_ctx, vt_lat, *(side.ins if side else []))


def _attn_delta(o, do, *, tq):
    n, aw = o.shape
    kvh = aw // (GQA_GROUP * HEAD_DIM)
    gw = GQA_GROUP * HEAD_DIM

    def body(o_ref, do_ref, d_ref):
        for g in range(GQA_GROUP):
            cols = slice(g * HEAD_DIM, (g + 1) * HEAD_DIM)
            prod = o_ref[:, cols].astype(F32) * do_ref[:, cols].astype(F32)
            d_ref[0, g, 0] = jnp.sum(prod.T, axis=0, keepdims=True)

    return pl.pallas_call(
        body, name="attn_delta", grid=(kvh, n // tq),
        out_shape=jax.ShapeDtypeStruct((kvh, GQA_GROUP, n // tq, 1, tq), F32),
        in_specs=[pl.BlockSpec((tq, gw), lambda h, i: (i, h)), pl.BlockSpec((tq, gw), lambda h, i: (i, h))],
        out_specs=pl.BlockSpec((1, GQA_GROUP, 1, 1, tq), lambda h, i: (h, 0, i, 0, 0)),
        compiler_params=_cparams("parallel", "parallel"),
    )(o, do)


def _flash_bwd(qr, do, k_all, proj, lse, delta, into, *, v_col0, tq, tk, scale, side=None):
    n, aw = qr.shape
    s_len, kvw = k_all.shape
    kvh = kvw // HEAD_DIM
    n_i, n_j = n // tq, s_len // tk
    gw = GQA_GROUP * HEAD_DIM
    vb0 = v_col0 // HEAD_DIM
    n_sin = len(side.ins) if side else 0
    n_sout = len(side.out_shapes) if side else 0

    def body(*refs):
        q_ref, do_ref, k_ref, v_ref, lse_ref, dl_ref = refs[:6]
        s_ins = refs[7:7 + n_sin]
        dq_ref, dk_ref, dvo_ref = refs[7 + n_sin:10 + n_sin]
        s_outs = refs[10 + n_sin:10 + n_sin + n_sout]
        dv_ref = refs[10 + n_sin + n_sout]
        sems = refs[11 + n_sin + n_sout:]
        jj = pl.program_id(1)
        if side:
            first, _, last = _side_at((kvh, n_j))
            pl.when(first)(lambda: side.start(s_ins, s_outs, sems))

        @pl.when(jj == 0)
        def _():
            dq_ref[...] = jnp.zeros_like(dq_ref)

        kj = k_ref[...]
        vj = v_ref[...]
        kjt = kj.astype(F32).T.astype(BF)
        dk_ref[...] = jnp.zeros_like(dk_ref)
        dv_ref[...] = jnp.zeros_like(dv_ref)

        def step(i, carry):
            rows = pl.ds(pl.multiple_of(i * tq, tq), tq)
            dv_part = dk_part = None
            for g in range(GQA_GROUP):
                cols = slice(g * HEAD_DIM, (g + 1) * HEAD_DIM)
                qg = q_ref[rows, cols]
                dog = do_ref[rows, cols]
                pt = jnp.exp2(_dot(kj, qg, 1, 1) - lse_ref[0, g, i])
                dst = (pt * (_dot(vj, dog, 1, 1) - dl_ref[0, g, i])).astype(BF)
                dv_g = _dot(pt.astype(BF), dog, 1, 0)
                dk_g = _dot(dst, qg, 1, 0)
                dv_part = dv_g if dv_part is None else dv_part + dv_g
                dk_part = dk_g if dk_part is None else dk_part + dk_g
                dq_ref[0, i, g] += _dot(kjt, dst, 1, 0)
            dv_ref[...] += dv_part
            dk_ref[...] += dk_part
            return carry

        lax.fori_loop(0, n_i, step, 0)
        dk_ref[...] = dk_ref[...] * (1.0 / LOG2E)
        dvo_ref[...] = dv_ref[...].astype(BF)

        @pl.when(jj == n_j - 1)
        def _():
            dq_ref[...] = dq_ref[...] * scale

        if side:
            pl.when(last)(lambda: side.finish(s_ins, s_outs, sems))

    stat_spec = pl.BlockSpec((1, GQA_GROUP, n_i, 1, tq), lambda h, j: (h, 0, 0, 0, 0))
    kv_spec = pl.BlockSpec((tk, HEAD_DIM), lambda h, j: (j, h))
    v_spec = pl.BlockSpec((tk, HEAD_DIM), lambda h, j: (j, vb0 + h))
    q_spec = pl.BlockSpec((n, gw), lambda h, j: (0, h))
    dq_spec = pl.BlockSpec((1, n_i, GQA_GROUP, HEAD_DIM, tq), lambda h, j: (h, 0, 0, 0, 0))
    aliases = {6: 2}
    if side:
        aliases.update({7 + i: 3 + o for i, o in side.aliases.items()})
    return pl.pallas_call(
        body, name="flash_bwd", grid=(kvh, n_j),
        out_shape=(jax.ShapeDtypeStruct((kvh, n_i, GQA_GROUP, HEAD_DIM, tq), F32), jax.ShapeDtypeStruct((s_len, kvw), F32),
                   jax.ShapeDtypeStruct(into.shape, BF), *(side.out_shapes if side else [])),
        in_specs=[q_spec, q_spec, kv_spec, v_spec, stat_spec, stat_spec, HBM_SPEC] + [HBM_SPEC] * n_sin,
        out_specs=(dq_spec, kv_spec, v_spec, *([HBM_SPEC] * n_sout)),
        input_output_aliases=aliases,
        scratch_shapes=[pltpu.VMEM((tk, HEAD_DIM), F32)] + (side.sem_shapes if side else []),
        compiler_params=_cparams("arbitrary", "arbitrary"),
    )(qr, do, k_all, proj, lse, delta, into, *(side.ins if side else []))


def _pool_window(r, gi, l, n, tm):
    win = tm + 2 * POOL_HALO
    start = jnp.clip(l + r * tm - POOL_HALO, l, l + n - win)
    half = lax.shift_left(jnp.int32(1), gi)
    tok = r * tm + lax.broadcasted_iota(jnp.int32, (tm, win), 0)
    src = (start - l) + lax.broadcasted_iota(jnp.int32, (tm, win), 1)
    tok1 = r * tm + lax.broadcasted_iota(jnp.int32, (tm, 1), 0)
    cnt = (jnp.minimum(tok1 + half, n) - jnp.maximum(tok1 - half, 0)).astype(F32)
    return start, win, tok, src, half, cnt


def _pool_fwd(proj, pool_w, *, l, n, col0, tm):
    s_len = proj.shape[0]
    pg = pool_w.shape[-1]
    cb0 = col0 // pg
    assert col0 % pg == 0 and l % POOL_HALO == 0 and n >= tm + 2 * POOL_HALO

    def body(u_ref, w_ref, d_ref, po_ref):
        gi, r = pl.program_id(0), pl.program_id(1)
        start, win, tok, src, half, cnt = _pool_window(r, gi, l, n, tm)
        band = jnp.logical_and(src >= tok - half, src < tok + half).astype(BF)
        uw = u_ref[pl.ds(pl.multiple_of(start, POOL_HALO), win), :]
        ut = u_ref[pl.ds(pl.multiple_of(l + r * tm, POOL_HALO), tm), :].astype(F32)
        dv = (_dot(band, uw, 1, 0) / cnt - ut).astype(BF)
        d_ref[...] = dv
        po_ref[...] = _dot(dv, w_ref[0], 1, 0).astype(BF)

    return pl.pallas_call(
        body, name="pool_fwd", grid=(N_POOL_GROUPS, n // tm),
        out_shape=(jax.ShapeDtypeStruct((n, N_POOL_GROUPS * pg), BF), jax.ShapeDtypeStruct((n, N_POOL_GROUPS * pg), BF)),
        in_specs=[pl.BlockSpec((s_len, pg), lambda g, r: (0, cb0 + g)), pl.BlockSpec((1, pg, pg), lambda g, r: (g, 0, 0))],
        out_specs=(pl.BlockSpec((tm, pg), lambda g, r: (r, g)), pl.BlockSpec((tm, pg), lambda g, r: (r, g))),
        compiler_params=_cparams("parallel", "parallel"),
    )(proj, pool_w)


def _pool_bwd_map(dval, pool_w, *, n, col0, tm):
    pg = pool_w.shape[-1]
    cb0 = col0 // pg

    def body(d_ref, w_ref, o_ref):
        gi, r = pl.program_id(0), pl.program_id(1)
        half = lax.shift_left(jnp.int32(1), gi)
        tok1 = r * tm + lax.broadcasted_iota(jnp.int32, (tm, 1), 0)
        cnt = (jnp.minimum(tok1 + half, n) - jnp.maximum(tok1 - half, 0)).astype(F32)
        o_ref[...] = (_dot(d_ref[...], w_ref[0], 1, 1) / cnt).astype(BF)

    return pl.pallas_call(
        body, name="pool_bwd_map", grid=(N_POOL_GROUPS, n // tm),
        out_shape=jax.ShapeDtypeStruct((n, N_POOL_GROUPS * pg), BF),
        in_specs=[pl.BlockSpec((tm, pg), lambda g, r: (r, cb0 + g)), pl.BlockSpec((1, pg, pg), lambda g, r: (g, 0, 0))],
        out_specs=pl.BlockSpec((tm, pg), lambda g, r: (r, g)), compiler_params=_cparams("parallel", "parallel"),
    )(dval, pool_w)


def _pool_bwd_window(dds, into, *, n, tm, row0, col0):
    pg = dds.shape[1] // N_POOL_GROUPS
    rb0, cb0 = row0 // tm, col0 // pg
    assert row0 % tm == 0 and col0 % pg == 0

    def body(d_ref, into_ref, o_ref):
        gi, r = pl.program_id(0), pl.program_id(1)
        start, win, tok, src, half, cnt = _pool_window(r, gi, 0, n, tm)
        band = jnp.logical_and(tok >= src - half, tok < src + half).astype(BF)
        dw = d_ref[pl.ds(pl.multiple_of(start, POOL_HALO), win), :]
        dt = d_ref[pl.ds(pl.multiple_of(r * tm, POOL_HALO), tm), :].astype(F32)
        o_ref[...] = (_dot(band, dw, 1, 0) - dt * cnt).astype(BF)

    return pl.pallas_call(
        body, name="pool_bwd_window", grid=(N_POOL_GROUPS, n // tm), out_shape=jax.ShapeDtypeStruct(into.shape, BF),
        in_specs=[pl.BlockSpec((n, pg), lambda g, r: (0, g)), HBM_SPEC],
        out_specs=pl.BlockSpec((tm, pg), lambda g, r: (r + rb0, g + cb0)),
        input_output_aliases={1: 0}, compiler_params=_cparams("parallel", "parallel"),
    )(dds, into)


def _pool_wgrad(dmat, dval, *, col0, tk):
    n, pw = dmat.shape
    pg = pw // N_POOL_GROUPS
    cb0 = col0 // pg
    nk = n // tk

    def body(a_ref, b_ref, o_ref, acc):
        k = pl.program_id(1)

        @pl.when(k == 0)
        def _():
            acc[...] = jnp.zeros_like(acc)

        acc[...] += _dot(a_ref[...], b_ref[...], 0, 0)

        @pl.when(k == nk - 1)
        def _():
            o_ref[0] = acc[...].astype(BF)

    return pl.pallas_call(
        body, name="pool_wgrad", grid=(N_POOL_GROUPS, nk), out_shape=jax.ShapeDtypeStruct((N_POOL_GROUPS, pg, pg), BF),
        in_specs=[pl.BlockSpec((tk, pg), lambda g, k: (k, g)), pl.BlockSpec((tk, pg), lambda g, k: (k, cb0 + g))],
        out_specs=pl.BlockSpec((1, pg, pg), lambda g, k: (g, 0, 0)), scratch_shapes=[pltpu.VMEM((pg, pg), F32)],
        compiler_params=_cparams("parallel", "arbitrary"),
    )(dmat, dval)


def _merge(attn_o, po_raw, proj, pool_scale, *, l, ga0, gp0, tm):
    n, aw = attn_o.shape
    bw = aw // 2
    rb0 = l // tm
    ga_b, gp_b = ga0 // bw, gp0 // bw
    assert ga0 % bw == 0 and gp0 % bw == 0 and l % tm == 0

    def body(a_ref, p_ref, g_ref, ps_ref, y_ref):
        cb = pl.program_id(1)
        sg, _ = _silu_and_grad(g_ref[...].astype(F32))

        @pl.when(cb < 2)
        def _():
            y_ref[...] = (a_ref[...].astype(F32) * sg).astype(BF)

        @pl.when(cb >= 2)
        def _():
            y_ref[...] = (p_ref[...].astype(F32) * ps_ref[...] * sg).astype(BF)

    return pl.pallas_call(
        body, name="merge", grid=(n // tm, 4), out_shape=jax.ShapeDtypeStruct((n, 2 * aw), BF),
        in_specs=[pl.BlockSpec((tm, bw), lambda i, cb: (i, jnp.minimum(cb, 1))),
                  pl.BlockSpec((tm, bw), lambda i, cb: (i, jnp.maximum(cb - 2, 0))),
                  pl.BlockSpec((tm, bw), lambda i, cb: (i + rb0, jnp.where(cb < 2, ga_b + cb, gp_b + cb - 2))),
                  pl.BlockSpec((1, bw), lambda i, cb: (0, jnp.maximum(cb - 2, 0)))],
        out_specs=pl.BlockSpec((tm, bw), lambda i, cb: (i, cb)), compiler_params=_cparams("parallel", "arbitrary"),
    )(attn_o, po_raw, proj, pool_scale)


def _merge_bwd(dy, attn_o, po_raw, proj, pool_scale, into, *, l, ga0, gp0, tm):
    n, aw = attn_o.shape
    bw = aw // 2
    rb0 = l // tm
    ga_b, gp_b = ga0 // bw, gp0 // bw

    def body(dy_ref, a_ref, p_ref, g_ref, ps_ref, into_ref, dv_ref, dg_ref, dps_ref):
        cb, i = pl.program_id(0), pl.program_id(1)
        sg, sgrad = _silu_and_grad(g_ref[...].astype(F32))
        dyv = dy_ref[...].astype(F32)

        @pl.when(cb < 2)
        def _():
            dv_ref[...] = (dyv * sg).astype(BF)
            dg_ref[...] = (dyv * a_ref[...].astype(F32) * sgrad).astype(BF)

        @pl.when(cb >= 2)
        def _():
            @pl.when(i == 0)
            def _():
                dps_ref[...] = jnp.zeros_like(dps_ref)

            pr = p_ref[...].astype(F32)
            dpo = dyv * sg
            dv_ref[...] = (dpo * ps_ref[...]).astype(BF)
            dg_ref[...] = (dyv * (pr * ps_ref[...]) * sgrad).astype(BF)
            dps_ref[0:1, :] += jnp.sum(dpo * pr, axis=0, keepdims=True)

    blk = pl.BlockSpec((tm, bw), lambda cb, i: (i, cb))
    gate_blk = pl.BlockSpec((tm, bw), lambda cb, i: (i + rb0, jnp.where(cb < 2, ga_b + cb, gp_b + cb - 2)))
    return pl.pallas_call(
        body, name="merge_bwd", grid=(4, n // tm),
        out_shape=(jax.ShapeDtypeStruct((n, 2 * aw), BF), jax.ShapeDtypeStruct(into.shape, BF),
                   jax.ShapeDtypeStruct((8, aw), F32)),
        in_specs=[blk, pl.BlockSpec((tm, bw), lambda cb, i: (i, jnp.minimum(cb, 1))),
                  pl.BlockSpec((tm, bw), lambda cb, i: (i, jnp.maximum(cb - 2, 0))), gate_blk,
                  pl.BlockSpec((1, bw), lambda cb, i: (0, jnp.maximum(cb - 2, 0))), HBM_SPEC],
        out_specs=(blk, gate_blk, pl.BlockSpec((8, bw), lambda cb, i: (0, jnp.maximum(cb - 2, 0)))),
        input_output_aliases={5: 1}, compiler_params=_cparams("arbitrary", "arbitrary"),
    )(dy, attn_o, po_raw, proj, pool_scale, into)


def _post(x, out, target, gate, g_post, *, tm):
    n, d = x.shape

    def body(x_ref, o_ref, t_ref, gate_ref, gp_ref, dxn_ref, dout_ref, st_ref):
        @pl.when(pl.program_id(0) == 0)
        def _():
            st_ref[...] = jnp.zeros_like(st_ref)

        ov = o_ref[...]
        rstd = lax.rsqrt(jnp.mean(ov * ov, axis=-1, keepdims=True) + EPS)
        on = ov * rstd
        rn = on * gp_ref[...]
        err = (x_ref[...] + gate_ref[...] * rn) - t_ref[...]
        dxn = err / d
        dxn_ref[...] = dxn
        drn = dxn * gate_ref[...]
        don = drn * gp_ref[...]
        dout_ref[...] = (rstd * (don - on * jnp.mean(don * on, axis=-1, keepdims=True))).astype(BF)
        st_ref[0:1, :] += jnp.sum(dxn * rn, axis=0, keepdims=True)
        st_ref[1:2, :] += jnp.sum(drn * on, axis=0, keepdims=True)
        st_ref[2:3, :] += jnp.sum(err * err, axis=0, keepdims=True)

    row = pl.BlockSpec((tm, d), lambda i: (i, 0))
    vec = pl.BlockSpec((1, d), lambda i: (0, 0))
    return pl.pallas_call(
        body, name="post", grid=(n // tm,),
        out_shape=(jax.ShapeDtypeStruct((n, d), F32), jax.ShapeDtypeStruct((n, d), BF), jax.ShapeDtypeStruct((8, d), F32)),
        in_specs=[row, row, row, vec, vec], out_specs=(row, row, pl.BlockSpec((8, d), lambda i: (0, 0))),
        compiler_params=_cparams("arbitrary"),
    )(x, out, target, gate, g_post)


def _prenorm_bwd(x, ctx, dh, dxn, g_pre, ss, *, tm):
    n, d = x.shape
    l = ctx.shape[0]
    nc = l // tm

    def body(x_ref, c_ref, dh_ref, dxn_ref, g_ref, ss_ref, gx_ref, st_ref):
        t = pl.program_id(0)

        @pl.when(t == 0)
        def _():
            st_ref[...] = jnp.zeros_like(st_ref)

        def go(src, scale, row):
            v = src[...]
            dhv = dh_ref[...].astype(F32)
            rstd = lax.rsqrt(jnp.mean(v * v, axis=-1, keepdims=True) + EPS)
            xn = v * rstd
            st_ref[row:row + 1, :] += jnp.sum(dhv, axis=0, keepdims=True)
            st_ref[row + 1:row + 2, :] += jnp.sum(dhv * (xn * g_ref[...]), axis=0, keepdims=True)
            dg = dhv * (1.0 + scale)
            st_ref[4:5, :] += jnp.sum(dg * xn, axis=0, keepdims=True)
            dxn_ = dg * g_ref[...]
            return rstd * (dxn_ - xn * jnp.mean(dxn_ * xn, axis=-1, keepdims=True))

        @pl.when(t < nc)
        def _():
            go(c_ref, ss_ref[2:3, :], 2)

        @pl.when(t >= nc)
        def _():
            gx_ref[...] = dxn_ref[...] + go(x_ref, ss_ref[0:1, :], 0)

    lat = pl.BlockSpec((tm, d), lambda t: (jnp.maximum(t - nc, 0), 0))
    return pl.pallas_call(
        body, name="prenorm_bwd", grid=((l + n) // tm,),
        out_shape=(jax.ShapeDtypeStruct((n, d), F32), jax.ShapeDtypeStruct((8, d), F32)),
        in_specs=[lat, pl.BlockSpec((tm, d), lambda t: (jnp.minimum(t, nc - 1), 0)),
                  pl.BlockSpec((tm, d), lambda t: (t, 0)), lat,
                  pl.BlockSpec((1, d), lambda t: (0, 0)), pl.BlockSpec((4, d), lambda t: (0, 0))],
        out_specs=(lat, pl.BlockSpec((8, d), lambda t: (0, 0))), compiler_params=_cparams("arbitrary"),
    )(x, ctx, dh, dxn, g_pre, ss)


def _ada_fwd(craw, w_ada, b_loc, *, tn):
    d, wn = w_ada.shape

    def body(c_ref, w_ref, b_ref, o_ref):
        act, _ = _silu_and_grad(c_ref[...])
        o_ref[...] = _dot(act.astype(BF), w_ref[...].astype(BF), 1, 0) + b_ref[...]

    return pl.pallas_call(
        body, name="ada_fwd", grid=(wn // tn,), out_shape=jax.ShapeDtypeStruct((16, wn), F32),
        in_specs=[pl.BlockSpec((16, d), lambda j: (0, 0)), pl.BlockSpec((d, tn), lambda j: (0, j)),
                  pl.BlockSpec((1, tn), lambda j: (0, j))],
        out_specs=pl.BlockSpec((16, tn), lambda j: (0, j)), compiler_params=_cparams("parallel"),
    )(craw, w_ada, b_loc)


def _ada_bwd(craw_t, dm, w, m, v, *, tm):
    d, wn = w.shape

    def body(ct_ref, dm_ref, w_ref, m_ref, v_ref, g_ref, dl_ref, nm_ref, nv_ref, da_ref):
        act, _ = _silu_and_grad(ct_ref[...])
        dmb = dm_ref[...].astype(BF)
        wv = w_ref[...]
        gv = _dot(act.astype(BF), dmb, 1, 0)
        da_ref[...] = _dot(dmb, wv.astype(BF), 1, 1)
        nm = ADAM_B1 * m_ref[...] + (1.0 - ADAM_B1) * gv
        nv = ADAM_B2 * v_ref[...] + (1.0 - ADAM_B2) * (gv * gv)
        m_hat = nm / (1.0 - ADAM_B1 ** ADAM_STEP)
        v_hat = nv / (1.0 - ADAM_B2 ** ADAM_STEP)
        g_ref[...] = gv
        dl_ref[...] = -ADAM_LR * (m_hat / (jnp.sqrt(v_hat) + ADAM_EPS) + ADAM_WD * wv)
        nm_ref[...] = nm
        nv_ref[...] = nv

    row = pl.BlockSpec((tm, wn), lambda i: (i, 0))
    return pl.pallas_call(
        body, name="ada_bwd", grid=(d // tm,),
        out_shape=tuple([jax.ShapeDtypeStruct((d, wn), F32)] * 4) + (jax.ShapeDtypeStruct((16, d), F32),),
        in_specs=[pl.BlockSpec((tm, 16), lambda i: (i, 0)), pl.BlockSpec((16, wn), lambda i: (0, 0)), row, row, row],
        out_specs=(row, row, row, row, pl.BlockSpec((16, tm), lambda i: (0, i))),
        compiler_params=_cparams("parallel"),
    )(craw_t, dm, w, m, v)


def _reduce_small(gath, d3):
    t = gath.shape[-1]

    def body(g_ref, dm_ref, s_ref):
        tot = g_ref[0]
        for b in range(1, 8):
            tot = tot + g_ref[b]
        s_ref[...] = tot
        dm_ref[...] = jnp.zeros_like(dm_ref)
        for b in range(8):
            dm_ref[b:b + 1, :] = g_ref[b][:, 0:d3]
        dm_ref[8:9, :] = tot[:, d3:2 * d3]

    return pl.pallas_call(
        body, name="reduce_small", out_shape=(jax.ShapeDtypeStruct((16, d3), F32), jax.ShapeDtypeStruct((1, t), F32)),
        in_specs=[VMEM_SPEC], out_specs=(VMEM_SPEC, VMEM_SPEC),
    )(gath)


def _cctx_grad(parts, c_ctx):
    def body(p_ref, c_ref, o_ref):
        tot = (p_ref[0] + p_ref[1]) + (p_ref[2] + p_ref[3])
        _, sgrad = _silu_and_grad(c_ref[...])
        o_ref[...] = tot * sgrad

    return pl.pallas_call(
        body, name="cctx_grad", out_shape=jax.ShapeDtypeStruct(c_ctx.shape, F32),
        in_specs=[VMEM_SPEC, VMEM_SPEC], out_specs=VMEM_SPEC,
    )(parts, c_ctx)


def _rope_tables(n, l):
    rows = n // GRID_W
    row = jnp.repeat(jnp.arange(rows, dtype=F32), GRID_W)
    col = jnp.tile(jnp.arange(GRID_W, dtype=F32), rows)
    inv = ROPE_THETA ** (-jnp.arange(ROPE_PAIRS, dtype=F32) / ROPE_PAIRS)
    ang = jnp.concatenate([row[:, None] * inv, col[:, None] * inv], axis=-1)
    cos, sin = jnp.cos(ang), jnp.sin(ang)
    cr, cc, sr, sc = cos[:, :ROPE_PAIRS], cos[:, ROPE_PAIRS:], sin[:, :ROPE_PAIRS], sin[:, ROPE_PAIRS:]
    ctab = jnp.concatenate([cr, cr, cc, cc], axis=-1)
    stab = jnp.concatenate([-sr, sr, -sc, sc], axis=-1)
    ctab = jnp.concatenate([jnp.ones((l, HEAD_DIM), F32), ctab], axis=0)
    stab = jnp.concatenate([jnp.zeros((l, HEAD_DIM), F32), stab], axis=0)
    return ctab, stab


def kernel(x, c, ctx, c_ctx, w_ada, b_ada, norm_pre, norm_post, w_in, q_norm, k_norm, pool_w, pool_scale, w_out, loss_target, m_c_ctx, m_w_ada, m_b_ada, m_norm_pre, m_norm_post, m_w_in, m_q_norm, m_k_norm, m_pool_w, m_pool_scale, m_w_out, v_c_ctx, v_w_ada, v_b_ada, v_norm_pre, v_norm_post, v_w_in, v_q_norm, v_k_norm, v_pool_w, v_pool_scale, v_w_out):
    x2, ctx2, tgt = x[0], ctx[0], loss_target[0]
    n, d = x2.shape
    l = ctx2.shape[0]
    s_len = l + n
    aw = d // 2
    pw = d - aw
    n_heads = aw // HEAD_DIM
    kvw = (n_heads // GQA_GROUP) * HEAD_DIM
    pg = pw // N_POOL_GROUPS
    k0, v0, ga0 = aw, aw + kvw, aw + 2 * kvw
    up0, gp0 = ga0 + aw, ga0 + aw + pw
    in_w = gp0 + pw
    d3 = 3 * d
    ada_w = w_ada.shape[-1]
    px, py, pc = lax.axis_index("x"), lax.axis_index("y"), lax.axis_index("c")
    chip = 2 * px + py
    tr = min(256, l)
    tq = min(512, n)
    tk = min(256, l)
    ts = s_len // 8 if (s_len // 8) % 16 == 0 else tr
    scale = HEAD_DIM ** -0.5

    c_all = _allgather_small(c, chips_only=False, name="gather_c").reshape(8, d)
    craw = jnp.concatenate([c_all, c_ctx.reshape(1, d), jnp.zeros((7, d), F32)], axis=0)
    b_loc = lax.dynamic_slice(b_ada, (0, chip * ada_w), (1, ada_w))
    mod_part = _ada_fwd(craw, w_ada[0], b_loc, tn=min(512, ada_w))
    mod_all = _allgather_small(mod_part, chips_only=True, name="gather_mod")
    mod_all = jnp.transpose(mod_all, (1, 0, 2)).reshape(16, d3)
    me = 4 * px + 2 * py + pc
    mod_b = lax.dynamic_slice(mod_all, (me, 0), (1, d3))
    mod_c = mod_all[8:9]
    ss = jnp.concatenate([mod_b[:, d:2 * d], mod_b[:, 0:d], mod_c[:, d:2 * d], mod_c[:, 0:d]], axis=0)
    gate = mod_b[:, 2 * d:]

    w_own = _cast_bf16(w_in[0], name="cast_w_in")
    later_weights = _gather_side([_cast_into(w_out[0], 0, chip, name="cast_w_out"),
                                  _cast_into(pool_w[0], 1, chip, name="cast_pool_w")], [(0, 1), (1, 2)])
    ids = jnp.stack([chip, jnp.bitwise_xor(chip, 2), jnp.bitwise_xor(chip, 1), jnp.bitwise_xor(chip, 3)])
    ws = in_w // 4
    tn_p = ws // 3 if (ws // 3) % HEAD_DIM == 0 else ws

    ctab, stab = _rope_tables(n, l)
    proj, h, w_x, w_y = _in_proj_own(x2, ctx2, norm_pre, ss, w_own, ids[0:1], width=in_w, tm=tr, tn=tn_p,
                                     side=_gather_parts_side(w_own, (0, 1)))
    proj, w_d = _in_proj_parts(h, [w_x, w_y], ids[1:3], proj, width=in_w, tm=ts, tn=tn_p, name="in_proj_xy",
                               side=_gather_parts_side(w_own, (2,)))
    (proj,) = _in_proj_parts(h, [w_d], ids[3:4], proj, width=in_w, tm=ts, tn=tn_p, name="in_proj_diag")
    w_parts = [w_own, w_x, w_y, w_d]
    qr = _qk_prep(proj, q_norm, ctab[l:], stab[l:], row0=l, col0=0, width=aw, nrows=n, tm=tr, name="q_prep",
                  out_scale=scale * LOG2E)
    kr = _qk_prep(proj, k_norm, ctab, stab, row0=0, col0=k0, width=kvw, nrows=s_len, tm=tr, name="k_prep")
    v_all = proj[:, v0:v0 + kvw]
    tkf = min(2048, n)
    vt_ctx = jnp.transpose(v_all[:l].reshape(1, l, kvw // HEAD_DIM, HEAD_DIM), (2, 0, 3, 1))
    vt_lat = jnp.transpose(v_all[l:].reshape(n // tkf, tkf, kvw // HEAD_DIM, HEAD_DIM), (2, 0, 3, 1))
    attn_o, lse, w_out_f, pool_f = _flash_fwd(qr, kr, vt_ctx, vt_lat, tq=tq, side=later_weights)
    dmat, po_raw = _pool_fwd(proj, pool_f, l=l, n=n, col0=up0, tm=tr)
    yv = _merge(attn_o, po_raw, proj, pool_scale, l=l, ga0=ga0, gp0=gp0, tm=tr)
    out = _matmul(yv, w_out_f, ca=1, cb=0, tm=min(512, n), tn=min(1024, d), tk=d, out_dtype=F32, name="out_proj")

    dxn, d_out, post_st = _post(x2, out, tgt, gate, norm_post, tm=min(128, n))
    loss = 0.5 * jnp.sum(post_st[2]) / d
    loss = lax.psum(loss, ("x", "y", "c"))
    g_wout = _matmul(yv, d_out, ca=0, cb=0, tm=min(512, d), tn=min(1024, d), tk=n, out_dtype=BF, name="wgrad_out")
    dy, from_sib_wout = _matmul(d_out, w_out_f, ca=1, cb=1, tm=min(512, n), tn=min(1024, d), tk=d, out_dtype=BF,
                                name="dgrad_out", side=_pair_side([g_wout], [1]))
    pair_wout = _add_own_half(g_wout, from_sib_wout, 1, pc, name="pair_sum_1")
    dproj = _zero_rows(l, (s_len, in_w), name="dproj_init")
    dval, dproj, dps = _merge_bwd(dy, attn_o, po_raw, proj, pool_scale, dproj, l=l, ga0=ga0, gp0=gp0, tm=tr)
    g_pool = _pool_wgrad(dmat, dval, col0=aw, tk=min(512, n))
    dds = _pool_bwd_map(dval, pool_f, n=n, col0=aw, tm=tr)
    dproj = _pool_bwd_window(dds, dproj, n=n, tm=tr, row0=l, col0=up0)
    delta = _attn_delta(attn_o, dval, tq=tq)
    tqb = min(2048, n)
    stat_shape = (kvw // HEAD_DIM, GQA_GROUP, n // tqb, 1, tqb)
    dq, dk, dproj, *from_chips_wout = _flash_bwd(
        qr, dval, kr, proj, lse.reshape(stat_shape), delta.reshape(stat_shape), dproj, v_col0=v0, tq=tqb, tk=tk,
        scale=scale, side=_chips_side([pair_wout], [0]))
    red_wout = _sum_own_block(pair_wout, from_chips_wout, 0, 1, chip, pc, name="chip_sum_1")
    dproj, dgq = _qk_bwd(dq, proj, q_norm, ctab[l:], stab[l:], dproj, row0=l, col0=0, tm=tr, name="q_bwd", transposed=True)
    dproj, dgk = _qk_bwd(dk, proj, k_norm, ctab, stab, dproj, row0=0, col0=k0, tm=tr, name="k_bwd")
    tn_w = 1024 if in_w % 1024 == 0 else 512
    g_send, g_wout_s = _wgrad_rows(h, dproj, 1 - pc, tm=min(512, d // 2), tn=tn_w, name="wgrad_in_send",
                                   side=_join_side([red_wout], [1]))
    g_keep, from_sib_win, from_sib_pool = _wgrad_rows(
        h, dproj, pc, tm=min(512, d // 2), tn=tn_w, name="wgrad_in_keep",
        side=_both(_swap_side([g_send]), _pair_side([g_pool], [2])))
    pair_win = _add2(g_keep, from_sib_win, name="pair_sum_0")
    pair_pool = _add_own_half(g_pool, from_sib_pool, 2, pc, name="pair_sum_2")
    dh, *from_chips = _dgrad_in_parts(dproj, w_parts, ids, tm=ts, tn=min(512, d), name="dgrad_in",
                                      side=_chips_side([pair_win, pair_pool], [1, 1]))
    red_win = _sum_own_block(pair_win, from_chips[0:3], 1, 0, chip, pc, name="chip_sum_0")
    red_pool = _sum_own_block(pair_pool, from_chips[3:6], 1, 2, chip, pc, name="chip_sum_2")
    grad_x, pre_st = _prenorm_bwd(x2, ctx2, dh, dxn, norm_pre, ss, tm=min(128, l))

    zero_d = jnp.zeros((1, d), F32)
    packed = jnp.concatenate([pre_st[0:1], pre_st[1:2], post_st[0:1], pre_st[2:3], pre_st[3:4], zero_d,
                              pre_st[4:5], post_st[1:2], dgq[0:1], dgk[0:1], dps[0:1]], axis=1)
    gath = _allgather_small(packed, chips_only=False, name="gather_small")
    dm, sums = _reduce_small(gath, d3)
    o1 = 2 * d3
    g_npre, g_npost = sums[:, o1:o1 + d], sums[:, o1 + d:o1 + 2 * d]
    g_q, g_k = sums[:, o1 + 2 * d:o1 + 2 * d + HEAD_DIM], sums[:, o1 + 2 * d + HEAD_DIM:o1 + 2 * d + 2 * HEAD_DIM]
    g_ps = sums[:, o1 + 2 * d + 2 * HEAD_DIM:]
    g_bada = sums[:, 0:d3] + sums[:, d3:2 * d3]
    dm_loc = lax.dynamic_slice(dm, (0, chip * ada_w), (16, ada_w))
    g_wada, dl_wada, nm_wada, nv_wada, dact = _ada_bwd(craw.T, dm_loc, w_ada[0], m_w_ada[0], v_w_ada[0], tm=128)
    cparts = _allgather_small(dact[8:9], chips_only=True, name="gather_cctx")
    g_cctx = _cctx_grad(cparts, c_ctx.reshape(1, d)).reshape(d)

    g_win_s, g_pool_s = _run_side(_join_side([red_win, red_pool], [0, 2]), name="reduce_join")

    def upd(w, g, m, v, name):
        return _adamw(w, g.reshape(w.shape), m, v, name=name)

    grads = {"c_ctx": g_cctx, "b_ada": g_bada, "norm_pre": g_npre, "norm_post": g_npost, "w_in": g_win_s[None],
             "q_norm": g_q, "k_norm": g_k, "pool_w": g_pool_s[None], "pool_scale": g_ps, "w_out": g_wout_s[None]}
    res = {"w_ada": (g_wada[None], dl_wada[None], nm_wada[None], nv_wada[None])}
    given = {"c_ctx": (c_ctx, m_c_ctx, v_c_ctx), "b_ada": (b_ada, m_b_ada, v_b_ada),
             "norm_pre": (norm_pre, m_norm_pre, v_norm_pre), "norm_post": (norm_post, m_norm_post, v_norm_post),
             "w_in": (w_in, m_w_in, v_w_in), "q_norm": (q_norm, m_q_norm, v_q_norm), "k_norm": (k_norm, m_k_norm, v_k_norm),
             "pool_w": (pool_w, m_pool_w, v_pool_w), "pool_scale": (pool_scale, m_pool_scale, v_pool_scale),
             "w_out": (w_out, m_w_out, v_w_out)}
    for nme, (w, m, v) in given.items():
        g = grads[nme].reshape(w.shape)
        res[nme] = (g,) + upd(w, g, m, v, "adamw_" + nme)
    order = ["c_ctx", "w_ada", "b_ada", "norm_pre", "norm_post", "w_in", "q_norm", "k_norm", "pool_w", "pool_scale", "w_out"]
    return (loss, grad_x[None], *[res[k][0] for k in order], *[res[k][1] for k in order],
            *[res[k][2] for k in order], *[res[k][3] for k in order])
```

```python
import functools

import jax
import jax.numpy as jnp
from jax import lax
from jax.experimental import pallas as pl
from jax.experimental.pallas import tpu as pltpu

F32 = jnp.float32
BF = jnp.bfloat16
MESH = pl.DeviceIdType.MESH

HEAD_DIM = 128
GQA_GROUP = 4
GRID_W = 64
ROPE_PAIRS = HEAD_DIM // 4
ROPE_THETA = 10000.0
EPS = 1e-6
N_POOL_GROUPS = 4
POOL_HALO = 128
ADAM_LR = 0.001
ADAM_B1 = 0.9
ADAM_B2 = 0.999
ADAM_EPS = 1e-08
ADAM_WD = 0.01
ADAM_STEP = 10
LOG2E = 1.4426950408889634
MIB = 2 ** 20
VMEM_LIMIT = 48 * MIB
CHIP_MASKS = ((1, 0, 0), (0, 1, 0), (1, 1, 0))
ALL_MASKS = ((0, 0, 1), (0, 1, 0), (0, 1, 1), (1, 0, 0), (1, 0, 1), (1, 1, 0), (1, 1, 1))
HBM_SPEC = pl.BlockSpec(memory_space=pl.ANY)
VMEM_SPEC = pl.BlockSpec(memory_space=pltpu.VMEM)


def _cparams(*sem):
    return pltpu.CompilerParams(dimension_semantics=sem, vmem_limit_bytes=VMEM_LIMIT)


def _sigmoid(v):
    return 0.5 * jnp.tanh(0.5 * v) + 0.5


def _silu_and_grad(v):
    s = _sigmoid(v)
    return v * s, s * (1.0 + v * (1.0 - s))


def _dot(a, b, ca, cb):
    return lax.dot_general(a, b, (((ca,), (cb,)), ((), ())), preferred_element_type=F32)


def _block_rows(rows, width, itemsize=4, target=MIB):
    best = 8
    for t in range(8, rows + 1, 8):
        if rows % t == 0 and t * width * itemsize <= target:
            best = t
    return best if rows % 8 == 0 else rows


def _my_pos():
    return lax.axis_index("x"), lax.axis_index("y"), lax.axis_index("c")


def _flip(pos, mask):
    return tuple(jnp.bitwise_xor(p, m) if m else p for p, m in zip(pos, mask))


def _allgather_small(v, *, chips_only, name, side=None):
    r, w = v.shape
    masks = CHIP_MASKS if chips_only else ALL_MASKS
    nslot = 4 if chips_only else 8
    n_sin = len(side.ins) if side else 0
    n_sout = len(side.out_shapes) if side else 0

    def slot(pos):
        return 2 * pos[0] + pos[1] if chips_only else 4 * pos[0] + 2 * pos[1] + pos[2]

    def body(v_ref, *refs):
        s_ins, o_ref = refs[:n_sin], refs[n_sin]
        s_outs = refs[n_sin + 1:n_sin + 1 + n_sout]
        send_sems, recv_sems, local_sem = refs[n_sin + 1 + n_sout:n_sin + 4 + n_sout]
        s_sems = refs[n_sin + 4 + n_sout:]
        if side:
            side.start(s_ins, s_outs, s_sems)
        me = _my_pos()

        def copy(k, block_of, to):
            return pltpu.make_async_remote_copy(
                src_ref=v_ref, dst_ref=o_ref.at[slot(block_of)], send_sem=send_sems.at[k], recv_sem=recv_sems.at[k],
                device_id=to, device_id_type=MESH)

        mine = pltpu.make_async_copy(v_ref, o_ref.at[slot(me)], local_sem)
        mine.start()
        sends = [copy(k, me, _flip(me, m)) for k, m in enumerate(masks)]
        for cp in sends:
            cp.start()
        for k, m in enumerate(masks):
            copy(k, _flip(me, m), me).wait_recv()
        for cp in sends:
            cp.wait_send()
        mine.wait()
        if side:
            side.finish(s_ins, s_outs, s_sems)

    o_shape = jax.ShapeDtypeStruct((nslot, r, w), v.dtype)
    sems = [pltpu.SemaphoreType.DMA((len(masks),)), pltpu.SemaphoreType.DMA((len(masks),)), pltpu.SemaphoreType.DMA]
    if not side:
        return pl.pallas_call(body, name=name, out_shape=o_shape, in_specs=[VMEM_SPEC], out_specs=VMEM_SPEC,
                              scratch_shapes=sems)(v)
    return pl.pallas_call(
        body, name=name, out_shape=(o_shape, *side.out_shapes),
        in_specs=[VMEM_SPEC] + [HBM_SPEC] * n_sin, out_specs=(VMEM_SPEC, *([HBM_SPEC] * n_sout)),
        input_output_aliases={1 + i: 1 + o for i, o in side.aliases.items()},
        scratch_shapes=sems + side.sem_shapes,
    )(v, *side.ins)


def _sl(ref, axis, start, size):
    idx = [slice(None)] * len(ref.shape)
    idx[axis] = pl.ds(start, size)
    return ref.at[tuple(idx)]


class _Side:
    def __init__(self, ins, out_shapes, aliases, sem_shapes, start, finish, mid=None):
        self.ins, self.out_shapes, self.aliases, self.sem_shapes = list(ins), list(out_shapes), dict(aliases), list(sem_shapes)
        self.start, self.finish, self.mid = start, finish, mid


def _run_side(side, *, name):
    n_in, n_out = len(side.ins), len(side.out_shapes)

    def body(*refs):
        parts = refs[:n_in], refs[n_in:n_in + n_out], refs[n_in + n_out:]
        side.start(*parts)
        if side.mid is not None:
            side.mid(*parts)
        side.finish(*parts)

    return pl.pallas_call(
        body, name=name, out_shape=tuple(side.out_shapes), in_specs=[HBM_SPEC] * n_in, out_specs=tuple([HBM_SPEC] * n_out),
        input_output_aliases=side.aliases, scratch_shapes=side.sem_shapes,
    )(*side.ins)


def _side_at(grid):
    pids = [pl.program_id(a) for a in range(len(grid))]

    def at(values):
        cond = pids[0] == values[0]
        for p, v in zip(pids[1:], values[1:]):
            cond = jnp.logical_and(cond, p == v)
        return cond

    zeros = [0] * len(grid)
    return at(zeros), at([grid[0] // 2] + zeros[1:]), at([g - 1 for g in grid])


def _gather_side(shards, layouts):
    nm = len(shards)

    def copies(outs, sems):
        send_sems, recv_sems, fsend_sems, frecv_sems = sems
        x, y, c = _my_pos()
        me = (x, y, c)
        sib = (x, y, 1 - c)
        sends, landed, fwds, from_sib = [], [], [], []
        for m in range(nm):
            a_s, a_h = layouts[m]
            ns, nh = outs[m].shape[a_s] // 4, outs[m].shape[a_h] // 2

            def region(chip_pos, half, m=m, a_s=a_s, a_h=a_h, ns=ns, nh=nh):
                j = 2 * chip_pos[0] + chip_pos[1]
                return _sl(_sl(outs[m], a_s, j * ns, ns), a_h, half * nh, nh)

            for k, mask in enumerate(CHIP_MASKS):
                other = _flip(me, mask)

                def rc(ref, ssem, rsem, to):
                    return pltpu.make_async_remote_copy(src_ref=ref, dst_ref=ref, send_sem=ssem, recv_sem=rsem,
                                                        device_id=to, device_id_type=MESH)

                sends.append(rc(region(me, c), send_sems.at[m, k], recv_sems.at[m, k], other))
                landed.append(rc(region(other, c), send_sems.at[m, k], recv_sems.at[m, k], me))
                fwds.append(rc(region(other, c), fsend_sems.at[m, k], frecv_sems.at[m, k], sib))
                from_sib.append(rc(region(other, 1 - c), fsend_sems.at[m, k], frecv_sems.at[m, k], sib))
        return sends, landed, fwds, from_sib

    def start(ins, outs, sems):
        for cp in copies(outs, sems)[0]:
            cp.start()

    def mid(ins, outs, sems):
        _, landed, fwds, _ = copies(outs, sems)
        for arrived, fw in zip(landed, fwds):
            arrived.wait_recv()
            fw.start()

    def finish(ins, outs, sems):
        sends, _, fwds, from_sib = copies(outs, sems)
        for cp in from_sib:
            cp.wait_recv()
        for cp in sends + fwds:
            cp.wait_send()

    return _Side(shards, [jax.ShapeDtypeStruct(s.shape, s.dtype) for s in shards], {i: i for i in range(nm)},
                 [pltpu.SemaphoreType.DMA((nm, 3))] * 4, start, finish, mid)


def _both(s1, s2):
    n1, m1, k1 = len(s1.ins), len(s1.out_shapes), len(s1.sem_shapes)

    def split(fn1, fn2):
        def run(ins, outs, sems):
            fn1(ins[:n1], outs[:m1], sems[:k1])
            fn2(ins[n1:], outs[m1:], sems[k1:])
        return run

    aliases = dict(s1.aliases)
    aliases.update({n1 + i: m1 + o for i, o in s2.aliases.items()})
    return _Side(s1.ins + s2.ins, s1.out_shapes + s2.out_shapes, aliases, s1.sem_shapes + s2.sem_shapes,
                 split(s1.start, s2.start), split(s1.finish, s2.finish))


def _swap_side(mats):
    nm = len(mats)

    def copies(ins, outs, sems):
        x, y, c = _my_pos()
        return [pltpu.make_async_remote_copy(src_ref=ins[m], dst_ref=outs[m], send_sem=sems[0].at[m], recv_sem=sems[1].at[m],
                                             device_id=(x, y, 1 - c), device_id_type=MESH) for m in range(nm)]

    def start(ins, outs, sems):
        for cp in copies(ins, outs, sems):
            cp.start()

    def finish(ins, outs, sems):
        for cp in copies(ins, outs, sems):
            cp.wait()

    return _Side(mats, [jax.ShapeDtypeStruct(s.shape, s.dtype) for s in mats], {},
                 [pltpu.SemaphoreType.DMA((nm,))] * 2, start, finish)


def _gather_parts_side(w_own, mask_ids):
    nt = len(mask_ids)
    nh = w_own.shape[0] // 2

    def copies(ins, outs, sems):
        send_sems, recv_sems, fsend_sems, frecv_sems = sems
        x, y, c = _my_pos()
        me, sib = (x, y, c), (x, y, 1 - c)
        sends, landed, fwds, from_sib = [], [], [], []
        for t, k in enumerate(mask_ids):
            mine = _sl(outs[t], 0, c * nh, nh)
            theirs = _sl(outs[t], 0, (1 - c) * nh, nh)

            def rc(src, dst, ssem, rsem, to):
                return pltpu.make_async_remote_copy(src_ref=src, dst_ref=dst, send_sem=ssem, recv_sem=rsem,
                                                    device_id=to, device_id_type=MESH)

            sends.append(rc(_sl(ins[0], 0, c * nh, nh), mine, send_sems.at[t], recv_sems.at[t], _flip(me, CHIP_MASKS[k])))
            landed.append(rc(mine, mine, send_sems.at[t], recv_sems.at[t], me))
            fwds.append(rc(mine, mine, fsend_sems.at[t], frecv_sems.at[t], sib))
            from_sib.append(rc(theirs, theirs, fsend_sems.at[t], frecv_sems.at[t], sib))
        return sends, landed, fwds, from_sib

    def start(ins, outs, sems):
        for cp in copies(ins, outs, sems)[0]:
            cp.start()

    def finish(ins, outs, sems):
        sends, landed, fwds, from_sib = copies(ins, outs, sems)
        for arrived, fw in zip(landed, fwds):
            arrived.wait_recv()
            fw.start()
        for cp in from_sib:
            cp.wait_recv()
        for cp in sends + fwds:
            cp.wait_send()

    return _Side([w_own], [jax.ShapeDtypeStruct(w_own.shape, w_own.dtype)] * nt, {},
                 [pltpu.SemaphoreType.DMA((nt,))] * 4, start, finish)


def _pair_side(mats, half_axes):
    nm = len(mats)

    def copies(ins, outs, sems):
        x, y, c = _my_pos()
        cps = []
        for m in range(nm):
            nh = ins[m].shape[half_axes[m]] // 2
            cps.append(pltpu.make_async_remote_copy(
                src_ref=_sl(ins[m], half_axes[m], (1 - c) * nh, nh), dst_ref=outs[m],
                send_sem=sems[0].at[m], recv_sem=sems[1].at[m], device_id=(x, y, 1 - c), device_id_type=MESH))
        return cps

    def start(ins, outs, sems):
        for cp in copies(ins, outs, sems):
            cp.start()

    def finish(ins, outs, sems):
        for cp in copies(ins, outs, sems):
            cp.wait()

    out_shapes = []
    for s, a_h in zip(mats, half_axes):
        shp = list(s.shape)
        shp[a_h] //= 2
        out_shapes.append(jax.ShapeDtypeStruct(tuple(shp), s.dtype))
    return _Side(mats, out_shapes, {}, [pltpu.SemaphoreType.DMA((nm,))] * 2, start, finish)


def _chips_side(mats, shard_axes):
    nm = len(mats)

    def copies(ins, outs, sems):
        me = _my_pos()
        cps = []
        for m in range(nm):
            ns = ins[m].shape[shard_axes[m]] // 4
            for k, mask in enumerate(CHIP_MASKS):
                other = _flip(me, mask)
                cps.append(pltpu.make_async_remote_copy(
                    src_ref=_sl(ins[m], shard_axes[m], (2 * other[0] + other[1]) * ns, ns), dst_ref=outs[3 * m + k],
                    send_sem=sems[0].at[m, k], recv_sem=sems[1].at[m, k], device_id=other, device_id_type=MESH))
        return cps

    def start(ins, outs, sems):
        for cp in copies(ins, outs, sems):
            cp.start()

    def finish(ins, outs, sems):
        for cp in copies(ins, outs, sems):
            cp.wait()

    out_shapes = []
    for s, a_s in zip(mats, shard_axes):
        shp = list(s.shape)
        shp[a_s] //= 4
        out_shapes += [jax.ShapeDtypeStruct(tuple(shp), s.dtype)] * 3
    return _Side(mats, out_shapes, {}, [pltpu.SemaphoreType.DMA((nm, 3))] * 2, start, finish)


def _join_side(halves, half_axes):
    nm = len(halves)

    def copies(outs, sems):
        x, y, c = _my_pos()
        mine, theirs = [], []
        for m in range(nm):
            nh = outs[m].shape[half_axes[m]] // 2
            for half, lst in ((c, mine), (1 - c, theirs)):
                ref = _sl(outs[m], half_axes[m], half * nh, nh)
                lst.append(pltpu.make_async_remote_copy(
                    src_ref=ref, dst_ref=ref, send_sem=sems[0].at[m], recv_sem=sems[1].at[m],
                    device_id=(x, y, 1 - c), device_id_type=MESH))
        return mine, theirs

    def start(ins, outs, sems):
        for cp in copies(outs, sems)[0]:
            cp.start()

    def finish(ins, outs, sems):
        mine, theirs = copies(outs, sems)
        for cp in mine:
            cp.wait_send()
        for cp in theirs:
            cp.wait_recv()

    return _Side(halves, [jax.ShapeDtypeStruct(s.shape, s.dtype) for s in halves], {i: i for i in range(nm)},
                 [pltpu.SemaphoreType.DMA((nm,))] * 2, start, finish)


def _part_specs(shp, axis, slot, itemsize, target=MIB):
    if len(shp) == 2:
        rows, width = shp
        tm = _block_rows(rows, width, itemsize, target)
        nb = rows // tm
        own = pl.BlockSpec((tm, width), lambda i, p: (i, 0))
        if axis == 1:
            part = pl.BlockSpec((tm, width), lambda i, p: (i, p[slot]))
        else:
            part = pl.BlockSpec((tm, width), lambda i, p: (p[slot] * nb + i, 0))
        return (nb,), own, part
    assert len(shp) == 3 and axis in (1, 2)
    g, rows, width = shp
    own = pl.BlockSpec((1, rows, width), lambda i, p: (i, 0, 0))
    if axis == 1:
        part = pl.BlockSpec((1, rows, width), lambda i, p: (i, p[slot], 0))
    else:
        part = pl.BlockSpec((1, rows, width), lambda i, p: (i, 0, p[slot]))
    return (g,), own, part


def _cast_into(w, axis, p, *, name):
    grid, own, part = _part_specs(w.shape, axis, 0, 4, 2 * MIB)
    big = list(w.shape)
    big[axis] *= 4

    def body(p_ref, w_ref, o_ref):
        o_ref[...] = w_ref[...].astype(BF)

    return pl.pallas_call(
        body, name=name, out_shape=jax.ShapeDtypeStruct(tuple(big), BF),
        grid_spec=pltpu.PrefetchScalarGridSpec(num_scalar_prefetch=1, grid=grid, in_specs=[own], out_specs=part),
        compiler_params=_cparams("parallel"),
    )(p.reshape(1), w)


def _cast_bf16(w, *, name):
    rows, width = w.shape
    tm = _block_rows(rows, width, 4, 2 * MIB)

    def body(w_ref, o_ref):
        o_ref[...] = w_ref[...].astype(BF)

    spec = pl.BlockSpec((tm, width), lambda i: (i, 0))
    return pl.pallas_call(body, name=name, grid=(rows // tm,), out_shape=jax.ShapeDtypeStruct(w.shape, BF),
                          in_specs=[spec], out_specs=spec, compiler_params=_cparams("parallel"))(w)


def _add2(a, b, *, name):
    rows, width = a.shape
    tm = _block_rows(rows, width, 2, 4 * MIB)

    def body(a_ref, b_ref, o_ref):
        o_ref[...] = (a_ref[...].astype(F32) + b_ref[...].astype(F32)).astype(BF)

    spec = pl.BlockSpec((tm, width), lambda i: (i, 0))
    return pl.pallas_call(body, name=name, grid=(rows // tm,), out_shape=jax.ShapeDtypeStruct(a.shape, BF),
                          in_specs=[spec, spec], out_specs=spec, compiler_params=_cparams("parallel"))(a, b)


def _add_own_half(full, recv, half_axis, c, *, name):
    grid, own, part = _part_specs(recv.shape, half_axis, 0, 2, 4 * MIB)

    def body(c_ref, f_ref, r_ref, o_ref):
        o_ref[...] = (f_ref[...].astype(F32) + r_ref[...].astype(F32)).astype(BF)

    return pl.pallas_call(
        body, name=name, out_shape=jax.ShapeDtypeStruct(recv.shape, BF),
        grid_spec=pltpu.PrefetchScalarGridSpec(num_scalar_prefetch=1, grid=grid, in_specs=[part, own], out_specs=own),
        compiler_params=_cparams("parallel"),
    )(c.reshape(1), full, recv)


def _sum_own_block(mine_all, recvs, shard_axis, half_axis, j, c, *, name):
    shp = recvs[0].shape
    grid, own, mine = _part_specs(shp, shard_axis, 0, 4, 2 * MIB)
    _, _, place = _part_specs(shp, half_axis, 1, 4, 2 * MIB)
    big = list(shp)
    big[half_axis] *= 2

    def body(p_ref, a_ref, r0, r1, r2, o_ref):
        o_ref[...] = ((a_ref[...].astype(F32) + r0[...].astype(F32)) + r1[...].astype(F32)) + r2[...].astype(F32)

    return pl.pallas_call(
        body, name=name, out_shape=jax.ShapeDtypeStruct(tuple(big), F32),
        grid_spec=pltpu.PrefetchScalarGridSpec(
            num_scalar_prefetch=1, grid=grid, in_specs=[mine, own, own, own], out_specs=place),
        compiler_params=_cparams("parallel"),
    )(jnp.stack([j, c]), mine_all, *recvs)


def _adamw(w, g, m, v, *, name):
    shp = w.shape
    width = shp[-1] if len(shp) > 1 else shp[0]
    rows = 1
    for d in shp[:-1]:
        rows *= d
    if len(shp) == 1:
        rows = 1
    args = [a.reshape(rows, width) for a in (w, g, m, v)]
    tm = _block_rows(rows, width, 4, 2 * MIB)

    def body(w_ref, g_ref, m_ref, v_ref, go_ref, d_ref, nm_ref, nv_ref):
        gv = g_ref[...]
        go_ref[...] = gv
        nm = ADAM_B1 * m_ref[...] + (1.0 - ADAM_B1) * gv
        nv = ADAM_B2 * v_ref[...] + (1.0 - ADAM_B2) * (gv * gv)
        m_hat = nm / (1.0 - ADAM_B1 ** ADAM_STEP)
        v_hat = nv / (1.0 - ADAM_B2 ** ADAM_STEP)
        d_ref[...] = -ADAM_LR * (m_hat / (jnp.sqrt(v_hat) + ADAM_EPS) + ADAM_WD * w_ref[...])
        nm_ref[...] = nm
        nv_ref[...] = nv

    spec = pl.BlockSpec((tm, width), lambda i: (i, 0))
    outs = pl.pallas_call(
        body, name=name, grid=(rows // tm,), out_shape=tuple([jax.ShapeDtypeStruct((rows, width), F32)] * 4),
        in_specs=[spec] * 4, out_specs=tuple([spec] * 4), compiler_params=_cparams("parallel"),
    )(*args)
    return tuple(o.reshape(shp) for o in outs)


def _matmul(a, b, *, ca, cb, tm, tn, tk, out_dtype, name, b_resident=False, side=None):
    m, kdim = a.shape[1 - ca], a.shape[ca]
    n = b.shape[1 - cb]
    nk = kdim // tk
    assert m % tm == 0 and n % tn == 0 and kdim % tk == 0
    n_sin = len(side.ins) if side else 0
    n_sout = len(side.out_shapes) if side else 0
    if b_resident:
        gi = lambda p, q, k: (q, p, k)
        grid = (n // tn, m // tm, nk)
    else:
        gi = lambda p, q, k: (p, q, k)
        grid = (m // tm, n // tn, nk)

    def body(*refs):
        a_ref, b_ref = refs[0], refs[1]
        s_ins = refs[2:2 + n_sin]
        o_ref = refs[2 + n_sin]
        s_outs = refs[3 + n_sin:3 + n_sin + n_sout]
        rest = refs[3 + n_sin + n_sout:]
        acc, sems = (rest[:1], rest[1:]) if nk > 1 else ((), rest)
        if side:
            first, middle, last = _side_at(grid)
            pl.when(first)(lambda: side.start(s_ins, s_outs, sems))
            if side.mid is not None:
                pl.when(middle)(lambda: side.mid(s_ins, s_outs, sems))
        part = _dot(a_ref[...], b_ref[...], ca, cb)
        if nk == 1:
            o_ref[...] = part.astype(o_ref.dtype)
        else:
            k = pl.program_id(2)

            @pl.when(k == 0)
            def _():
                acc[0][...] = part

            @pl.when(jnp.logical_and(k > 0, k < nk - 1))
            def _():
                acc[0][...] += part

            @pl.when(k == nk - 1)
            def _():
                o_ref[...] = (acc[0][...] + part).astype(o_ref.dtype)
        if side:
            pl.when(last)(lambda: side.finish(s_ins, s_outs, sems))

    def a_map(p, q, k):
        i, _, kk = gi(p, q, k)
        return (i, kk) if ca == 1 else (kk, i)

    def b_map(p, q, k):
        _, j, kk = gi(p, q, k)
        return (kk, j) if cb == 0 else (j, kk)

    def o_map(p, q, k):
        i, j, _ = gi(p, q, k)
        return (i, j)

    a_spec = pl.BlockSpec((tm, tk) if ca == 1 else (tk, tm), a_map)
    b_spec = pl.BlockSpec((tk, tn) if cb == 0 else (tn, tk), b_map)
    o_shape = jax.ShapeDtypeStruct((m, n), out_dtype)
    acc_shapes = [pltpu.VMEM((tm, tn), F32)] if nk > 1 else []
    if not side:
        return pl.pallas_call(
            body, name=name, grid=grid, out_shape=o_shape,
            in_specs=[a_spec, b_spec], out_specs=pl.BlockSpec((tm, tn), o_map), scratch_shapes=acc_shapes,
            compiler_params=_cparams("parallel", "parallel", "arbitrary"),
        )(a, b)
    return pl.pallas_call(
        body, name=name, grid=grid, out_shape=(o_shape, *side.out_shapes),
        in_specs=[a_spec, b_spec] + [HBM_SPEC] * n_sin,
        out_specs=(pl.BlockSpec((tm, tn), o_map), *([HBM_SPEC] * n_sout)),
        input_output_aliases={2 + i: 1 + o for i, o in side.aliases.items()},
        scratch_shapes=acc_shapes + side.sem_shapes,
        compiler_params=_cparams("arbitrary", "arbitrary", "arbitrary"),
    )(a, b, *side.ins)


def _wgrad_rows(a, b, half, *, tm, tn, name, side=None):
    kdim, m = a.shape
    n = b.shape[1]
    nb = (m // 2) // tm
    grid = (nb, n // tn)
    n_sin = len(side.ins) if side else 0
    n_sout = len(side.out_shapes) if side else 0

    def body(p_ref, a_ref, b_ref, *refs):
        s_ins, o_ref = refs[:n_sin], refs[n_sin]
        s_outs, sems = refs[n_sin + 1:n_sin + 1 + n_sout], refs[n_sin + 1 + n_sout:]
        if side:
            first, _, last = _side_at(grid)
            pl.when(first)(lambda: side.start(s_ins, s_outs, sems))
        o_ref[...] = _dot(a_ref[...], b_ref[...], 0, 0).astype(o_ref.dtype)
        if side:
            pl.when(last)(lambda: side.finish(s_ins, s_outs, sems))

    outs = pl.pallas_call(
        body, name=name, out_shape=(jax.ShapeDtypeStruct((m // 2, n), BF), *(side.out_shapes if side else [])),
        grid_spec=pltpu.PrefetchScalarGridSpec(
            num_scalar_prefetch=1, grid=grid,
            in_specs=[pl.BlockSpec((kdim, tm), lambda i, j, p: (0, p[0] * nb + i)),
                      pl.BlockSpec((kdim, tn), lambda i, j, p: (0, j))] + [HBM_SPEC] * n_sin,
            out_specs=(pl.BlockSpec((tm, tn), lambda i, j, p: (i, j)), *([HBM_SPEC] * n_sout)),
            scratch_shapes=side.sem_shapes if side else []),
        input_output_aliases={3 + i: 1 + o for i, o in side.aliases.items()} if side else {},
        compiler_params=_cparams("arbitrary", "arbitrary"),
    )(half.reshape(1), a, b, *(side.ins if side else []))
    return outs if side else outs[0]


def _in_proj_parts(h, w_parts, ids, prev, *, width, tm, tn, name, side=None):
    s_len, d = h.shape
    nb = w_parts[0].shape[1] // tn
    n_p = len(w_parts)
    n_prev = 0 if prev is None else 1
    n_sin = len(side.ins) if side else 0
    n_sout = len(side.out_shapes) if side else 0
    grid = (n_p, s_len // tm, nb)
    base = 1 + n_p + n_prev

    def body(ids_ref, *refs):
        h_ref, w_refs = refs[0], refs[1:1 + n_p]
        s_ins = refs[base:base + n_sin]
        o_ref = refs[base + n_sin]
        s_outs = refs[base + n_sin + 1:base + n_sin + 1 + n_sout]
        sems = refs[base + n_sin + 1 + n_sout:]
        if side:
            first, _, last = _side_at(grid)
            pl.when(first)(lambda: side.start(s_ins, s_outs, sems))
        for s in range(n_p):
            @pl.when(pl.program_id(0) == s)
            def _(s=s):
                o_ref[...] = _dot(h_ref[...], w_refs[s][...], 1, 0).astype(o_ref.dtype)
        if side:
            pl.when(last)(lambda: side.finish(s_ins, s_outs, sems))

    in_specs = [pl.BlockSpec((tm, d), lambda p, i, j, ids_ref: (i, 0))]
    for s in range(n_p):
        in_specs.append(pl.BlockSpec((d, tn), lambda p, i, j, ids_ref, s=s: (0, jnp.where(p == s, j, 0))))
    in_specs += [HBM_SPEC] * (n_prev + n_sin)
    o_spec = pl.BlockSpec((tm, tn), lambda p, i, j, ids_ref: (i, ids_ref[p] * nb + j))
    aliases = {2 + n_p: 0} if prev is not None else {}
    if side:
        aliases.update({1 + base + i: 1 + o for i, o in side.aliases.items()})
    outs = pl.pallas_call(
        body, name=name, out_shape=(jax.ShapeDtypeStruct((s_len, width), BF), *(side.out_shapes if side else [])),
        grid_spec=pltpu.PrefetchScalarGridSpec(
            num_scalar_prefetch=1, grid=grid, in_specs=in_specs, out_specs=(o_spec, *([HBM_SPEC] * n_sout)),
            scratch_shapes=side.sem_shapes if side else []),
        input_output_aliases=aliases, compiler_params=_cparams("arbitrary", "arbitrary", "arbitrary"),
    )(ids, h, *w_parts, *([prev] if prev is not None else []), *(side.ins if side else []))
    return outs


def _in_proj_own(x, ctx, g_pre, ss, w_own, ids, *, width, tm, tn, side):
    n, d = x.shape
    l = ctx.shape[0]
    nc = l // tm
    nb = w_own.shape[1] // tn
    grid = ((l + n) // tm, nb)
    n_sin, n_sout = len(side.ins), len(side.out_shapes)

    def body(ids_ref, x_ref, c_ref, g_ref, ss_ref, w_ref, *refs):
        s_ins = refs[:n_sin]
        o_ref, h_ref = refs[n_sin:n_sin + 2]
        s_outs, sems = refs[n_sin + 2:n_sin + 2 + n_sout], refs[n_sin + 2 + n_sout:]
        first, _, last = _side_at(grid)
        pl.when(first)(lambda: side.start(s_ins, s_outs, sems))
        t, j = pl.program_id(0), pl.program_id(1)

        def norm(src, scale, shift):
            v = src[...]
            rstd = lax.rsqrt(jnp.mean(v * v, axis=-1, keepdims=True) + EPS)
            h_ref[...] = ((v * rstd * g_ref[...]) * (1.0 + scale) + shift).astype(BF)

        @pl.when(jnp.logical_and(j == 0, t < nc))
        def _():
            norm(c_ref, ss_ref[2:3, :], ss_ref[3:4, :])

        @pl.when(jnp.logical_and(j == 0, t >= nc))
        def _():
            norm(x_ref, ss_ref[0:1, :], ss_ref[1:2, :])

        o_ref[...] = _dot(h_ref[...], w_ref[...], 1, 0).astype(o_ref.dtype)
        pl.when(last)(lambda: side.finish(s_ins, s_outs, sems))

    return pl.pallas_call(
        body, name="in_proj_own",
        out_shape=(jax.ShapeDtypeStruct((l + n, width), BF), jax.ShapeDtypeStruct((l + n, d), BF), *side.out_shapes),
        grid_spec=pltpu.PrefetchScalarGridSpec(
            num_scalar_prefetch=1, grid=grid,
            in_specs=[pl.BlockSpec((tm, d), lambda t, j, ids_ref: (jnp.maximum(t - nc, 0), 0)),
                      pl.BlockSpec((tm, d), lambda t, j, ids_ref: (jnp.minimum(t, nc - 1), 0)),
                      pl.BlockSpec((1, d), lambda t, j, ids_ref: (0, 0)), pl.BlockSpec((4, d), lambda t, j, ids_ref: (0, 0)),
                      pl.BlockSpec((d, tn), lambda t, j, ids_ref: (0, j))] + [HBM_SPEC] * n_sin,
            out_specs=(pl.BlockSpec((tm, tn), lambda t, j, ids_ref: (t, ids_ref[0] * nb + j)),
                       pl.BlockSpec((tm, d), lambda t, j, ids_ref: (t, 0)), *([HBM_SPEC] * n_sout)),
            scratch_shapes=side.sem_shapes),
        input_output_aliases={6 + i: 2 + o for i, o in side.aliases.items()},
        compiler_params=_cparams("arbitrary", "arbitrary"),
    )(ids, x, ctx, g_pre, ss, w_own, *side.ins)


def _dgrad_in_parts(dproj, w_parts, ids, *, tm, tn, name, side=None):
    s_len = dproj.shape[0]
    d, ws = w_parts[0].shape
    n_p = len(w_parts)
    n_sin = len(side.ins) if side else 0
    n_sout = len(side.out_shapes) if side else 0
    grid = (s_len // tm, d // tn)

    def body(ids_ref, *refs):
        a_refs, w_refs = refs[:n_p], refs[n_p:2 * n_p]
        s_ins = refs[2 * n_p:2 * n_p + n_sin]
        o_ref = refs[2 * n_p + n_sin]
        s_outs = refs[2 * n_p + n_sin + 1:2 * n_p + n_sin + 1 + n_sout]
        sems = refs[2 * n_p + n_sin + 1 + n_sout:]
        if side:
            first, _, last = _side_at(grid)
            pl.when(first)(lambda: side.start(s_ins, s_outs, sems))
        tot = _dot(a_refs[0][...], w_refs[0][...], 1, 1)
        for s in range(1, n_p):
            tot = tot + _dot(a_refs[s][...], w_refs[s][...], 1, 1)
        o_ref[...] = tot.astype(o_ref.dtype)
        if side:
            pl.when(last)(lambda: side.finish(s_ins, s_outs, sems))

    in_specs = [pl.BlockSpec((tm, ws), lambda i, j, ids_ref, s=s: (i, ids_ref[s])) for s in range(n_p)]
    in_specs += [pl.BlockSpec((tn, ws), lambda i, j, ids_ref: (j, 0))] * n_p
    in_specs += [HBM_SPEC] * n_sin
    aliases = {1 + 2 * n_p + i: 1 + o for i, o in side.aliases.items()} if side else {}
    return pl.pallas_call(
        body, name=name, out_shape=(jax.ShapeDtypeStruct((s_len, d), BF), *(side.out_shapes if side else [])),
        grid_spec=pltpu.PrefetchScalarGridSpec(
            num_scalar_prefetch=1, grid=grid, in_specs=in_specs,
            out_specs=(pl.BlockSpec((tm, tn), lambda i, j, ids_ref: (i, j)), *([HBM_SPEC] * n_sout)),
            scratch_shapes=side.sem_shapes if side else []),
        input_output_aliases=aliases, compiler_params=_cparams("arbitrary", "arbitrary"),
    )(ids, *([dproj] * n_p), *w_parts, *(side.ins if side else []))


def _swap32(v):
    lane = lax.broadcasted_iota(jnp.int32, v.shape, 1)
    return jnp.where((lane % 64) < 32, pltpu.roll(v, 96, 1), pltpu.roll(v, 32, 1))


def _qk_prep(proj, gain, ctab, stab, *, row0, col0, width, nrows, tm, name, out_scale=1.0):
    cw = min(512, width)
    rb0, cb0 = row0 // tm, col0 // cw
    assert row0 % tm == 0 and col0 % cw == 0 and width % cw == 0 and nrows % tm == 0

    def body(p_ref, g_ref, c_ref, s_ref, o_ref):
        row = lax.broadcasted_iota(jnp.int32, (HEAD_DIM, HEAD_DIM), 0)
        col = lax.broadcasted_iota(jnp.int32, (HEAD_DIM, HEAD_DIM), 1)
        perm = (row == jnp.where((col % 64) < 32, col + 32, col - 32)).astype(BF)
        gc = g_ref[...] * c_ref[...] * out_scale
        gs = _swap32(jnp.broadcast_to(g_ref[...], c_ref.shape)) * s_ref[...] * out_scale
        for hd in range(cw // HEAD_DIM):
            cols = slice(hd * HEAD_DIM, (hd + 1) * HEAD_DIM)
            vb = p_ref[:, cols]
            v = vb.astype(F32)
            rstd = lax.rsqrt(jnp.mean(v * v, axis=-1, keepdims=True) + EPS)
            o_ref[:, cols] = (rstd * (v * gc + _dot(vb, perm, 1, 0) * gs)).astype(BF)

    return pl.pallas_call(
        body, name=name, grid=(nrows // tm, width // cw), out_shape=jax.ShapeDtypeStruct((nrows, width), BF),
        in_specs=[pl.BlockSpec((tm, cw), lambda i, j: (i + rb0, j + cb0)), pl.BlockSpec((1, HEAD_DIM), lambda i, j: (0, 0)),
                  pl.BlockSpec((tm, HEAD_DIM), lambda i, j: (i, 0)), pl.BlockSpec((tm, HEAD_DIM), lambda i, j: (i, 0))],
        out_specs=pl.BlockSpec((tm, cw), lambda i, j: (i, j)), compiler_params=_cparams("parallel", "parallel"),
    )(proj, gain, ctab, stab)


def _zero_rows(rows, shape, *, name):
    def body(o_ref):
        o_ref[...] = jnp.zeros_like(o_ref)

    return pl.pallas_call(body, name=name, grid=(1,), out_shape=jax.ShapeDtypeStruct(shape, BF),
                          out_specs=pl.BlockSpec((rows, shape[1]), lambda i: (0, 0)))()


def _qk_bwd(dy, proj, gain, ctab, stab, into, *, row0, col0, tm, name, transposed=False):
    if transposed:
        tq = dy.shape[-1]
        nrows, width = dy.shape[1] * tq, dy.shape[0] * GQA_GROUP * HEAD_DIM
        per = tq // tm
        dy_spec = pl.BlockSpec((1, 1, GQA_GROUP, HEAD_DIM, tm), lambda i, j: (j, i // per, 0, 0, i % per))
    else:
        nrows, width = dy.shape
    cw = min(512, width)
    rb0, cb0 = row0 // tm, col0 // cw
    if transposed:
        assert cw == GQA_GROUP * HEAD_DIM and tq % tm == 0
    else:
        dy_spec = pl.BlockSpec((tm, cw), lambda i, j: (i, j))

    def body(d_ref, p_ref, g_ref, c_ref, s_ref, into_ref, o_ref, dg_ref):
        @pl.when(jnp.logical_and(pl.program_id(0) == 0, pl.program_id(1) == 0))
        def _():
            dg_ref[...] = jnp.zeros_like(dg_ref)

        dg = jnp.zeros((1, HEAD_DIM), F32)
        for hd in range(cw // HEAD_DIM):
            cols = slice(hd * HEAD_DIM, (hd + 1) * HEAD_DIM)
            v = p_ref[:, cols].astype(F32)
            rstd = lax.rsqrt(jnp.mean(v * v, axis=-1, keepdims=True) + EPS)
            nv = v * rstd
            d = d_ref[0, 0, hd].T if transposed else d_ref[:, cols]
            dyu = d * c_ref[...] + _swap32(d * s_ref[...])
            dg = dg + jnp.sum(dyu * nv, axis=0, keepdims=True)
            dn = dyu * g_ref[...]
            o_ref[:, cols] = (rstd * (dn - nv * jnp.mean(dn * nv, axis=-1, keepdims=True))).astype(BF)
        dg_ref[0:1, :] += dg

    placed = pl.BlockSpec((tm, cw), lambda i, j: (i + rb0, j + cb0))
    return pl.pallas_call(
        body, name=name, grid=(nrows // tm, width // cw),
        out_shape=(jax.ShapeDtypeStruct(into.shape, BF), jax.ShapeDtypeStruct((8, HEAD_DIM), F32)),
        in_specs=[dy_spec, placed, pl.BlockSpec((1, HEAD_DIM), lambda i, j: (0, 0)),
                  pl.BlockSpec((tm, HEAD_DIM), lambda i, j: (i, 0)), pl.BlockSpec((tm, HEAD_DIM), lambda i, j: (i, 0)), HBM_SPEC],
        out_specs=(placed, pl.BlockSpec((8, HEAD_DIM), lambda i, j: (0, 0))),
        input_output_aliases={5: 0}, compiler_params=_cparams("arbitrary", "arbitrary"),
    )(dy, proj, gain, ctab, stab, into)


def _flash_fwd(qr, k_all, vt_ctx, vt_lat, *, tq, side=None):
    n, aw = qr.shape
    s_len, kvw = k_all.shape
    l = vt_ctx.shape[-1]
    n_j, tk = vt_lat.shape[1], vt_lat.shape[-1]
    kvh = kvw // HEAD_DIM
    n_i = n // tq
    gw = GQA_GROUP * HEAD_DIM
    n_sin = len(side.ins) if side else 0
    n_sout = len(side.out_shapes) if side else 0

    def body(*refs):
        q_ref, k_ref, vc_ref, vl_ref = refs[:4]
        s_ins = refs[4:4 + n_sin]
        o_ref, lse_ref = refs[4 + n_sin:6 + n_sin]
        s_outs = refs[6 + n_sin:6 + n_sin + n_sout]
        acc_ref, m_ref, l_ref = refs[6 + n_sin + n_sout:9 + n_sin + n_sout]
        sems = refs[9 + n_sin + n_sout:]
        if side:
            first, middle, last = _side_at((kvh, n_i))
            pl.when(first)(lambda: side.start(s_ins, s_outs, sems))
            if side.mid is not None:
                pl.when(middle)(lambda: side.mid(s_ins, s_outs, sems))
        acc_ref[...] = jnp.zeros_like(acc_ref)
        l_ref[...] = jnp.zeros_like(l_ref)
        m_ref[...] = jnp.full(m_ref.shape, -1e30, F32)

        def tile(kj, vtj):
            for g in range(GQA_GROUP):
                st = _dot(kj, q_ref[:, g * HEAD_DIM:(g + 1) * HEAD_DIM], 1, 1)
                m_old = m_ref[g]
                m_new = jnp.maximum(m_old, jnp.max(st, axis=0, keepdims=True))
                alpha = jnp.exp2(m_old - m_new)
                pt = jnp.exp2(st - m_new)
                l_ref[g] = alpha * l_ref[g] + jnp.sum(pt, axis=0, keepdims=True)
                m_ref[g] = m_new
                acc_ref[g] = acc_ref[g] * alpha + _dot(vtj, pt.astype(BF), 1, 0)

        tile(k_ref[0:l, :], vc_ref[0, 0])

        def step(j, carry):
            tile(k_ref[pl.ds(pl.multiple_of(l + j * tk, min(l, tk)), tk), :], vl_ref[0, j])
            return carry

        lax.fori_loop(0, n_j, step, 0)
        for g in range(GQA_GROUP):
            o_ref[:, g * HEAD_DIM:(g + 1) * HEAD_DIM] = (acc_ref[g] / l_ref[g]).T.astype(BF)
            lse_ref[0, g, 0] = m_ref[g] + jnp.log(l_ref[g]) * LOG2E
        if side:
            pl.when(last)(lambda: side.finish(s_ins, s_outs, sems))

    return pl.pallas_call(
        body, name="flash_fwd", grid=(kvh, n_i),
        out_shape=(jax.ShapeDtypeStruct((n, aw), BF), jax.ShapeDtypeStruct((kvh, GQA_GROUP, n_i, 1, tq), F32),
                   *(side.out_shapes if side else [])),
        in_specs=[pl.BlockSpec((tq, gw), lambda h, i: (i, h)), pl.BlockSpec((s_len, HEAD_DIM), lambda h, i: (0, h)),
                  pl.BlockSpec((1, 1, HEAD_DIM, l), lambda h, i: (h, 0, 0, 0)),
                  pl.BlockSpec((1, n_j, HEAD_DIM, tk), lambda h, i: (h, 0, 0, 0))] + [HBM_SPEC] * n_sin,
        out_specs=(pl.BlockSpec((tq, gw), lambda h, i: (i, h)),
                   pl.BlockSpec((1, GQA_GROUP, 1, 1, tq), lambda h, i: (h, 0, i, 0, 0)), *([HBM_SPEC] * n_sout)),
        input_output_aliases={4 + i: 2 + o for i, o in side.aliases.items()} if side else {},
        scratch_shapes=[pltpu.VMEM((GQA_GROUP, HEAD_DIM, tq), F32), pltpu.VMEM((GQA_GROUP, 1, tq), F32),
                        pltpu.VMEM((GQA_GROUP, 1, tq), F32)] + (side.sem_shapes if side else []),
        compiler_params=_cparams("arbitrary", "arbitrary") if side else _cparams("parallel", "parallel"),
    )(qr, k_all, vt_ctx, vt_lat, *(side.ins if side else []))


def _attn_delta(o, do, *, tq):
    n, aw = o.shape
    kvh = aw // (GQA_GROUP * HEAD_DIM)
    gw = GQA_GROUP * HEAD_DIM

    def body(o_ref, do_ref, d_ref):
        for g in range(GQA_GROUP):
            cols = slice(g * HEAD_DIM, (g + 1) * HEAD_DIM)
            prod = o_ref[:, cols].astype(F32) * do_ref[:, cols].astype(F32)
            d_ref[0, g, 0] = jnp.sum(prod.T, axis=0, keepdims=True)

    return pl.pallas_call(
        body, name="attn_delta", grid=(kvh, n // tq),
        out_shape=jax.ShapeDtypeStruct((kvh, GQA_GROUP, n // tq, 1, tq), F32),
        in_specs=[pl.BlockSpec((tq, gw), lambda h, i: (i, h)), pl.BlockSpec((tq, gw), lambda h, i: (i, h))],
        out_specs=pl.BlockSpec((1, GQA_GROUP, 1, 1, tq), lambda h, i: (h, 0, i, 0, 0)),
        compiler_params=_cparams("parallel", "parallel"),
    )(o, do)


def _flash_bwd(qr, do, k_all, proj, lse, delta, into, *, v_col0, tq, tk, scale, side=None):
    n, aw = qr.shape
    s_len, kvw = k_all.shape
    kvh = kvw // HEAD_DIM
    n_i, n_j = n // tq, s_len // tk
    gw = GQA_GROUP * HEAD_DIM
    vb0 = v_col0 // HEAD_DIM
    n_sin = len(side.ins) if side else 0
    n_sout = len(side.out_shapes) if side else 0

    def body(*refs):
        q_ref, do_ref, k_ref, v_ref, lse_ref, dl_ref = refs[:6]
        s_ins = refs[7:7 + n_sin]
        dq_ref, dk_ref, dvo_ref = refs[7 + n_sin:10 + n_sin]
        s_outs = refs[10 + n_sin:10 + n_sin + n_sout]
        dv_ref = refs[10 + n_sin + n_sout]
        sems = refs[11 + n_sin + n_sout:]
        jj = pl.program_id(1)
        if side:
            first, _, last = _side_at((kvh, n_j))
            pl.when(first)(lambda: side.start(s_ins, s_outs, sems))

        @pl.when(jj == 0)
        def _():
            dq_ref[...] = jnp.zeros_like(dq_ref)

        kj = k_ref[...]
        vj = v_ref[...]
        kjt = kj.astype(F32).T.astype(BF)
        dk_ref[...] = jnp.zeros_like(dk_ref)
        dv_ref[...] = jnp.zeros_like(dv_ref)

        def step(i, carry):
            rows = pl.ds(pl.multiple_of(i * tq, tq), tq)
            dv_part = dk_part = None
            for g in range(GQA_GROUP):
                cols = slice(g * HEAD_DIM, (g + 1) * HEAD_DIM)
                qg = q_ref[rows, cols]
                dog = do_ref[rows, cols]
                pt = jnp.exp2(_dot(kj, qg, 1, 1) - lse_ref[0, g, i])
                dst = (pt * (_dot(vj, dog, 1, 1) - dl_ref[0, g, i])).astype(BF)
                dv_g = _dot(pt.astype(BF), dog, 1, 0)
                dk_g = _dot(dst, qg, 1, 0)
                dv_part = dv_g if dv_part is None else dv_part + dv_g
                dk_part = dk_g if dk_part is None else dk_part + dk_g
                dq_ref[0, i, g] += _dot(kjt, dst, 1, 0)
            dv_ref[...] += dv_part
            dk_ref[...] += dk_part
            return carry

        lax.fori_loop(0, n_i, step, 0)
        dk_ref[...] = dk_ref[...] * (1.0 / LOG2E)
        dvo_ref[...] = dv_ref[...].astype(BF)

        @pl.when(jj == n_j - 1)
        def _():
            dq_ref[...] = dq_ref[...] * scale

        if side:
            pl.when(last)(lambda: side.finish(s_ins, s_outs, sems))

    stat_spec = pl.BlockSpec((1, GQA_GROUP, n_i, 1, tq), lambda h, j: (h, 0, 0, 0, 0))
    kv_spec = pl.BlockSpec((tk, HEAD_DIM), lambda h, j: (j, h))
    v_spec = pl.BlockSpec((tk, HEAD_DIM), lambda h, j: (j, vb0 + h))
    q_spec = pl.BlockSpec((n, gw), lambda h, j: (0, h))
    dq_spec = pl.BlockSpec((1, n_i, GQA_GROUP, HEAD_DIM, tq), lambda h, j: (h, 0, 0, 0, 0))
    aliases = {6: 2}
    if side:
        aliases.update({7 + i: 3 + o for i, o in side.aliases.items()})
    return pl.pallas_call(
        body, name="flash_bwd", grid=(kvh, n_j),
        out_shape=(jax.ShapeDtypeStruct((kvh, n_i, GQA_GROUP, HEAD_DIM, tq), F32), jax.ShapeDtypeStruct((s_len, kvw), F32),
                   jax.ShapeDtypeStruct(into.shape, BF), *(side.out_shapes if side else [])),
        in_specs=[q_spec, q_spec, kv_spec, v_spec, stat_spec, stat_spec, HBM_SPEC] + [HBM_SPEC] * n_sin,
        out_specs=(dq_spec, kv_spec, v_spec, *([HBM_SPEC] * n_sout)),
        input_output_aliases=aliases,
        scratch_shapes=[pltpu.VMEM((tk, HEAD_DIM), F32)] + (side.sem_shapes if side else []),
        compiler_params=_cparams("arbitrary", "arbitrary"),
    )(qr, do, k_all, proj, lse, delta, into, *(side.ins if side else []))


def _pool_window(r, gi, l, n, tm):
    win = tm + 2 * POOL_HALO
    start = jnp.clip(l + r * tm - POOL_HALO, l, l + n - win)
    half = lax.shift_left(jnp.int32(1), gi)
    tok = r * tm + lax.broadcasted_iota(jnp.int32, (tm, win), 0)
    src = (start - l) + lax.broadcasted_iota(jnp.int32, (tm, win), 1)
    tok1 = r * tm + lax.broadcasted_iota(jnp.int32, (tm, 1), 0)
    cnt = (jnp.minimum(tok1 + half, n) - jnp.maximum(tok1 - half, 0)).astype(F32)
    return start, win, tok, src, half, cnt


def _pool_fwd(proj, pool_w, *, l, n, col0, tm):
    s_len = proj.shape[0]
    pg = pool_w.shape[-1]
    cb0 = col0 // pg
    assert col0 % pg == 0 and l % POOL_HALO == 0 and n >= tm + 2 * POOL_HALO

    def body(u_ref, w_ref, d_ref, po_ref):
        gi, r = pl.program_id(0), pl.program_id(1)
        start, win, tok, src, half, cnt = _pool_window(r, gi, l, n, tm)
        band = jnp.logical_and(src >= tok - half, src < tok + half).astype(BF)
        uw = u_ref[pl.ds(pl.multiple_of(start, POOL_HALO), win), :]
        ut = u_ref[pl.ds(pl.multiple_of(l + r * tm, POOL_HALO), tm), :].astype(F32)
        dv = (_dot(band, uw, 1, 0) / cnt - ut).astype(BF)
        d_ref[...] = dv
        po_ref[...] = _dot(dv, w_ref[0], 1, 0).astype(BF)

    return pl.pallas_call(
        body, name="pool_fwd", grid=(N_POOL_GROUPS, n // tm),
        out_shape=(jax.ShapeDtypeStruct((n, N_POOL_GROUPS * pg), BF), jax.ShapeDtypeStruct((n, N_POOL_GROUPS * pg), BF)),
        in_specs=[pl.BlockSpec((s_len, pg), lambda g, r: (0, cb0 + g)), pl.BlockSpec((1, pg, pg), lambda g, r: (g, 0, 0))],
        out_specs=(pl.BlockSpec((tm, pg), lambda g, r: (r, g)), pl.BlockSpec((tm, pg), lambda g, r: (r, g))),
        compiler_params=_cparams("parallel", "parallel"),
    )(proj, pool_w)


def _pool_bwd_map(dval, pool_w, *, n, col0, tm):
    pg = pool_w.shape[-1]
    cb0 = col0 // pg

    def body(d_ref, w_ref, o_ref):
        gi, r = pl.program_id(0), pl.program_id(1)
        half = lax.shift_left(jnp.int32(1), gi)
        tok1 = r * tm + lax.broadcasted_iota(jnp.int32, (tm, 1), 0)
        cnt = (jnp.minimum(tok1 + half, n) - jnp.maximum(tok1 - half, 0)).astype(F32)
        o_ref[...] = (_dot(d_ref[...], w_ref[0], 1, 1) / cnt).astype(BF)

    return pl.pallas_call(
        body, name="pool_bwd_map", grid=(N_POOL_GROUPS, n // tm),
        out_shape=jax.ShapeDtypeStruct((n, N_POOL_GROUPS * pg), BF),
        in_specs=[pl.BlockSpec((tm, pg), lambda g, r: (r, cb0 + g)), pl.BlockSpec((1, pg, pg), lambda g, r: (g, 0, 0))],
        out_specs=pl.BlockSpec((tm, pg), lambda g, r: (r, g)), compiler_params=_cparams("parallel", "parallel"),
    )(dval, pool_w)


def _pool_bwd_window(dds, into, *, n, tm, row0, col0):
    pg = dds.shape[1] // N_POOL_GROUPS
    rb0, cb0 = row0 // tm, col0 // pg
    assert row0 % tm == 0 and col0 % pg == 0

    def body(d_ref, into_ref, o_ref):
        gi, r = pl.program_id(0), pl.program_id(1)
        start, win, tok, src, half, cnt = _pool_window(r, gi, 0, n, tm)
        band = jnp.logical_and(tok >= src - half, tok < src + half).astype(BF)
        dw = d_ref[pl.ds(pl.multiple_of(start, POOL_HALO), win), :]
        dt = d_ref[pl.ds(pl.multiple_of(r * tm, POOL_HALO), tm), :].astype(F32)
        o_ref[...] = (_dot(band, dw, 1, 0) - dt * cnt).astype(BF)

    return pl.pallas_call(
        body, name="pool_bwd_window", grid=(N_POOL_GROUPS, n // tm), out_shape=jax.ShapeDtypeStruct(into.shape, BF),
        in_specs=[pl.BlockSpec((n, pg), lambda g, r: (0, g)), HBM_SPEC],
        out_specs=pl.BlockSpec((tm, pg), lambda g, r: (r + rb0, g + cb0)),
        input_output_aliases={1: 0}, compiler_params=_cparams("parallel", "parallel"),
    )(dds, into)


def _pool_wgrad(dmat, dval, *, col0, tk):
    n, pw = dmat.shape
    pg = pw // N_POOL_GROUPS
    cb0 = col0 // pg
    nk = n // tk

    def body(a_ref, b_ref, o_ref, acc):
        k = pl.program_id(1)

        @pl.when(k == 0)
        def _():
            acc[...] = jnp.zeros_like(acc)

        acc[...] += _dot(a_ref[...], b_ref[...], 0, 0)

        @pl.when(k == nk - 1)
        def _():
            o_ref[0] = acc[...].astype(BF)

    return pl.pallas_call(
        body, name="pool_wgrad", grid=(N_POOL_GROUPS, nk), out_shape=jax.ShapeDtypeStruct((N_POOL_GROUPS, pg, pg), BF),
        in_specs=[pl.BlockSpec((tk, pg), lambda g, k: (k, g)), pl.BlockSpec((tk, pg), lambda g, k: (k, cb0 + g))],
        out_specs=pl.BlockSpec((1, pg, pg), lambda g, k: (g, 0, 0)), scratch_shapes=[pltpu.VMEM((pg, pg), F32)],
        compiler_params=_cparams("parallel", "arbitrary"),
    )(dmat, dval)


def _merge(attn_o, po_raw, proj, pool_scale, *, l, ga0, gp0, tm):
    n, aw = attn_o.shape
    bw = aw // 2
    rb0 = l // tm
    ga_b, gp_b = ga0 // bw, gp0 // bw
    assert ga0 % bw == 0 and gp0 % bw == 0 and l % tm == 0

    def body(a_ref, p_ref, g_ref, ps_ref, y_ref):
        cb = pl.program_id(1)
        sg, _ = _silu_and_grad(g_ref[...].astype(F32))

        @pl.when(cb < 2)
        def _():
            y_ref[...] = (a_ref[...].astype(F32) * sg).astype(BF)

        @pl.when(cb >= 2)
        def _():
            y_ref[...] = (p_ref[...].astype(F32) * ps_ref[...] * sg).astype(BF)

    return pl.pallas_call(
        body, name="merge", grid=(n // tm, 4), out_shape=jax.ShapeDtypeStruct((n, 2 * aw), BF),
        in_specs=[pl.BlockSpec((tm, bw), lambda i, cb: (i, jnp.minimum(cb, 1))),
                  pl.BlockSpec((tm, bw), lambda i, cb: (i, jnp.maximum(cb - 2, 0))),
                  pl.BlockSpec((tm, bw), lambda i, cb: (i + rb0, jnp.where(cb < 2, ga_b + cb, gp_b + cb - 2))),
                  pl.BlockSpec((1, bw), lambda i, cb: (0, jnp.maximum(cb - 2, 0)))],
        out_specs=pl.BlockSpec((tm, bw), lambda i, cb: (i, cb)), compiler_params=_cparams("parallel", "arbitrary"),
    )(attn_o, po_raw, proj, pool_scale)


def _merge_bwd(dy, attn_o, po_raw, proj, pool_scale, into, *, l, ga0, gp0, tm):
    n, aw = attn_o.shape
    bw = aw // 2
    rb0 = l // tm
    ga_b, gp_b = ga0 // bw, gp0 // bw

    def body(dy_ref, a_ref, p_ref, g_ref, ps_ref, into_ref, dv_ref, dg_ref, dps_ref):
        cb, i = pl.program_id(0), pl.program_id(1)
        sg, sgrad = _silu_and_grad(g_ref[...].astype(F32))
        dyv = dy_ref[...].astype(F32)

        @pl.when(cb < 2)
        def _():
            dv_ref[...] = (dyv * sg).astype(BF)
            dg_ref[...] = (dyv * a_ref[...].astype(F32) * sgrad).astype(BF)

        @pl.when(cb >= 2)
        def _():
            @pl.when(i == 0)
            def _():
                dps_ref[...] = jnp.zeros_like(dps_ref)

            pr = p_ref[...].astype(F32)
            dpo = dyv * sg
            dv_ref[...] = (dpo * ps_ref[...]).astype(BF)
            dg_ref[...] = (dyv * (pr * ps_ref[...]) * sgrad).astype(BF)
            dps_ref[0:1, :] += jnp.sum(dpo * pr, axis=0, keepdims=True)

    blk = pl.BlockSpec((tm, bw), lambda cb, i: (i, cb))
    gate_blk = pl.BlockSpec((tm, bw), lambda cb, i: (i + rb0, jnp.where(cb < 2, ga_b + cb, gp_b + cb - 2)))
    return pl.pallas_call(
        body, name="merge_bwd", grid=(4, n // tm),
        out_shape=(jax.ShapeDtypeStruct((n, 2 * aw), BF), jax.ShapeDtypeStruct(into.shape, BF),
                   jax.ShapeDtypeStruct((8, aw), F32)),
        in_specs=[blk, pl.BlockSpec((tm, bw), lambda cb, i: (i, jnp.minimum(cb, 1))),
                  pl.BlockSpec((tm, bw), lambda cb, i: (i, jnp.maximum(cb - 2, 0))), gate_blk,
                  pl.BlockSpec((1, bw), lambda cb, i: (0, jnp.maximum(cb - 2, 0))), HBM_SPEC],
        out_specs=(blk, gate_blk, pl.BlockSpec((8, bw), lambda cb, i: (0, jnp.maximum(cb - 2, 0)))),
        input_output_aliases={5: 1}, compiler_params=_cparams("arbitrary", "arbitrary"),
    )(dy, attn_o, po_raw, proj, pool_scale, into)


def _post(x, out, target, gate, g_post, *, tm):
    n, d = x.shape

    def body(x_ref, o_ref, t_ref, gate_ref, gp_ref, dxn_ref, dout_ref, st_ref):
        @pl.when(pl.program_id(0) == 0)
        def _():
            st_ref[...] = jnp.zeros_like(st_ref)

        ov = o_ref[...]
        rstd = lax.rsqrt(jnp.mean(ov * ov, axis=-1, keepdims=True) + EPS)
        on = ov * rstd
        rn = on * gp_ref[...]
        err = (x_ref[...] + gate_ref[...] * rn) - t_ref[...]
        dxn = err / d
        dxn_ref[...] = dxn
        drn = dxn * gate_ref[...]
        don = drn * gp_ref[...]
        dout_ref[...] = (rstd * (don - on * jnp.mean(don * on, axis=-1, keepdims=True))).astype(BF)
        st_ref[0:1, :] += jnp.sum(dxn * rn, axis=0, keepdims=True)
        st_ref[1:2, :] += jnp.sum(drn * on, axis=0, keepdims=True)
        st_ref[2:3, :] += jnp.sum(err * err, axis=0, keepdims=True)

    row = pl.BlockSpec((tm, d), lambda i: (i, 0))
    vec = pl.BlockSpec((1, d), lambda i: (0, 0))
    return pl.pallas_call(
        body, name="post", grid=(n // tm,),
        out_shape=(jax.ShapeDtypeStruct((n, d), F32), jax.ShapeDtypeStruct((n, d), BF), jax.ShapeDtypeStruct((8, d), F32)),
        in_specs=[row, row, row, vec, vec], out_specs=(row, row, pl.BlockSpec((8, d), lambda i: (0, 0))),
        compiler_params=_cparams("arbitrary"),
    )(x, out, target, gate, g_post)


def _prenorm_bwd(x, ctx, dh, dxn, g_pre, ss, *, tm):
    n, d = x.shape
    l = ctx.shape[0]
    nc = l // tm

    def body(x_ref, c_ref, dh_ref, dxn_ref, g_ref, ss_ref, gx_ref, st_ref):
        t = pl.program_id(0)

        @pl.when(t == 0)
        def _():
            st_ref[...] = jnp.zeros_like(st_ref)

        def go(src, scale, row):
            v = src[...]
            dhv = dh_ref[...].astype(F32)
            rstd = lax.rsqrt(jnp.mean(v * v, axis=-1, keepdims=True) + EPS)
            xn = v * rstd
            st_ref[row:row + 1, :] += jnp.sum(dhv, axis=0, keepdims=True)
            st_ref[row + 1:row + 2, :] += jnp.sum(dhv * (xn * g_ref[...]), axis=0, keepdims=True)
            dg = dhv * (1.0 + scale)
            st_ref[4:5, :] += jnp.sum(dg * xn, axis=0, keepdims=True)
            dxn_ = dg * g_ref[...]
            return rstd * (dxn_ - xn * jnp.mean(dxn_ * xn, axis=-1, keepdims=True))

        @pl.when(t < nc)
        def _():
            go(c_ref, ss_ref[2:3, :], 2)

        @pl.when(t >= nc)
        def _():
            gx_ref[...] = dxn_ref[...] + go(x_ref, ss_ref[0:1, :], 0)

    lat = pl.BlockSpec((tm, d), lambda t: (jnp.maximum(t - nc, 0), 0))
    return pl.pallas_call(
        body, name="prenorm_bwd", grid=((l + n) // tm,),
        out_shape=(jax.ShapeDtypeStruct((n, d), F32), jax.ShapeDtypeStruct((8, d), F32)),
        in_specs=[lat, pl.BlockSpec((tm, d), lambda t: (jnp.minimum(t, nc - 1), 0)),
                  pl.BlockSpec((tm, d), lambda t: (t, 0)), lat,
                  pl.BlockSpec((1, d), lambda t: (0, 0)), pl.BlockSpec((4, d), lambda t: (0, 0))],
        out_specs=(lat, pl.BlockSpec((8, d), lambda t: (0, 0))), compiler_params=_cparams("arbitrary"),
    )(x, ctx, dh, dxn, g_pre, ss)


def _ada_fwd(craw, w_ada, b_loc, *, tn):
    d, wn = w_ada.shape

    def body(c_ref, w_ref, b_ref, o_ref):
        act, _ = _silu_and_grad(c_ref[...])
        o_ref[...] = _dot(act.astype(BF), w_ref[...].astype(BF), 1, 0) + b_ref[...]

    return pl.pallas_call(
        body, name="ada_fwd", grid=(wn // tn,), out_shape=jax.ShapeDtypeStruct((16, wn), F32),
        in_specs=[pl.BlockSpec((16, d), lambda j: (0, 0)), pl.BlockSpec((d, tn), lambda j: (0, j)),
                  pl.BlockSpec((1, tn), lambda j: (0, j))],
        out_specs=pl.BlockSpec((16, tn), lambda j: (0, j)), compiler_params=_cparams("parallel"),
    )(craw, w_ada, b_loc)


def _ada_bwd(craw_t, dm, w, m, v, *, tm):
    d, wn = w.shape

    def body(ct_ref, dm_ref, w_ref, m_ref, v_ref, g_ref, dl_ref, nm_ref, nv_ref, da_ref):
        act, _ = _silu_and_grad(ct_ref[...])
        dmb = dm_ref[...].astype(BF)
        wv = w_ref[...]
        gv = _dot(act.astype(BF), dmb, 1, 0)
        da_ref[...] = _dot(dmb, wv.astype(BF), 1, 1)
        nm = ADAM_B1 * m_ref[...] + (1.0 - ADAM_B1) * gv
        nv = ADAM_B2 * v_ref[...] + (1.0 - ADAM_B2) * (gv * gv)
        m_hat = nm / (1.0 - ADAM_B1 ** ADAM_STEP)
        v_hat = nv / (1.0 - ADAM_B2 ** ADAM_STEP)
        g_ref[...] = gv
        dl_ref[...] = -ADAM_LR * (m_hat / (jnp.sqrt(v_hat) + ADAM_EPS) + ADAM_WD * wv)
        nm_ref[...] = nm
        nv_ref[...] = nv

    row = pl.BlockSpec((tm, wn), lambda i: (i, 0))
    return pl.pallas_call(
        body, name="ada_bwd", grid=(d // tm,),
        out_shape=tuple([jax.ShapeDtypeStruct((d, wn), F32)] * 4) + (jax.ShapeDtypeStruct((16, d), F32),),
        in_specs=[pl.BlockSpec((tm, 16), lambda i: (i, 0)), pl.BlockSpec((16, wn), lambda i: (0, 0)), row, row, row],
        out_specs=(row, row, row, row, pl.BlockSpec((16, tm), lambda i: (0, i))),
        compiler_params=_cparams("parallel"),
    )(craw_t, dm, w, m, v)


def _reduce_small(gath, d3):
    t = gath.shape[-1]

    def body(g_ref, dm_ref, s_ref):
        tot = g_ref[0]
        for b in range(1, 8):
            tot = tot + g_ref[b]
        s_ref[...] = tot
        dm_ref[...] = jnp.zeros_like(dm_ref)
        for b in range(8):
            dm_ref[b:b + 1, :] = g_ref[b][:, 0:d3]
        dm_ref[8:9, :] = tot[:, d3:2 * d3]

    return pl.pallas_call(
        body, name="reduce_small", out_shape=(jax.ShapeDtypeStruct((16, d3), F32), jax.ShapeDtypeStruct((1, t), F32)),
        in_specs=[VMEM_SPEC], out_specs=(VMEM_SPEC, VMEM_SPEC),
    )(gath)


def _cctx_grad(parts, c_ctx):
    def body(p_ref, c_ref, o_ref):
        tot = (p_ref[0] + p_ref[1]) + (p_ref[2] + p_ref[3])
        _, sgrad = _silu_and_grad(c_ref[...])
        o_ref[...] = tot * sgrad

    return pl.pallas_call(
        body, name="cctx_grad", out_shape=jax.ShapeDtypeStruct(c_ctx.shape, F32),
        in_specs=[VMEM_SPEC, VMEM_SPEC], out_specs=VMEM_SPEC,
    )(parts, c_ctx)


def _rope_tables(n, l):
    rows = n // GRID_W
    row = jnp.repeat(jnp.arange(rows, dtype=F32), GRID_W)
    col = jnp.tile(jnp.arange(GRID_W, dtype=F32), rows)
    inv = ROPE_THETA ** (-jnp.arange(ROPE_PAIRS, dtype=F32) / ROPE_PAIRS)
    ang = jnp.concatenate([row[:, None] * inv, col[:, None] * inv], axis=-1)
    cos, sin = jnp.cos(ang), jnp.sin(ang)
    cr, cc, sr, sc = cos[:, :ROPE_PAIRS], cos[:, ROPE_PAIRS:], sin[:, :ROPE_PAIRS], sin[:, ROPE_PAIRS:]
    ctab = jnp.concatenate([cr, cr, cc, cc], axis=-1)
    stab = jnp.concatenate([-sr, sr, -sc, sc], axis=-1)
    ctab = jnp.concatenate([jnp.ones((l, HEAD_DIM), F32), ctab], axis=0)
    stab = jnp.concatenate([jnp.zeros((l, HEAD_DIM), F32), stab], axis=0)
    return ctab, stab


def kernel(x, c, ctx, c_ctx, w_ada, b_ada, norm_pre, norm_post, w_in, q_norm, k_norm, pool_w, pool_scale, w_out, loss_target, m_c_ctx, m_w_ada, m_b_ada, m_norm_pre, m_norm_post, m_w_in, m_q_norm, m_k_norm, m_pool_w, m_pool_scale, m_w_out, v_c_ctx, v_w_ada, v_b_ada, v_norm_pre, v_norm_post, v_w_in, v_q_norm, v_k_norm, v_pool_w, v_pool_scale, v_w_out):
    x2, ctx2, tgt = x[0], ctx[0], loss_target[0]
    n, d = x2.shape
    l = ctx2.shape[0]
    s_len = l + n
    aw = d // 2
    pw = d - aw
    n_heads = aw // HEAD_DIM
    kvw = (n_heads // GQA_GROUP) * HEAD_DIM
    pg = pw // N_POOL_GROUPS
    k0, v0, ga0 = aw, aw + kvw, aw + 2 * kvw
    up0, gp0 = ga0 + aw, ga0 + aw + pw
    in_w = gp0 + pw
    d3 = 3 * d
    ada_w = w_ada.shape[-1]
    px, py, pc = lax.axis_index("x"), lax.axis_index("y"), lax.axis_index("c")
    chip = 2 * px + py
    tr = min(256, l)
    tq = min(512, n)
    tk = min(256, l)
    ts = s_len // 8 if (s_len // 8) % 16 == 0 else tr
    scale = HEAD_DIM ** -0.5

    c_all = _allgather_small(c, chips_only=False, name="gather_c").reshape(8, d)
    craw = jnp.concatenate([c_all, c_ctx.reshape(1, d), jnp.zeros((7, d), F32)], axis=0)
    b_loc = lax.dynamic_slice(b_ada, (0, chip * ada_w), (1, ada_w))
    mod_part = _ada_fwd(craw, w_ada[0], b_loc, tn=min(512, ada_w))
    mod_all = _allgather_small(mod_part, chips_only=True, name="gather_mod")
    mod_all = jnp.transpose(mod_all, (1, 0, 2)).reshape(16, d3)
    me = 4 * px + 2 * py + pc
    mod_b = lax.dynamic_slice(mod_all, (me, 0), (1, d3))
    mod_c = mod_all[8:9]
    ss = jnp.concatenate([mod_b[:, d:2 * d], mod_b[:, 0:d], mod_c[:, d:2 * d], mod_c[:, 0:d]], axis=0)
    gate = mod_b[:, 2 * d:]

    w_own = _cast_bf16(w_in[0], name="cast_w_in")
    later_weights = _gather_side([_cast_into(w_out[0], 0, chip, name="cast_w_out"),
                                  _cast_into(pool_w[0], 1, chip, name="cast_pool_w")], [(0, 1), (1, 2)])
    ids = jnp.stack([chip, jnp.bitwise_xor(chip, 2), jnp.bitwise_xor(chip, 1), jnp.bitwise_xor(chip, 3)])
    ws = in_w // 4
    tn_p = ws // 3 if (ws // 3) % HEAD_DIM == 0 else ws

    ctab, stab = _rope_tables(n, l)
    proj, h, w_x, w_y = _in_proj_own(x2, ctx2, norm_pre, ss, w_own, ids[0:1], width=in_w, tm=tr, tn=tn_p,
                                     side=_gather_parts_side(w_own, (0, 1)))
    proj, w_d = _in_proj_parts(h, [w_x, w_y], ids[1:3], proj, width=in_w, tm=ts, tn=tn_p, name="in_proj_xy",
                               side=_gather_parts_side(w_own, (2,)))
    (proj,) = _in_proj_parts(h, [w_d], ids[3:4], proj, width=in_w, tm=ts, tn=tn_p, name="in_proj_diag")
    w_parts = [w_own, w_x, w_y, w_d]
    qr = _qk_prep(proj, q_norm, ctab[l:], stab[l:], row0=l, col0=0, width=aw, nrows=n, tm=tr, name="q_prep",
                  out_scale=scale * LOG2E)
    kr = _qk_prep(proj, k_norm, ctab, stab, row0=0, col0=k0, width=kvw, nrows=s_len, tm=tr, name="k_prep")
    v_all = proj[:, v0:v0 + kvw]
    tkf = min(2048, n)
    vt_ctx = jnp.transpose(v_all[:l].reshape(1, l, kvw // HEAD_DIM, HEAD_DIM), (2, 0, 3, 1))
    vt_lat = jnp.transpose(v_all[l:].reshape(n // tkf, tkf, kvw // HEAD_DIM, HEAD_DIM), (2, 0, 3, 1))
    attn_o, lse, w_out_f, pool_f = _flash_fwd(qr, kr, vt_ctx, vt_lat, tq=tq, side=later_weights)
    dmat, po_raw = _pool_fwd(proj, pool_f, l=l, n=n, col0=up0, tm=min(512, n // 2))
    yv = _merge(attn_o, po_raw, proj, pool_scale, l=l, ga0=ga0, gp0=gp0, tm=tr)
    out = _matmul(yv, w_out_f, ca=1, cb=0, tm=min(512, n), tn=min(1024, d), tk=d, out_dtype=F32, name="out_proj")

    dxn, d_out, post_st = _post(x2, out, tgt, gate, norm_post, tm=min(128, n))
    loss = 0.5 * jnp.sum(post_st[2]) / d
    loss = lax.psum(loss, ("x", "y", "c"))
    g_wout = _matmul(yv, d_out, ca=0, cb=0, tm=min(512, d), tn=min(1024, d), tk=n, out_dtype=BF, name="wgrad_out")
    dy, from_sib_wout = _matmul(d_out, w_out_f, ca=1, cb=1, tm=min(512, n), tn=min(1024, d), tk=d, out_dtype=BF,
                                name="dgrad_out", side=_pair_side([g_wout], [1]))
    pair_wout = _add_own_half(g_wout, from_sib_wout, 1, pc, name="pair_sum_1")
    dproj = _zero_rows(l, (s_len, in_w), name="dproj_init")
    dval, dproj, dps = _merge_bwd(dy, attn_o, po_raw, proj, pool_scale, dproj, l=l, ga0=ga0, gp0=gp0, tm=tr)
    g_pool = _pool_wgrad(dmat, dval, col0=aw, tk=min(512, n))
    dds = _pool_bwd_map(dval, pool_f, n=n, col0=aw, tm=min(1024, n))
    dproj = _pool_bwd_window(dds, dproj, n=n, tm=tr, row0=l, col0=up0)
    delta = _attn_delta(attn_o, dval, tq=tq)
    tqb = min(2048, n)
    stat_shape = (kvw // HEAD_DIM, GQA_GROUP, n // tqb, 1, tqb)
    dq, dk, dproj, *from_chips_wout = _flash_bwd(
        qr, dval, kr, proj, lse.reshape(stat_shape), delta.reshape(stat_shape), dproj, v_col0=v0, tq=tqb, tk=tk,
        scale=scale, side=_chips_side([pair_wout], [0]))
    red_wout = _sum_own_block(pair_wout, from_chips_wout, 0, 1, chip, pc, name="chip_sum_1")
    dproj, dgq = _qk_bwd(dq, proj, q_norm, ctab[l:], stab[l:], dproj, row0=l, col0=0, tm=tr, name="q_bwd", transposed=True)
    dproj, dgk = _qk_bwd(dk, proj, k_norm, ctab, stab, dproj, row0=0, col0=k0, tm=tr, name="k_bwd")
    tn_w = 1024 if in_w % 1024 == 0 else 512
    g_send, g_wout_s = _wgrad_rows(h, dproj, 1 - pc, tm=min(512, d // 2), tn=tn_w, name="wgrad_in_send",
                                   side=_join_side([red_wout], [1]))
    g_keep, from_sib_win, from_sib_pool = _wgrad_rows(
        h, dproj, pc, tm=min(512, d // 2), tn=tn_w, name="wgrad_in_keep",
        side=_both(_swap_side([g_send]), _pair_side([g_pool], [2])))
    pair_win = _add2(g_keep, from_sib_win, name="pair_sum_0")
    pair_pool = _add_own_half(g_pool, from_sib_pool, 2, pc, name="pair_sum_2")
    dh, *from_chips = _dgrad_in_parts(dproj, w_parts, ids, tm=ts, tn=min(512, d), name="dgrad_in",
                                      side=_chips_side([pair_win, pair_pool], [1, 1]))
    red_win = _sum_own_block(pair_win, from_chips[0:3], 1, 0, chip, pc, name="chip_sum_0")
    red_pool = _sum_own_block(pair_pool, from_chips[3:6], 1, 2, chip, pc, name="chip_sum_2")
    grad_x, pre_st = _prenorm_bwd(x2, ctx2, dh, dxn, norm_pre, ss, tm=min(128, l))

    zero_d = jnp.zeros((1, d), F32)
    packed = jnp.concatenate([pre_st[0:1], pre_st[1:2], post_st[0:1], pre_st[2:3], pre_st[3:4], zero_d,
                              pre_st[4:5], post_st[1:2], dgq[0:1], dgk[0:1], dps[0:1]], axis=1)
    gath, g_win_s, g_pool_s = _allgather_small(packed, chips_only=False, name="gather_small",
                                               side=_join_side([red_win, red_pool], [0, 2]))
    dm, sums = _reduce_small(gath, d3)
    o1 = 2 * d3
    g_npre, g_npost = sums[:, o1:o1 + d], sums[:, o1 + d:o1 + 2 * d]
    g_q, g_k = sums[:, o1 + 2 * d:o1 + 2 * d + HEAD_DIM], sums[:, o1 + 2 * d + HEAD_DIM:o1 + 2 * d + 2 * HEAD_DIM]
    g_ps = sums[:, o1 + 2 * d + 2 * HEAD_DIM:]
    g_bada = sums[:, 0:d3] + sums[:, d3:2 * d3]
    dm_loc = lax.dynamic_slice(dm, (0, chip * ada_w), (16, ada_w))
    g_wada, dl_wada, nm_wada, nv_wada, dact = _ada_bwd(craw.T, dm_loc, w_ada[0], m_w_ada[0], v_w_ada[0], tm=128)
    cparts = _allgather_small(dact[8:9], chips_only=True, name="gather_cctx")
    g_cctx = _cctx_grad(cparts, c_ctx.reshape(1, d)).reshape(d)

    def upd(w, g, m, v, name):
        return _adamw(w, g.reshape(w.shape), m, v, name=name)

    grads = {"c_ctx": g_cctx, "b_ada": g_bada, "norm_pre": g_npre, "norm_post": g_npost, "w_in": g_win_s[None],
             "q_norm": g_q, "k_norm": g_k, "pool_w": g_pool_s[None], "pool_scale": g_ps, "w_out": g_wout_s[None]}
    res = {"w_ada": (g_wada[None], dl_wada[None], nm_wada[None], nv_wada[None])}
    given = {"c_ctx": (c_ctx, m_c_ctx, v_c_ctx), "b_ada": (b_ada, m_b_ada, v_b_ada),
             "norm_pre": (norm_pre, m_norm_pre, v_norm_pre), "norm_post": (norm_post, m_norm_post, v_norm_post),
             "w_in": (w_in, m_w_in, v_w_in), "q_norm": (q_norm, m_q_norm, v_q_norm), "k_norm": (k_norm, m_k_norm, v_k_norm),
             "pool_w": (pool_w, m_pool_w, v_pool_w), "pool_scale": (pool_scale, m_pool_scale, v_pool_scale),
             "w_out": (w_out, m_w_out, v_w_out)}
    for nme, (w, m, v) in given.items():
        g = grads[nme].reshape(w.shape)
        res[nme] = upd(w, g, m, v, "adamw_" + nme)
    order = ["c_ctx", "w_ada", "b_ada", "norm_pre", "norm_post", "w_in", "q_norm", "k_norm", "pool_w", "pool_scale", "w_out"]
    return (loss, grad_x[None], *[res[k][0] for k in order], *[res[k][1] for k in order],
            *[res[k][2] for k in order], *[res[k][3] for k in order])
```

```python
import functools

import jax
import jax.numpy as jnp
from jax import lax
from jax.experimental import pallas as pl
from jax.experimental.pallas import tpu as pltpu

F32 = jnp.float32
BF = jnp.bfloat16
MESH = pl.DeviceIdType.MESH

HEAD_DIM = 128
GQA_GROUP = 4
GRID_W = 64
ROPE_PAIRS = HEAD_DIM // 4
ROPE_THETA = 10000.0
EPS = 1e-6
N_POOL_GROUPS = 4
POOL_HALO = 128
ADAM_LR = 0.001
ADAM_B1 = 0.9
ADAM_B2 = 0.999
ADAM_EPS = 1e-08
ADAM_WD = 0.01
ADAM_STEP = 10
LOG2E = 1.4426950408889634
MIB = 2 ** 20
VMEM_LIMIT = 48 * MIB
CHIP_MASKS = ((1, 0, 0), (0, 1, 0), (1, 1, 0))
ALL_MASKS = ((0, 0, 1), (0, 1, 0), (0, 1, 1), (1, 0, 0), (1, 0, 1), (1, 1, 0), (1, 1, 1))
HBM_SPEC = pl.BlockSpec(memory_space=pl.ANY)
VMEM_SPEC = pl.BlockSpec(memory_space=pltpu.VMEM)


def _cparams(*sem):
    return pltpu.CompilerParams(dimension_semantics=sem, vmem_limit_bytes=VMEM_LIMIT)


def _sigmoid(v):
    return 0.5 * jnp.tanh(0.5 * v) + 0.5


def _silu_and_grad(v):
    s = _sigmoid(v)
    return v * s, s * (1.0 + v * (1.0 - s))


def _dot(a, b, ca, cb):
    return lax.dot_general(a, b, (((ca,), (cb,)), ((), ())), preferred_element_type=F32)


def _block_rows(rows, width, itemsize=4, target=MIB):
    best = 8
    for t in range(8, rows + 1, 8):
        if rows % t == 0 and t * width * itemsize <= target:
            best = t
    return best if rows % 8 == 0 else rows


def _my_pos():
    return lax.axis_index("x"), lax.axis_index("y"), lax.axis_index("c")


def _flip(pos, mask):
    return tuple(jnp.bitwise_xor(p, m) if m else p for p, m in zip(pos, mask))


def _allgather_small(v, *, chips_only, name, side=None):
    r, w = v.shape
    masks = CHIP_MASKS if chips_only else ALL_MASKS
    nslot = 4 if chips_only else 8
    n_sin = len(side.ins) if side else 0
    n_sout = len(side.out_shapes) if side else 0

    def slot(pos):
        return 2 * pos[0] + pos[1] if chips_only else 4 * pos[0] + 2 * pos[1] + pos[2]

    def body(v_ref, *refs):
        s_ins, o_ref = refs[:n_sin], refs[n_sin]
        s_outs = refs[n_sin + 1:n_sin + 1 + n_sout]
        send_sems, recv_sems, local_sem = refs[n_sin + 1 + n_sout:n_sin + 4 + n_sout]
        s_sems = refs[n_sin + 4 + n_sout:]
        if side:
            side.start(s_ins, s_outs, s_sems)
        me = _my_pos()

        def copy(k, block_of, to):
            return pltpu.make_async_remote_copy(
                src_ref=v_ref, dst_ref=o_ref.at[slot(block_of)], send_sem=send_sems.at[k], recv_sem=recv_sems.at[k],
                device_id=to, device_id_type=MESH)

        mine = pltpu.make_async_copy(v_ref, o_ref.at[slot(me)], local_sem)
        mine.start()
        sends = [copy(k, me, _flip(me, m)) for k, m in enumerate(masks)]
        for cp in sends:
            cp.start()
        for k, m in enumerate(masks):
            copy(k, _flip(me, m), me).wait_recv()
        for cp in sends:
            cp.wait_send()
        mine.wait()
        if side:
            side.finish(s_ins, s_outs, s_sems)

    o_shape = jax.ShapeDtypeStruct((nslot, r, w), v.dtype)
    sems = [pltpu.SemaphoreType.DMA((len(masks),)), pltpu.SemaphoreType.DMA((len(masks),)), pltpu.SemaphoreType.DMA]
    if not side:
        return pl.pallas_call(body, name=name, out_shape=o_shape, in_specs=[VMEM_SPEC], out_specs=VMEM_SPEC,
                              scratch_shapes=sems)(v)
    return pl.pallas_call(
        body, name=name, out_shape=(o_shape, *side.out_shapes),
        in_specs=[VMEM_SPEC] + [HBM_SPEC] * n_sin, out_specs=(VMEM_SPEC, *([HBM_SPEC] * n_sout)),
        input_output_aliases={1 + i: 1 + o for i, o in side.aliases.items()},
        scratch_shapes=sems + side.sem_shapes,
    )(v, *side.ins)


def _sl(ref, axis, start, size):
    idx = [slice(None)] * len(ref.shape)
    idx[axis] = pl.ds(start, size)
    return ref.at[tuple(idx)]


class _Side:
    def __init__(self, ins, out_shapes, aliases, sem_shapes, start, finish, mid=None):
        self.ins, self.out_shapes, self.aliases, self.sem_shapes = list(ins), list(out_shapes), dict(aliases), list(sem_shapes)
        self.start, self.finish, self.mid = start, finish, mid


def _run_side(side, *, name):
    n_in, n_out = len(side.ins), len(side.out_shapes)

    def body(*refs):
        parts = refs[:n_in], refs[n_in:n_in + n_out], refs[n_in + n_out:]
        side.start(*parts)
        if side.mid is not None:
            side.mid(*parts)
        side.finish(*parts)

    return pl.pallas_call(
        body, name=name, out_shape=tuple(side.out_shapes), in_specs=[HBM_SPEC] * n_in, out_specs=tuple([HBM_SPEC] * n_out),
        input_output_aliases=side.aliases, scratch_shapes=side.sem_shapes,
    )(*side.ins)


def _side_at(grid):
    pids = [pl.program_id(a) for a in range(len(grid))]

    def at(values):
        cond = pids[0] == values[0]
        for p, v in zip(pids[1:], values[1:]):
            cond = jnp.logical_and(cond, p == v)
        return cond

    zeros = [0] * len(grid)
    return at(zeros), at([grid[0] // 2] + zeros[1:]), at([g - 1 for g in grid])


def _gather_side(shards, layouts):
    nm = len(shards)

    def copies(outs, sems):
        send_sems, recv_sems, fsend_sems, frecv_sems = sems
        x, y, c = _my_pos()
        me = (x, y, c)
        sib = (x, y, 1 - c)
        sends, landed, fwds, from_sib = [], [], [], []
        for m in range(nm):
            a_s, a_h = layouts[m]
            ns, nh = outs[m].shape[a_s] // 4, outs[m].shape[a_h] // 2

            def region(chip_pos, half, m=m, a_s=a_s, a_h=a_h, ns=ns, nh=nh):
                j = 2 * chip_pos[0] + chip_pos[1]
                return _sl(_sl(outs[m], a_s, j * ns, ns), a_h, half * nh, nh)

            for k, mask in enumerate(CHIP_MASKS):
                other = _flip(me, mask)

                def rc(ref, ssem, rsem, to):
                    return pltpu.make_async_remote_copy(src_ref=ref, dst_ref=ref, send_sem=ssem, recv_sem=rsem,
                                                        device_id=to, device_id_type=MESH)

                sends.append(rc(region(me, c), send_sems.at[m, k], recv_sems.at[m, k], other))
                landed.append(rc(region(other, c), send_sems.at[m, k], recv_sems.at[m, k], me))
                fwds.append(rc(region(other, c), fsend_sems.at[m, k], frecv_sems.at[m, k], sib))
                from_sib.append(rc(region(other, 1 - c), fsend_sems.at[m, k], frecv_sems.at[m, k], sib))
        return sends, landed, fwds, from_sib

    def start(ins, outs, sems):
        for cp in copies(outs, sems)[0]:
            cp.start()

    def mid(ins, outs, sems):
        _, landed, fwds, _ = copies(outs, sems)
        for arrived, fw in zip(landed, fwds):
            arrived.wait_recv()
            fw.start()

    def finish(ins, outs, sems):
        sends, _, fwds, from_sib = copies(outs, sems)
        for cp in from_sib:
            cp.wait_recv()
        for cp in sends + fwds:
            cp.wait_send()

    return _Side(shards, [jax.ShapeDtypeStruct(s.shape, s.dtype) for s in shards], {i: i for i in range(nm)},
                 [pltpu.SemaphoreType.DMA((nm, 3))] * 4, start, finish, mid)


def _both(s1, s2):
    n1, m1, k1 = len(s1.ins), len(s1.out_shapes), len(s1.sem_shapes)

    def split(fn1, fn2):
        def run(ins, outs, sems):
            fn1(ins[:n1], outs[:m1], sems[:k1])
            fn2(ins[n1:], outs[m1:], sems[k1:])
        return run

    aliases = dict(s1.aliases)
    aliases.update({n1 + i: m1 + o for i, o in s2.aliases.items()})
    return _Side(s1.ins + s2.ins, s1.out_shapes + s2.out_shapes, aliases, s1.sem_shapes + s2.sem_shapes,
                 split(s1.start, s2.start), split(s1.finish, s2.finish))


def _swap_side(mats):
    nm = len(mats)

    def copies(ins, outs, sems):
        x, y, c = _my_pos()
        return [pltpu.make_async_remote_copy(src_ref=ins[m], dst_ref=outs[m], send_sem=sems[0].at[m], recv_sem=sems[1].at[m],
                                             device_id=(x, y, 1 - c), device_id_type=MESH) for m in range(nm)]

    def start(ins, outs, sems):
        for cp in copies(ins, outs, sems):
            cp.start()

    def finish(ins, outs, sems):
        for cp in copies(ins, outs, sems):
            cp.wait()

    return _Side(mats, [jax.ShapeDtypeStruct(s.shape, s.dtype) for s in mats], {},
                 [pltpu.SemaphoreType.DMA((nm,))] * 2, start, finish)


def _gather_parts_side(w_own, pieces, cw):
    nt = len(pieces)
    nh = w_own.shape[0] // 2

    def copies(ins, outs, sems):
        send_sems, recv_sems, fsend_sems, frecv_sems = sems
        x, y, c = _my_pos()
        me, sib = (x, y, c), (x, y, 1 - c)
        sends, landed, fwds, from_sib = [], [], [], []
        for t, (k, r) in enumerate(pieces):
            mine = _sl(outs[t], 0, c * nh, nh)
            theirs = _sl(outs[t], 0, (1 - c) * nh, nh)

            def rc(src, dst, ssem, rsem, to):
                return pltpu.make_async_remote_copy(src_ref=src, dst_ref=dst, send_sem=ssem, recv_sem=rsem,
                                                    device_id=to, device_id_type=MESH)

            src = ins[0].at[pl.ds(c * nh, nh), pl.ds(r * cw, cw)]
            sends.append(rc(src, mine, send_sems.at[t], recv_sems.at[t], _flip(me, CHIP_MASKS[k])))
            landed.append(rc(mine, mine, send_sems.at[t], recv_sems.at[t], me))
            fwds.append(rc(mine, mine, fsend_sems.at[t], frecv_sems.at[t], sib))
            from_sib.append(rc(theirs, theirs, fsend_sems.at[t], frecv_sems.at[t], sib))
        return sends, landed, fwds, from_sib

    def start(ins, outs, sems):
        for cp in copies(ins, outs, sems)[0]:
            cp.start()

    def finish(ins, outs, sems):
        sends, landed, fwds, from_sib = copies(ins, outs, sems)
        for arrived, fw in zip(landed, fwds):
            arrived.wait_recv()
            fw.start()
        for cp in from_sib:
            cp.wait_recv()
        for cp in sends + fwds:
            cp.wait_send()

    return _Side([w_own], [jax.ShapeDtypeStruct((w_own.shape[0], cw), w_own.dtype)] * nt, {},
                 [pltpu.SemaphoreType.DMA((nt,))] * 4, start, finish)


def _pair_side(mats, half_axes):
    nm = len(mats)

    def copies(ins, outs, sems):
        x, y, c = _my_pos()
        cps = []
        for m in range(nm):
            nh = ins[m].shape[half_axes[m]] // 2
            cps.append(pltpu.make_async_remote_copy(
                src_ref=_sl(ins[m], half_axes[m], (1 - c) * nh, nh), dst_ref=outs[m],
                send_sem=sems[0].at[m], recv_sem=sems[1].at[m], device_id=(x, y, 1 - c), device_id_type=MESH))
        return cps

    def start(ins, outs, sems):
        for cp in copies(ins, outs, sems):
            cp.start()

    def finish(ins, outs, sems):
        for cp in copies(ins, outs, sems):
            cp.wait()

    out_shapes = []
    for s, a_h in zip(mats, half_axes):
        shp = list(s.shape)
        shp[a_h] //= 2
        out_shapes.append(jax.ShapeDtypeStruct(tuple(shp), s.dtype))
    return _Side(mats, out_shapes, {}, [pltpu.SemaphoreType.DMA((nm,))] * 2, start, finish)


def _chips_side(mats, shard_axes):
    nm = len(mats)

    def copies(ins, outs, sems):
        me = _my_pos()
        cps = []
        for m in range(nm):
            ns = ins[m].shape[shard_axes[m]] // 4
            for k, mask in enumerate(CHIP_MASKS):
                other = _flip(me, mask)
                cps.append(pltpu.make_async_remote_copy(
                    src_ref=_sl(ins[m], shard_axes[m], (2 * other[0] + other[1]) * ns, ns), dst_ref=outs[3 * m + k],
                    send_sem=sems[0].at[m, k], recv_sem=sems[1].at[m, k], device_id=other, device_id_type=MESH))
        return cps

    def start(ins, outs, sems):
        for cp in copies(ins, outs, sems):
            cp.start()

    def finish(ins, outs, sems):
        for cp in copies(ins, outs, sems):
            cp.wait()

    out_shapes = []
    for s, a_s in zip(mats, shard_axes):
        shp = list(s.shape)
        shp[a_s] //= 4
        out_shapes += [jax.ShapeDtypeStruct(tuple(shp), s.dtype)] * 3
    return _Side(mats, out_shapes, {}, [pltpu.SemaphoreType.DMA((nm, 3))] * 2, start, finish)


def _join_side(halves, half_axes):
    nm = len(halves)

    def copies(outs, sems):
        x, y, c = _my_pos()
        mine, theirs = [], []
        for m in range(nm):
            nh = outs[m].shape[half_axes[m]] // 2
            for half, lst in ((c, mine), (1 - c, theirs)):
                ref = _sl(outs[m], half_axes[m], half * nh, nh)
                lst.append(pltpu.make_async_remote_copy(
                    src_ref=ref, dst_ref=ref, send_sem=sems[0].at[m], recv_sem=sems[1].at[m],
                    device_id=(x, y, 1 - c), device_id_type=MESH))
        return mine, theirs

    def start(ins, outs, sems):
        for cp in copies(outs, sems)[0]:
            cp.start()

    def finish(ins, outs, sems):
        mine, theirs = copies(outs, sems)
        for cp in mine:
            cp.wait_send()
        for cp in theirs:
            cp.wait_recv()

    return _Side(halves, [jax.ShapeDtypeStruct(s.shape, s.dtype) for s in halves], {i: i for i in range(nm)},
                 [pltpu.SemaphoreType.DMA((nm,))] * 2, start, finish)


def _part_specs(shp, axis, slot, itemsize, target=MIB):
    if len(shp) == 2:
        rows, width = shp
        tm = _block_rows(rows, width, itemsize, target)
        nb = rows // tm
        own = pl.BlockSpec((tm, width), lambda i, p: (i, 0))
        if axis == 1:
            part = pl.BlockSpec((tm, width), lambda i, p: (i, p[slot]))
        else:
            part = pl.BlockSpec((tm, width), lambda i, p: (p[slot] * nb + i, 0))
        return (nb,), own, part
    assert len(shp) == 3 and axis in (1, 2)
    g, rows, width = shp
    own = pl.BlockSpec((1, rows, width), lambda i, p: (i, 0, 0))
    if axis == 1:
        part = pl.BlockSpec((1, rows, width), lambda i, p: (i, p[slot], 0))
    else:
        part = pl.BlockSpec((1, rows, width), lambda i, p: (i, 0, p[slot]))
    return (g,), own, part


def _cast_into(w, axis, p, *, name):
    grid, own, part = _part_specs(w.shape, axis, 0, 4, 2 * MIB)
    big = list(w.shape)
    big[axis] *= 4

    def body(p_ref, w_ref, o_ref):
        o_ref[...] = w_ref[...].astype(BF)

    return pl.pallas_call(
        body, name=name, out_shape=jax.ShapeDtypeStruct(tuple(big), BF),
        grid_spec=pltpu.PrefetchScalarGridSpec(num_scalar_prefetch=1, grid=grid, in_specs=[own], out_specs=part),
        compiler_params=_cparams("parallel"),
    )(p.reshape(1), w)


def _cast_bf16(w, *, name):
    rows, width = w.shape
    tm = _block_rows(rows, width, 4, 2 * MIB)

    def body(w_ref, o_ref):
        o_ref[...] = w_ref[...].astype(BF)

    spec = pl.BlockSpec((tm, width), lambda i: (i, 0))
    return pl.pallas_call(body, name=name, grid=(rows // tm,), out_shape=jax.ShapeDtypeStruct(w.shape, BF),
                          in_specs=[spec], out_specs=spec, compiler_params=_cparams("parallel"))(w)


def _add2(a, b, *, name):
    rows, width = a.shape
    tm = _block_rows(rows, width, 2, 4 * MIB)

    def body(a_ref, b_ref, o_ref):
        o_ref[...] = (a_ref[...].astype(F32) + b_ref[...].astype(F32)).astype(BF)

    spec = pl.BlockSpec((tm, width), lambda i: (i, 0))
    return pl.pallas_call(body, name=name, grid=(rows // tm,), out_shape=jax.ShapeDtypeStruct(a.shape, BF),
                          in_specs=[spec, spec], out_specs=spec, compiler_params=_cparams("parallel"))(a, b)


def _add_own_half(full, recv, half_axis, c, *, name):
    grid, own, part = _part_specs(recv.shape, half_axis, 0, 2, 4 * MIB)

    def body(c_ref, f_ref, r_ref, o_ref):
        o_ref[...] = (f_ref[...].astype(F32) + r_ref[...].astype(F32)).astype(BF)

    return pl.pallas_call(
        body, name=name, out_shape=jax.ShapeDtypeStruct(recv.shape, BF),
        grid_spec=pltpu.PrefetchScalarGridSpec(num_scalar_prefetch=1, grid=grid, in_specs=[part, own], out_specs=own),
        compiler_params=_cparams("parallel"),
    )(c.reshape(1), full, recv)


def _sum_own_block(mine_all, recvs, shard_axis, half_axis, j, c, *, name):
    shp = recvs[0].shape
    grid, own, mine = _part_specs(shp, shard_axis, 0, 4, 2 * MIB)
    _, _, place = _part_specs(shp, half_axis, 1, 4, 2 * MIB)
    big = list(shp)
    big[half_axis] *= 2

    def body(p_ref, a_ref, r0, r1, r2, o_ref):
        o_ref[...] = ((a_ref[...].astype(F32) + r0[...].astype(F32)) + r1[...].astype(F32)) + r2[...].astype(F32)

    return pl.pallas_call(
        body, name=name, out_shape=jax.ShapeDtypeStruct(tuple(big), F32),
        grid_spec=pltpu.PrefetchScalarGridSpec(
            num_scalar_prefetch=1, grid=grid, in_specs=[mine, own, own, own], out_specs=place),
        compiler_params=_cparams("parallel"),
    )(jnp.stack([j, c]), mine_all, *recvs)


def _adamw(w, g, m, v, *, name):
    shp = w.shape
    width = shp[-1] if len(shp) > 1 else shp[0]
    rows = 1
    for d in shp[:-1]:
        rows *= d
    if len(shp) == 1:
        rows = 1
    args = [a.reshape(rows, width) for a in (w, g, m, v)]
    tm = _block_rows(rows, width, 4, 2 * MIB)

    def body(w_ref, g_ref, m_ref, v_ref, go_ref, d_ref, nm_ref, nv_ref):
        gv = g_ref[...]
        go_ref[...] = gv
        nm = ADAM_B1 * m_ref[...] + (1.0 - ADAM_B1) * gv
        nv = ADAM_B2 * v_ref[...] + (1.0 - ADAM_B2) * (gv * gv)
        m_hat = nm / (1.0 - ADAM_B1 ** ADAM_STEP)
        v_hat = nv / (1.0 - ADAM_B2 ** ADAM_STEP)
        d_ref[...] = -ADAM_LR * (m_hat / (jnp.sqrt(v_hat) + ADAM_EPS) + ADAM_WD * w_ref[...])
        nm_ref[...] = nm
        nv_ref[...] = nv

    spec = pl.BlockSpec((tm, width), lambda i: (i, 0))
    outs = pl.pallas_call(
        body, name=name, grid=(rows // tm,), out_shape=tuple([jax.ShapeDtypeStruct((rows, width), F32)] * 4),
        in_specs=[spec] * 4, out_specs=tuple([spec] * 4), compiler_params=_cparams("parallel"),
    )(*args)
    return tuple(o.reshape(shp) for o in outs)


def _matmul(a, b, *, ca, cb, tm, tn, tk, out_dtype, name, b_resident=False, side=None):
    m, kdim = a.shape[1 - ca], a.shape[ca]
    n = b.shape[1 - cb]
    nk = kdim // tk
    assert m % tm == 0 and n % tn == 0 and kdim % tk == 0
    n_sin = len(side.ins) if side else 0
    n_sout = len(side.out_shapes) if side else 0
    if b_resident:
        gi = lambda p, q, k: (q, p, k)
        grid = (n // tn, m // tm, nk)
    else:
        gi = lambda p, q, k: (p, q, k)
        grid = (m // tm, n // tn, nk)

    def body(*refs):
        a_ref, b_ref = refs[0], refs[1]
        s_ins = refs[2:2 + n_sin]
        o_ref = refs[2 + n_sin]
        s_outs = refs[3 + n_sin:3 + n_sin + n_sout]
        rest = refs[3 + n_sin + n_sout:]
        acc, sems = (rest[:1], rest[1:]) if nk > 1 else ((), rest)
        if side:
            first, middle, last = _side_at(grid)
            pl.when(first)(lambda: side.start(s_ins, s_outs, sems))
            if side.mid is not None:
                pl.when(middle)(lambda: side.mid(s_ins, s_outs, sems))
        part = _dot(a_ref[...], b_ref[...], ca, cb)
        if nk == 1:
            o_ref[...] = part.astype(o_ref.dtype)
        else:
            k = pl.program_id(2)

            @pl.when(k == 0)
            def _():
                acc[0][...] = part

            @pl.when(jnp.logical_and(k > 0, k < nk - 1))
            def _():
                acc[0][...] += part

            @pl.when(k == nk - 1)
            def _():
                o_ref[...] = (acc[0][...] + part).astype(o_ref.dtype)
        if side:
            pl.when(last)(lambda: side.finish(s_ins, s_outs, sems))

    def a_map(p, q, k):
        i, _, kk = gi(p, q, k)
        return (i, kk) if ca == 1 else (kk, i)

    def b_map(p, q, k):
        _, j, kk = gi(p, q, k)
        return (kk, j) if cb == 0 else (j, kk)

    def o_map(p, q, k):
        i, j, _ = gi(p, q, k)
        return (i, j)

    a_spec = pl.BlockSpec((tm, tk) if ca == 1 else (tk, tm), a_map)
    b_spec = pl.BlockSpec((tk, tn) if cb == 0 else (tn, tk), b_map)
    o_shape = jax.ShapeDtypeStruct((m, n), out_dtype)
    acc_shapes = [pltpu.VMEM((tm, tn), F32)] if nk > 1 else []
    if not side:
        return pl.pallas_call(
            body, name=name, grid=grid, out_shape=o_shape,
            in_specs=[a_spec, b_spec], out_specs=pl.BlockSpec((tm, tn), o_map), scratch_shapes=acc_shapes,
            compiler_params=_cparams("parallel", "parallel", "arbitrary"),
        )(a, b)
    return pl.pallas_call(
        body, name=name, grid=grid, out_shape=(o_shape, *side.out_shapes),
        in_specs=[a_spec, b_spec] + [HBM_SPEC] * n_sin,
        out_specs=(pl.BlockSpec((tm, tn), o_map), *([HBM_SPEC] * n_sout)),
        input_output_aliases={2 + i: 1 + o for i, o in side.aliases.items()},
        scratch_shapes=acc_shapes + side.sem_shapes,
        compiler_params=_cparams("arbitrary", "arbitrary", "arbitrary"),
    )(a, b, *side.ins)


def _wgrad_rows(a, b, half, *, tm, tn, name, side=None):
    kdim, m = a.shape
    n = b.shape[1]
    nb = (m // 2) // tm
    grid = (nb, n // tn)
    n_sin = len(side.ins) if side else 0
    n_sout = len(side.out_shapes) if side else 0

    def body(p_ref, a_ref, b_ref, *refs):
        s_ins, o_ref = refs[:n_sin], refs[n_sin]
        s_outs, sems = refs[n_sin + 1:n_sin + 1 + n_sout], refs[n_sin + 1 + n_sout:]
        if side:
            first, _, last = _side_at(grid)
            pl.when(first)(lambda: side.start(s_ins, s_outs, sems))
        o_ref[...] = _dot(a_ref[...], b_ref[...], 0, 0).astype(o_ref.dtype)
        if side:
            pl.when(last)(lambda: side.finish(s_ins, s_outs, sems))

    outs = pl.pallas_call(
        body, name=name, out_shape=(jax.ShapeDtypeStruct((m // 2, n), BF), *(side.out_shapes if side else [])),
        grid_spec=pltpu.PrefetchScalarGridSpec(
            num_scalar_prefetch=1, grid=grid,
            in_specs=[pl.BlockSpec((kdim, tm), lambda i, j, p: (0, p[0] * nb + i)),
                      pl.BlockSpec((kdim, tn), lambda i, j, p: (0, j))] + [HBM_SPEC] * n_sin,
            out_specs=(pl.BlockSpec((tm, tn), lambda i, j, p: (i, j)), *([HBM_SPEC] * n_sout)),
            scratch_shapes=side.sem_shapes if side else []),
        input_output_aliases={3 + i: 1 + o for i, o in side.aliases.items()} if side else {},
        compiler_params=_cparams("arbitrary", "arbitrary"),
    )(half.reshape(1), a, b, *(side.ins if side else []))
    return outs if side else outs[0]


def _in_proj_parts(h, w_parts, ids, prev, *, width, tm, tn, name, side=None):
    s_len, d = h.shape
    nb = w_parts[0].shape[1] // tn
    n_p = len(w_parts)
    n_prev = 0 if prev is None else 1
    n_sin = len(side.ins) if side else 0
    n_sout = len(side.out_shapes) if side else 0
    grid = (n_p, s_len // tm, nb)
    base = 1 + n_p + n_prev

    def body(ids_ref, *refs):
        h_ref, w_refs = refs[0], refs[1:1 + n_p]
        s_ins = refs[base:base + n_sin]
        o_ref = refs[base + n_sin]
        s_outs = refs[base + n_sin + 1:base + n_sin + 1 + n_sout]
        sems = refs[base + n_sin + 1 + n_sout:]
        if side:
            first, _, last = _side_at(grid)
            pl.when(first)(lambda: side.start(s_ins, s_outs, sems))
        for s in range(n_p):
            @pl.when(pl.program_id(0) == s)
            def _(s=s):
                o_ref[...] = _dot(h_ref[...], w_refs[s][...], 1, 0).astype(o_ref.dtype)
        if side:
            pl.when(last)(lambda: side.finish(s_ins, s_outs, sems))

    in_specs = [pl.BlockSpec((tm, d), lambda p, i, j, ids_ref: (i, 0))]
    for s in range(n_p):
        in_specs.append(pl.BlockSpec((d, tn), lambda p, i, j, ids_ref, s=s: (0, jnp.where(p == s, j, 0))))
    in_specs += [HBM_SPEC] * (n_prev + n_sin)
    o_spec = pl.BlockSpec((tm, tn), lambda p, i, j, ids_ref: (i, ids_ref[p] * nb + j))
    aliases = {2 + n_p: 0} if prev is not None else {}
    if side:
        aliases.update({1 + base + i: 1 + o for i, o in side.aliases.items()})
    outs = pl.pallas_call(
        body, name=name, out_shape=(jax.ShapeDtypeStruct((s_len, width), BF), *(side.out_shapes if side else [])),
        grid_spec=pltpu.PrefetchScalarGridSpec(
            num_scalar_prefetch=1, grid=grid, in_specs=in_specs, out_specs=(o_spec, *([HBM_SPEC] * n_sout)),
            scratch_shapes=side.sem_shapes if side else []),
        input_output_aliases=aliases, compiler_params=_cparams("arbitrary", "arbitrary", "arbitrary"),
    )(ids, h, *w_parts, *([prev] if prev is not None else []), *(side.ins if side else []))
    return outs


def _in_proj_own(x, ctx, g_pre, ss, w_own, ids, *, width, tm, tn, side):
    n, d = x.shape
    l = ctx.shape[0]
    nc = l // tm
    nb = w_own.shape[1] // tn
    grid = ((l + n) // tm, nb)
    n_sin, n_sout = len(side.ins), len(side.out_shapes)

    def body(ids_ref, x_ref, c_ref, g_ref, ss_ref, w_ref, *refs):
        s_ins = refs[:n_sin]
        o_ref, h_ref = refs[n_sin:n_sin + 2]
        s_outs, sems = refs[n_sin + 2:n_sin + 2 + n_sout], refs[n_sin + 2 + n_sout:]
        first, _, last = _side_at(grid)
        pl.when(first)(lambda: side.start(s_ins, s_outs, sems))
        t, j = pl.program_id(0), pl.program_id(1)

        def norm(src, scale, shift):
            v = src[...]
            rstd = lax.rsqrt(jnp.mean(v * v, axis=-1, keepdims=True) + EPS)
            h_ref[...] = ((v * rstd * g_ref[...]) * (1.0 + scale) + shift).astype(BF)

        @pl.when(jnp.logical_and(j == 0, t < nc))
        def _():
            norm(c_ref, ss_ref[2:3, :], ss_ref[3:4, :])

        @pl.when(jnp.logical_and(j == 0, t >= nc))
        def _():
            norm(x_ref, ss_ref[0:1, :], ss_ref[1:2, :])

        o_ref[...] = _dot(h_ref[...], w_ref[...], 1, 0).astype(o_ref.dtype)
        pl.when(last)(lambda: side.finish(s_ins, s_outs, sems))

    return pl.pallas_call(
        body, name="in_proj_own",
        out_shape=(jax.ShapeDtypeStruct((l + n, width), BF), jax.ShapeDtypeStruct((l + n, d), BF), *side.out_shapes),
        grid_spec=pltpu.PrefetchScalarGridSpec(
            num_scalar_prefetch=1, grid=grid,
            in_specs=[pl.BlockSpec((tm, d), lambda t, j, ids_ref: (jnp.maximum(t - nc, 0), 0)),
                      pl.BlockSpec((tm, d), lambda t, j, ids_ref: (jnp.minimum(t, nc - 1), 0)),
                      pl.BlockSpec((1, d), lambda t, j, ids_ref: (0, 0)), pl.BlockSpec((4, d), lambda t, j, ids_ref: (0, 0)),
                      pl.BlockSpec((d, tn), lambda t, j, ids_ref: (0, j))] + [HBM_SPEC] * n_sin,
            out_specs=(pl.BlockSpec((tm, tn), lambda t, j, ids_ref: (t, ids_ref[0] * nb + j)),
                       pl.BlockSpec((tm, d), lambda t, j, ids_ref: (t, 0)), *([HBM_SPEC] * n_sout)),
            scratch_shapes=side.sem_shapes),
        input_output_aliases={6 + i: 2 + o for i, o in side.aliases.items()},
        compiler_params=_cparams("arbitrary", "arbitrary"),
    )(ids, x, ctx, g_pre, ss, w_own, *side.ins)


def _dgrad_in_parts(dproj, w_parts, ids, *, cw, tm, tn, name, side=None):
    s_len = dproj.shape[0]
    d, ws = w_parts[0][0].shape[0], cw
    n_p = len(w_parts)
    n_sin = len(side.ins) if side else 0
    n_sout = len(side.out_shapes) if side else 0
    grid = (s_len // tm, d // tn)

    def body(ids_ref, *refs):
        a_refs, w_refs = refs[:n_p], refs[n_p:2 * n_p]
        s_ins = refs[2 * n_p:2 * n_p + n_sin]
        o_ref = refs[2 * n_p + n_sin]
        s_outs = refs[2 * n_p + n_sin + 1:2 * n_p + n_sin + 1 + n_sout]
        sems = refs[2 * n_p + n_sin + 1 + n_sout:]
        if side:
            first, _, last = _side_at(grid)
            pl.when(first)(lambda: side.start(s_ins, s_outs, sems))
        tot = _dot(a_refs[0][...], w_refs[0][...], 1, 1)
        for s in range(1, n_p):
            tot = tot + _dot(a_refs[s][...], w_refs[s][...], 1, 1)
        o_ref[...] = tot.astype(o_ref.dtype)
        if side:
            pl.when(last)(lambda: side.finish(s_ins, s_outs, sems))

    in_specs = [pl.BlockSpec((tm, ws), lambda i, j, ids_ref, s=s: (i, ids_ref[s])) for s in range(n_p)]
    in_specs += [pl.BlockSpec((tn, ws), lambda i, j, ids_ref, wc=wc: (j, wc)) for _, wc in w_parts]
    in_specs += [HBM_SPEC] * n_sin
    aliases = {1 + 2 * n_p + i: 1 + o for i, o in side.aliases.items()} if side else {}
    return pl.pallas_call(
        body, name=name, out_shape=(jax.ShapeDtypeStruct((s_len, d), BF), *(side.out_shapes if side else [])),
        grid_spec=pltpu.PrefetchScalarGridSpec(
            num_scalar_prefetch=1, grid=grid, in_specs=in_specs,
            out_specs=(pl.BlockSpec((tm, tn), lambda i, j, ids_ref: (i, j)), *([HBM_SPEC] * n_sout)),
            scratch_shapes=side.sem_shapes if side else []),
        input_output_aliases=aliases, compiler_params=_cparams("arbitrary", "arbitrary"),
    )(ids, *([dproj] * n_p), *[w for w, _ in w_parts], *(side.ins if side else []))


def _swap32(v):
    lane = lax.broadcasted_iota(jnp.int32, v.shape, 1)
    return jnp.where((lane % 64) < 32, pltpu.roll(v, 96, 1), pltpu.roll(v, 32, 1))


def _qk_prep(proj, gain, ctab, stab, *, row0, col0, width, nrows, tm, name, out_scale=1.0):
    cw = min(512, width)
    rb0, cb0 = row0 // tm, col0 // cw
    assert row0 % tm == 0 and col0 % cw == 0 and width % cw == 0 and nrows % tm == 0

    def body(p_ref, g_ref, c_ref, s_ref, o_ref):
        row = lax.broadcasted_iota(jnp.int32, (HEAD_DIM, HEAD_DIM), 0)
        col = lax.broadcasted_iota(jnp.int32, (HEAD_DIM, HEAD_DIM), 1)
        perm = (row == jnp.where((col % 64) < 32, col + 32, col - 32)).astype(BF)
        gc = g_ref[...] * c_ref[...] * out_scale
        gs = _swap32(jnp.broadcast_to(g_ref[...], c_ref.shape)) * s_ref[...] * out_scale
        for hd in range(cw // HEAD_DIM):
            cols = slice(hd * HEAD_DIM, (hd + 1) * HEAD_DIM)
            vb = p_ref[:, cols]
            v = vb.astype(F32)
            rstd = lax.rsqrt(jnp.mean(v * v, axis=-1, keepdims=True) + EPS)
            o_ref[:, cols] = (rstd * (v * gc + _dot(vb, perm, 1, 0) * gs)).astype(BF)

    return pl.pallas_call(
        body, name=name, grid=(nrows // tm, width // cw), out_shape=jax.ShapeDtypeStruct((nrows, width), BF),
        in_specs=[pl.BlockSpec((tm, cw), lambda i, j: (i + rb0, j + cb0)), pl.BlockSpec((1, HEAD_DIM), lambda i, j: (0, 0)),
                  pl.BlockSpec((tm, HEAD_DIM), lambda i, j: (i, 0)), pl.BlockSpec((tm, HEAD_DIM), lambda i, j: (i, 0))],
        out_specs=pl.BlockSpec((tm, cw), lambda i, j: (i, j)), compiler_params=_cparams("parallel", "parallel"),
    )(proj, gain, ctab, stab)


def _zero_rows(rows, shape, *, name):
    def body(o_ref):
        o_ref[...] = jnp.zeros_like(o_ref)

    return pl.pallas_call(body, name=name, grid=(1,), out_shape=jax.ShapeDtypeStruct(shape, BF),
                          out_specs=pl.BlockSpec((rows, shape[1]), lambda i: (0, 0)))()


def _qk_bwd(dy, proj, gain, ctab, stab, into, *, row0, col0, tm, name, transposed=False):
    if transposed:
        tq = dy.shape[-1]
        nrows, width = dy.shape[1] * tq, dy.shape[0] * GQA_GROUP * HEAD_DIM
        per = tq // tm
        dy_spec = pl.BlockSpec((1, 1, GQA_GROUP, HEAD_DIM, tm), lambda i, j: (j, i // per, 0, 0, i % per))
    else:
        nrows, width = dy.shape
    cw = min(512, width)
    rb0, cb0 = row0 // tm, col0 // cw
    if transposed:
        assert cw == GQA_GROUP * HEAD_DIM and tq % tm == 0
    else:
        dy_spec = pl.BlockSpec((tm, cw), lambda i, j: (i, j))

    def body(d_ref, p_ref, g_ref, c_ref, s_ref, into_ref, o_ref, dg_ref):
        @pl.when(jnp.logical_and(pl.program_id(0) == 0, pl.program_id(1) == 0))
        def _():
            dg_ref[...] = jnp.zeros_like(dg_ref)

        dg = jnp.zeros((1, HEAD_DIM), F32)
        for hd in range(cw // HEAD_DIM):
            cols = slice(hd * HEAD_DIM, (hd + 1) * HEAD_DIM)
            v = p_ref[:, cols].astype(F32)
            rstd = lax.rsqrt(jnp.mean(v * v, axis=-1, keepdims=True) + EPS)
            nv = v * rstd
            d = d_ref[0, 0, hd].T if transposed else d_ref[:, cols]
            dyu = d * c_ref[...] + _swap32(d * s_ref[...])
            dg = dg + jnp.sum(dyu * nv, axis=0, keepdims=True)
            dn = dyu * g_ref[...]
            o_ref[:, cols] = (rstd * (dn - nv * jnp.mean(dn * nv, axis=-1, keepdims=True))).astype(BF)
        dg_ref[0:1, :] += dg

    placed = pl.BlockSpec((tm, cw), lambda i, j: (i + rb0, j + cb0))
    return pl.pallas_call(
        body, name=name, grid=(nrows // tm, width // cw),
        out_shape=(jax.ShapeDtypeStruct(into.shape, BF), jax.ShapeDtypeStruct((8, HEAD_DIM), F32)),
        in_specs=[dy_spec, placed, pl.BlockSpec((1, HEAD_DIM), lambda i, j: (0, 0)),
                  pl.BlockSpec((tm, HEAD_DIM), lambda i, j: (i, 0)), pl.BlockSpec((tm, HEAD_DIM), lambda i, j: (i, 0)), HBM_SPEC],
        out_specs=(placed, pl.BlockSpec((8, HEAD_DIM), lambda i, j: (0, 0))),
        input_output_aliases={5: 0}, compiler_params=_cparams("arbitrary", "arbitrary"),
    )(dy, proj, gain, ctab, stab, into)


def _flash_fwd(qr, k_all, vt_ctx, vt_lat, *, tq, side=None):
    n, aw = qr.shape
    s_len, kvw = k_all.shape
    l = vt_ctx.shape[-1]
    n_j, tk = vt_lat.shape[1], vt_lat.shape[-1]
    kvh = kvw // HEAD_DIM
    n_i = n // tq
    gw = GQA_GROUP * HEAD_DIM
    n_sin = len(side.ins) if side else 0
    n_sout = len(side.out_shapes) if side else 0

    def body(*refs):
        q_ref, k_ref, vc_ref, vl_ref = refs[:4]
        s_ins = refs[4:4 + n_sin]
        o_ref, lse_ref = refs[4 + n_sin:6 + n_sin]
        s_outs = refs[6 + n_sin:6 + n_sin + n_sout]
        acc_ref, m_ref, l_ref = refs[6 + n_sin + n_sout:9 + n_sin + n_sout]
        sems = refs[9 + n_sin + n_sout:]
        if side:
            first, middle, last = _side_at((kvh, n_i))
            pl.when(first)(lambda: side.start(s_ins, s_outs, sems))
            if side.mid is not None:
                pl.when(middle)(lambda: side.mid(s_ins, s_outs, sems))
        acc_ref[...] = jnp.zeros_like(acc_ref)
        l_ref[...] = jnp.zeros_like(l_ref)
        m_ref[...] = jnp.full(m_ref.shape, -1e30, F32)

        def tile(kj, vtj):
            for g in range(GQA_GROUP):
                st = _dot(kj, q_ref[:, g * HEAD_DIM:(g + 1) * HEAD_DIM], 1, 1)
                m_old = m_ref[g]
                m_new = jnp.maximum(m_old, jnp.max(st, axis=0, keepdims=True))
                alpha = jnp.exp2(m_old - m_new)
                pt = jnp.exp2(st - m_new)
                l_ref[g] = alpha * l_ref[g] + jnp.sum(pt, axis=0, keepdims=True)
                m_ref[g] = m_new
                acc_ref[g] = acc_ref[g] * alpha + _dot(vtj, pt.astype(BF), 1, 0)

        tile(k_ref[0:l, :], vc_ref[0, 0])

        def step(j, carry):
            tile(k_ref[pl.ds(pl.multiple_of(l + j * tk, min(l, tk)), tk), :], vl_ref[0, j])
            return carry

        lax.fori_loop(0, n_j, step, 0)
        for g in range(GQA_GROUP):
            o_ref[:, g * HEAD_DIM:(g + 1) * HEAD_DIM] = (acc_ref[g] / l_ref[g]).T.astype(BF)
            lse_ref[0, g, 0] = m_ref[g] + jnp.log(l_ref[g]) * LOG2E
        if side:
            pl.when(last)(lambda: side.finish(s_ins, s_outs, sems))

    return pl.pallas_call(
        body, name="flash_fwd", grid=(kvh, n_i),
        out_shape=(jax.ShapeDtypeStruct((n, aw), BF), jax.ShapeDtypeStruct((kvh, GQA_GROUP, n_i, 1, tq), F32),
                   *(side.out_shapes if side else [])),
        in_specs=[pl.BlockSpec((tq, gw), lambda h, i: (i, h)), pl.BlockSpec((s_len, HEAD_DIM), lambda h, i: (0, h)),
                  pl.BlockSpec((1, 1, HEAD_DIM, l), lambda h, i: (h, 0, 0, 0)),
                  pl.BlockSpec((1, n_j, HEAD_DIM, tk), lambda h, i: (h, 0, 0, 0))] + [HBM_SPEC] * n_sin,
        out_specs=(pl.BlockSpec((tq, gw), lambda h, i: (i, h)),
                   pl.BlockSpec((1, GQA_GROUP, 1, 1, tq), lambda h, i: (h, 0, i, 0, 0)), *([HBM_SPEC] * n_sout)),
        input_output_aliases={4 + i: 2 + o for i, o in side.aliases.items()} if side else {},
        scratch_shapes=[pltpu.VMEM((GQA_GROUP, HEAD_DIM, tq), F32), pltpu.VMEM((GQA_GROUP, 1, tq), F32),
                        pltpu.VMEM((GQA_GROUP, 1, tq), F32)] + (side.sem_shapes if side else []),
        compiler_params=_cparams("arbitrary", "arbitrary") if side else _cparams("parallel", "parallel"),
    )(qr, k_all, vt_ctx, vt_lat, *(side.ins if side else []))


def _attn_delta(o, do, *, tq):
    n, aw = o.shape
    kvh = aw // (GQA_GROUP * HEAD_DIM)
    gw = GQA_GROUP * HEAD_DIM

    def body(o_ref, do_ref, d_ref):
        for g in range(GQA_GROUP):
            cols = slice(g * HEAD_DIM, (g + 1) * HEAD_DIM)
            prod = o_ref[:, cols].astype(F32) * do_ref[:, cols].astype(F32)
            d_ref[0, g, 0] = jnp.sum(prod.T, axis=0, keepdims=True)

    return pl.pallas_call(
        body, name="attn_delta", grid=(kvh, n // tq),
        out_shape=jax.ShapeDtypeStruct((kvh, GQA_GROUP, n // tq, 1, tq), F32),
        in_specs=[pl.BlockSpec((tq, gw), lambda h, i: (i, h)), pl.BlockSpec((tq, gw), lambda h, i: (i, h))],
        out_specs=pl.BlockSpec((1, GQA_GROUP, 1, 1, tq), lambda h, i: (h, 0, i, 0, 0)),
        compiler_params=_cparams("parallel", "parallel"),
    )(o, do)


def _flash_bwd(qr, do, k_all, proj, lse, delta, into, *, v_col0, tq, tk, scale, side=None):
    n, aw = qr.shape
    s_len, kvw = k_all.shape
    kvh = kvw // HEAD_DIM
    n_i, n_j = n // tq, s_len // tk
    gw = GQA_GROUP * HEAD_DIM
    vb0 = v_col0 // HEAD_DIM
    n_sin = len(side.ins) if side else 0
    n_sout = len(side.out_shapes) if side else 0

    def body(*refs):
        q_ref, do_ref, k_ref, v_ref, lse_ref, dl_ref = refs[:6]
        s_ins = refs[7:7 + n_sin]
        dq_ref, dk_ref, dvo_ref = refs[7 + n_sin:10 + n_sin]
        s_outs = refs[10 + n_sin:10 + n_sin + n_sout]
        dv_ref = refs[10 + n_sin + n_sout]
        sems = refs[11 + n_sin + n_sout:]
        jj = pl.program_id(1)
        if side:
            first, _, last = _side_at((kvh, n_j))
            pl.when(first)(lambda: side.start(s_ins, s_outs, sems))

        @pl.when(jj == 0)
        def _():
            dq_ref[...] = jnp.zeros_like(dq_ref)

        kj = k_ref[...]
        vj = v_ref[...]
        kjt = kj.astype(F32).T.astype(BF)
        dk_ref[...] = jnp.zeros_like(dk_ref)
        dv_ref[...] = jnp.zeros_like(dv_ref)

        def step(i, carry):
            rows = pl.ds(pl.multiple_of(i * tq, tq), tq)
            dv_part = dk_part = None
            for g in range(GQA_GROUP):
                cols = slice(g * HEAD_DIM, (g + 1) * HEAD_DIM)
                qg = q_ref[rows, cols]
                dog = do_ref[rows, cols]
                pt = jnp.exp2(_dot(kj, qg, 1, 1) - lse_ref[0, g, i])
                dst = (pt * (_dot(vj, dog, 1, 1) - dl_ref[0, g, i])).astype(BF)
                dv_g = _dot(pt.astype(BF), dog, 1, 0)
                dk_g = _dot(dst, qg, 1, 0)
                dv_part = dv_g if dv_part is None else dv_part + dv_g
                dk_part = dk_g if dk_part is None else dk_part + dk_g
                dq_ref[0, i, g] += _dot(kjt, dst, 1, 0)
            dv_ref[...] += dv_part
            dk_ref[...] += dk_part
            return carry

        lax.fori_loop(0, n_i, step, 0)
        dk_ref[...] = dk_ref[...] * (1.0 / LOG2E)
        dvo_ref[...] = dv_ref[...].astype(BF)

        @pl.when(jj == n_j - 1)
        def _():
            dq_ref[...] = dq_ref[...] * scale

        if side:
            pl.when(last)(lambda: side.finish(s_ins, s_outs, sems))

    stat_spec = pl.BlockSpec((1, GQA_GROUP, n_i, 1, tq), lambda h, j: (h, 0, 0, 0, 0))
    kv_spec = pl.BlockSpec((tk, HEAD_DIM), lambda h, j: (j, h))
    v_spec = pl.BlockSpec((tk, HEAD_DIM), lambda h, j: (j, vb0 + h))
    q_spec = pl.BlockSpec((n, gw), lambda h, j: (0, h))
    dq_spec = pl.BlockSpec((1, n_i, GQA_GROUP, HEAD_DIM, tq), lambda h, j: (h, 0, 0, 0, 0))
    aliases = {6: 2}
    if side:
        aliases.update({7 + i: 3 + o for i, o in side.aliases.items()})
    return pl.pallas_call(
        body, name="flash_bwd", grid=(kvh, n_j),
        out_shape=(jax.ShapeDtypeStruct((kvh, n_i, GQA_GROUP, HEAD_DIM, tq), F32), jax.ShapeDtypeStruct((s_len, kvw), F32),
                   jax.ShapeDtypeStruct(into.shape, BF), *(side.out_shapes if side else [])),
        in_specs=[q_spec, q_spec, kv_spec, v_spec, stat_spec, stat_spec, HBM_SPEC] + [HBM_SPEC] * n_sin,
        out_specs=(dq_spec, kv_spec, v_spec, *([HBM_SPEC] * n_sout)),
        input_output_aliases=aliases,
        scratch_shapes=[pltpu.VMEM((tk, HEAD_DIM), F32)] + (side.sem_shapes if side else []),
        compiler_params=_cparams("arbitrary", "arbitrary"),
    )(qr, do, k_all, proj, lse, delta, into, *(side.ins if side else []))


def _pool_window(r, gi, l, n, tm):
    win = tm + 2 * POOL_HALO
    start = jnp.clip(l + r * tm - POOL_HALO, l, l + n - win)
    half = lax.shift_left(jnp.int32(1), gi)
    tok = r * tm + lax.broadcasted_iota(jnp.int32, (tm, win), 0)
    src = (start - l) + lax.broadcasted_iota(jnp.int32, (tm, win), 1)
    tok1 = r * tm + lax.broadcasted_iota(jnp.int32, (tm, 1), 0)
    cnt = (jnp.minimum(tok1 + half, n) - jnp.maximum(tok1 - half, 0)).astype(F32)
    return start, win, tok, src, half, cnt


def _pool_fwd(proj, pool_w, *, l, n, col0, tm):
    s_len = proj.shape[0]
    pg = pool_w.shape[-1]
    cb0 = col0 // pg
    assert col0 % pg == 0 and l % POOL_HALO == 0 and n >= tm + 2 * POOL_HALO

    def body(u_ref, w_ref, d_ref, po_ref):
        gi, r = pl.program_id(0), pl.program_id(1)
        start, win, tok, src, half, cnt = _pool_window(r, gi, l, n, tm)
        band = jnp.logical_and(src >= tok - half, src < tok + half).astype(BF)
        uw = u_ref[pl.ds(pl.multiple_of(start, POOL_HALO), win), :]
        ut = u_ref[pl.ds(pl.multiple_of(l + r * tm, POOL_HALO), tm), :].astype(F32)
        dv = (_dot(band, uw, 1, 0) / cnt - ut).astype(BF)
        d_ref[...] = dv
        po_ref[...] = _dot(dv, w_ref[0], 1, 0).astype(BF)

    return pl.pallas_call(
        body, name="pool_fwd", grid=(N_POOL_GROUPS, n // tm),
        out_shape=(jax.ShapeDtypeStruct((n, N_POOL_GROUPS * pg), BF), jax.ShapeDtypeStruct((n, N_POOL_GROUPS * pg), BF)),
        in_specs=[pl.BlockSpec((s_len, pg), lambda g, r: (0, cb0 + g)), pl.BlockSpec((1, pg, pg), lambda g, r: (g, 0, 0))],
        out_specs=(pl.BlockSpec((tm, pg), lambda g, r: (r, g)), pl.BlockSpec((tm, pg), lambda g, r: (r, g))),
        compiler_params=_cparams("parallel", "parallel"),
    )(proj, pool_w)


def _pool_bwd_map(dval, pool_w, *, n, col0, tm):
    pg = pool_w.shape[-1]
    cb0 = col0 // pg

    def body(d_ref, w_ref, o_ref):
        gi, r = pl.program_id(0), pl.program_id(1)
        half = lax.shift_left(jnp.int32(1), gi)
        tok1 = r * tm + lax.broadcasted_iota(jnp.int32, (tm, 1), 0)
        cnt = (jnp.minimum(tok1 + half, n) - jnp.maximum(tok1 - half, 0)).astype(F32)
        o_ref[...] = (_dot(d_ref[...], w_ref[0], 1, 1) / cnt).astype(BF)

    return pl.pallas_call(
        body, name="pool_bwd_map", grid=(N_POOL_GROUPS, n // tm),
        out_shape=jax.ShapeDtypeStruct((n, N_POOL_GROUPS * pg), BF),
        in_specs=[pl.BlockSpec((tm, pg), lambda g, r: (r, cb0 + g)), pl.BlockSpec((1, pg, pg), lambda g, r: (g, 0, 0))],
        out_specs=pl.BlockSpec((tm, pg), lambda g, r: (r, g)), compiler_params=_cparams("parallel", "parallel"),
    )(dval, pool_w)


def _pool_bwd_window(dds, into, *, n, tm, row0, col0):
    pg = dds.shape[1] // N_POOL_GROUPS
    rb0, cb0 = row0 // tm, col0 // pg
    assert row0 % tm == 0 and col0 % pg == 0

    def body(d_ref, into_ref, o_ref):
        gi, r = pl.program_id(0), pl.program_id(1)
        start, win, tok, src, half, cnt = _pool_window(r, gi, 0, n, tm)
        band = jnp.logical_and(tok >= src - half, tok < src + half).astype(BF)
        dw = d_ref[pl.ds(pl.multiple_of(start, POOL_HALO), win), :]
        dt = d_ref[pl.ds(pl.multiple_of(r * tm, POOL_HALO), tm), :].astype(F32)
        o_ref[...] = (_dot(band, dw, 1, 0) - dt * cnt).astype(BF)

    return pl.pallas_call(
        body, name="pool_bwd_window", grid=(N_POOL_GROUPS, n // tm), out_shape=jax.ShapeDtypeStruct(into.shape, BF),
        in_specs=[pl.BlockSpec((n, pg), lambda g, r: (0, g)), HBM_SPEC],
        out_specs=pl.BlockSpec((tm, pg), lambda g, r: (r + rb0, g + cb0)),
        input_output_aliases={1: 0}, compiler_params=_cparams("parallel", "parallel"),
    )(dds, into)


def _pool_wgrad(dmat, dval, *, col0, tk):
    n, pw = dmat.shape
    pg = pw // N_POOL_GROUPS
    cb0 = col0 // pg
    nk = n // tk

    def body(a_ref, b_ref, o_ref, acc):
        k = pl.program_id(1)

        @pl.when(k == 0)
        def _():
            acc[...] = jnp.zeros_like(acc)

        acc[...] += _dot(a_ref[...], b_ref[...], 0, 0)

        @pl.when(k == nk - 1)
        def _():
            o_ref[0] = acc[...].astype(BF)

    return pl.pallas_call(
        body, name="pool_wgrad", grid=(N_POOL_GROUPS, nk), out_shape=jax.ShapeDtypeStruct((N_POOL_GROUPS, pg, pg), BF),
        in_specs=[pl.BlockSpec((tk, pg), lambda g, k: (k, g)), pl.BlockSpec((tk, pg), lambda g, k: (k, cb0 + g))],
        out_specs=pl.BlockSpec((1, pg, pg), lambda g, k: (g, 0, 0)), scratch_shapes=[pltpu.VMEM((pg, pg), F32)],
        compiler_params=_cparams("parallel", "arbitrary"),
    )(dmat, dval)


def _merge(attn_o, po_raw, proj, pool_scale, *, l, ga0, gp0, tm):
    n, aw = attn_o.shape
    bw = aw // 2
    rb0 = l // tm
    ga_b, gp_b = ga0 // bw, gp0 // bw
    assert ga0 % bw == 0 and gp0 % bw == 0 and l % tm == 0

    def body(a_ref, p_ref, g_ref, ps_ref, y_ref):
        cb = pl.program_id(1)
        sg, _ = _silu_and_grad(g_ref[...].astype(F32))

        @pl.when(cb < 2)
        def _():
            y_ref[...] = (a_ref[...].astype(F32) * sg).astype(BF)

        @pl.when(cb >= 2)
        def _():
            y_ref[...] = (p_ref[...].astype(F32) * ps_ref[...] * sg).astype(BF)

    return pl.pallas_call(
        body, name="merge", grid=(n // tm, 4), out_shape=jax.ShapeDtypeStruct((n, 2 * aw), BF),
        in_specs=[pl.BlockSpec((tm, bw), lambda i, cb: (i, jnp.minimum(cb, 1))),
                  pl.BlockSpec((tm, bw), lambda i, cb: (i, jnp.maximum(cb - 2, 0))),
                  pl.BlockSpec((tm, bw), lambda i, cb: (i + rb0, jnp.where(cb < 2, ga_b + cb, gp_b + cb - 2))),
                  pl.BlockSpec((1, bw), lambda i, cb: (0, jnp.maximum(cb - 2, 0)))],
        out_specs=pl.BlockSpec((tm, bw), lambda i, cb: (i, cb)), compiler_params=_cparams("parallel", "arbitrary"),
    )(attn_o, po_raw, proj, pool_scale)


def _merge_bwd(dy, attn_o, po_raw, proj, pool_scale, into, *, l, ga0, gp0, tm):
    n, aw = attn_o.shape
    bw = aw // 2
    rb0 = l // tm
    ga_b, gp_b = ga0 // bw, gp0 // bw

    def body(dy_ref, a_ref, p_ref, g_ref, ps_ref, into_ref, dv_ref, dg_ref, dps_ref):
        cb, i = pl.program_id(0), pl.program_id(1)
        sg, sgrad = _silu_and_grad(g_ref[...].astype(F32))
        dyv = dy_ref[...].astype(F32)

        @pl.when(cb < 2)
        def _():
            dv_ref[...] = (dyv * sg).astype(BF)
            dg_ref[...] = (dyv * a_ref[...].astype(F32) * sgrad).astype(BF)

        @pl.when(cb >= 2)
        def _():
            @pl.when(i == 0)
            def _():
                dps_ref[...] = jnp.zeros_like(dps_ref)

            pr = p_ref[...].astype(F32)
            dpo = dyv * sg
            dv_ref[...] = (dpo * ps_ref[...]).astype(BF)
            dg_ref[...] = (dyv * (pr * ps_ref[...]) * sgrad).astype(BF)
            dps_ref[0:1, :] += jnp.sum(dpo * pr, axis=0, keepdims=True)

    blk = pl.BlockSpec((tm, bw), lambda cb, i: (i, cb))
    gate_blk = pl.BlockSpec((tm, bw), lambda cb, i: (i + rb0, jnp.where(cb < 2, ga_b + cb, gp_b + cb - 2)))
    return pl.pallas_call(
        body, name="merge_bwd", grid=(4, n // tm),
        out_shape=(jax.ShapeDtypeStruct((n, 2 * aw), BF), jax.ShapeDtypeStruct(into.shape, BF),
                   jax.ShapeDtypeStruct((8, aw), F32)),
        in_specs=[blk, pl.BlockSpec((tm, bw), lambda cb, i: (i, jnp.minimum(cb, 1))),
                  pl.BlockSpec((tm, bw), lambda cb, i: (i, jnp.maximum(cb - 2, 0))), gate_blk,
                  pl.BlockSpec((1, bw), lambda cb, i: (0, jnp.maximum(cb - 2, 0))), HBM_SPEC],
        out_specs=(blk, gate_blk, pl.BlockSpec((8, bw), lambda cb, i: (0, jnp.maximum(cb - 2, 0)))),
        input_output_aliases={5: 1}, compiler_params=_cparams("arbitrary", "arbitrary"),
    )(dy, attn_o, po_raw, proj, pool_scale, into)


def _post(x, out, target, gate, g_post, *, tm):
    n, d = x.shape

    def body(x_ref, o_ref, t_ref, gate_ref, gp_ref, dxn_ref, dout_ref, st_ref):
        @pl.when(pl.program_id(0) == 0)
        def _():
            st_ref[...] = jnp.zeros_like(st_ref)

        ov = o_ref[...]
        rstd = lax.rsqrt(jnp.mean(ov * ov, axis=-1, keepdims=True) + EPS)
        on = ov * rstd
        rn = on * gp_ref[...]
        err = (x_ref[...] + gate_ref[...] * rn) - t_ref[...]
        dxn = err / d
        dxn_ref[...] = dxn
        drn = dxn * gate_ref[...]
        don = drn * gp_ref[...]
        dout_ref[...] = (rstd * (don - on * jnp.mean(don * on, axis=-1, keepdims=True))).astype(BF)
        st_ref[0:1, :] += jnp.sum(dxn * rn, axis=0, keepdims=True)
        st_ref[1:2, :] += jnp.sum(drn * on, axis=0, keepdims=True)
        st_ref[2:3, :] += jnp.sum(err * err, axis=0, keepdims=True)

    row = pl.BlockSpec((tm, d), lambda i: (i, 0))
    vec = pl.BlockSpec((1, d), lambda i: (0, 0))
    return pl.pallas_call(
        body, name="post", grid=(n // tm,),
        out_shape=(jax.ShapeDtypeStruct((n, d), F32), jax.ShapeDtypeStruct((n, d), BF), jax.ShapeDtypeStruct((8, d), F32)),
        in_specs=[row, row, row, vec, vec], out_specs=(row, row, pl.BlockSpec((8, d), lambda i: (0, 0))),
        compiler_params=_cparams("arbitrary"),
    )(x, out, target, gate, g_post)


def _prenorm_bwd(x, ctx, dh, dxn, g_pre, ss, *, tm):
    n, d = x.shape
    l = ctx.shape[0]
    nc = l // tm

    def body(x_ref, c_ref, dh_ref, dxn_ref, g_ref, ss_ref, gx_ref, st_ref):
        t = pl.program_id(0)

        @pl.when(t == 0)
        def _():
            st_ref[...] = jnp.zeros_like(st_ref)

        def go(src, scale, row):
            v = src[...]
            dhv = dh_ref[...].astype(F32)
            rstd = lax.rsqrt(jnp.mean(v * v, axis=-1, keepdims=True) + EPS)
            xn = v * rstd
            st_ref[row:row + 1, :] += jnp.sum(dhv, axis=0, keepdims=True)
            st_ref[row + 1:row + 2, :] += jnp.sum(dhv * (xn * g_ref[...]), axis=0, keepdims=True)
            dg = dhv * (1.0 + scale)
            st_ref[4:5, :] += jnp.sum(dg * xn, axis=0, keepdims=True)
            dxn_ = dg * g_ref[...]
            return rstd * (dxn_ - xn * jnp.mean(dxn_ * xn, axis=-1, keepdims=True))

        @pl.when(t < nc)
        def _():
            go(c_ref, ss_ref[2:3, :], 2)

        @pl.when(t >= nc)
        def _():
            gx_ref[...] = dxn_ref[...] + go(x_ref, ss_ref[0:1, :], 0)

    lat = pl.BlockSpec((tm, d), lambda t: (jnp.maximum(t - nc, 0), 0))
    return pl.pallas_call(
        body, name="prenorm_bwd", grid=((l + n) // tm,),
        out_shape=(jax.ShapeDtypeStruct((n, d), F32), jax.ShapeDtypeStruct((8, d), F32)),
        in_specs=[lat, pl.BlockSpec((tm, d), lambda t: (jnp.minimum(t, nc - 1), 0)),
                  pl.BlockSpec((tm, d), lambda t: (t, 0)), lat,
                  pl.BlockSpec((1, d), lambda t: (0, 0)), pl.BlockSpec((4, d), lambda t: (0, 0))],
        out_specs=(lat, pl.BlockSpec((8, d), lambda t: (0, 0))), compiler_params=_cparams("arbitrary"),
    )(x, ctx, dh, dxn, g_pre, ss)


def _ada_fwd(craw, w_ada, b_loc, *, tn):
    d, wn = w_ada.shape

    def body(c_ref, w_ref, b_ref, o_ref):
        act, _ = _silu_and_grad(c_ref[...])
        o_ref[...] = _dot(act.astype(BF), w_ref[...].astype(BF), 1, 0) + b_ref[...]

    return pl.pallas_call(
        body, name="ada_fwd", grid=(wn // tn,), out_shape=jax.ShapeDtypeStruct((16, wn), F32),
        in_specs=[pl.BlockSpec((16, d), lambda j: (0, 0)), pl.BlockSpec((d, tn), lambda j: (0, j)),
                  pl.BlockSpec((1, tn), lambda j: (0, j))],
        out_specs=pl.BlockSpec((16, tn), lambda j: (0, j)), compiler_params=_cparams("parallel"),
    )(craw, w_ada, b_loc)


def _ada_bwd(craw_t, dm, w, m, v, *, tm):
    d, wn = w.shape

    def body(ct_ref, dm_ref, w_ref, m_ref, v_ref, g_ref, dl_ref, nm_ref, nv_ref, da_ref):
        act, _ = _silu_and_grad(ct_ref[...])
        dmb = dm_ref[...].astype(BF)
        wv = w_ref[...]
        gv = _dot(act.astype(BF), dmb, 1, 0)
        da_ref[...] = _dot(dmb, wv.astype(BF), 1, 1)
        nm = ADAM_B1 * m_ref[...] + (1.0 - ADAM_B1) * gv
        nv = ADAM_B2 * v_ref[...] + (1.0 - ADAM_B2) * (gv * gv)
        m_hat = nm / (1.0 - ADAM_B1 ** ADAM_STEP)
        v_hat = nv / (1.0 - ADAM_B2 ** ADAM_STEP)
        g_ref[...] = gv
        dl_ref[...] = -ADAM_LR * (m_hat / (jnp.sqrt(v_hat) + ADAM_EPS) + ADAM_WD * wv)
        nm_ref[...] = nm
        nv_ref[...] = nv

    row = pl.BlockSpec((tm, wn), lambda i: (i, 0))
    return pl.pallas_call(
        body, name="ada_bwd", grid=(d // tm,),
        out_shape=tuple([jax.ShapeDtypeStruct((d, wn), F32)] * 4) + (jax.ShapeDtypeStruct((16, d), F32),),
        in_specs=[pl.BlockSpec((tm, 16), lambda i: (i, 0)), pl.BlockSpec((16, wn), lambda i: (0, 0)), row, row, row],
        out_specs=(row, row, row, row, pl.BlockSpec((16, tm), lambda i: (0, i))),
        compiler_params=_cparams("parallel"),
    )(craw_t, dm, w, m, v)


def _reduce_small(gath, d3):
    t = gath.shape[-1]

    def body(g_ref, dm_ref, s_ref):
        tot = g_ref[0]
        for b in range(1, 8):
            tot = tot + g_ref[b]
        s_ref[...] = tot
        dm_ref[...] = jnp.zeros_like(dm_ref)
        for b in range(8):
            dm_ref[b:b + 1, :] = g_ref[b][:, 0:d3]
        dm_ref[8:9, :] = tot[:, d3:2 * d3]

    return pl.pallas_call(
        body, name="reduce_small", out_shape=(jax.ShapeDtypeStruct((16, d3), F32), jax.ShapeDtypeStruct((1, t), F32)),
        in_specs=[VMEM_SPEC], out_specs=(VMEM_SPEC, VMEM_SPEC),
    )(gath)


def _cctx_grad(parts, c_ctx):
    def body(p_ref, c_ref, o_ref):
        tot = (p_ref[0] + p_ref[1]) + (p_ref[2] + p_ref[3])
        _, sgrad = _silu_and_grad(c_ref[...])
        o_ref[...] = tot * sgrad

    return pl.pallas_call(
        body, name="cctx_grad", out_shape=jax.ShapeDtypeStruct(c_ctx.shape, F32),
        in_specs=[VMEM_SPEC, VMEM_SPEC], out_specs=VMEM_SPEC,
    )(parts, c_ctx)


def _rope_tables(n, l):
    rows = n // GRID_W
    row = jnp.repeat(jnp.arange(rows, dtype=F32), GRID_W)
    col = jnp.tile(jnp.arange(GRID_W, dtype=F32), rows)
    inv = ROPE_THETA ** (-jnp.arange(ROPE_PAIRS, dtype=F32) / ROPE_PAIRS)
    ang = jnp.concatenate([row[:, None] * inv, col[:, None] * inv], axis=-1)
    cos, sin = jnp.cos(ang), jnp.sin(ang)
    cr, cc, sr, sc = cos[:, :ROPE_PAIRS], cos[:, ROPE_PAIRS:], sin[:, :ROPE_PAIRS], sin[:, ROPE_PAIRS:]
    ctab = jnp.concatenate([cr, cr, cc, cc], axis=-1)
    stab = jnp.concatenate([-sr, sr, -sc, sc], axis=-1)
    ctab = jnp.concatenate([jnp.ones((l, HEAD_DIM), F32), ctab], axis=0)
    stab = jnp.concatenate([jnp.zeros((l, HEAD_DIM), F32), stab], axis=0)
    return ctab, stab


def kernel(x, c, ctx, c_ctx, w_ada, b_ada, norm_pre, norm_post, w_in, q_norm, k_norm, pool_w, pool_scale, w_out, loss_target, m_c_ctx, m_w_ada, m_b_ada, m_norm_pre, m_norm_post, m_w_in, m_q_norm, m_k_norm, m_pool_w, m_pool_scale, m_w_out, v_c_ctx, v_w_ada, v_b_ada, v_norm_pre, v_norm_post, v_w_in, v_q_norm, v_k_norm, v_pool_w, v_pool_scale, v_w_out):
    x2, ctx2, tgt = x[0], ctx[0], loss_target[0]
    n, d = x2.shape
    l = ctx2.shape[0]
    s_len = l + n
    aw = d // 2
    pw = d - aw
    n_heads = aw // HEAD_DIM
    kvw = (n_heads // GQA_GROUP) * HEAD_DIM
    pg = pw // N_POOL_GROUPS
    k0, v0, ga0 = aw, aw + kvw, aw + 2 * kvw
    up0, gp0 = ga0 + aw, ga0 + aw + pw
    in_w = gp0 + pw
    d3 = 3 * d
    ada_w = w_ada.shape[-1]
    px, py, pc = lax.axis_index("x"), lax.axis_index("y"), lax.axis_index("c")
    chip = 2 * px + py
    tr = min(256, l)
    tq = min(512, n)
    tk = min(256, l)
    ts = s_len // 8 if (s_len // 8) % 16 == 0 else tr
    scale = HEAD_DIM ** -0.5

    c_all = _allgather_small(c, chips_only=False, name="gather_c").reshape(8, d)
    craw = jnp.concatenate([c_all, c_ctx.reshape(1, d), jnp.zeros((7, d), F32)], axis=0)
    b_loc = lax.dynamic_slice(b_ada, (0, chip * ada_w), (1, ada_w))
    mod_part = _ada_fwd(craw, w_ada[0], b_loc, tn=min(512, ada_w))
    mod_all = _allgather_small(mod_part, chips_only=True, name="gather_mod")
    mod_all = jnp.transpose(mod_all, (1, 0, 2)).reshape(16, d3)
    me = 4 * px + 2 * py + pc
    mod_b = lax.dynamic_slice(mod_all, (me, 0), (1, d3))
    mod_c = mod_all[8:9]
    ss = jnp.concatenate([mod_b[:, d:2 * d], mod_b[:, 0:d], mod_c[:, d:2 * d], mod_c[:, 0:d]], axis=0)
    gate = mod_b[:, 2 * d:]

    w_own = _cast_bf16(w_in[0], name="cast_w_in")
    later_weights = _gather_side([_cast_into(w_out[0], 0, chip, name="cast_w_out"),
                                  _cast_into(pool_w[0], 1, chip, name="cast_pool_w")], [(0, 1), (1, 2)])
    cw = in_w // 12
    assert in_w % 12 == 0 and cw % HEAD_DIM == 0
    others = [jnp.bitwise_xor(chip, 2), jnp.bitwise_xor(chip, 1), jnp.bitwise_xor(chip, 3)]

    def third_ids(pieces):
        return jnp.stack([others[k] * 3 + r for k, r in pieces])

    def fetch(pieces):
        return _gather_parts_side(w_own, pieces, cw)

    ctab, stab = _rope_tables(n, l)
    proj, h, x0, x1, y0, y1 = _in_proj_own(x2, ctx2, norm_pre, ss, w_own, chip.reshape(1), width=in_w, tm=tr, tn=cw,
                                           side=fetch([(0, 0), (0, 1), (1, 0), (1, 1)]))

    def in_proj(parts, pieces, prev, name, side=None):
        return _in_proj_parts(h, parts, third_ids(pieces), prev, width=in_w, tm=ts, tn=cw, name=name, side=side)

    proj, x2_, y2_ = in_proj([x0, y0], [(0, 0), (1, 0)], proj, "in_proj_1", fetch([(0, 2), (1, 2)]))
    proj, d0 = in_proj([x1, y1], [(0, 1), (1, 1)], proj, "in_proj_2", fetch([(2, 0)]))
    proj, d1 = in_proj([x2_, y2_], [(0, 2), (1, 2)], proj, "in_proj_3", fetch([(2, 1)]))
    proj, d2 = in_proj([d0], [(2, 0)], proj, "in_proj_4", fetch([(2, 2)]))
    (proj,) = in_proj([d1, d2], [(2, 1), (2, 2)], proj, "in_proj_5")
    w_parts = [(w_own, 0), (w_own, 1), (w_own, 2)] + [(w, 0) for w in (x0, x1, x2_, y0, y1, y2_, d0, d1, d2)]
    part_ids = jnp.concatenate([chip * 3 + jnp.arange(3, dtype=jnp.int32),
                                third_ids([(k, r) for k in range(3) for r in range(3)])])
    qr = _qk_prep(proj, q_norm, ctab[l:], stab[l:], row0=l, col0=0, width=aw, nrows=n, tm=tr, name="q_prep",
                  out_scale=scale * LOG2E)
    kr = _qk_prep(proj, k_norm, ctab, stab, row0=0, col0=k0, width=kvw, nrows=s_len, tm=tr, name="k_prep")
    v_all = proj[:, v0:v0 + kvw]
    tkf = min(2048, n)
    vt_ctx = jnp.transpose(v_all[:l].reshape(1, l, kvw // HEAD_DIM, HEAD_DIM), (2, 0, 3, 1))
    vt_lat = jnp.transpose(v_all[l:].reshape(n // tkf, tkf, kvw // HEAD_DIM, HEAD_DIM), (2, 0, 3, 1))
    attn_o, lse, w_out_f, pool_f = _flash_fwd(qr, kr, vt_ctx, vt_lat, tq=tq, side=later_weights)
    dmat, po_raw = _pool_fwd(proj, pool_f, l=l, n=n, col0=up0, tm=min(512, n // 2))
    yv = _merge(attn_o, po_raw, proj, pool_scale, l=l, ga0=ga0, gp0=gp0, tm=tr)
    out = _matmul(yv, w_out_f, ca=1, cb=0, tm=min(512, n), tn=min(1024, d), tk=d, out_dtype=F32, name="out_proj")

    dxn, d_out, post_st = _post(x2, out, tgt, gate, norm_post, tm=min(128, n))
    loss = 0.5 * jnp.sum(post_st[2]) / d
    loss = lax.psum(loss, ("x", "y", "c"))
    g_wout = _matmul(yv, d_out, ca=0, cb=0, tm=min(512, d), tn=min(1024, d), tk=n, out_dtype=BF, name="wgrad_out")
    dy, from_sib_wout = _matmul(d_out, w_out_f, ca=1, cb=1, tm=min(512, n), tn=min(1024, d), tk=d, out_dtype=BF,
                                name="dgrad_out", side=_pair_side([g_wout], [1]))
    pair_wout = _add_own_half(g_wout, from_sib_wout, 1, pc, name="pair_sum_1")
    dproj = _zero_rows(l, (s_len, in_w), name="dproj_init")
    dval, dproj, dps = _merge_bwd(dy, attn_o, po_raw, proj, pool_scale, dproj, l=l, ga0=ga0, gp0=gp0, tm=tr)
    g_pool = _pool_wgrad(dmat, dval, col0=aw, tk=min(512, n))
    dds = _pool_bwd_map(dval, pool_f, n=n, col0=aw, tm=min(1024, n))
    dproj = _pool_bwd_window(dds, dproj, n=n, tm=tr, row0=l, col0=up0)
    delta = _attn_delta(attn_o, dval, tq=tq)
    tqb = min(2048, n)
    stat_shape = (kvw // HEAD_DIM, GQA_GROUP, n // tqb, 1, tqb)
    dq, dk, dproj, *from_chips_wout = _flash_bwd(
        qr, dval, kr, proj, lse.reshape(stat_shape), delta.reshape(stat_shape), dproj, v_col0=v0, tq=tqb, tk=tk,
        scale=scale, side=_chips_side([pair_wout], [0]))
    red_wout = _sum_own_block(pair_wout, from_chips_wout, 0, 1, chip, pc, name="chip_sum_1")
    dproj, dgq = _qk_bwd(dq, proj, q_norm, ctab[l:], stab[l:], dproj, row0=l, col0=0, tm=tr, name="q_bwd", transposed=True)
    dproj, dgk = _qk_bwd(dk, proj, k_norm, ctab, stab, dproj, row0=0, col0=k0, tm=tr, name="k_bwd")
    tn_w = 1024 if in_w % 1024 == 0 else 512
    g_send, g_wout_s = _wgrad_rows(h, dproj, 1 - pc, tm=min(512, d // 2), tn=tn_w, name="wgrad_in_send",
                                   side=_join_side([red_wout], [1]))
    g_keep, from_sib_win, from_sib_pool = _wgrad_rows(
        h, dproj, pc, tm=min(512, d // 2), tn=tn_w, name="wgrad_in_keep",
        side=_both(_swap_side([g_send]), _pair_side([g_pool], [2])))
    pair_win = _add2(g_keep, from_sib_win, name="pair_sum_0")
    pair_pool = _add_own_half(g_pool, from_sib_pool, 2, pc, name="pair_sum_2")
    dh, *from_chips = _dgrad_in_parts(dproj, w_parts, part_ids, cw=cw, tm=ts, tn=min(512, d), name="dgrad_in",
                                      side=_chips_side([pair_win, pair_pool], [1, 1]))
    red_win = _sum_own_block(pair_win, from_chips[0:3], 1, 0, chip, pc, name="chip_sum_0")
    red_pool = _sum_own_block(pair_pool, from_chips[3:6], 1, 2, chip, pc, name="chip_sum_2")
    grad_x, pre_st = _prenorm_bwd(x2, ctx2, dh, dxn, norm_pre, ss, tm=min(128, l))

    zero_d = jnp.zeros((1, d), F32)
    packed = jnp.concatenate([pre_st[0:1], pre_st[1:2], post_st[0:1], pre_st[2:3], pre_st[3:4], zero_d,
                              pre_st[4:5], post_st[1:2], dgq[0:1], dgk[0:1], dps[0:1]], axis=1)
    gath, g_win_s, g_pool_s = _allgather_small(packed, chips_only=False, name="gather_small",
                                               side=_join_side([red_win, red_pool], [0, 2]))
    dm, sums = _reduce_small(gath, d3)
    o1 = 2 * d3
    g_npre, g_npost = sums[:, o1:o1 + d], sums[:, o1 + d:o1 + 2 * d]
    g_q, g_k = sums[:, o1 + 2 * d:o1 + 2 * d + HEAD_DIM], sums[:, o1 + 2 * d + HEAD_DIM:o1 + 2 * d + 2 * HEAD_DIM]
    g_ps = sums[:, o1 + 2 * d + 2 * HEAD_DIM:]
    g_bada = sums[:, 0:d3] + sums[:, d3:2 * d3]
    dm_loc = lax.dynamic_slice(dm, (0, chip * ada_w), (16, ada_w))
    g_wada, dl_wada, nm_wada, nv_wada, dact = _ada_bwd(craw.T, dm_loc, w_ada[0], m_w_ada[0], v_w_ada[0], tm=128)
    cparts = _allgather_small(dact[8:9], chips_only=True, name="gather_cctx")
    g_cctx = _cctx_grad(cparts, c_ctx.reshape(1, d)).reshape(d)

    def upd(w, g, m, v, name):
        return _adamw(w, g.reshape(w.shape), m, v, name=name)

    grads = {"c_ctx": g_cctx, "b_ada": g_bada, "norm_pre": g_npre, "norm_post": g_npost, "w_in": g_win_s[None],
             "q_norm": g_q, "k_norm": g_k, "pool_w": g_pool_s[None], "pool_scale": g_ps, "w_out": g_wout_s[None]}
    res = {"w_ada": (g_wada[None], dl_wada[None], nm_wada[None], nv_wada[None])}
    given = {"c_ctx": (c_ctx, m_c_ctx, v_c_ctx), "b_ada": (b_ada, m_b_ada, v_b_ada),
             "norm_pre": (norm_pre, m_norm_pre, v_norm_pre), "norm_post": (norm_post, m_norm_post, v_norm_post),
             "w_in": (w_in, m_w_in, v_w_in), "q_norm": (q_norm, m_q_norm, v_q_norm), "k_norm": (k_norm, m_k_norm, v_k_norm),
             "pool_w": (pool_w, m_pool_w, v_pool_w), "pool_scale": (pool_scale, m_pool_scale, v_pool_scale),
             "w_out": (w_out, m_w_out, v_w_out)}
    for nme, (w, m, v) in given.items():
        g = grads[nme].reshape(w.shape)
        res[nme] = upd(w, g, m, v, "adamw_" + nme)
    order = ["c_ctx", "w_ada", "b_ada", "norm_pre", "norm_post", "w_in", "q_norm", "k_norm", "pool_w", "pool_scale", "w_out"]
    return (loss, grad_x[None], *[res[k][0] for k in order], *[res[k][1] for k in order],
            *[res[k][2] for k in order], *[res[k][3] for k in order])
```

```python
import functools

import jax
import jax.numpy as jnp
from jax import lax
from jax.experimental import pallas as pl
from jax.experimental.pallas import tpu as pltpu

F32 = jnp.float32
BF = jnp.bfloat16
MESH = pl.DeviceIdType.MESH

HEAD_DIM = 128
GQA_GROUP = 4
GRID_W = 64
ROPE_PAIRS = HEAD_DIM // 4
ROPE_THETA = 10000.0
EPS = 1e-6
N_POOL_GROUPS = 4
POOL_HALO = 128
ADAM_LR = 0.001
ADAM_B1 = 0.9
ADAM_B2 = 0.999
ADAM_EPS = 1e-08
ADAM_WD = 0.01
ADAM_STEP = 10
LOG2E = 1.4426950408889634
MIB = 2 ** 20
VMEM_LIMIT = 48 * MIB
CHIP_MASKS = ((1, 0, 0), (0, 1, 0), (1, 1, 0))
ALL_MASKS = ((0, 0, 1), (0, 1, 0), (0, 1, 1), (1, 0, 0), (1, 0, 1), (1, 1, 0), (1, 1, 1))
HBM_SPEC = pl.BlockSpec(memory_space=pl.ANY)
VMEM_SPEC = pl.BlockSpec(memory_space=pltpu.VMEM)


def _cparams(*sem):
    return pltpu.CompilerParams(dimension_semantics=sem, vmem_limit_bytes=VMEM_LIMIT)


def _sigmoid(v):
    return 0.5 * jnp.tanh(0.5 * v) + 0.5


def _silu_and_grad(v):
    s = _sigmoid(v)
    return v * s, s * (1.0 + v * (1.0 - s))


def _dot(a, b, ca, cb):
    return lax.dot_general(a, b, (((ca,), (cb,)), ((), ())), preferred_element_type=F32)


def _block_rows(rows, width, itemsize=4, target=MIB):
    best = 8
    for t in range(8, rows + 1, 8):
        if rows % t == 0 and t * width * itemsize <= target:
            best = t
    return best if rows % 8 == 0 else rows


def _my_pos():
    return lax.axis_index("x"), lax.axis_index("y"), lax.axis_index("c")


def _flip(pos, mask):
    return tuple(jnp.bitwise_xor(p, m) if m else p for p, m in zip(pos, mask))


def _allgather_small(v, *, chips_only, name, side=None):
    r, w = v.shape
    masks = CHIP_MASKS if chips_only else ALL_MASKS
    nslot = 4 if chips_only else 8
    n_sin = len(side.ins) if side else 0
    n_sout = len(side.out_shapes) if side else 0

    def slot(pos):
        return 2 * pos[0] + pos[1] if chips_only else 4 * pos[0] + 2 * pos[1] + pos[2]

    def body(v_ref, *refs):
        s_ins, o_ref = refs[:n_sin], refs[n_sin]
        s_outs = refs[n_sin + 1:n_sin + 1 + n_sout]
        send_sems, recv_sems, local_sem = refs[n_sin + 1 + n_sout:n_sin + 4 + n_sout]
        s_sems = refs[n_sin + 4 + n_sout:]
        if side:
            side.start(s_ins, s_outs, s_sems)
        me = _my_pos()

        def copy(k, block_of, to):
            return pltpu.make_async_remote_copy(
                src_ref=v_ref, dst_ref=o_ref.at[slot(block_of)], send_sem=send_sems.at[k], recv_sem=recv_sems.at[k],
                device_id=to, device_id_type=MESH)

        mine = pltpu.make_async_copy(v_ref, o_ref.at[slot(me)], local_sem)
        mine.start()
        sends = [copy(k, me, _flip(me, m)) for k, m in enumerate(masks)]
        for cp in sends:
            cp.start()
        for k, m in enumerate(masks):
            copy(k, _flip(me, m), me).wait_recv()
        for cp in sends:
            cp.wait_send()
        mine.wait()
        if side:
            side.finish(s_ins, s_outs, s_sems)

    o_shape = jax.ShapeDtypeStruct((nslot, r, w), v.dtype)
    sems = [pltpu.SemaphoreType.DMA((len(masks),)), pltpu.SemaphoreType.DMA((len(masks),)), pltpu.SemaphoreType.DMA]
    if not side:
        return pl.pallas_call(body, name=name, out_shape=o_shape, in_specs=[VMEM_SPEC], out_specs=VMEM_SPEC,
                              scratch_shapes=sems)(v)
    return pl.pallas_call(
        body, name=name, out_shape=(o_shape, *side.out_shapes),
        in_specs=[VMEM_SPEC] + [HBM_SPEC] * n_sin, out_specs=(VMEM_SPEC, *([HBM_SPEC] * n_sout)),
        input_output_aliases={1 + i: 1 + o for i, o in side.aliases.items()},
        scratch_shapes=sems + side.sem_shapes,
    )(v, *side.ins)


def _sl(ref, axis, start, size):
    idx = [slice(None)] * len(ref.shape)
    idx[axis] = pl.ds(start, size)
    return ref.at[tuple(idx)]


class _Side:
    def __init__(self, ins, out_shapes, aliases, sem_shapes, start, finish, mid=None):
        self.ins, self.out_shapes, self.aliases, self.sem_shapes = list(ins), list(out_shapes), dict(aliases), list(sem_shapes)
        self.start, self.finish, self.mid = start, finish, mid


def _run_side(side, *, name):
    n_in, n_out = len(side.ins), len(side.out_shapes)

    def body(*refs):
        parts = refs[:n_in], refs[n_in:n_in + n_out], refs[n_in + n_out:]
        side.start(*parts)
        if side.mid is not None:
            side.mid(*parts)
        side.finish(*parts)

    return pl.pallas_call(
        body, name=name, out_shape=tuple(side.out_shapes), in_specs=[HBM_SPEC] * n_in, out_specs=tuple([HBM_SPEC] * n_out),
        input_output_aliases=side.aliases, scratch_shapes=side.sem_shapes,
    )(*side.ins)


def _side_at(grid):
    pids = [pl.program_id(a) for a in range(len(grid))]

    def at(values):
        cond = pids[0] == values[0]
        for p, v in zip(pids[1:], values[1:]):
            cond = jnp.logical_and(cond, p == v)
        return cond

    zeros = [0] * len(grid)
    return at(zeros), at([grid[0] // 2] + zeros[1:]), at([g - 1 for g in grid])


def _gather_side(shards, layouts):
    nm = len(shards)

    def copies(outs, sems):
        send_sems, recv_sems, fsend_sems, frecv_sems = sems
        x, y, c = _my_pos()
        me = (x, y, c)
        sib = (x, y, 1 - c)
        sends, landed, fwds, from_sib = [], [], [], []
        for m in range(nm):
            a_s, a_h = layouts[m]
            ns, nh = outs[m].shape[a_s] // 4, outs[m].shape[a_h] // 2

            def region(chip_pos, half, m=m, a_s=a_s, a_h=a_h, ns=ns, nh=nh):
                j = 2 * chip_pos[0] + chip_pos[1]
                return _sl(_sl(outs[m], a_s, j * ns, ns), a_h, half * nh, nh)

            for k, mask in enumerate(CHIP_MASKS):
                other = _flip(me, mask)

                def rc(ref, ssem, rsem, to):
                    return pltpu.make_async_remote_copy(src_ref=ref, dst_ref=ref, send_sem=ssem, recv_sem=rsem,
                                                        device_id=to, device_id_type=MESH)

                sends.append(rc(region(me, c), send_sems.at[m, k], recv_sems.at[m, k], other))
                landed.append(rc(region(other, c), send_sems.at[m, k], recv_sems.at[m, k], me))
                fwds.append(rc(region(other, c), fsend_sems.at[m, k], frecv_sems.at[m, k], sib))
                from_sib.append(rc(region(other, 1 - c), fsend_sems.at[m, k], frecv_sems.at[m, k], sib))
        return sends, landed, fwds, from_sib

    def start(ins, outs, sems):
        for cp in copies(outs, sems)[0]:
            cp.start()

    def mid(ins, outs, sems):
        _, landed, fwds, _ = copies(outs, sems)
        for arrived, fw in zip(landed, fwds):
            arrived.wait_recv()
            fw.start()

    def finish(ins, outs, sems):
        sends, _, fwds, from_sib = copies(outs, sems)
        for cp in from_sib:
            cp.wait_recv()
        for cp in sends + fwds:
            cp.wait_send()

    return _Side(shards, [jax.ShapeDtypeStruct(s.shape, s.dtype) for s in shards], {i: i for i in range(nm)},
                 [pltpu.SemaphoreType.DMA((nm, 3))] * 4, start, finish, mid)


def _both(s1, s2):
    n1, m1, k1 = len(s1.ins), len(s1.out_shapes), len(s1.sem_shapes)

    def split(fn1, fn2):
        def run(ins, outs, sems):
            fn1(ins[:n1], outs[:m1], sems[:k1])
            fn2(ins[n1:], outs[m1:], sems[k1:])
        return run

    aliases = dict(s1.aliases)
    aliases.update({n1 + i: m1 + o for i, o in s2.aliases.items()})
    return _Side(s1.ins + s2.ins, s1.out_shapes + s2.out_shapes, aliases, s1.sem_shapes + s2.sem_shapes,
                 split(s1.start, s2.start), split(s1.finish, s2.finish))


def _swap_side(mats):
    nm = len(mats)

    def copies(ins, outs, sems):
        x, y, c = _my_pos()
        return [pltpu.make_async_remote_copy(src_ref=ins[m], dst_ref=outs[m], send_sem=sems[0].at[m], recv_sem=sems[1].at[m],
                                             device_id=(x, y, 1 - c), device_id_type=MESH) for m in range(nm)]

    def start(ins, outs, sems):
        for cp in copies(ins, outs, sems):
            cp.start()

    def finish(ins, outs, sems):
        for cp in copies(ins, outs, sems):
            cp.wait()

    return _Side(mats, [jax.ShapeDtypeStruct(s.shape, s.dtype) for s in mats], {},
                 [pltpu.SemaphoreType.DMA((nm,))] * 2, start, finish)


def _gather_parts_side(w_own, mask_ids):
    nt = len(mask_ids)
    nh = w_own.shape[0] // 2

    def copies(ins, outs, sems):
        send_sems, recv_sems, fsend_sems, frecv_sems = sems
        x, y, c = _my_pos()
        me, sib = (x, y, c), (x, y, 1 - c)
        sends, landed, fwds, from_sib = [], [], [], []
        for t, k in enumerate(mask_ids):
            mine = _sl(outs[t], 0, c * nh, nh)
            theirs = _sl(outs[t], 0, (1 - c) * nh, nh)

            def rc(src, dst, ssem, rsem, to):
                return pltpu.make_async_remote_copy(src_ref=src, dst_ref=dst, send_sem=ssem, recv_sem=rsem,
                                                    device_id=to, device_id_type=MESH)

            sends.append(rc(_sl(ins[0], 0, c * nh, nh), mine, send_sems.at[t], recv_sems.at[t], _flip(me, CHIP_MASKS[k])))
            landed.append(rc(mine, mine, send_sems.at[t], recv_sems.at[t], me))
            fwds.append(rc(mine, mine, fsend_sems.at[t], frecv_sems.at[t], sib))
            from_sib.append(rc(theirs, theirs, fsend_sems.at[t], frecv_sems.at[t], sib))
        return sends, landed, fwds, from_sib

    def start(ins, outs, sems):
        for cp in copies(ins, outs, sems)[0]:
            cp.start()

    def finish(ins, outs, sems):
        sends, landed, fwds, from_sib = copies(ins, outs, sems)
        for arrived, fw in zip(landed, fwds):
            arrived.wait_recv()
            fw.start()
        for cp in from_sib:
            cp.wait_recv()
        for cp in sends + fwds:
            cp.wait_send()

    return _Side([w_own], [jax.ShapeDtypeStruct(w_own.shape, w_own.dtype)] * nt, {},
                 [pltpu.SemaphoreType.DMA((nt,))] * 4, start, finish)


def _pair_side(mats, half_axes):
    nm = len(mats)

    def copies(ins, outs, sems):
        x, y, c = _my_pos()
        cps = []
        for m in range(nm):
            nh = ins[m].shape[half_axes[m]] // 2
            cps.append(pltpu.make_async_remote_copy(
                src_ref=_sl(ins[m], half_axes[m], (1 - c) * nh, nh), dst_ref=outs[m],
                send_sem=sems[0].at[m], recv_sem=sems[1].at[m], device_id=(x, y, 1 - c), device_id_type=MESH))
        return cps

    def start(ins, outs, sems):
        for cp in copies(ins, outs, sems):
            cp.start()

    def finish(ins, outs, sems):
        for cp in copies(ins, outs, sems):
            cp.wait()

    out_shapes = []
    for s, a_h in zip(mats, half_axes):
        shp = list(s.shape)
        shp[a_h] //= 2
        out_shapes.append(jax.ShapeDtypeStruct(tuple(shp), s.dtype))
    return _Side(mats, out_shapes, {}, [pltpu.SemaphoreType.DMA((nm,))] * 2, start, finish)


def _chips_side(mats, shard_axes):
    nm = len(mats)

    def copies(ins, outs, sems):
        me = _my_pos()
        cps = []
        for m in range(nm):
            ns = ins[m].shape[shard_axes[m]] // 4
            for k, mask in enumerate(CHIP_MASKS):
                other = _flip(me, mask)
                cps.append(pltpu.make_async_remote_copy(
                    src_ref=_sl(ins[m], shard_axes[m], (2 * other[0] + other[1]) * ns, ns), dst_ref=outs[3 * m + k],
                    send_sem=sems[0].at[m, k], recv_sem=sems[1].at[m, k], device_id=other, device_id_type=MESH))
        return cps

    def start(ins, outs, sems):
        for cp in copies(ins, outs, sems):
            cp.start()

    def finish(ins, outs, sems):
        for cp in copies(ins, outs, sems):
            cp.wait()

    out_shapes = []
    for s, a_s in zip(mats, shard_axes):
        shp = list(s.shape)
        shp[a_s] //= 4
        out_shapes += [jax.ShapeDtypeStruct(tuple(shp), s.dtype)] * 3
    return _Side(mats, out_shapes, {}, [pltpu.SemaphoreType.DMA((nm, 3))] * 2, start, finish)


def _join_side(halves, half_axes):
    nm = len(halves)

    def copies(outs, sems):
        x, y, c = _my_pos()
        mine, theirs = [], []
        for m in range(nm):
            nh = outs[m].shape[half_axes[m]] // 2
            for half, lst in ((c, mine), (1 - c, theirs)):
                ref = _sl(outs[m], half_axes[m], half * nh, nh)
                lst.append(pltpu.make_async_remote_copy(
                    src_ref=ref, dst_ref=ref, send_sem=sems[0].at[m], recv_sem=sems[1].at[m],
                    device_id=(x, y, 1 - c), device_id_type=MESH))
        return mine, theirs

    def start(ins, outs, sems):
        for cp in copies(outs, sems)[0]:
            cp.start()

    def finish(ins, outs, sems):
        mine, theirs = copies(outs, sems)
        for cp in mine:
            cp.wait_send()
        for cp in theirs:
            cp.wait_recv()

    return _Side(halves, [jax.ShapeDtypeStruct(s.shape, s.dtype) for s in halves], {i: i for i in range(nm)},
                 [pltpu.SemaphoreType.DMA((nm,))] * 2, start, finish)


def _part_specs(shp, axis, slot, itemsize, target=MIB):
    if len(shp) == 2:
        rows, width = shp
        tm = _block_rows(rows, width, itemsize, target)
        nb = rows // tm
        own = pl.BlockSpec((tm, width), lambda i, p: (i, 0))
        if axis == 1:
            part = pl.BlockSpec((tm, width), lambda i, p: (i, p[slot]))
        else:
            part = pl.BlockSpec((tm, width), lambda i, p: (p[slot] * nb + i, 0))
        return (nb,), own, part
    assert len(shp) == 3 and axis in (1, 2)
    g, rows, width = shp
    own = pl.BlockSpec((1, rows, width), lambda i, p: (i, 0, 0))
    if axis == 1:
        part = pl.BlockSpec((1, rows, width), lambda i, p: (i, p[slot], 0))
    else:
        part = pl.BlockSpec((1, rows, width), lambda i, p: (i, 0, p[slot]))
    return (g,), own, part


def _cast_into(w, axis, p, *, name):
    grid, own, part = _part_specs(w.shape, axis, 0, 4, 2 * MIB)
    big = list(w.shape)
    big[axis] *= 4

    def body(p_ref, w_ref, o_ref):
        o_ref[...] = w_ref[...].astype(BF)

    return pl.pallas_call(
        body, name=name, out_shape=jax.ShapeDtypeStruct(tuple(big), BF),
        grid_spec=pltpu.PrefetchScalarGridSpec(num_scalar_prefetch=1, grid=grid, in_specs=[own], out_specs=part),
        compiler_params=_cparams("parallel"),
    )(p.reshape(1), w)


def _cast_bf16(w, *, name):
    rows, width = w.shape
    tm = _block_rows(rows, width, 4, 2 * MIB)

    def body(w_ref, o_ref):
        o_ref[...] = w_ref[...].astype(BF)

    spec = pl.BlockSpec((tm, width), lambda i: (i, 0))
    return pl.pallas_call(body, name=name, grid=(rows // tm,), out_shape=jax.ShapeDtypeStruct(w.shape, BF),
                          in_specs=[spec], out_specs=spec, compiler_params=_cparams("parallel"))(w)


def _add2(a, b, *, name):
    rows, width = a.shape
    tm = _block_rows(rows, width, 2, 4 * MIB)

    def body(a_ref, b_ref, o_ref):
        o_ref[...] = (a_ref[...].astype(F32) + b_ref[...].astype(F32)).astype(BF)

    spec = pl.BlockSpec((tm, width), lambda i: (i, 0))
    return pl.pallas_call(body, name=name, grid=(rows // tm,), out_shape=jax.ShapeDtypeStruct(a.shape, BF),
                          in_specs=[spec, spec], out_specs=spec, compiler_params=_cparams("parallel"))(a, b)


def _add_own_half(full, recv, half_axis, c, *, name):
    grid, own, part = _part_specs(recv.shape, half_axis, 0, 2, 4 * MIB)

    def body(c_ref, f_ref, r_ref, o_ref):
        o_ref[...] = (f_ref[...].astype(F32) + r_ref[...].astype(F32)).astype(BF)

    return pl.pallas_call(
        body, name=name, out_shape=jax.ShapeDtypeStruct(recv.shape, BF),
        grid_spec=pltpu.PrefetchScalarGridSpec(num_scalar_prefetch=1, grid=grid, in_specs=[part, own], out_specs=own),
        compiler_params=_cparams("parallel"),
    )(c.reshape(1), full, recv)


def _sum_own_block(mine_all, recvs, shard_axis, half_axis, j, c, *, name):
    shp = recvs[0].shape
    grid, own, mine = _part_specs(shp, shard_axis, 0, 4, 2 * MIB)
    _, _, place = _part_specs(shp, half_axis, 1, 4, 2 * MIB)
    big = list(shp)
    big[half_axis] *= 2

    def body(p_ref, a_ref, r0, r1, r2, o_ref):
        o_ref[...] = ((a_ref[...].astype(F32) + r0[...].astype(F32)) + r1[...].astype(F32)) + r2[...].astype(F32)

    return pl.pallas_call(
        body, name=name, out_shape=jax.ShapeDtypeStruct(tuple(big), F32),
        grid_spec=pltpu.PrefetchScalarGridSpec(
            num_scalar_prefetch=1, grid=grid, in_specs=[mine, own, own, own], out_specs=place),
        compiler_params=_cparams("parallel"),
    )(jnp.stack([j, c]), mine_all, *recvs)


def _adamw(w, g, m, v, *, name):
    shp = w.shape
    width = shp[-1] if len(shp) > 1 else shp[0]
    rows = 1
    for d in shp[:-1]:
        rows *= d
    if len(shp) == 1:
        rows = 1
    args = [a.reshape(rows, width) for a in (w, g, m, v)]
    tm = _block_rows(rows, width, 4, 2 * MIB)

    def body(w_ref, g_ref, m_ref, v_ref, go_ref, d_ref, nm_ref, nv_ref):
        gv = g_ref[...]
        go_ref[...] = gv
        nm = ADAM_B1 * m_ref[...] + (1.0 - ADAM_B1) * gv
        nv = ADAM_B2 * v_ref[...] + (1.0 - ADAM_B2) * (gv * gv)
        m_hat = nm / (1.0 - ADAM_B1 ** ADAM_STEP)
        v_hat = nv / (1.0 - ADAM_B2 ** ADAM_STEP)
        d_ref[...] = -ADAM_LR * (m_hat / (jnp.sqrt(v_hat) + ADAM_EPS) + ADAM_WD * w_ref[...])
        nm_ref[...] = nm
        nv_ref[...] = nv

    spec = pl.BlockSpec((tm, width), lambda i: (i, 0))
    outs = pl.pallas_call(
        body, name=name, grid=(rows // tm,), out_shape=tuple([jax.ShapeDtypeStruct((rows, width), F32)] * 4),
        in_specs=[spec] * 4, out_specs=tuple([spec] * 4), compiler_params=_cparams("parallel"),
    )(*args)
    return tuple(o.reshape(shp) for o in outs)


def _matmul(a, b, *, ca, cb, tm, tn, tk, out_dtype, name, b_resident=False, side=None):
    m, kdim = a.shape[1 - ca], a.shape[ca]
    n = b.shape[1 - cb]
    nk = kdim // tk
    assert m % tm == 0 and n % tn == 0 and kdim % tk == 0
    n_sin = len(side.ins) if side else 0
    n_sout = len(side.out_shapes) if side else 0
    if b_resident:
        gi = lambda p, q, k: (q, p, k)
        grid = (n // tn, m // tm, nk)
    else:
        gi = lambda p, q, k: (p, q, k)
        grid = (m // tm, n // tn, nk)

    def body(*refs):
        a_ref, b_ref = refs[0], refs[1]
        s_ins = refs[2:2 + n_sin]
        o_ref = refs[2 + n_sin]
        s_outs = refs[3 + n_sin:3 + n_sin + n_sout]
        rest = refs[3 + n_sin + n_sout:]
        acc, sems = (rest[:1], rest[1:]) if nk > 1 else ((), rest)
        if side:
            first, middle, last = _side_at(grid)
            pl.when(first)(lambda: side.start(s_ins, s_outs, sems))
            if side.mid is not None:
                pl.when(middle)(lambda: side.mid(s_ins, s_outs, sems))
        part = _dot(a_ref[...], b_ref[...], ca, cb)
        if nk == 1:
            o_ref[...] = part.astype(o_ref.dtype)
        else:
            k = pl.program_id(2)

            @pl.when(k == 0)
            def _():
                acc[0][...] = part

            @pl.when(jnp.logical_and(k > 0, k < nk - 1))
            def _():
                acc[0][...] += part

            @pl.when(k == nk - 1)
            def _():
                o_ref[...] = (acc[0][...] + part).astype(o_ref.dtype)
        if side:
            pl.when(last)(lambda: side.finish(s_ins, s_outs, sems))

    def a_map(p, q, k):
        i, _, kk = gi(p, q, k)
        return (i, kk) if ca == 1 else (kk, i)

    def b_map(p, q, k):
        _, j, kk = gi(p, q, k)
        return (kk, j) if cb == 0 else (j, kk)

    def o_map(p, q, k):
        i, j, _ = gi(p, q, k)
        return (i, j)

    a_spec = pl.BlockSpec((tm, tk) if ca == 1 else (tk, tm), a_map)
    b_spec = pl.BlockSpec((tk, tn) if cb == 0 else (tn, tk), b_map)
    o_shape = jax.ShapeDtypeStruct((m, n), out_dtype)
    acc_shapes = [pltpu.VMEM((tm, tn), F32)] if nk > 1 else []
    if not side:
        return pl.pallas_call(
            body, name=name, grid=grid, out_shape=o_shape,
            in_specs=[a_spec, b_spec], out_specs=pl.BlockSpec((tm, tn), o_map), scratch_shapes=acc_shapes,
            compiler_params=_cparams("parallel", "parallel", "arbitrary"),
        )(a, b)
    return pl.pallas_call(
        body, name=name, grid=grid, out_shape=(o_shape, *side.out_shapes),
        in_specs=[a_spec, b_spec] + [HBM_SPEC] * n_sin,
        out_specs=(pl.BlockSpec((tm, tn), o_map), *([HBM_SPEC] * n_sout)),
        input_output_aliases={2 + i: 1 + o for i, o in side.aliases.items()},
        scratch_shapes=acc_shapes + side.sem_shapes,
        compiler_params=_cparams("arbitrary", "arbitrary", "arbitrary"),
    )(a, b, *side.ins)


def _wgrad_rows(a, b, half, *, tm, tn, name, side=None):
    kdim, m = a.shape
    n = b.shape[1]
    nb = (m // 2) // tm
    grid = (nb, n // tn)
    n_sin = len(side.ins) if side else 0
    n_sout = len(side.out_shapes) if side else 0

    def body(p_ref, a_ref, b_ref, *refs):
        s_ins, o_ref = refs[:n_sin], refs[n_sin]
        s_outs, sems = refs[n_sin + 1:n_sin + 1 + n_sout], refs[n_sin + 1 + n_sout:]
        if side:
            first, _, last = _side_at(grid)
            pl.when(first)(lambda: side.start(s_ins, s_outs, sems))
        o_ref[...] = _dot(a_ref[...], b_ref[...], 0, 0).astype(o_ref.dtype)
        if side:
            pl.when(last)(lambda: side.finish(s_ins, s_outs, sems))

    outs = pl.pallas_call(
        body, name=name, out_shape=(jax.ShapeDtypeStruct((m // 2, n), BF), *(side.out_shapes if side else [])),
        grid_spec=pltpu.PrefetchScalarGridSpec(
            num_scalar_prefetch=1, grid=grid,
            in_specs=[pl.BlockSpec((kdim, tm), lambda i, j, p: (0, p[0] * nb + i)),
                      pl.BlockSpec((kdim, tn), lambda i, j, p: (0, j))] + [HBM_SPEC] * n_sin,
            out_specs=(pl.BlockSpec((tm, tn), lambda i, j, p: (i, j)), *([HBM_SPEC] * n_sout)),
            scratch_shapes=side.sem_shapes if side else []),
        input_output_aliases={3 + i: 1 + o for i, o in side.aliases.items()} if side else {},
        compiler_params=_cparams("arbitrary", "arbitrary"),
    )(half.reshape(1), a, b, *(side.ins if side else []))
    return outs if side else outs[0]


def _in_proj_parts(h, w_parts, ids, prev, *, width, tm, tn, name, side=None):
    s_len, d = h.shape
    nb = w_parts[0].shape[1] // tn
    n_p = len(w_parts)
    n_prev = 0 if prev is None else 1
    n_sin = len(side.ins) if side else 0
    n_sout = len(side.out_shapes) if side else 0
    grid = (n_p, s_len // tm, nb)
    base = 1 + n_p + n_prev

    def body(ids_ref, *refs):
        h_ref, w_refs = refs[0], refs[1:1 + n_p]
        s_ins = refs[base:base + n_sin]
        o_ref = refs[base + n_sin]
        s_outs = refs[base + n_sin + 1:base + n_sin + 1 + n_sout]
        sems = refs[base + n_sin + 1 + n_sout:]
        if side:
            first, _, last = _side_at(grid)
            pl.when(first)(lambda: side.start(s_ins, s_outs, sems))
        for s in range(n_p):
            @pl.when(pl.program_id(0) == s)
            def _(s=s):
                o_ref[...] = _dot(h_ref[...], w_refs[s][...], 1, 0).astype(o_ref.dtype)
        if side:
            pl.when(last)(lambda: side.finish(s_ins, s_outs, sems))

    in_specs = [pl.BlockSpec((tm, d), lambda p, i, j, ids_ref: (i, 0))]
    for s in range(n_p):
        in_specs.append(pl.BlockSpec((d, tn), lambda p, i, j, ids_ref, s=s: (0, jnp.where(p == s, j, 0))))
    in_specs += [HBM_SPEC] * (n_prev + n_sin)
    o_spec = pl.BlockSpec((tm, tn), lambda p, i, j, ids_ref: (i, ids_ref[p] * nb + j))
    aliases = {2 + n_p: 0} if prev is not None else {}
    if side:
        aliases.update({1 + base + i: 1 + o for i, o in side.aliases.items()})
    outs = pl.pallas_call(
        body, name=name, out_shape=(jax.ShapeDtypeStruct((s_len, width), BF), *(side.out_shapes if side else [])),
        grid_spec=pltpu.PrefetchScalarGridSpec(
            num_scalar_prefetch=1, grid=grid, in_specs=in_specs, out_specs=(o_spec, *([HBM_SPEC] * n_sout)),
            scratch_shapes=side.sem_shapes if side else []),
        input_output_aliases=aliases, compiler_params=_cparams("arbitrary", "arbitrary", "arbitrary"),
    )(ids, h, *w_parts, *([prev] if prev is not None else []), *(side.ins if side else []))
    return outs


def _in_proj_own(x, ctx, g_pre, ss, w_own, ids, *, width, tm, tn, side):
    n, d = x.shape
    l = ctx.shape[0]
    nc = l // tm
    nb = w_own.shape[1] // tn
    grid = ((l + n) // tm, nb)
    n_sin, n_sout = len(side.ins), len(side.out_shapes)

    def body(ids_ref, x_ref, c_ref, g_ref, ss_ref, w_ref, *refs):
        s_ins = refs[:n_sin]
        o_ref, h_ref = refs[n_sin:n_sin + 2]
        s_outs, sems = refs[n_sin + 2:n_sin + 2 + n_sout], refs[n_sin + 2 + n_sout:]
        first, _, last = _side_at(grid)
        pl.when(first)(lambda: side.start(s_ins, s_outs, sems))
        t, j = pl.program_id(0), pl.program_id(1)

        def norm(src, scale, shift):
            v = src[...]
            rstd = lax.rsqrt(jnp.mean(v * v, axis=-1, keepdims=True) + EPS)
            h_ref[...] = ((v * rstd * g_ref[...]) * (1.0 + scale) + shift).astype(BF)

        @pl.when(jnp.logical_and(j == 0, t < nc))
        def _():
            norm(c_ref, ss_ref[2:3, :], ss_ref[3:4, :])

        @pl.when(jnp.logical_and(j == 0, t >= nc))
        def _():
            norm(x_ref, ss_ref[0:1, :], ss_ref[1:2, :])

        o_ref[...] = _dot(h_ref[...], w_ref[...], 1, 0).astype(o_ref.dtype)
        pl.when(last)(lambda: side.finish(s_ins, s_outs, sems))

    return pl.pallas_call(
        body, name="in_proj_own",
        out_shape=(jax.ShapeDtypeStruct((l + n, width), BF), jax.ShapeDtypeStruct((l + n, d), BF), *side.out_shapes),
        grid_spec=pltpu.PrefetchScalarGridSpec(
            num_scalar_prefetch=1, grid=grid,
            in_specs=[pl.BlockSpec((tm, d), lambda t, j, ids_ref: (jnp.maximum(t - nc, 0), 0)),
                      pl.BlockSpec((tm, d), lambda t, j, ids_ref: (jnp.minimum(t, nc - 1), 0)),
                      pl.BlockSpec((1, d), lambda t, j, ids_ref: (0, 0)), pl.BlockSpec((4, d), lambda t, j, ids_ref: (0, 0)),
                      pl.BlockSpec((d, tn), lambda t, j, ids_ref: (0, j))] + [HBM_SPEC] * n_sin,
            out_specs=(pl.BlockSpec((tm, tn), lambda t, j, ids_ref: (t, ids_ref[0] * nb + j)),
                       pl.BlockSpec((tm, d), lambda t, j, ids_ref: (t, 0)), *([HBM_SPEC] * n_sout)),
            scratch_shapes=side.sem_shapes),
        input_output_aliases={6 + i: 2 + o for i, o in side.aliases.items()},
        compiler_params=_cparams("arbitrary", "arbitrary"),
    )(ids, x, ctx, g_pre, ss, w_own, *side.ins)


def _dgrad_in_parts(dproj, w_parts, ids, *, tm, tn, name, side=None):
    s_len = dproj.shape[0]
    d, ws = w_parts[0].shape
    n_p = len(w_parts)
    n_sin = len(side.ins) if side else 0
    n_sout = len(side.out_shapes) if side else 0
    grid = (s_len // tm, d // tn)

    def body(ids_ref, *refs):
        a_refs, w_refs = refs[:n_p], refs[n_p:2 * n_p]
        s_ins = refs[2 * n_p:2 * n_p + n_sin]
        o_ref = refs[2 * n_p + n_sin]
        s_outs = refs[2 * n_p + n_sin + 1:2 * n_p + n_sin + 1 + n_sout]
        sems = refs[2 * n_p + n_sin + 1 + n_sout:]
        if side:
            first, _, last = _side_at(grid)
            pl.when(first)(lambda: side.start(s_ins, s_outs, sems))
        tot = _dot(a_refs[0][...], w_refs[0][...], 1, 1)
        for s in range(1, n_p):
            tot = tot + _dot(a_refs[s][...], w_refs[s][...], 1, 1)
        o_ref[...] = tot.astype(o_ref.dtype)
        if side:
            pl.when(last)(lambda: side.finish(s_ins, s_outs, sems))

    in_specs = [pl.BlockSpec((tm, ws), lambda i, j, ids_ref, s=s: (i, ids_ref[s])) for s in range(n_p)]
    in_specs += [pl.BlockSpec((tn, ws), lambda i, j, ids_ref: (j, 0))] * n_p
    in_specs += [HBM_SPEC] * n_sin
    aliases = {1 + 2 * n_p + i: 1 + o for i, o in side.aliases.items()} if side else {}
    return pl.pallas_call(
        body, name=name, out_shape=(jax.ShapeDtypeStruct((s_len, d), BF), *(side.out_shapes if side else [])),
        grid_spec=pltpu.PrefetchScalarGridSpec(
            num_scalar_prefetch=1, grid=grid, in_specs=in_specs,
            out_specs=(pl.BlockSpec((tm, tn), lambda i, j, ids_ref: (i, j)), *([HBM_SPEC] * n_sout)),
            scratch_shapes=side.sem_shapes if side else []),
        input_output_aliases=aliases, compiler_params=_cparams("arbitrary", "arbitrary"),
    )(ids, *([dproj] * n_p), *w_parts, *(side.ins if side else []))


def _swap32(v):
    lane = lax.broadcasted_iota(jnp.int32, v.shape, 1)
    return jnp.where((lane % 64) < 32, pltpu.roll(v, 96, 1), pltpu.roll(v, 32, 1))


def _qk_prep(proj, gain, ctab, stab, *, row0, col0, width, nrows, tm, name, out_scale=1.0):
    cw = min(512, width)
    rb0, cb0 = row0 // tm, col0 // cw
    assert row0 % tm == 0 and col0 % cw == 0 and width % cw == 0 and nrows % tm == 0

    def body(p_ref, g_ref, c_ref, s_ref, o_ref):
        row = lax.broadcasted_iota(jnp.int32, (HEAD_DIM, HEAD_DIM), 0)
        col = lax.broadcasted_iota(jnp.int32, (HEAD_DIM, HEAD_DIM), 1)
        perm = (row == jnp.where((col % 64) < 32, col + 32, col - 32)).astype(BF)
        gc = g_ref[...] * c_ref[...] * out_scale
        gs = _swap32(jnp.broadcast_to(g_ref[...], c_ref.shape)) * s_ref[...] * out_scale
        for hd in range(cw // HEAD_DIM):
            cols = slice(hd * HEAD_DIM, (hd + 1) * HEAD_DIM)
            vb = p_ref[:, cols]
            v = vb.astype(F32)
            rstd = lax.rsqrt(jnp.mean(v * v, axis=-1, keepdims=True) + EPS)
            o_ref[:, cols] = (rstd * (v * gc + _dot(vb, perm, 1, 0) * gs)).astype(BF)

    return pl.pallas_call(
        body, name=name, grid=(nrows // tm, width // cw), out_shape=jax.ShapeDtypeStruct((nrows, width), BF),
        in_specs=[pl.BlockSpec((tm, cw), lambda i, j: (i + rb0, j + cb0)), pl.BlockSpec((1, HEAD_DIM), lambda i, j: (0, 0)),
                  pl.BlockSpec((tm, HEAD_DIM), lambda i, j: (i, 0)), pl.BlockSpec((tm, HEAD_DIM), lambda i, j: (i, 0))],
        out_specs=pl.BlockSpec((tm, cw), lambda i, j: (i, j)), compiler_params=_cparams("parallel", "parallel"),
    )(proj, gain, ctab, stab)


def _zero_rows(rows, shape, *, name):
    def body(o_ref):
        o_ref[...] = jnp.zeros_like(o_ref)

    return pl.pallas_call(body, name=name, grid=(1,), out_shape=jax.ShapeDtypeStruct(shape, BF),
                          out_specs=pl.BlockSpec((rows, shape[1]), lambda i: (0, 0)))()


def _qk_bwd(dy, proj, gain, ctab, stab, into, *, row0, col0, tm, name, transposed=False):
    if transposed:
        tq = dy.shape[-1]
        nrows, width = dy.shape[1] * tq, dy.shape[0] * GQA_GROUP * HEAD_DIM
        per = tq // tm
        dy_spec = pl.BlockSpec((1, 1, GQA_GROUP, HEAD_DIM, tm), lambda i, j: (j, i // per, 0, 0, i % per))
    else:
        nrows, width = dy.shape
    cw = min(512, width)
    rb0, cb0 = row0 // tm, col0 // cw
    if transposed:
        assert cw == GQA_GROUP * HEAD_DIM and tq % tm == 0
    else:
        dy_spec = pl.BlockSpec((tm, cw), lambda i, j: (i, j))

    def body(d_ref, p_ref, g_ref, c_ref, s_ref, into_ref, o_ref, dg_ref):
        @pl.when(jnp.logical_and(pl.program_id(0) == 0, pl.program_id(1) == 0))
        def _():
            dg_ref[...] = jnp.zeros_like(dg_ref)

        dg = jnp.zeros((1, HEAD_DIM), F32)
        for hd in range(cw // HEAD_DIM):
            cols = slice(hd * HEAD_DIM, (hd + 1) * HEAD_DIM)
            v = p_ref[:, cols].astype(F32)
            rstd = lax.rsqrt(jnp.mean(v * v, axis=-1, keepdims=True) + EPS)
            nv = v * rstd
            d = d_ref[0, 0, hd].T if transposed else d_ref[:, cols]
            dyu = d * c_ref[...] + _swap32(d * s_ref[...])
            dg = dg + jnp.sum(dyu * nv, axis=0, keepdims=True)
            dn = dyu * g_ref[...]
            o_ref[:, cols] = (rstd * (dn - nv * jnp.mean(dn * nv, axis=-1, keepdims=True))).astype(BF)
        dg_ref[0:1, :] += dg

    placed = pl.BlockSpec((tm, cw), lambda i, j: (i + rb0, j + cb0))
    return pl.pallas_call(
        body, name=name, grid=(nrows // tm, width // cw),
        out_shape=(jax.ShapeDtypeStruct(into.shape, BF), jax.ShapeDtypeStruct((8, HEAD_DIM), F32)),
        in_specs=[dy_spec, placed, pl.BlockSpec((1, HEAD_DIM), lambda i, j: (0, 0)),
                  pl.BlockSpec((tm, HEAD_DIM), lambda i, j: (i, 0)), pl.BlockSpec((tm, HEAD_DIM), lambda i, j: (i, 0)), HBM_SPEC],
        out_specs=(placed, pl.BlockSpec((8, HEAD_DIM), lambda i, j: (0, 0))),
        input_output_aliases={5: 0}, compiler_params=_cparams("arbitrary", "arbitrary"),
    )(dy, proj, gain, ctab, stab, into)


def _flash_fwd(qr, k_all, vt_ctx, vt_lat, *, tq, side=None):
    n, aw = qr.shape
    s_len, kvw = k_all.shape
    l = vt_ctx.shape[-1]
    n_j, tk = vt_lat.shape[1], vt_lat.shape[-1]
    kvh = kvw // HEAD_DIM
    n_i = n // tq
    gw = GQA_GROUP * HEAD_DIM
    n_sin = len(side.ins) if side else 0
    n_sout = len(side.out_shapes) if side else 0

    def body(*refs):
        q_ref, k_ref, vc_ref, vl_ref = refs[:4]
        s_ins = refs[4:4 + n_sin]
        o_ref, lse_ref = refs[4 + n_sin:6 + n_sin]
        s_outs = refs[6 + n_sin:6 + n_sin + n_sout]
        acc_ref, m_ref, l_ref = refs[6 + n_sin + n_sout:9 + n_sin + n_sout]
        sems = refs[9 + n_sin + n_sout:]
        if side:
            first, middle, last = _side_at((kvh, n_i))
            pl.when(first)(lambda: side.start(s_ins, s_outs, sems))
            if side.mid is not None:
                pl.when(middle)(lambda: side.mid(s_ins, s_outs, sems))
        acc_ref[...] = jnp.zeros_like(acc_ref)
        l_ref[...] = jnp.zeros_like(l_ref)
        m_ref[...] = jnp.full(m_ref.shape, -1e30, F32)

        def tile(kj, vtj):
            for g in range(GQA_GROUP):
                st = _dot(kj, q_ref[:, g * HEAD_DIM:(g + 1) * HEAD_DIM], 1, 1)
                m_old = m_ref[g]
                m_new = jnp.maximum(m_old, jnp.max(st, axis=0, keepdims=True))
                alpha = jnp.exp2(m_old - m_new)
                pt = jnp.exp2(st - m_new)
                l_ref[g] = alpha * l_ref[g] + jnp.sum(pt, axis=0, keepdims=True)
                m_ref[g] = m_new
                acc_ref[g] = acc_ref[g] * alpha + _dot(vtj, pt.astype(BF), 1, 0)

        tile(k_ref[0:l, :], vc_ref[0, 0])

        def step(j, carry):
            tile(k_ref[pl.ds(pl.multiple_of(l + j * tk, min(l, tk)), tk), :], vl_ref[0, j])
            return carry

        lax.fori_loop(0, n_j, step, 0)
        for g in range(GQA_GROUP):
            o_ref[:, g * HEAD_DIM:(g + 1) * HEAD_DIM] = (acc_ref[g] / l_ref[g]).T.astype(BF)
            lse_ref[0, g, 0] = m_ref[g] + jnp.log(l_ref[g]) * LOG2E
        if side:
            pl.when(last)(lambda: side.finish(s_ins, s_outs, sems))

    return pl.pallas_call(
        body, name="flash_fwd", grid=(kvh, n_i),
        out_shape=(jax.ShapeDtypeStruct((n, aw), BF), jax.ShapeDtypeStruct((kvh, GQA_GROUP, n_i, 1, tq), F32),
                   *(side.out_shapes if side else [])),
        in_specs=[pl.BlockSpec((tq, gw), lambda h, i: (i, h)), pl.BlockSpec((s_len, HEAD_DIM), lambda h, i: (0, h)),
                  pl.BlockSpec((1, 1, HEAD_DIM, l), lambda h, i: (h, 0, 0, 0)),
                  pl.BlockSpec((1, n_j, HEAD_DIM, tk), lambda h, i: (h, 0, 0, 0))] + [HBM_SPEC] * n_sin,
        out_specs=(pl.BlockSpec((tq, gw), lambda h, i: (i, h)),
                   pl.BlockSpec((1, GQA_GROUP, 1, 1, tq), lambda h, i: (h, 0, i, 0, 0)), *([HBM_SPEC] * n_sout)),
        input_output_aliases={4 + i: 2 + o for i, o in side.aliases.items()} if side else {},
        scratch_shapes=[pltpu.VMEM((GQA_GROUP, HEAD_DIM, tq), F32), pltpu.VMEM((GQA_GROUP, 1, tq), F32),
                        pltpu.VMEM((GQA_GROUP, 1, tq), F32)] + (side.sem_shapes if side else []),
        compiler_params=_cparams("arbitrary", "arbitrary") if side else _cparams("parallel", "parallel"),
    )(qr, k_all, vt_ctx, vt_lat, *(side.ins if side else []))


def _attn_delta(o, do, *, tq):
    n, aw = o.shape
    kvh = aw // (GQA_GROUP * HEAD_DIM)
    gw = GQA_GROUP * HEAD_DIM

    def body(o_ref, do_ref, d_ref):
        for g in range(GQA_GROUP):
            cols = slice(g * HEAD_DIM, (g + 1) * HEAD_DIM)
            prod = o_ref[:, cols].astype(F32) * do_ref[:, cols].astype(F32)
            d_ref[0, g, 0] = jnp.sum(prod.T, axis=0, keepdims=True)

    return pl.pallas_call(
        body, name="attn_delta", grid=(kvh, n // tq),
        out_shape=jax.ShapeDtypeStruct((kvh, GQA_GROUP, n // tq, 1, tq), F32),
        in_specs=[pl.BlockSpec((tq, gw), lambda h, i: (i, h)), pl.BlockSpec((tq, gw), lambda h, i: (i, h))],
        out_specs=pl.BlockSpec((1, GQA_GROUP, 1, 1, tq), lambda h, i: (h, 0, i, 0, 0)),
        compiler_params=_cparams("parallel", "parallel"),
    )(o, do)


def _flash_bwd(qr, do, k_all, proj, lse, delta, into, *, v_col0, tq, tk, scale, side=None):
    n, aw = qr.shape
    s_len, kvw = k_all.shape
    kvh = kvw // HEAD_DIM
    n_i, n_j = n // tq, s_len // tk
    gw = GQA_GROUP * HEAD_DIM
    vb0 = v_col0 // HEAD_DIM
    n_sin = len(side.ins) if side else 0
    n_sout = len(side.out_shapes) if side else 0

    def body(*refs):
        q_ref, do_ref, k_ref, v_ref, lse_ref, dl_ref = refs[:6]
        s_ins = refs[7:7 + n_sin]
        dq_ref, dk_ref, dvo_ref = refs[7 + n_sin:10 + n_sin]
        s_outs = refs[10 + n_sin:10 + n_sin + n_sout]
        dv_ref = refs[10 + n_sin + n_sout]
        sems = refs[11 + n_sin + n_sout:]
        jj = pl.program_id(1)
        if side:
            first, _, last = _side_at((kvh, n_j))
            pl.when(first)(lambda: side.start(s_ins, s_outs, sems))

        @pl.when(jj == 0)
        def _():
            dq_ref[...] = jnp.zeros_like(dq_ref)

        kj = k_ref[...]
        vj = v_ref[...]
        kjt = kj.astype(F32).T.astype(BF)
        dk_ref[...] = jnp.zeros_like(dk_ref)
        dv_ref[...] = jnp.zeros_like(dv_ref)

        def step(i, carry):
            rows = pl.ds(pl.multiple_of(i * tq, tq), tq)
            dv_part = dk_part = None
            for g in range(GQA_GROUP):
                cols = slice(g * HEAD_DIM, (g + 1) * HEAD_DIM)
                qg = q_ref[rows, cols]
                dog = do_ref[rows, cols]
                pt = jnp.exp2(_dot(kj, qg, 1, 1) - lse_ref[0, g, i])
                dst = (pt * (_dot(vj, dog, 1, 1) - dl_ref[0, g, i])).astype(BF)
                dv_g = _dot(pt.astype(BF), dog, 1, 0)
                dk_g = _dot(dst, qg, 1, 0)
                dv_part = dv_g if dv_part is None else dv_part + dv_g
                dk_part = dk_g if dk_part is None else dk_part + dk_g
                dq_ref[0, i, g] += _dot(kjt, dst, 1, 0)
            dv_ref[...] += dv_part
            dk_ref[...] += dk_part
            return carry

        lax.fori_loop(0, n_i, step, 0)
        dk_ref[...] = dk_ref[...] * (1.0 / LOG2E)
        dvo_ref[...] = dv_ref[...].astype(BF)

        @pl.when(jj == n_j - 1)
        def _():
            dq_ref[...] = dq_ref[...] * scale

        if side:
            pl.when(last)(lambda: side.finish(s_ins, s_outs, sems))

    stat_spec = pl.BlockSpec((1, GQA_GROUP, n_i, 1, tq), lambda h, j: (h, 0, 0, 0, 0))
    kv_spec = pl.BlockSpec((tk, HEAD_DIM), lambda h, j: (j, h))
    v_spec = pl.BlockSpec((tk, HEAD_DIM), lambda h, j: (j, vb0 + h))
    q_spec = pl.BlockSpec((n, gw), lambda h, j: (0, h))
    dq_spec = pl.BlockSpec((1, n_i, GQA_GROUP, HEAD_DIM, tq), lambda h, j: (h, 0, 0, 0, 0))
    aliases = {6: 2}
    if side:
        aliases.update({7 + i: 3 + o for i, o in side.aliases.items()})
    return pl.pallas_call(
        body, name="flash_bwd", grid=(kvh, n_j),
        out_shape=(jax.ShapeDtypeStruct((kvh, n_i, GQA_GROUP, HEAD_DIM, tq), F32), jax.ShapeDtypeStruct((s_len, kvw), F32),
                   jax.ShapeDtypeStruct(into.shape, BF), *(side.out_shapes if side else [])),
        in_specs=[q_spec, q_spec, kv_spec, v_spec, stat_spec, stat_spec, HBM_SPEC] + [HBM_SPEC] * n_sin,
        out_specs=(dq_spec, kv_spec, v_spec, *([HBM_SPEC] * n_sout)),
        input_output_aliases=aliases,
        scratch_shapes=[pltpu.VMEM((tk, HEAD_DIM), F32)] + (side.sem_shapes if side else []),
        compiler_params=_cparams("arbitrary", "arbitrary"),
    )(qr, do, k_all, proj, lse, delta, into, *(side.ins if side else []))


def _pool_window(r, gi, l, n, tm):
    win = tm + 2 * POOL_HALO
    start = jnp.clip(l + r * tm - POOL_HALO, l, l + n - win)
    half = lax.shift_left(jnp.int32(1), gi)
    tok = r * tm + lax.broadcasted_iota(jnp.int32, (tm, win), 0)
    src = (start - l) + lax.broadcasted_iota(jnp.int32, (tm, win), 1)
    tok1 = r * tm + lax.broadcasted_iota(jnp.int32, (tm, 1), 0)
    cnt = (jnp.minimum(tok1 + half, n) - jnp.maximum(tok1 - half, 0)).astype(F32)
    return start, win, tok, src, half, cnt


def _pool_fwd(proj, pool_w, *, l, n, col0, tm):
    s_len = proj.shape[0]
    pg = pool_w.shape[-1]
    cb0 = col0 // pg
    assert col0 % pg == 0 and l % POOL_HALO == 0 and n >= tm + 2 * POOL_HALO

    def body(u_ref, w_ref, d_ref, po_ref):
        gi, r = pl.program_id(0), pl.program_id(1)
        start, win, tok, src, half, cnt = _pool_window(r, gi, l, n, tm)
        band = jnp.logical_and(src >= tok - half, src < tok + half).astype(BF)
        uw = u_ref[pl.ds(pl.multiple_of(start, POOL_HALO), win), :]
        ut = u_ref[pl.ds(pl.multiple_of(l + r * tm, POOL_HALO), tm), :].astype(F32)
        dv = (_dot(band, uw, 1, 0) / cnt - ut).astype(BF)
        d_ref[...] = dv
        po_ref[...] = _dot(dv, w_ref[0], 1, 0).astype(BF)

    return pl.pallas_call(
        body, name="pool_fwd", grid=(N_POOL_GROUPS, n // tm),
        out_shape=(jax.ShapeDtypeStruct((n, N_POOL_GROUPS * pg), BF), jax.ShapeDtypeStruct((n, N_POOL_GROUPS * pg), BF)),
        in_specs=[pl.BlockSpec((s_len, pg), lambda g, r: (0, cb0 + g)), pl.BlockSpec((1, pg, pg), lambda g, r: (g, 0, 0))],
        out_specs=(pl.BlockSpec((tm, pg), lambda g, r: (r, g)), pl.BlockSpec((tm, pg), lambda g, r: (r, g))),
        compiler_params=_cparams("parallel", "parallel"),
    )(proj, pool_w)


def _pool_bwd_map(dval, pool_w, *, n, col0, tm):
    pg = pool_w.shape[-1]
    cb0 = col0 // pg

    def body(d_ref, w_ref, o_ref):
        gi, r = pl.program_id(0), pl.program_id(1)
        half = lax.shift_left(jnp.int32(1), gi)
        tok1 = r * tm + lax.broadcasted_iota(jnp.int32, (tm, 1), 0)
        cnt = (jnp.minimum(tok1 + half, n) - jnp.maximum(tok1 - half, 0)).astype(F32)
        o_ref[...] = (_dot(d_ref[...], w_ref[0], 1, 1) / cnt).astype(BF)

    return pl.pallas_call(
        body, name="pool_bwd_map", grid=(N_POOL_GROUPS, n // tm),
        out_shape=jax.ShapeDtypeStruct((n, N_POOL_GROUPS * pg), BF),
        in_specs=[pl.BlockSpec((tm, pg), lambda g, r: (r, cb0 + g)), pl.BlockSpec((1, pg, pg), lambda g, r: (g, 0, 0))],
        out_specs=pl.BlockSpec((tm, pg), lambda g, r: (r, g)), compiler_params=_cparams("parallel", "parallel"),
    )(dval, pool_w)


def _pool_bwd_window(dds, into, *, n, tm, row0, col0):
    pg = dds.shape[1] // N_POOL_GROUPS
    rb0, cb0 = row0 // tm, col0 // pg
    assert row0 % tm == 0 and col0 % pg == 0

    def body(d_ref, into_ref, o_ref):
        gi, r = pl.program_id(0), pl.program_id(1)
        start, win, tok, src, half, cnt = _pool_window(r, gi, 0, n, tm)
        band = jnp.logical_and(tok >= src - half, tok < src + half).astype(BF)
        dw = d_ref[pl.ds(pl.multiple_of(start, POOL_HALO), win), :]
        dt = d_ref[pl.ds(pl.multiple_of(r * tm, POOL_HALO), tm), :].astype(F32)
        o_ref[...] = (_dot(band, dw, 1, 0) - dt * cnt).astype(BF)

    return pl.pallas_call(
        body, name="pool_bwd_window", grid=(N_POOL_GROUPS, n // tm), out_shape=jax.ShapeDtypeStruct(into.shape, BF),
        in_specs=[pl.BlockSpec((n, pg), lambda g, r: (0, g)), HBM_SPEC],
        out_specs=pl.BlockSpec((tm, pg), lambda g, r: (r + rb0, g + cb0)),
        input_output_aliases={1: 0}, compiler_params=_cparams("parallel", "parallel"),
    )(dds, into)


def _pool_wgrad(dmat, dval, *, col0, tk):
    n, pw = dmat.shape
    pg = pw // N_POOL_GROUPS
    cb0 = col0 // pg
    nk = n // tk

    def body(a_ref, b_ref, o_ref, acc):
        k = pl.program_id(1)

        @pl.when(k == 0)
        def _():
            acc[...] = jnp.zeros_like(acc)

        acc[...] += _dot(a_ref[...], b_ref[...], 0, 0)

        @pl.when(k == nk - 1)
        def _():
            o_ref[0] = acc[...].astype(BF)

    return pl.pallas_call(
        body, name="pool_wgrad", grid=(N_POOL_GROUPS, nk), out_shape=jax.ShapeDtypeStruct((N_POOL_GROUPS, pg, pg), BF),
        in_specs=[pl.BlockSpec((tk, pg), lambda g, k: (k, g)), pl.BlockSpec((tk, pg), lambda g, k: (k, cb0 + g))],
        out_specs=pl.BlockSpec((1, pg, pg), lambda g, k: (g, 0, 0)), scratch_shapes=[pltpu.VMEM((pg, pg), F32)],
        compiler_params=_cparams("parallel", "arbitrary"),
    )(dmat, dval)


def _merge(attn_o, po_raw, proj, pool_scale, *, l, ga0, gp0, tm):
    n, aw = attn_o.shape
    bw = aw // 2
    rb0 = l // tm
    ga_b, gp_b = ga0 // bw, gp0 // bw
    assert ga0 % bw == 0 and gp0 % bw == 0 and l % tm == 0

    def body(a_ref, p_ref, g_ref, ps_ref, y_ref):
        cb = pl.program_id(1)
        sg, _ = _silu_and_grad(g_ref[...].astype(F32))

        @pl.when(cb < 2)
        def _():
            y_ref[...] = (a_ref[...].astype(F32) * sg).astype(BF)

        @pl.when(cb >= 2)
        def _():
            y_ref[...] = (p_ref[...].astype(F32) * ps_ref[...] * sg).astype(BF)

    return pl.pallas_call(
        body, name="merge", grid=(n // tm, 4), out_shape=jax.ShapeDtypeStruct((n, 2 * aw), BF),
        in_specs=[pl.BlockSpec((tm, bw), lambda i, cb: (i, jnp.minimum(cb, 1))),
                  pl.BlockSpec((tm, bw), lambda i, cb: (i, jnp.maximum(cb - 2, 0))),
                  pl.BlockSpec((tm, bw), lambda i, cb: (i + rb0, jnp.where(cb < 2, ga_b + cb, gp_b + cb - 2))),
                  pl.BlockSpec((1, bw), lambda i, cb: (0, jnp.maximum(cb - 2, 0)))],
        out_specs=pl.BlockSpec((tm, bw), lambda i, cb: (i, cb)), compiler_params=_cparams("parallel", "arbitrary"),
    )(attn_o, po_raw, proj, pool_scale)


def _merge_bwd(dy, attn_o, po_raw, proj, pool_scale, into, *, l, ga0, gp0, tm):
    n, aw = attn_o.shape
    bw = aw // 2
    rb0 = l // tm
    ga_b, gp_b = ga0 // bw, gp0 // bw

    def body(dy_ref, a_ref, p_ref, g_ref, ps_ref, into_ref, dv_ref, dg_ref, dps_ref):
        cb, i = pl.program_id(0), pl.program_id(1)
        sg, sgrad = _silu_and_grad(g_ref[...].astype(F32))
        dyv = dy_ref[...].astype(F32)

        @pl.when(cb < 2)
        def _():
            dv_ref[...] = (dyv * sg).astype(BF)
            dg_ref[...] = (dyv * a_ref[...].astype(F32) * sgrad).astype(BF)

        @pl.when(cb >= 2)
        def _():
            @pl.when(i == 0)
            def _():
                dps_ref[...] = jnp.zeros_like(dps_ref)

            pr = p_ref[...].astype(F32)
            dpo = dyv * sg
            dv_ref[...] = (dpo * ps_ref[...]).astype(BF)
            dg_ref[...] = (dyv * (pr * ps_ref[...]) * sgrad).astype(BF)
            dps_ref[0:1, :] += jnp.sum(dpo * pr, axis=0, keepdims=True)

    blk = pl.BlockSpec((tm, bw), lambda cb, i: (i, cb))
    gate_blk = pl.BlockSpec((tm, bw), lambda cb, i: (i + rb0, jnp.where(cb < 2, ga_b + cb, gp_b + cb - 2)))
    return pl.pallas_call(
        body, name="merge_bwd", grid=(4, n // tm),
        out_shape=(jax.ShapeDtypeStruct((n, 2 * aw), BF), jax.ShapeDtypeStruct(into.shape, BF),
                   jax.ShapeDtypeStruct((8, aw), F32)),
        in_specs=[blk, pl.BlockSpec((tm, bw), lambda cb, i: (i, jnp.minimum(cb, 1))),
                  pl.BlockSpec((tm, bw), lambda cb, i: (i, jnp.maximum(cb - 2, 0))), gate_blk,
                  pl.BlockSpec((1, bw), lambda cb, i: (0, jnp.maximum(cb - 2, 0))), HBM_SPEC],
        out_specs=(blk, gate_blk, pl.BlockSpec((8, bw), lambda cb, i: (0, jnp.maximum(cb - 2, 0)))),
        input_output_aliases={5: 1}, compiler_params=_cparams("arbitrary", "arbitrary"),
    )(dy, attn_o, po_raw, proj, pool_scale, into)


def _post(x, out, target, gate, g_post, *, tm):
    n, d = x.shape

    def body(x_ref, o_ref, t_ref, gate_ref, gp_ref, dxn_ref, dout_ref, st_ref):
        @pl.when(pl.program_id(0) == 0)
        def _():
            st_ref[...] = jnp.zeros_like(st_ref)

        ov = o_ref[...]
        rstd = lax.rsqrt(jnp.mean(ov * ov, axis=-1, keepdims=True) + EPS)
        on = ov * rstd
        rn = on * gp_ref[...]
        err = (x_ref[...] + gate_ref[...] * rn) - t_ref[...]
        dxn = err / d
        dxn_ref[...] = dxn
        drn = dxn * gate_ref[...]
        don = drn * gp_ref[...]
        dout_ref[...] = (rstd * (don - on * jnp.mean(don * on, axis=-1, keepdims=True))).astype(BF)
        st_ref[0:1, :] += jnp.sum(dxn * rn, axis=0, keepdims=True)
        st_ref[1:2, :] += jnp.sum(drn * on, axis=0, keepdims=True)
        st_ref[2:3, :] += jnp.sum(err * err, axis=0, keepdims=True)

    row = pl.BlockSpec((tm, d), lambda i: (i, 0))
    vec = pl.BlockSpec((1, d), lambda i: (0, 0))
    return pl.pallas_call(
        body, name="post", grid=(n // tm,),
        out_shape=(jax.ShapeDtypeStruct((n, d), F32), jax.ShapeDtypeStruct((n, d), BF), jax.ShapeDtypeStruct((8, d), F32)),
        in_specs=[row, row, row, vec, vec], out_specs=(row, row, pl.BlockSpec((8, d), lambda i: (0, 0))),
        compiler_params=_cparams("arbitrary"),
    )(x, out, target, gate, g_post)


def _prenorm_bwd(x, ctx, dh, dxn, g_pre, ss, *, tm):
    n, d = x.shape
    l = ctx.shape[0]
    nc = l // tm

    def body(x_ref, c_ref, dh_ref, dxn_ref, g_ref, ss_ref, gx_ref, st_ref):
        t = pl.program_id(0)

        @pl.when(t == 0)
        def _():
            st_ref[...] = jnp.zeros_like(st_ref)

        def go(src, scale, row):
            v = src[...]
            dhv = dh_ref[...].astype(F32)
            rstd = lax.rsqrt(jnp.mean(v * v, axis=-1, keepdims=True) + EPS)
            xn = v * rstd
            st_ref[row:row + 1, :] += jnp.sum(dhv, axis=0, keepdims=True)
            st_ref[row + 1:row + 2, :] += jnp.sum(dhv * (xn * g_ref[...]), axis=0, keepdims=True)
            dg = dhv * (1.0 + scale)
            st_ref[4:5, :] += jnp.sum(dg * xn, axis=0, keepdims=True)
            dxn_ = dg * g_ref[...]
            return rstd * (dxn_ - xn * jnp.mean(dxn_ * xn, axis=-1, keepdims=True))

        @pl.when(t < nc)
        def _():
            go(c_ref, ss_ref[2:3, :], 2)

        @pl.when(t >= nc)
        def _():
            gx_ref[...] = dxn_ref[...] + go(x_ref, ss_ref[0:1, :], 0)

    lat = pl.BlockSpec((tm, d), lambda t: (jnp.maximum(t - nc, 0), 0))
    return pl.pallas_call(
        body, name="prenorm_bwd", grid=((l + n) // tm,),
        out_shape=(jax.ShapeDtypeStruct((n, d), F32), jax.ShapeDtypeStruct((8, d), F32)),
        in_specs=[lat, pl.BlockSpec((tm, d), lambda t: (jnp.minimum(t, nc - 1), 0)),
                  pl.BlockSpec((tm, d), lambda t: (t, 0)), lat,
                  pl.BlockSpec((1, d), lambda t: (0, 0)), pl.BlockSpec((4, d), lambda t: (0, 0))],
        out_specs=(lat, pl.BlockSpec((8, d), lambda t: (0, 0))), compiler_params=_cparams("arbitrary"),
    )(x, ctx, dh, dxn, g_pre, ss)


def _ada_fwd(craw, w_ada, b_loc, *, tn):
    d, wn = w_ada.shape

    def body(c_ref, w_ref, b_ref, o_ref):
        act, _ = _silu_and_grad(c_ref[...])
        o_ref[...] = _dot(act.astype(BF), w_ref[...].astype(BF), 1, 0) + b_ref[...]

    return pl.pallas_call(
        body, name="ada_fwd", grid=(wn // tn,), out_shape=jax.ShapeDtypeStruct((16, wn), F32),
        in_specs=[pl.BlockSpec((16, d), lambda j: (0, 0)), pl.BlockSpec((d, tn), lambda j: (0, j)),
                  pl.BlockSpec((1, tn), lambda j: (0, j))],
        out_specs=pl.BlockSpec((16, tn), lambda j: (0, j)), compiler_params=_cparams("parallel"),
    )(craw, w_ada, b_loc)


def _ada_bwd(craw_t, dm, w, m, v, *, tm):
    d, wn = w.shape

    def body(ct_ref, dm_ref, w_ref, m_ref, v_ref, g_ref, dl_ref, nm_ref, nv_ref, da_ref):
        act, _ = _silu_and_grad(ct_ref[...])
        dmb = dm_ref[...].astype(BF)
        wv = w_ref[...]
        gv = _dot(act.astype(BF), dmb, 1, 0)
        da_ref[...] = _dot(dmb, wv.astype(BF), 1, 1)
        nm = ADAM_B1 * m_ref[...] + (1.0 - ADAM_B1) * gv
        nv = ADAM_B2 * v_ref[...] + (1.0 - ADAM_B2) * (gv * gv)
        m_hat = nm / (1.0 - ADAM_B1 ** ADAM_STEP)
        v_hat = nv / (1.0 - ADAM_B2 ** ADAM_STEP)
        g_ref[...] = gv
        dl_ref[...] = -ADAM_LR * (m_hat / (jnp.sqrt(v_hat) + ADAM_EPS) + ADAM_WD * wv)
        nm_ref[...] = nm
        nv_ref[...] = nv

    row = pl.BlockSpec((tm, wn), lambda i: (i, 0))
    return pl.pallas_call(
        body, name="ada_bwd", grid=(d // tm,),
        out_shape=tuple([jax.ShapeDtypeStruct((d, wn), F32)] * 4) + (jax.ShapeDtypeStruct((16, d), F32),),
        in_specs=[pl.BlockSpec((tm, 16), lambda i: (i, 0)), pl.BlockSpec((16, wn), lambda i: (0, 0)), row, row, row],
        out_specs=(row, row, row, row, pl.BlockSpec((16, tm), lambda i: (0, i))),
        compiler_params=_cparams("parallel"),
    )(craw_t, dm, w, m, v)


def _reduce_small(gath, d3):
    t = gath.shape[-1]

    def body(g_ref, dm_ref, s_ref):
        tot = g_ref[0]
        for b in range(1, 8):
            tot = tot + g_ref[b]
        s_ref[...] = tot
        dm_ref[...] = jnp.zeros_like(dm_ref)
        for b in range(8):
            dm_ref[b:b + 1, :] = g_ref[b][:, 0:d3]
        dm_ref[8:9, :] = tot[:, d3:2 * d3]

    return pl.pallas_call(
        body, name="reduce_small", out_shape=(jax.ShapeDtypeStruct((16, d3), F32), jax.ShapeDtypeStruct((1, t), F32)),
        in_specs=[VMEM_SPEC], out_specs=(VMEM_SPEC, VMEM_SPEC),
    )(gath)


def _cctx_grad(parts, c_ctx):
    def body(p_ref, c_ref, o_ref):
        tot = (p_ref[0] + p_ref[1]) + (p_ref[2] + p_ref[3])
        _, sgrad = _silu_and_grad(c_ref[...])
        o_ref[...] = tot * sgrad

    return pl.pallas_call(
        body, name="cctx_grad", out_shape=jax.ShapeDtypeStruct(c_ctx.shape, F32),
        in_specs=[VMEM_SPEC, VMEM_SPEC], out_specs=VMEM_SPEC,
    )(parts, c_ctx)


def _rope_tables(n, l):
    rows = n // GRID_W
    row = jnp.repeat(jnp.arange(rows, dtype=F32), GRID_W)
    col = jnp.tile(jnp.arange(GRID_W, dtype=F32), rows)
    inv = ROPE_THETA ** (-jnp.arange(ROPE_PAIRS, dtype=F32) / ROPE_PAIRS)
    ang = jnp.concatenate([row[:, None] * inv, col[:, None] * inv], axis=-1)
    cos, sin = jnp.cos(ang), jnp.sin(ang)
    cr, cc, sr, sc = cos[:, :ROPE_PAIRS], cos[:, ROPE_PAIRS:], sin[:, :ROPE_PAIRS], sin[:, ROPE_PAIRS:]
    ctab = jnp.concatenate([cr, cr, cc, cc], axis=-1)
    stab = jnp.concatenate([-sr, sr, -sc, sc], axis=-1)
    ctab = jnp.concatenate([jnp.ones((l, HEAD_DIM), F32), ctab], axis=0)
    stab = jnp.concatenate([jnp.zeros((l, HEAD_DIM), F32), stab], axis=0)
    return ctab, stab


def kernel(x, c, ctx, c_ctx, w_ada, b_ada, norm_pre, norm_post, w_in, q_norm, k_norm, pool_w, pool_scale, w_out, loss_target, m_c_ctx, m_w_ada, m_b_ada, m_norm_pre, m_norm_post, m_w_in, m_q_norm, m_k_norm, m_pool_w, m_pool_scale, m_w_out, v_c_ctx, v_w_ada, v_b_ada, v_norm_pre, v_norm_post, v_w_in, v_q_norm, v_k_norm, v_pool_w, v_pool_scale, v_w_out):
    x2, ctx2, tgt = x[0], ctx[0], loss_target[0]
    n, d = x2.shape
    l = ctx2.shape[0]
    s_len = l + n
    aw = d // 2
    pw = d - aw
    n_heads = aw // HEAD_DIM
    kvw = (n_heads // GQA_GROUP) * HEAD_DIM
    pg = pw // N_POOL_GROUPS
    k0, v0, ga0 = aw, aw + kvw, aw + 2 * kvw
    up0, gp0 = ga0 + aw, ga0 + aw + pw
    in_w = gp0 + pw
    d3 = 3 * d
    ada_w = w_ada.shape[-1]
    px, py, pc = lax.axis_index("x"), lax.axis_index("y"), lax.axis_index("c")
    chip = 2 * px + py
    tr = min(256, l)
    tq = min(1024, n)
    tk = min(256, l)
    ts = s_len // 8 if (s_len // 8) % 16 == 0 else tr
    scale = HEAD_DIM ** -0.5

    c_all = _allgather_small(c, chips_only=False, name="gather_c").reshape(8, d)
    craw = jnp.concatenate([c_all, c_ctx.reshape(1, d), jnp.zeros((7, d), F32)], axis=0)
    b_loc = lax.dynamic_slice(b_ada, (0, chip * ada_w), (1, ada_w))
    mod_part = _ada_fwd(craw, w_ada[0], b_loc, tn=min(512, ada_w))
    mod_all = _allgather_small(mod_part, chips_only=True, name="gather_mod")
    mod_all = jnp.transpose(mod_all, (1, 0, 2)).reshape(16, d3)
    me = 4 * px + 2 * py + pc
    mod_b = lax.dynamic_slice(mod_all, (me, 0), (1, d3))
    mod_c = mod_all[8:9]
    ss = jnp.concatenate([mod_b[:, d:2 * d], mod_b[:, 0:d], mod_c[:, d:2 * d], mod_c[:, 0:d]], axis=0)
    gate = mod_b[:, 2 * d:]

    w_own = _cast_bf16(w_in[0], name="cast_w_in")
    later_weights = _gather_side([_cast_into(w_out[0], 0, chip, name="cast_w_out"),
                                  _cast_into(pool_w[0], 1, chip, name="cast_pool_w")], [(0, 1), (1, 2)])
    ids = jnp.stack([chip, jnp.bitwise_xor(chip, 2), jnp.bitwise_xor(chip, 1), jnp.bitwise_xor(chip, 3)])
    ws = in_w // 4
    tn_p = ws // 3 if (ws // 3) % HEAD_DIM == 0 else ws

    ctab, stab = _rope_tables(n, l)
    proj, h, w_x, w_y = _in_proj_own(x2, ctx2, norm_pre, ss, w_own, ids[0:1], width=in_w, tm=tr, tn=tn_p,
                                     side=_gather_parts_side(w_own, (0, 1)))
    proj, w_d = _in_proj_parts(h, [w_x, w_y], ids[1:3], proj, width=in_w, tm=ts, tn=tn_p, name="in_proj_xy",
                               side=_gather_parts_side(w_own, (2,)))
    (proj,) = _in_proj_parts(h, [w_d], ids[3:4], proj, width=in_w, tm=ts, tn=tn_p, name="in_proj_diag")
    w_parts = [w_own, w_x, w_y, w_d]
    qr = _qk_prep(proj, q_norm, ctab[l:], stab[l:], row0=l, col0=0, width=aw, nrows=n, tm=tr, name="q_prep",
                  out_scale=scale * LOG2E)
    kr = _qk_prep(proj, k_norm, ctab, stab, row0=0, col0=k0, width=kvw, nrows=s_len, tm=tr, name="k_prep")
    v_all = proj[:, v0:v0 + kvw]
    tkf = min(4096, n)
    vt_ctx = jnp.transpose(v_all[:l].reshape(1, l, kvw // HEAD_DIM, HEAD_DIM), (2, 0, 3, 1))
    vt_lat = jnp.transpose(v_all[l:].reshape(n // tkf, tkf, kvw // HEAD_DIM, HEAD_DIM), (2, 0, 3, 1))
    attn_o, lse, w_out_f, pool_f = _flash_fwd(qr, kr, vt_ctx, vt_lat, tq=tq, side=later_weights)
    dmat, po_raw = _pool_fwd(proj, pool_f, l=l, n=n, col0=up0, tm=min(512, n // 2))
    yv = _merge(attn_o, po_raw, proj, pool_scale, l=l, ga0=ga0, gp0=gp0, tm=tr)
    out = _matmul(yv, w_out_f, ca=1, cb=0, tm=min(512, n), tn=min(1024, d), tk=d, out_dtype=F32, name="out_proj")

    dxn, d_out, post_st = _post(x2, out, tgt, gate, norm_post, tm=min(128, n))
    loss = 0.5 * jnp.sum(post_st[2]) / d
    loss = lax.psum(loss, ("x", "y", "c"))
    g_wout = _matmul(yv, d_out, ca=0, cb=0, tm=min(512, d), tn=min(1024, d), tk=n, out_dtype=BF, name="wgrad_out")
    dy, from_sib_wout = _matmul(d_out, w_out_f, ca=1, cb=1, tm=min(512, n), tn=min(1024, d), tk=d, out_dtype=BF,
                                name="dgrad_out", side=_pair_side([g_wout], [1]))
    pair_wout = _add_own_half(g_wout, from_sib_wout, 1, pc, name="pair_sum_1")
    dproj = _zero_rows(l, (s_len, in_w), name="dproj_init")
    dval, dproj, dps = _merge_bwd(dy, attn_o, po_raw, proj, pool_scale, dproj, l=l, ga0=ga0, gp0=gp0, tm=tr)
    g_pool = _pool_wgrad(dmat, dval, col0=aw, tk=min(512, n))
    dds = _pool_bwd_map(dval, pool_f, n=n, col0=aw, tm=min(1024, n))
    dproj = _pool_bwd_window(dds, dproj, n=n, tm=tr, row0=l, col0=up0)
    delta = _attn_delta(attn_o, dval, tq=tq)
    tqb = min(4096, n)
    stat_shape = (kvw // HEAD_DIM, GQA_GROUP, n // tqb, 1, tqb)
    dq, dk, dproj, *from_chips_wout = _flash_bwd(
        qr, dval, kr, proj, lse.reshape(stat_shape), delta.reshape(stat_shape), dproj, v_col0=v0, tq=tqb, tk=tk,
        scale=scale, side=_chips_side([pair_wout], [0]))
    red_wout = _sum_own_block(pair_wout, from_chips_wout, 0, 1, chip, pc, name="chip_sum_1")
    dproj, dgq = _qk_bwd(dq, proj, q_norm, ctab[l:], stab[l:], dproj, row0=l, col0=0, tm=tr, name="q_bwd", transposed=True)
    dproj, dgk = _qk_bwd(dk, proj, k_norm, ctab, stab, dproj, row0=0, col0=k0, tm=tr, name="k_bwd")
    tn_w = 1024 if in_w % 1024 == 0 else 512
    g_send, g_wout_s = _wgrad_rows(h, dproj, 1 - pc, tm=min(512, d // 2), tn=tn_w, name="wgrad_in_send",
                                   side=_join_side([red_wout], [1]))
    g_keep, from_sib_win, from_sib_pool = _wgrad_rows(
        h, dproj, pc, tm=min(512, d // 2), tn=tn_w, name="wgrad_in_keep",
        side=_both(_swap_side([g_send]), _pair_side([g_pool], [2])))
    pair_win = _add2(g_keep, from_sib_win, name="pair_sum_0")
    pair_pool = _add_own_half(g_pool, from_sib_pool, 2, pc, name="pair_sum_2")
    dh, *from_chips = _dgrad_in_parts(dproj, w_parts, ids, tm=ts, tn=min(512, d), name="dgrad_in",
                                      side=_chips_side([pair_win, pair_pool], [1, 1]))
    red_win = _sum_own_block(pair_win, from_chips[0:3], 1, 0, chip, pc, name="chip_sum_0")
    red_pool = _sum_own_block(pair_pool, from_chips[3:6], 1, 2, chip, pc, name="chip_sum_2")
    grad_x, pre_st = _prenorm_bwd(x2, ctx2, dh, dxn, norm_pre, ss, tm=min(128, l))

    zero_d = jnp.zeros((1, d), F32)
    packed = jnp.concatenate([pre_st[0:1], pre_st[1:2], post_st[0:1], pre_st[2:3], pre_st[3:4], zero_d,
                              pre_st[4:5], post_st[1:2], dgq[0:1], dgk[0:1], dps[0:1]], axis=1)
    gath, g_win_s, g_pool_s = _allgather_small(packed, chips_only=False, name="gather_small",
                                               side=_join_side([red_win, red_pool], [0, 2]))
    dm, sums = _reduce_small(gath, d3)
    o1 = 2 * d3
    g_npre, g_npost = sums[:, o1:o1 + d], sums[:, o1 + d:o1 + 2 * d]
    g_q, g_k = sums[:, o1 + 2 * d:o1 + 2 * d + HEAD_DIM], sums[:, o1 + 2 * d + HEAD_DIM:o1 + 2 * d + 2 * HEAD_DIM]
    g_ps = sums[:, o1 + 2 * d + 2 * HEAD_DIM:]
    g_bada = sums[:, 0:d3] + sums[:, d3:2 * d3]
    dm_loc = lax.dynamic_slice(dm, (0, chip * ada_w), (16, ada_w))
    g_wada, dl_wada, nm_wada, nv_wada, dact = _ada_bwd(craw.T, dm_loc, w_ada[0], m_w_ada[0], v_w_ada[0], tm=128)
    cparts = _allgather_small(dact[8:9], chips_only=True, name="gather_cctx")
    g_cctx = _cctx_grad(cparts, c_ctx.reshape(1, d)).reshape(d)

    def upd(w, g, m, v, name):
        return _adamw(w, g.reshape(w.shape), m, v, name=name)

    grads = {"c_ctx": g_cctx, "b_ada": g_bada, "norm_pre": g_npre, "norm_post": g_npost, "w_in": g_win_s[None],
             "q_norm": g_q, "k_norm": g_k, "pool_w": g_pool_s[None], "pool_scale": g_ps, "w_out": g_wout_s[None]}
    res = {"w_ada": (g_wada[None], dl_wada[None], nm_wada[None], nv_wada[None])}
    given = {"c_ctx": (c_ctx, m_c_ctx, v_c_ctx), "b_ada": (b_ada, m_b_ada, v_b_ada),
             "norm_pre": (norm_pre, m_norm_pre, v_norm_pre), "norm_post": (norm_post, m_norm_post, v_norm_post),
             "w_in": (w_in, m_w_in, v_w_in), "q_norm": (q_norm, m_q_norm, v_q_norm), "k_norm": (k_norm, m_k_norm, v_k_norm),
             "pool_w": (pool_w, m_pool_w, v_pool_w), "pool_scale": (pool_scale, m_pool_scale, v_pool_scale),
             "w_out": (w_out, m_w_out, v_w_out)}
    for nme, (w, m, v) in given.items():
        g = grads[nme].reshape(w.shape)
        res[nme] = upd(w, g, m, v, "adamw_" + nme)
    order = ["c_ctx", "w_ada", "b_ada", "norm_pre", "norm_post", "w_in", "q_norm", "k_norm", "pool_w", "pool_scale", "w_out"]
    return (loss, grad_x[None], *[res[k][0] for k in order], *[res[k][1] for k in order],
            *[res[k][2] for k in order], *[res[k][3] for k in order])
```

```python
import functools

import jax
import jax.numpy as jnp
from jax import lax
from jax.experimental import pallas as pl
from jax.experimental.pallas import tpu as pltpu

F32 = jnp.float32
BF = jnp.bfloat16
MESH = pl.DeviceIdType.MESH

HEAD_DIM = 128
GQA_GROUP = 4
FWD_HEADS = 2
GRID_W = 64
ROPE_PAIRS = HEAD_DIM // 4
ROPE_THETA = 10000.0
EPS = 1e-6
N_POOL_GROUPS = 4
POOL_HALO = 128
ADAM_LR = 0.001
ADAM_B1 = 0.9
ADAM_B2 = 0.999
ADAM_EPS = 1e-08
ADAM_WD = 0.01
ADAM_STEP = 10
LOG2E = 1.4426950408889634
MIB = 2 ** 20
VMEM_LIMIT = 48 * MIB
CHIP_MASKS = ((1, 0, 0), (0, 1, 0), (1, 1, 0))
ALL_MASKS = ((0, 0, 1), (0, 1, 0), (0, 1, 1), (1, 0, 0), (1, 0, 1), (1, 1, 0), (1, 1, 1))
HBM_SPEC = pl.BlockSpec(memory_space=pl.ANY)
VMEM_SPEC = pl.BlockSpec(memory_space=pltpu.VMEM)


def _cparams(*sem):
    return pltpu.CompilerParams(dimension_semantics=sem, vmem_limit_bytes=VMEM_LIMIT)


def _sigmoid(v):
    return 0.5 * jnp.tanh(0.5 * v) + 0.5


def _silu_and_grad(v):
    s = _sigmoid(v)
    return v * s, s * (1.0 + v * (1.0 - s))


def _dot(a, b, ca, cb):
    return lax.dot_general(a, b, (((ca,), (cb,)), ((), ())), preferred_element_type=F32)


def _block_rows(rows, width, itemsize=4, target=MIB):
    best = 8
    for t in range(8, rows + 1, 8):
        if rows % t == 0 and t * width * itemsize <= target:
            best = t
    return best if rows % 8 == 0 else rows


def _my_pos():
    return lax.axis_index("x"), lax.axis_index("y"), lax.axis_index("c")


def _flip(pos, mask):
    return tuple(jnp.bitwise_xor(p, m) if m else p for p, m in zip(pos, mask))


def _allgather_small(v, *, chips_only, name, side=None):
    r, w = v.shape
    masks = CHIP_MASKS if chips_only else ALL_MASKS
    nslot = 4 if chips_only else 8
    n_sin = len(side.ins) if side else 0
    n_sout = len(side.out_shapes) if side else 0

    def slot(pos):
        return 2 * pos[0] + pos[1] if chips_only else 4 * pos[0] + 2 * pos[1] + pos[2]

    def body(v_ref, *refs):
        s_ins, o_ref = refs[:n_sin], refs[n_sin]
        s_outs = refs[n_sin + 1:n_sin + 1 + n_sout]
        send_sems, recv_sems, local_sem = refs[n_sin + 1 + n_sout:n_sin + 4 + n_sout]
        s_sems = refs[n_sin + 4 + n_sout:]
        if side:
            side.start(s_ins, s_outs, s_sems)
        me = _my_pos()

        def copy(k, block_of, to):
            return pltpu.make_async_remote_copy(
                src_ref=v_ref, dst_ref=o_ref.at[slot(block_of)], send_sem=send_sems.at[k], recv_sem=recv_sems.at[k],
                device_id=to, device_id_type=MESH)

        mine = pltpu.make_async_copy(v_ref, o_ref.at[slot(me)], local_sem)
        mine.start()
        sends = [copy(k, me, _flip(me, m)) for k, m in enumerate(masks)]
        for cp in sends:
            cp.start()
        for k, m in enumerate(masks):
            copy(k, _flip(me, m), me).wait_recv()
        for cp in sends:
            cp.wait_send()
        mine.wait()
        if side:
            side.finish(s_ins, s_outs, s_sems)

    o_shape = jax.ShapeDtypeStruct((nslot, r, w), v.dtype)
    sems = [pltpu.SemaphoreType.DMA((len(masks),)), pltpu.SemaphoreType.DMA((len(masks),)), pltpu.SemaphoreType.DMA]
    if not side:
        return pl.pallas_call(body, name=name, out_shape=o_shape, in_specs=[VMEM_SPEC], out_specs=VMEM_SPEC,
                              scratch_shapes=sems)(v)
    return pl.pallas_call(
        body, name=name, out_shape=(o_shape, *side.out_shapes),
        in_specs=[VMEM_SPEC] + [HBM_SPEC] * n_sin, out_specs=(VMEM_SPEC, *([HBM_SPEC] * n_sout)),
        input_output_aliases={1 + i: 1 + o for i, o in side.aliases.items()},
        scratch_shapes=sems + side.sem_shapes,
    )(v, *side.ins)


def _sl(ref, axis, start, size):
    idx = [slice(None)] * len(ref.shape)
    idx[axis] = pl.ds(start, size)
    return ref.at[tuple(idx)]


class _Side:
    def __init__(self, ins, out_shapes, aliases, sem_shapes, start, finish, mid=None):
        self.ins, self.out_shapes, self.aliases, self.sem_shapes = list(ins), list(out_shapes), dict(aliases), list(sem_shapes)
        self.start, self.finish, self.mid = start, finish, mid


def _run_side(side, *, name):
    n_in, n_out = len(side.ins), len(side.out_shapes)

    def body(*refs):
        parts = refs[:n_in], refs[n_in:n_in + n_out], refs[n_in + n_out:]
        side.start(*parts)
        if side.mid is not None:
            side.mid(*parts)
        side.finish(*parts)

    return pl.pallas_call(
        body, name=name, out_shape=tuple(side.out_shapes), in_specs=[HBM_SPEC] * n_in, out_specs=tuple([HBM_SPEC] * n_out),
        input_output_aliases=side.aliases, scratch_shapes=side.sem_shapes,
    )(*side.ins)


def _side_at(grid):
    pids = [pl.program_id(a) for a in range(len(grid))]

    def at(values):
        cond = pids[0] == values[0]
        for p, v in zip(pids[1:], values[1:]):
            cond = jnp.logical_and(cond, p == v)
        return cond

    zeros = [0] * len(grid)
    return at(zeros), at([grid[0] // 2] + zeros[1:]), at([g - 1 for g in grid])


def _gather_side(shards, layouts):
    nm = len(shards)

    def copies(outs, sems):
        send_sems, recv_sems, fsend_sems, frecv_sems = sems
        x, y, c = _my_pos()
        me = (x, y, c)
        sib = (x, y, 1 - c)
        sends, landed, fwds, from_sib = [], [], [], []
        for m in range(nm):
            a_s, a_h = layouts[m]
            ns, nh = outs[m].shape[a_s] // 4, outs[m].shape[a_h] // 2

            def region(chip_pos, half, m=m, a_s=a_s, a_h=a_h, ns=ns, nh=nh):
                j = 2 * chip_pos[0] + chip_pos[1]
                return _sl(_sl(outs[m], a_s, j * ns, ns), a_h, half * nh, nh)

            for k, mask in enumerate(CHIP_MASKS):
                other = _flip(me, mask)

                def rc(ref, ssem, rsem, to):
                    return pltpu.make_async_remote_copy(src_ref=ref, dst_ref=ref, send_sem=ssem, recv_sem=rsem,
                                                        device_id=to, device_id_type=MESH)

                sends.append(rc(region(me, c), send_sems.at[m, k], recv_sems.at[m, k], other))
                landed.append(rc(region(other, c), send_sems.at[m, k], recv_sems.at[m, k], me))
                fwds.append(rc(region(other, c), fsend_sems.at[m, k], frecv_sems.at[m, k], sib))
                from_sib.append(rc(region(other, 1 - c), fsend_sems.at[m, k], frecv_sems.at[m, k], sib))
        return sends, landed, fwds, from_sib

    def start(ins, outs, sems):
        for cp in copies(outs, sems)[0]:
            cp.start()

    def mid(ins, outs, sems):
        _, landed, fwds, _ = copies(outs, sems)
        for arrived, fw in zip(landed, fwds):
            arrived.wait_recv()
            fw.start()

    def finish(ins, outs, sems):
        sends, _, fwds, from_sib = copies(outs, sems)
        for cp in from_sib:
            cp.wait_recv()
        for cp in sends + fwds:
            cp.wait_send()

    return _Side(shards, [jax.ShapeDtypeStruct(s.shape, s.dtype) for s in shards], {i: i for i in range(nm)},
                 [pltpu.SemaphoreType.DMA((nm, 3))] * 4, start, finish, mid)


def _both(s1, s2):
    n1, m1, k1 = len(s1.ins), len(s1.out_shapes), len(s1.sem_shapes)

    def split(fn1, fn2):
        def run(ins, outs, sems):
            fn1(ins[:n1], outs[:m1], sems[:k1])
            fn2(ins[n1:], outs[m1:], sems[k1:])
        return run

    aliases = dict(s1.aliases)
    aliases.update({n1 + i: m1 + o for i, o in s2.aliases.items()})
    return _Side(s1.ins + s2.ins, s1.out_shapes + s2.out_shapes, aliases, s1.sem_shapes + s2.sem_shapes,
                 split(s1.start, s2.start), split(s1.finish, s2.finish))


def _swap_side(mats):
    nm = len(mats)

    def copies(ins, outs, sems):
        x, y, c = _my_pos()
        return [pltpu.make_async_remote_copy(src_ref=ins[m], dst_ref=outs[m], send_sem=sems[0].at[m], recv_sem=sems[1].at[m],
                                             device_id=(x, y, 1 - c), device_id_type=MESH) for m in range(nm)]

    def start(ins, outs, sems):
        for cp in copies(ins, outs, sems):
            cp.start()

    def finish(ins, outs, sems):
        for cp in copies(ins, outs, sems):
            cp.wait()

    return _Side(mats, [jax.ShapeDtypeStruct(s.shape, s.dtype) for s in mats], {},
                 [pltpu.SemaphoreType.DMA((nm,))] * 2, start, finish)


def _gather_parts_side(w_own, mask_ids):
    nt = len(mask_ids)
    nh = w_own.shape[0] // 2

    def copies(ins, outs, sems):
        send_sems, recv_sems, fsend_sems, frecv_sems = sems
        x, y, c = _my_pos()
        me, sib = (x, y, c), (x, y, 1 - c)
        sends, landed, fwds, from_sib = [], [], [], []
        for t, k in enumerate(mask_ids):
            mine = _sl(outs[t], 0, c * nh, nh)
            theirs = _sl(outs[t], 0, (1 - c) * nh, nh)

            def rc(src, dst, ssem, rsem, to):
                return pltpu.make_async_remote_copy(src_ref=src, dst_ref=dst, send_sem=ssem, recv_sem=rsem,
                                                    device_id=to, device_id_type=MESH)

            sends.append(rc(_sl(ins[0], 0, c * nh, nh), mine, send_sems.at[t], recv_sems.at[t], _flip(me, CHIP_MASKS[k])))
            landed.append(rc(mine, mine, send_sems.at[t], recv_sems.at[t], me))
            fwds.append(rc(mine, mine, fsend_sems.at[t], frecv_sems.at[t], sib))
            from_sib.append(rc(theirs, theirs, fsend_sems.at[t], frecv_sems.at[t], sib))
        return sends, landed, fwds, from_sib

    def start(ins, outs, sems):
        for cp in copies(ins, outs, sems)[0]:
            cp.start()

    def finish(ins, outs, sems):
        sends, landed, fwds, from_sib = copies(ins, outs, sems)
        for arrived, fw in zip(landed, fwds):
            arrived.wait_recv()
            fw.start()
        for cp in from_sib:
            cp.wait_recv()
        for cp in sends + fwds:
            cp.wait_send()

    return _Side([w_own], [jax.ShapeDtypeStruct(w_own.shape, w_own.dtype)] * nt, {},
                 [pltpu.SemaphoreType.DMA((nt,))] * 4, start, finish)


def _pair_side(mats, half_axes):
    nm = len(mats)

    def copies(ins, outs, sems):
        x, y, c = _my_pos()
        cps = []
        for m in range(nm):
            nh = ins[m].shape[half_axes[m]] // 2
            cps.append(pltpu.make_async_remote_copy(
                src_ref=_sl(ins[m], half_axes[m], (1 - c) * nh, nh), dst_ref=outs[m],
                send_sem=sems[0].at[m], recv_sem=sems[1].at[m], device_id=(x, y, 1 - c), device_id_type=MESH))
        return cps

    def start(ins, outs, sems):
        for cp in copies(ins, outs, sems):
            cp.start()

    def finish(ins, outs, sems):
        for cp in copies(ins, outs, sems):
            cp.wait()

    out_shapes = []
    for s, a_h in zip(mats, half_axes):
        shp = list(s.shape)
        shp[a_h] //= 2
        out_shapes.append(jax.ShapeDtypeStruct(tuple(shp), s.dtype))
    return _Side(mats, out_shapes, {}, [pltpu.SemaphoreType.DMA((nm,))] * 2, start, finish)


def _chips_side(mats, shard_axes):
    nm = len(mats)

    def copies(ins, outs, sems):
        me = _my_pos()
        cps = []
        for m in range(nm):
            ns = ins[m].shape[shard_axes[m]] // 4
            for k, mask in enumerate(CHIP_MASKS):
                other = _flip(me, mask)
                cps.append(pltpu.make_async_remote_copy(
                    src_ref=_sl(ins[m], shard_axes[m], (2 * other[0] + other[1]) * ns, ns), dst_ref=outs[3 * m + k],
                    send_sem=sems[0].at[m, k], recv_sem=sems[1].at[m, k], device_id=other, device_id_type=MESH))
        return cps

    def start(ins, outs, sems):
        for cp in copies(ins, outs, sems):
            cp.start()

    def finish(ins, outs, sems):
        for cp in copies(ins, outs, sems):
            cp.wait()

    out_shapes = []
    for s, a_s in zip(mats, shard_axes):
        shp = list(s.shape)
        shp[a_s] //= 4
        out_shapes += [jax.ShapeDtypeStruct(tuple(shp), s.dtype)] * 3
    return _Side(mats, out_shapes, {}, [pltpu.SemaphoreType.DMA((nm, 3))] * 2, start, finish)


def _join_side(halves, half_axes):
    nm = len(halves)

    def copies(outs, sems):
        x, y, c = _my_pos()
        mine, theirs = [], []
        for m in range(nm):
            nh = outs[m].shape[half_axes[m]] // 2
            for half, lst in ((c, mine), (1 - c, theirs)):
                ref = _sl(outs[m], half_axes[m], half * nh, nh)
                lst.append(pltpu.make_async_remote_copy(
                    src_ref=ref, dst_ref=ref, send_sem=sems[0].at[m], recv_sem=sems[1].at[m],
                    device_id=(x, y, 1 - c), device_id_type=MESH))
        return mine, theirs

    def start(ins, outs, sems):
        for cp in copies(outs, sems)[0]:
            cp.start()

    def finish(ins, outs, sems):
        mine, theirs = copies(outs, sems)
        for cp in mine:
            cp.wait_send()
        for cp in theirs:
            cp.wait_recv()

    return _Side(halves, [jax.ShapeDtypeStruct(s.shape, s.dtype) for s in halves], {i: i for i in range(nm)},
                 [pltpu.SemaphoreType.DMA((nm,))] * 2, start, finish)


def _part_specs(shp, axis, slot, itemsize, target=MIB):
    if len(shp) == 2:
        rows, width = shp
        tm = _block_rows(rows, width, itemsize, target)
        nb = rows // tm
        own = pl.BlockSpec((tm, width), lambda i, p: (i, 0))
        if axis == 1:
            part = pl.BlockSpec((tm, width), lambda i, p: (i, p[slot]))
        else:
            part = pl.BlockSpec((tm, width), lambda i, p: (p[slot] * nb + i, 0))
        return (nb,), own, part
    assert len(shp) == 3 and axis in (1, 2)
    g, rows, width = shp
    own = pl.BlockSpec((1, rows, width), lambda i, p: (i, 0, 0))
    if axis == 1:
        part = pl.BlockSpec((1, rows, width), lambda i, p: (i, p[slot], 0))
    else:
        part = pl.BlockSpec((1, rows, width), lambda i, p: (i, 0, p[slot]))
    return (g,), own, part


def _cast_into(w, axis, p, *, name):
    grid, own, part = _part_specs(w.shape, axis, 0, 4, 2 * MIB)
    big = list(w.shape)
    big[axis] *= 4

    def body(p_ref, w_ref, o_ref):
        o_ref[...] = w_ref[...].astype(BF)

    return pl.pallas_call(
        body, name=name, out_shape=jax.ShapeDtypeStruct(tuple(big), BF),
        grid_spec=pltpu.PrefetchScalarGridSpec(num_scalar_prefetch=1, grid=grid, in_specs=[own], out_specs=part),
        compiler_params=_cparams("parallel"),
    )(p.reshape(1), w)


def _cast_bf16(w, *, name):
    rows, width = w.shape
    tm = _block_rows(rows, width, 4, 2 * MIB)

    def body(w_ref, o_ref):
        o_ref[...] = w_ref[...].astype(BF)

    spec = pl.BlockSpec((tm, width), lambda i: (i, 0))
    return pl.pallas_call(body, name=name, grid=(rows // tm,), out_shape=jax.ShapeDtypeStruct(w.shape, BF),
                          in_specs=[spec], out_specs=spec, compiler_params=_cparams("parallel"))(w)


def _add2(a, b, *, name):
    rows, width = a.shape
    tm = _block_rows(rows, width, 2, 4 * MIB)

    def body(a_ref, b_ref, o_ref):
        o_ref[...] = (a_ref[...].astype(F32) + b_ref[...].astype(F32)).astype(BF)

    spec = pl.BlockSpec((tm, width), lambda i: (i, 0))
    return pl.pallas_call(body, name=name, grid=(rows // tm,), out_shape=jax.ShapeDtypeStruct(a.shape, BF),
                          in_specs=[spec, spec], out_specs=spec, compiler_params=_cparams("parallel"))(a, b)


def _add_own_half(full, recv, half_axis, c, *, name):
    grid, own, part = _part_specs(recv.shape, half_axis, 0, 2, 4 * MIB)

    def body(c_ref, f_ref, r_ref, o_ref):
        o_ref[...] = (f_ref[...].astype(F32) + r_ref[...].astype(F32)).astype(BF)

    return pl.pallas_call(
        body, name=name, out_shape=jax.ShapeDtypeStruct(recv.shape, BF),
        grid_spec=pltpu.PrefetchScalarGridSpec(num_scalar_prefetch=1, grid=grid, in_specs=[part, own], out_specs=own),
        compiler_params=_cparams("parallel"),
    )(c.reshape(1), full, recv)


def _sum_own_block(mine_all, recvs, shard_axis, half_axis, j, c, *, name):
    shp = recvs[0].shape
    grid, own, mine = _part_specs(shp, shard_axis, 0, 4, 2 * MIB)
    _, _, place = _part_specs(shp, half_axis, 1, 4, 2 * MIB)
    big = list(shp)
    big[half_axis] *= 2

    def body(p_ref, a_ref, r0, r1, r2, o_ref):
        o_ref[...] = ((a_ref[...].astype(F32) + r0[...].astype(F32)) + r1[...].astype(F32)) + r2[...].astype(F32)

    return pl.pallas_call(
        body, name=name, out_shape=jax.ShapeDtypeStruct(tuple(big), F32),
        grid_spec=pltpu.PrefetchScalarGridSpec(
            num_scalar_prefetch=1, grid=grid, in_specs=[mine, own, own, own], out_specs=place),
        compiler_params=_cparams("parallel"),
    )(jnp.stack([j, c]), mine_all, *recvs)


def _adamw(w, g, m, v, *, name):
    shp = w.shape
    width = shp[-1] if len(shp) > 1 else shp[0]
    rows = 1
    for d in shp[:-1]:
        rows *= d
    if len(shp) == 1:
        rows = 1
    args = [a.reshape(rows, width) for a in (w, g, m, v)]
    tm = _block_rows(rows, width, 4, 2 * MIB)

    def body(w_ref, g_ref, m_ref, v_ref, go_ref, d_ref, nm_ref, nv_ref):
        gv = g_ref[...]
        go_ref[...] = gv
        nm = ADAM_B1 * m_ref[...] + (1.0 - ADAM_B1) * gv
        nv = ADAM_B2 * v_ref[...] + (1.0 - ADAM_B2) * (gv * gv)
        m_hat = nm / (1.0 - ADAM_B1 ** ADAM_STEP)
        v_hat = nv / (1.0 - ADAM_B2 ** ADAM_STEP)
        d_ref[...] = -ADAM_LR * (m_hat / (jnp.sqrt(v_hat) + ADAM_EPS) + ADAM_WD * w_ref[...])
        nm_ref[...] = nm
        nv_ref[...] = nv

    spec = pl.BlockSpec((tm, width), lambda i: (i, 0))
    outs = pl.pallas_call(
        body, name=name, grid=(rows // tm,), out_shape=tuple([jax.ShapeDtypeStruct((rows, width), F32)] * 4),
        in_specs=[spec] * 4, out_specs=tuple([spec] * 4), compiler_params=_cparams("parallel"),
    )(*args)
    return tuple(o.reshape(shp) for o in outs)


def _matmul(a, b, *, ca, cb, tm, tn, tk, out_dtype, name, b_resident=False, side=None):
    m, kdim = a.shape[1 - ca], a.shape[ca]
    n = b.shape[1 - cb]
    nk = kdim // tk
    assert m % tm == 0 and n % tn == 0 and kdim % tk == 0
    n_sin = len(side.ins) if side else 0
    n_sout = len(side.out_shapes) if side else 0
    if b_resident:
        gi = lambda p, q, k: (q, p, k)
        grid = (n // tn, m // tm, nk)
    else:
        gi = lambda p, q, k: (p, q, k)
        grid = (m // tm, n // tn, nk)

    def body(*refs):
        a_ref, b_ref = refs[0], refs[1]
        s_ins = refs[2:2 + n_sin]
        o_ref = refs[2 + n_sin]
        s_outs = refs[3 + n_sin:3 + n_sin + n_sout]
        rest = refs[3 + n_sin + n_sout:]
        acc, sems = (rest[:1], rest[1:]) if nk > 1 else ((), rest)
        if side:
            first, middle, last = _side_at(grid)
            pl.when(first)(lambda: side.start(s_ins, s_outs, sems))
            if side.mid is not None:
                pl.when(middle)(lambda: side.mid(s_ins, s_outs, sems))
        part = _dot(a_ref[...], b_ref[...], ca, cb)
        if nk == 1:
            o_ref[...] = part.astype(o_ref.dtype)
        else:
            k = pl.program_id(2)

            @pl.when(k == 0)
            def _():
                acc[0][...] = part

            @pl.when(jnp.logical_and(k > 0, k < nk - 1))
            def _():
                acc[0][...] += part

            @pl.when(k == nk - 1)
            def _():
                o_ref[...] = (acc[0][...] + part).astype(o_ref.dtype)
        if side:
            pl.when(last)(lambda: side.finish(s_ins, s_outs, sems))

    def a_map(p, q, k):
        i, _, kk = gi(p, q, k)
        return (i, kk) if ca == 1 else (kk, i)

    def b_map(p, q, k):
        _, j, kk = gi(p, q, k)
        return (kk, j) if cb == 0 else (j, kk)

    def o_map(p, q, k):
        i, j, _ = gi(p, q, k)
        return (i, j)

    a_spec = pl.BlockSpec((tm, tk) if ca == 1 else (tk, tm), a_map)
    b_spec = pl.BlockSpec((tk, tn) if cb == 0 else (tn, tk), b_map)
    o_shape = jax.ShapeDtypeStruct((m, n), out_dtype)
    acc_shapes = [pltpu.VMEM((tm, tn), F32)] if nk > 1 else []
    if not side:
        return pl.pallas_call(
            body, name=name, grid=grid, out_shape=o_shape,
            in_specs=[a_spec, b_spec], out_specs=pl.BlockSpec((tm, tn), o_map), scratch_shapes=acc_shapes,
            compiler_params=_cparams("parallel", "parallel", "arbitrary"),
        )(a, b)
    return pl.pallas_call(
        body, name=name, grid=grid, out_shape=(o_shape, *side.out_shapes),
        in_specs=[a_spec, b_spec] + [HBM_SPEC] * n_sin,
        out_specs=(pl.BlockSpec((tm, tn), o_map), *([HBM_SPEC] * n_sout)),
        input_output_aliases={2 + i: 1 + o for i, o in side.aliases.items()},
        scratch_shapes=acc_shapes + side.sem_shapes,
        compiler_params=_cparams("arbitrary", "arbitrary", "arbitrary"),
    )(a, b, *side.ins)


def _wgrad_rows(a, b, half, *, tm, tn, name, side=None):
    kdim, m = a.shape
    n = b.shape[1]
    nb = (m // 2) // tm
    grid = (nb, n // tn)
    n_sin = len(side.ins) if side else 0
    n_sout = len(side.out_shapes) if side else 0

    def body(p_ref, a_ref, b_ref, *refs):
        s_ins, o_ref = refs[:n_sin], refs[n_sin]
        s_outs, sems = refs[n_sin + 1:n_sin + 1 + n_sout], refs[n_sin + 1 + n_sout:]
        if side:
            first, _, last = _side_at(grid)
            pl.when(first)(lambda: side.start(s_ins, s_outs, sems))
        o_ref[...] = _dot(a_ref[...], b_ref[...], 0, 0).astype(o_ref.dtype)
        if side:
            pl.when(last)(lambda: side.finish(s_ins, s_outs, sems))

    outs = pl.pallas_call(
        body, name=name, out_shape=(jax.ShapeDtypeStruct((m // 2, n), BF), *(side.out_shapes if side else [])),
        grid_spec=pltpu.PrefetchScalarGridSpec(
            num_scalar_prefetch=1, grid=grid,
            in_specs=[pl.BlockSpec((kdim, tm), lambda i, j, p: (0, p[0] * nb + i)),
                      pl.BlockSpec((kdim, tn), lambda i, j, p: (0, j))] + [HBM_SPEC] * n_sin,
            out_specs=(pl.BlockSpec((tm, tn), lambda i, j, p: (i, j)), *([HBM_SPEC] * n_sout)),
            scratch_shapes=side.sem_shapes if side else []),
        input_output_aliases={3 + i: 1 + o for i, o in side.aliases.items()} if side else {},
        compiler_params=_cparams("arbitrary", "arbitrary"),
    )(half.reshape(1), a, b, *(side.ins if side else []))
    return outs if side else outs[0]


def _in_proj_parts(h, w_parts, ids, prev, *, width, tm, tn, name, side=None):
    s_len, d = h.shape
    nb = w_parts[0].shape[1] // tn
    n_p = len(w_parts)
    n_prev = 0 if prev is None else 1
    n_sin = len(side.ins) if side else 0
    n_sout = len(side.out_shapes) if side else 0
    grid = (n_p, s_len // tm, nb)
    base = 1 + n_p + n_prev

    def body(ids_ref, *refs):
        h_ref, w_refs = refs[0], refs[1:1 + n_p]
        s_ins = refs[base:base + n_sin]
        o_ref = refs[base + n_sin]
        s_outs = refs[base + n_sin + 1:base + n_sin + 1 + n_sout]
        sems = refs[base + n_sin + 1 + n_sout:]
        if side:
            first, _, last = _side_at(grid)
            pl.when(first)(lambda: side.start(s_ins, s_outs, sems))
        for s in range(n_p):
            @pl.when(pl.program_id(0) == s)
            def _(s=s):
                o_ref[...] = _dot(h_ref[...], w_refs[s][...], 1, 0).astype(o_ref.dtype)
        if side:
            pl.when(last)(lambda: side.finish(s_ins, s_outs, sems))

    in_specs = [pl.BlockSpec((tm, d), lambda p, i, j, ids_ref: (i, 0))]
    for s in range(n_p):
        in_specs.append(pl.BlockSpec((d, tn), lambda p, i, j, ids_ref, s=s: (0, jnp.where(p == s, j, 0))))
    in_specs += [HBM_SPEC] * (n_prev + n_sin)
    o_spec = pl.BlockSpec((tm, tn), lambda p, i, j, ids_ref: (i, ids_ref[p] * nb + j))
    aliases = {2 + n_p: 0} if prev is not None else {}
    if side:
        aliases.update({1 + base + i: 1 + o for i, o in side.aliases.items()})
    outs = pl.pallas_call(
        body, name=name, out_shape=(jax.ShapeDtypeStruct((s_len, width), BF), *(side.out_shapes if side else [])),
        grid_spec=pltpu.PrefetchScalarGridSpec(
            num_scalar_prefetch=1, grid=grid, in_specs=in_specs, out_specs=(o_spec, *([HBM_SPEC] * n_sout)),
            scratch_shapes=side.sem_shapes if side else []),
        input_output_aliases=aliases, compiler_params=_cparams("arbitrary", "arbitrary", "arbitrary"),
    )(ids, h, *w_parts, *([prev] if prev is not None else []), *(side.ins if side else []))
    return outs


def _in_proj_own(x, ctx, g_pre, ss, w_own, ids, *, width, tm, tn, side):
    n, d = x.shape
    l = ctx.shape[0]
    nc = l // tm
    nb = w_own.shape[1] // tn
    grid = ((l + n) // tm, nb)
    n_sin, n_sout = len(side.ins), len(side.out_shapes)

    def body(ids_ref, x_ref, c_ref, g_ref, ss_ref, w_ref, *refs):
        s_ins = refs[:n_sin]
        o_ref, h_ref = refs[n_sin:n_sin + 2]
        s_outs, sems = refs[n_sin + 2:n_sin + 2 + n_sout], refs[n_sin + 2 + n_sout:]
        first, _, last = _side_at(grid)
        pl.when(first)(lambda: side.start(s_ins, s_outs, sems))
        t, j = pl.program_id(0), pl.program_id(1)

        def norm(src, scale, shift):
            v = src[...]
            rstd = lax.rsqrt(jnp.mean(v * v, axis=-1, keepdims=True) + EPS)
            h_ref[...] = ((v * rstd * g_ref[...]) * (1.0 + scale) + shift).astype(BF)

        @pl.when(jnp.logical_and(j == 0, t < nc))
        def _():
            norm(c_ref, ss_ref[2:3, :], ss_ref[3:4, :])

        @pl.when(jnp.logical_and(j == 0, t >= nc))
        def _():
            norm(x_ref, ss_ref[0:1, :], ss_ref[1:2, :])

        o_ref[...] = _dot(h_ref[...], w_ref[...], 1, 0).astype(o_ref.dtype)
        pl.when(last)(lambda: side.finish(s_ins, s_outs, sems))

    return pl.pallas_call(
        body, name="in_proj_own",
        out_shape=(jax.ShapeDtypeStruct((l + n, width), BF), jax.ShapeDtypeStruct((l + n, d), BF), *side.out_shapes),
        grid_spec=pltpu.PrefetchScalarGridSpec(
            num_scalar_prefetch=1, grid=grid,
            in_specs=[pl.BlockSpec((tm, d), lambda t, j, ids_ref: (jnp.maximum(t - nc, 0), 0)),
                      pl.BlockSpec((tm, d), lambda t, j, ids_ref: (jnp.minimum(t, nc - 1), 0)),
                      pl.BlockSpec((1, d), lambda t, j, ids_ref: (0, 0)), pl.BlockSpec((4, d), lambda t, j, ids_ref: (0, 0)),
                      pl.BlockSpec((d, tn), lambda t, j, ids_ref: (0, j))] + [HBM_SPEC] * n_sin,
            out_specs=(pl.BlockSpec((tm, tn), lambda t, j, ids_ref: (t, ids_ref[0] * nb + j)),
                       pl.BlockSpec((tm, d), lambda t, j, ids_ref: (t, 0)), *([HBM_SPEC] * n_sout)),
            scratch_shapes=side.sem_shapes),
        input_output_aliases={6 + i: 2 + o for i, o in side.aliases.items()},
        compiler_params=_cparams("arbitrary", "arbitrary"),
    )(ids, x, ctx, g_pre, ss, w_own, *side.ins)


def _dgrad_in_parts(dproj, w_parts, ids, *, tm, tn, name, side=None):
    s_len = dproj.shape[0]
    d, ws = w_parts[0].shape
    n_p = len(w_parts)
    n_sin = len(side.ins) if side else 0
    n_sout = len(side.out_shapes) if side else 0
    grid = (s_len // tm, d // tn)

    def body(ids_ref, *refs):
        a_refs, w_refs = refs[:n_p], refs[n_p:2 * n_p]
        s_ins = refs[2 * n_p:2 * n_p + n_sin]
        o_ref = refs[2 * n_p + n_sin]
        s_outs = refs[2 * n_p + n_sin + 1:2 * n_p + n_sin + 1 + n_sout]
        sems = refs[2 * n_p + n_sin + 1 + n_sout:]
        if side:
            first, _, last = _side_at(grid)
            pl.when(first)(lambda: side.start(s_ins, s_outs, sems))
        tot = _dot(a_refs[0][...], w_refs[0][...], 1, 1)
        for s in range(1, n_p):
            tot = tot + _dot(a_refs[s][...], w_refs[s][...], 1, 1)
        o_ref[...] = tot.astype(o_ref.dtype)
        if side:
            pl.when(last)(lambda: side.finish(s_ins, s_outs, sems))

    in_specs = [pl.BlockSpec((tm, ws), lambda i, j, ids_ref, s=s: (i, ids_ref[s])) for s in range(n_p)]
    in_specs += [pl.BlockSpec((tn, ws), lambda i, j, ids_ref: (j, 0))] * n_p
    in_specs += [HBM_SPEC] * n_sin
    aliases = {1 + 2 * n_p + i: 1 + o for i, o in side.aliases.items()} if side else {}
    return pl.pallas_call(
        body, name=name, out_shape=(jax.ShapeDtypeStruct((s_len, d), BF), *(side.out_shapes if side else [])),
        grid_spec=pltpu.PrefetchScalarGridSpec(
            num_scalar_prefetch=1, grid=grid, in_specs=in_specs,
            out_specs=(pl.BlockSpec((tm, tn), lambda i, j, ids_ref: (i, j)), *([HBM_SPEC] * n_sout)),
            scratch_shapes=side.sem_shapes if side else []),
        input_output_aliases=aliases, compiler_params=_cparams("arbitrary", "arbitrary"),
    )(ids, *([dproj] * n_p), *w_parts, *(side.ins if side else []))


def _swap32(v):
    lane = lax.broadcasted_iota(jnp.int32, v.shape, 1)
    return jnp.where((lane % 64) < 32, pltpu.roll(v, 96, 1), pltpu.roll(v, 32, 1))


def _qk_prep(proj, gain, ctab, stab, *, row0, col0, width, nrows, tm, name, out_scale=1.0):
    cw = min(512, width)
    rb0, cb0 = row0 // tm, col0 // cw
    assert row0 % tm == 0 and col0 % cw == 0 and width % cw == 0 and nrows % tm == 0

    def body(p_ref, g_ref, c_ref, s_ref, o_ref):
        row = lax.broadcasted_iota(jnp.int32, (HEAD_DIM, HEAD_DIM), 0)
        col = lax.broadcasted_iota(jnp.int32, (HEAD_DIM, HEAD_DIM), 1)
        perm = (row == jnp.where((col % 64) < 32, col + 32, col - 32)).astype(BF)
        gc = g_ref[...] * c_ref[...] * out_scale
        gs = _swap32(jnp.broadcast_to(g_ref[...], c_ref.shape)) * s_ref[...] * out_scale
        for hd in range(cw // HEAD_DIM):
            cols = slice(hd * HEAD_DIM, (hd + 1) * HEAD_DIM)
            vb = p_ref[:, cols]
            v = vb.astype(F32)
            rstd = lax.rsqrt(jnp.mean(v * v, axis=-1, keepdims=True) + EPS)
            o_ref[:, cols] = (rstd * (v * gc + _dot(vb, perm, 1, 0) * gs)).astype(BF)

    return pl.pallas_call(
        body, name=name, grid=(nrows // tm, width // cw), out_shape=jax.ShapeDtypeStruct((nrows, width), BF),
        in_specs=[pl.BlockSpec((tm, cw), lambda i, j: (i + rb0, j + cb0)), pl.BlockSpec((1, HEAD_DIM), lambda i, j: (0, 0)),
                  pl.BlockSpec((tm, HEAD_DIM), lambda i, j: (i, 0)), pl.BlockSpec((tm, HEAD_DIM), lambda i, j: (i, 0))],
        out_specs=pl.BlockSpec((tm, cw), lambda i, j: (i, j)), compiler_params=_cparams("parallel", "parallel"),
    )(proj, gain, ctab, stab)


def _zero_rows(rows, shape, *, name):
    def body(o_ref):
        o_ref[...] = jnp.zeros_like(o_ref)

    return pl.pallas_call(body, name=name, grid=(1,), out_shape=jax.ShapeDtypeStruct(shape, BF),
                          out_specs=pl.BlockSpec((rows, shape[1]), lambda i: (0, 0)))()


def _qk_bwd(dy, proj, gain, ctab, stab, into, *, row0, col0, tm, name, transposed=False):
    if transposed:
        tq = dy.shape[-1]
        nrows, width = dy.shape[1] * tq, dy.shape[0] * GQA_GROUP * HEAD_DIM
        per = tq // tm
        dy_spec = pl.BlockSpec((1, 1, GQA_GROUP, HEAD_DIM, tm), lambda i, j: (j, i // per, 0, 0, i % per))
    else:
        nrows, width = dy.shape
    cw = min(512, width)
    rb0, cb0 = row0 // tm, col0 // cw
    if transposed:
        assert cw == GQA_GROUP * HEAD_DIM and tq % tm == 0
    else:
        dy_spec = pl.BlockSpec((tm, cw), lambda i, j: (i, j))

    def body(d_ref, p_ref, g_ref, c_ref, s_ref, into_ref, o_ref, dg_ref):
        @pl.when(jnp.logical_and(pl.program_id(0) == 0, pl.program_id(1) == 0))
        def _():
            dg_ref[...] = jnp.zeros_like(dg_ref)

        dg = jnp.zeros((1, HEAD_DIM), F32)
        for hd in range(cw // HEAD_DIM):
            cols = slice(hd * HEAD_DIM, (hd + 1) * HEAD_DIM)
            v = p_ref[:, cols].astype(F32)
            rstd = lax.rsqrt(jnp.mean(v * v, axis=-1, keepdims=True) + EPS)
            nv = v * rstd
            d = d_ref[0, 0, hd].T if transposed else d_ref[:, cols]
            dyu = d * c_ref[...] + _swap32(d * s_ref[...])
            dg = dg + jnp.sum(dyu * nv, axis=0, keepdims=True)
            dn = dyu * g_ref[...]
            o_ref[:, cols] = (rstd * (dn - nv * jnp.mean(dn * nv, axis=-1, keepdims=True))).astype(BF)
        dg_ref[0:1, :] += dg

    placed = pl.BlockSpec((tm, cw), lambda i, j: (i + rb0, j + cb0))
    return pl.pallas_call(
        body, name=name, grid=(nrows // tm, width // cw),
        out_shape=(jax.ShapeDtypeStruct(into.shape, BF), jax.ShapeDtypeStruct((8, HEAD_DIM), F32)),
        in_specs=[dy_spec, placed, pl.BlockSpec((1, HEAD_DIM), lambda i, j: (0, 0)),
                  pl.BlockSpec((tm, HEAD_DIM), lambda i, j: (i, 0)), pl.BlockSpec((tm, HEAD_DIM), lambda i, j: (i, 0)), HBM_SPEC],
        out_specs=(placed, pl.BlockSpec((8, HEAD_DIM), lambda i, j: (0, 0))),
        input_output_aliases={5: 0}, compiler_params=_cparams("arbitrary", "arbitrary"),
    )(dy, proj, gain, ctab, stab, into)


def _flash_fwd(qr, k_all, vt_ctx, vt_lat, *, tq, side=None):
    n, aw = qr.shape
    s_len, kvw = k_all.shape
    l = vt_ctx.shape[-1]
    n_j, tk = vt_lat.shape[1], vt_lat.shape[-1]
    kvh = kvw // HEAD_DIM
    n_i = n // tq
    gw = GQA_GROUP * HEAD_DIM
    n_sin = len(side.ins) if side else 0
    n_sout = len(side.out_shapes) if side else 0

    def body(*refs):
        q_ref, k_ref, vc_ref, vl_ref = refs[:4]
        s_ins = refs[4:4 + n_sin]
        o_ref, lse_ref = refs[4 + n_sin:6 + n_sin]
        s_outs = refs[6 + n_sin:6 + n_sin + n_sout]
        acc_ref, m_ref, l_ref = refs[6 + n_sin + n_sout:9 + n_sin + n_sout]
        sems = refs[9 + n_sin + n_sout:]
        if side:
            first, middle, last = _side_at((kvh, n_i))
            pl.when(first)(lambda: side.start(s_ins, s_outs, sems))
            if side.mid is not None:
                pl.when(middle)(lambda: side.mid(s_ins, s_outs, sems))
        for h0 in range(0, GQA_GROUP, FWD_HEADS):
            acc_ref[...] = jnp.zeros_like(acc_ref)
            l_ref[...] = jnp.zeros_like(l_ref)
            m_ref[...] = jnp.full(m_ref.shape, -1e30, F32)

            def tile(kj, vtj, h0=h0):
                for gg in range(FWD_HEADS):
                    g = h0 + gg
                    st = _dot(kj, q_ref[:, g * HEAD_DIM:(g + 1) * HEAD_DIM], 1, 1)
                    m_old = m_ref[gg]
                    m_new = jnp.maximum(m_old, jnp.max(st, axis=0, keepdims=True))
                    alpha = jnp.exp2(m_old - m_new)
                    pt = jnp.exp2(st - m_new)
                    l_ref[gg] = alpha * l_ref[gg] + jnp.sum(pt, axis=0, keepdims=True)
                    m_ref[gg] = m_new
                    acc_ref[gg] = acc_ref[gg] * alpha + _dot(vtj, pt.astype(BF), 1, 0)

            tile(k_ref[0:l, :], vc_ref[0, 0])

            def step(j, carry, tile=tile):
                tile(k_ref[pl.ds(pl.multiple_of(l + j * tk, min(l, tk)), tk), :], vl_ref[0, j])
                return carry

            lax.fori_loop(0, n_j, step, 0)
            for gg in range(FWD_HEADS):
                g = h0 + gg
                o_ref[:, g * HEAD_DIM:(g + 1) * HEAD_DIM] = (acc_ref[gg] / l_ref[gg]).T.astype(BF)
                lse_ref[0, g, 0] = m_ref[gg] + jnp.log(l_ref[gg]) * LOG2E
        if side:
            pl.when(last)(lambda: side.finish(s_ins, s_outs, sems))

    return pl.pallas_call(
        body, name="flash_fwd", grid=(kvh, n_i),
        out_shape=(jax.ShapeDtypeStruct((n, aw), BF), jax.ShapeDtypeStruct((kvh, GQA_GROUP, n_i, 1, tq), F32),
                   *(side.out_shapes if side else [])),
        in_specs=[pl.BlockSpec((tq, gw), lambda h, i: (i, h)), pl.BlockSpec((s_len, HEAD_DIM), lambda h, i: (0, h)),
                  pl.BlockSpec((1, 1, HEAD_DIM, l), lambda h, i: (h, 0, 0, 0)),
                  pl.BlockSpec((1, n_j, HEAD_DIM, tk), lambda h, i: (h, 0, 0, 0))] + [HBM_SPEC] * n_sin,
        out_specs=(pl.BlockSpec((tq, gw), lambda h, i: (i, h)),
                   pl.BlockSpec((1, GQA_GROUP, 1, 1, tq), lambda h, i: (h, 0, i, 0, 0)), *([HBM_SPEC] * n_sout)),
        input_output_aliases={4 + i: 2 + o for i, o in side.aliases.items()} if side else {},
        scratch_shapes=[pltpu.VMEM((FWD_HEADS, HEAD_DIM, tq), F32), pltpu.VMEM((FWD_HEADS, 1, tq), F32),
                        pltpu.VMEM((FWD_HEADS, 1, tq), F32)] + (side.sem_shapes if side else []),
        compiler_params=_cparams("arbitrary", "arbitrary") if side else _cparams("parallel", "parallel"),
    )(qr, k_all, vt_ctx, vt_lat, *(side.ins if side else []))


def _attn_delta(o, do, *, tq):
    n, aw = o.shape
    kvh = aw // (GQA_GROUP * HEAD_DIM)
    gw = GQA_GROUP * HEAD_DIM

    def body(o_ref, do_ref, d_ref):
        for g in range(GQA_GROUP):
            cols = slice(g * HEAD_DIM, (g + 1) * HEAD_DIM)
            prod = o_ref[:, cols].astype(F32) * do_ref[:, cols].astype(F32)
            d_ref[0, g, 0] = jnp.sum(prod.T, axis=0, keepdims=True)

    return pl.pallas_call(
        body, name="attn_delta", grid=(kvh, n // tq),
        out_shape=jax.ShapeDtypeStruct((kvh, GQA_GROUP, n // tq, 1, tq), F32),
        in_specs=[pl.BlockSpec((tq, gw), lambda h, i: (i, h)), pl.BlockSpec((tq, gw), lambda h, i: (i, h))],
        out_specs=pl.BlockSpec((1, GQA_GROUP, 1, 1, tq), lambda h, i: (h, 0, i, 0, 0)),
        compiler_params=_cparams("parallel", "parallel"),
    )(o, do)


def _flash_bwd(qr, do, k_all, proj, lse, delta, into, *, v_col0, tq, tk, scale, side=None):
    n, aw = qr.shape
    s_len, kvw = k_all.shape
    kvh = kvw // HEAD_DIM
    n_i, n_j = n // tq, s_len // tk
    gw = GQA_GROUP * HEAD_DIM
    vb0 = v_col0 // HEAD_DIM
    n_sin = len(side.ins) if side else 0
    n_sout = len(side.out_shapes) if side else 0

    def body(*refs):
        q_ref, do_ref, k_ref, v_ref, lse_ref, dl_ref = refs[:6]
        s_ins = refs[7:7 + n_sin]
        dq_ref, dk_ref, dvo_ref = refs[7 + n_sin:10 + n_sin]
        s_outs = refs[10 + n_sin:10 + n_sin + n_sout]
        dv_ref = refs[10 + n_sin + n_sout]
        sems = refs[11 + n_sin + n_sout:]
        jj = pl.program_id(1)
        if side:
            first, _, last = _side_at((kvh, n_j))
            pl.when(first)(lambda: side.start(s_ins, s_outs, sems))

        @pl.when(jj == 0)
        def _():
            dq_ref[...] = jnp.zeros_like(dq_ref)

        kj = k_ref[...]
        vj = v_ref[...]
        kjt = kj.astype(F32).T.astype(BF)
        dk_ref[...] = jnp.zeros_like(dk_ref)
        dv_ref[...] = jnp.zeros_like(dv_ref)

        def step(i, carry):
            rows = pl.ds(pl.multiple_of(i * tq, tq), tq)
            dv_part = dk_part = None
            for g in range(GQA_GROUP):
                cols = slice(g * HEAD_DIM, (g + 1) * HEAD_DIM)
                qg = q_ref[rows, cols]
                dog = do_ref[rows, cols]
                pt = jnp.exp2(_dot(kj, qg, 1, 1) - lse_ref[0, g, i])
                dst = (pt * (_dot(vj, dog, 1, 1) - dl_ref[0, g, i])).astype(BF)
                dv_g = _dot(pt.astype(BF), dog, 1, 0)
                dk_g = _dot(dst, qg, 1, 0)
                dv_part = dv_g if dv_part is None else dv_part + dv_g
                dk_part = dk_g if dk_part is None else dk_part + dk_g
                dq_ref[0, i, g] += _dot(kjt, dst, 1, 0)
            dv_ref[...] += dv_part
            dk_ref[...] += dk_part
            return carry

        lax.fori_loop(0, n_i, step, 0)
        dk_ref[...] = dk_ref[...] * (1.0 / LOG2E)
        dvo_ref[...] = dv_ref[...].astype(BF)

        @pl.when(jj == n_j - 1)
        def _():
            dq_ref[...] = dq_ref[...] * scale

        if side:
            pl.when(last)(lambda: side.finish(s_ins, s_outs, sems))

    stat_spec = pl.BlockSpec((1, GQA_GROUP, n_i, 1, tq), lambda h, j: (h, 0, 0, 0, 0))
    kv_spec = pl.BlockSpec((tk, HEAD_DIM), lambda h, j: (j, h))
    v_spec = pl.BlockSpec((tk, HEAD_DIM), lambda h, j: (j, vb0 + h))
    q_spec = pl.BlockSpec((n, gw), lambda h, j: (0, h))
    dq_spec = pl.BlockSpec((1, n_i, GQA_GROUP, HEAD_DIM, tq), lambda h, j: (h, 0, 0, 0, 0))
    aliases = {6: 2}
    if side:
        aliases.update({7 + i: 3 + o for i, o in side.aliases.items()})
    return pl.pallas_call(
        body, name="flash_bwd", grid=(kvh, n_j),
        out_shape=(jax.ShapeDtypeStruct((kvh, n_i, GQA_GROUP, HEAD_DIM, tq), F32), jax.ShapeDtypeStruct((s_len, kvw), F32),
                   jax.ShapeDtypeStruct(into.shape, BF), *(side.out_shapes if side else [])),
        in_specs=[q_spec, q_spec, kv_spec, v_spec, stat_spec, stat_spec, HBM_SPEC] + [HBM_SPEC] * n_sin,
        out_specs=(dq_spec, kv_spec, v_spec, *([HBM_SPEC] * n_sout)),
        input_output_aliases=aliases,
        scratch_shapes=[pltpu.VMEM((tk, HEAD_DIM), F32)] + (side.sem_shapes if side else []),
        compiler_params=_cparams("arbitrary", "arbitrary"),
    )(qr, do, k_all, proj, lse, delta, into, *(side.ins if side else []))


def _pool_window(r, gi, l, n, tm):
    win = tm + 2 * POOL_HALO
    start = jnp.clip(l + r * tm - POOL_HALO, l, l + n - win)
    half = lax.shift_left(jnp.int32(1), gi)
    tok = r * tm + lax.broadcasted_iota(jnp.int32, (tm, win), 0)
    src = (start - l) + lax.broadcasted_iota(jnp.int32, (tm, win), 1)
    tok1 = r * tm + lax.broadcasted_iota(jnp.int32, (tm, 1), 0)
    cnt = (jnp.minimum(tok1 + half, n) - jnp.maximum(tok1 - half, 0)).astype(F32)
    return start, win, tok, src, half, cnt


def _pool_fwd(proj, pool_w, *, l, n, col0, tm):
    s_len = proj.shape[0]
    pg = pool_w.shape[-1]
    cb0 = col0 // pg
    assert col0 % pg == 0 and l % POOL_HALO == 0 and n >= tm + 2 * POOL_HALO

    def body(u_ref, w_ref, d_ref, po_ref):
        gi, r = pl.program_id(0), pl.program_id(1)
        start, win, tok, src, half, cnt = _pool_window(r, gi, l, n, tm)
        band = jnp.logical_and(src >= tok - half, src < tok + half).astype(BF)
        uw = u_ref[pl.ds(pl.multiple_of(start, POOL_HALO), win), :]
        ut = u_ref[pl.ds(pl.multiple_of(l + r * tm, POOL_HALO), tm), :].astype(F32)
        dv = (_dot(band, uw, 1, 0) / cnt - ut).astype(BF)
        d_ref[...] = dv
        po_ref[...] = _dot(dv, w_ref[0], 1, 0).astype(BF)

    return pl.pallas_call(
        body, name="pool_fwd", grid=(N_POOL_GROUPS, n // tm),
        out_shape=(jax.ShapeDtypeStruct((n, N_POOL_GROUPS * pg), BF), jax.ShapeDtypeStruct((n, N_POOL_GROUPS * pg), BF)),
        in_specs=[pl.BlockSpec((s_len, pg), lambda g, r: (0, cb0 + g)), pl.BlockSpec((1, pg, pg), lambda g, r: (g, 0, 0))],
        out_specs=(pl.BlockSpec((tm, pg), lambda g, r: (r, g)), pl.BlockSpec((tm, pg), lambda g, r: (r, g))),
        compiler_params=_cparams("parallel", "parallel"),
    )(proj, pool_w)


def _pool_bwd_map(dval, pool_w, *, n, col0, tm):
    pg = pool_w.shape[-1]
    cb0 = col0 // pg

    def body(d_ref, w_ref, o_ref):
        gi, r = pl.program_id(0), pl.program_id(1)
        half = lax.shift_left(jnp.int32(1), gi)
        tok1 = r * tm + lax.broadcasted_iota(jnp.int32, (tm, 1), 0)
        cnt = (jnp.minimum(tok1 + half, n) - jnp.maximum(tok1 - half, 0)).astype(F32)
        o_ref[...] = (_dot(d_ref[...], w_ref[0], 1, 1) / cnt).astype(BF)

    return pl.pallas_call(
        body, name="pool_bwd_map", grid=(N_POOL_GROUPS, n // tm),
        out_shape=jax.ShapeDtypeStruct((n, N_POOL_GROUPS * pg), BF),
        in_specs=[pl.BlockSpec((tm, pg), lambda g, r: (r, cb0 + g)), pl.BlockSpec((1, pg, pg), lambda g, r: (g, 0, 0))],
        out_specs=pl.BlockSpec((tm, pg), lambda g, r: (r, g)), compiler_params=_cparams("parallel", "parallel"),
    )(dval, pool_w)


def _pool_bwd_window(dds, into, *, n, tm, row0, col0):
    pg = dds.shape[1] // N_POOL_GROUPS
    rb0, cb0 = row0 // tm, col0 // pg
    assert row0 % tm == 0 and col0 % pg == 0

    def body(d_ref, into_ref, o_ref):
        gi, r = pl.program_id(0), pl.program_id(1)
        start, win, tok, src, half, cnt = _pool_window(r, gi, 0, n, tm)
        band = jnp.logical_and(tok >= src - half, tok < src + half).astype(BF)
        dw = d_ref[pl.ds(pl.multiple_of(start, POOL_HALO), win), :]
        dt = d_ref[pl.ds(pl.multiple_of(r * tm, POOL_HALO), tm), :].astype(F32)
        o_ref[...] = (_dot(band, dw, 1, 0) - dt * cnt).astype(BF)

    return pl.pallas_call(
        body, name="pool_bwd_window", grid=(N_POOL_GROUPS, n // tm), out_shape=jax.ShapeDtypeStruct(into.shape, BF),
        in_specs=[pl.BlockSpec((n, pg), lambda g, r: (0, g)), HBM_SPEC],
        out_specs=pl.BlockSpec((tm, pg), lambda g, r: (r + rb0, g + cb0)),
        input_output_aliases={1: 0}, compiler_params=_cparams("parallel", "parallel"),
    )(dds, into)


def _pool_wgrad(dmat, dval, *, col0, tk):
    n, pw = dmat.shape
    pg = pw // N_POOL_GROUPS
    cb0 = col0 // pg
    nk = n // tk

    def body(a_ref, b_ref, o_ref, acc):
        k = pl.program_id(1)

        @pl.when(k == 0)
        def _():
            acc[...] = jnp.zeros_like(acc)

        acc[...] += _dot(a_ref[...], b_ref[...], 0, 0)

        @pl.when(k == nk - 1)
        def _():
            o_ref[0] = acc[...].astype(BF)

    return pl.pallas_call(
        body, name="pool_wgrad", grid=(N_POOL_GROUPS, nk), out_shape=jax.ShapeDtypeStruct((N_POOL_GROUPS, pg, pg), BF),
        in_specs=[pl.BlockSpec((tk, pg), lambda g, k: (k, g)), pl.BlockSpec((tk, pg), lambda g, k: (k, cb0 + g))],
        out_specs=pl.BlockSpec((1, pg, pg), lambda g, k: (g, 0, 0)), scratch_shapes=[pltpu.VMEM((pg, pg), F32)],
        compiler_params=_cparams("parallel", "arbitrary"),
    )(dmat, dval)


def _merge(attn_o, po_raw, proj, pool_scale, *, l, ga0, gp0, tm):
    n, aw = attn_o.shape
    bw = aw // 2
    rb0 = l // tm
    ga_b, gp_b = ga0 // bw, gp0 // bw
    assert ga0 % bw == 0 and gp0 % bw == 0 and l % tm == 0

    def body(a_ref, p_ref, g_ref, ps_ref, y_ref):
        cb = pl.program_id(1)
        sg, _ = _silu_and_grad(g_ref[...].astype(F32))

        @pl.when(cb < 2)
        def _():
            y_ref[...] = (a_ref[...].astype(F32) * sg).astype(BF)

        @pl.when(cb >= 2)
        def _():
            y_ref[...] = (p_ref[...].astype(F32) * ps_ref[...] * sg).astype(BF)

    return pl.pallas_call(
        body, name="merge", grid=(n // tm, 4), out_shape=jax.ShapeDtypeStruct((n, 2 * aw), BF),
        in_specs=[pl.BlockSpec((tm, bw), lambda i, cb: (i, jnp.minimum(cb, 1))),
                  pl.BlockSpec((tm, bw), lambda i, cb: (i, jnp.maximum(cb - 2, 0))),
                  pl.BlockSpec((tm, bw), lambda i, cb: (i + rb0, jnp.where(cb < 2, ga_b + cb, gp_b + cb - 2))),
                  pl.BlockSpec((1, bw), lambda i, cb: (0, jnp.maximum(cb - 2, 0)))],
        out_specs=pl.BlockSpec((tm, bw), lambda i, cb: (i, cb)), compiler_params=_cparams("parallel", "arbitrary"),
    )(attn_o, po_raw, proj, pool_scale)


def _merge_bwd(dy, attn_o, po_raw, proj, pool_scale, into, *, l, ga0, gp0, tm):
    n, aw = attn_o.shape
    bw = aw // 2
    rb0 = l // tm
    ga_b, gp_b = ga0 // bw, gp0 // bw

    def body(dy_ref, a_ref, p_ref, g_ref, ps_ref, into_ref, dv_ref, dg_ref, dps_ref):
        cb, i = pl.program_id(0), pl.program_id(1)
        sg, sgrad = _silu_and_grad(g_ref[...].astype(F32))
        dyv = dy_ref[...].astype(F32)

        @pl.when(cb < 2)
        def _():
            dv_ref[...] = (dyv * sg).astype(BF)
            dg_ref[...] = (dyv * a_ref[...].astype(F32) * sgrad).astype(BF)

        @pl.when(cb >= 2)
        def _():
            @pl.when(i == 0)
            def _():
                dps_ref[...] = jnp.zeros_like(dps_ref)

            pr = p_ref[...].astype(F32)
            dpo = dyv * sg
            dv_ref[...] = (dpo * ps_ref[...]).astype(BF)
            dg_ref[...] = (dyv * (pr * ps_ref[...]) * sgrad).astype(BF)
            dps_ref[0:1, :] += jnp.sum(dpo * pr, axis=0, keepdims=True)

    blk = pl.BlockSpec((tm, bw), lambda cb, i: (i, cb))
    gate_blk = pl.BlockSpec((tm, bw), lambda cb, i: (i + rb0, jnp.where(cb < 2, ga_b + cb, gp_b + cb - 2)))
    return pl.pallas_call(
        body, name="merge_bwd", grid=(4, n // tm),
        out_shape=(jax.ShapeDtypeStruct((n, 2 * aw), BF), jax.ShapeDtypeStruct(into.shape, BF),
                   jax.ShapeDtypeStruct((8, aw), F32)),
        in_specs=[blk, pl.BlockSpec((tm, bw), lambda cb, i: (i, jnp.minimum(cb, 1))),
                  pl.BlockSpec((tm, bw), lambda cb, i: (i, jnp.maximum(cb - 2, 0))), gate_blk,
                  pl.BlockSpec((1, bw), lambda cb, i: (0, jnp.maximum(cb - 2, 0))), HBM_SPEC],
        out_specs=(blk, gate_blk, pl.BlockSpec((8, bw), lambda cb, i: (0, jnp.maximum(cb - 2, 0)))),
        input_output_aliases={5: 1}, compiler_params=_cparams("arbitrary", "arbitrary"),
    )(dy, attn_o, po_raw, proj, pool_scale, into)


def _post(x, out, target, gate, g_post, *, tm):
    n, d = x.shape

    def body(x_ref, o_ref, t_ref, gate_ref, gp_ref, dxn_ref, dout_ref, st_ref):
        @pl.when(pl.program_id(0) == 0)
        def _():
            st_ref[...] = jnp.zeros_like(st_ref)

        ov = o_ref[...]
        rstd = lax.rsqrt(jnp.mean(ov * ov, axis=-1, keepdims=True) + EPS)
        on = ov * rstd
        rn = on * gp_ref[...]
        err = (x_ref[...] + gate_ref[...] * rn) - t_ref[...]
        dxn = err / d
        dxn_ref[...] = dxn
        drn = dxn * gate_ref[...]
        don = drn * gp_ref[...]
        dout_ref[...] = (rstd * (don - on * jnp.mean(don * on, axis=-1, keepdims=True))).astype(BF)
        st_ref[0:1, :] += jnp.sum(dxn * rn, axis=0, keepdims=True)
        st_ref[1:2, :] += jnp.sum(drn * on, axis=0, keepdims=True)
        st_ref[2:3, :] += jnp.sum(err * err, axis=0, keepdims=True)

    row = pl.BlockSpec((tm, d), lambda i: (i, 0))
    vec = pl.BlockSpec((1, d), lambda i: (0, 0))
    return pl.pallas_call(
        body, name="post", grid=(n // tm,),
        out_shape=(jax.ShapeDtypeStruct((n, d), F32), jax.ShapeDtypeStruct((n, d), BF), jax.ShapeDtypeStruct((8, d), F32)),
        in_specs=[row, row, row, vec, vec], out_specs=(row, row, pl.BlockSpec((8, d), lambda i: (0, 0))),
        compiler_params=_cparams("arbitrary"),
    )(x, out, target, gate, g_post)


def _prenorm_bwd(x, ctx, dh, dxn, g_pre, ss, *, tm):
    n, d = x.shape
    l = ctx.shape[0]
    nc = l // tm

    def body(x_ref, c_ref, dh_ref, dxn_ref, g_ref, ss_ref, gx_ref, st_ref):
        t = pl.program_id(0)

        @pl.when(t == 0)
        def _():
            st_ref[...] = jnp.zeros_like(st_ref)

        def go(src, scale, row):
            v = src[...]
            dhv = dh_ref[...].astype(F32)
            rstd = lax.rsqrt(jnp.mean(v * v, axis=-1, keepdims=True) + EPS)
            xn = v * rstd
            st_ref[row:row + 1, :] += jnp.sum(dhv, axis=0, keepdims=True)
            st_ref[row + 1:row + 2, :] += jnp.sum(dhv * (xn * g_ref[...]), axis=0, keepdims=True)
            dg = dhv * (1.0 + scale)
            st_ref[4:5, :] += jnp.sum(dg * xn, axis=0, keepdims=True)
            dxn_ = dg * g_ref[...]
            return rstd * (dxn_ - xn * jnp.mean(dxn_ * xn, axis=-1, keepdims=True))

        @pl.when(t < nc)
        def _():
            go(c_ref, ss_ref[2:3, :], 2)

        @pl.when(t >= nc)
        def _():
            gx_ref[...] = dxn_ref[...] + go(x_ref, ss_ref[0:1, :], 0)

    lat = pl.BlockSpec((tm, d), lambda t: (jnp.maximum(t - nc, 0), 0))
    return pl.pallas_call(
        body, name="prenorm_bwd", grid=((l + n) // tm,),
        out_shape=(jax.ShapeDtypeStruct((n, d), F32), jax.ShapeDtypeStruct((8, d), F32)),
        in_specs=[lat, pl.BlockSpec((tm, d), lambda t: (jnp.minimum(t, nc - 1), 0)),
                  pl.BlockSpec((tm, d), lambda t: (t, 0)), lat,
                  pl.BlockSpec((1, d), lambda t: (0, 0)), pl.BlockSpec((4, d), lambda t: (0, 0))],
        out_specs=(lat, pl.BlockSpec((8, d), lambda t: (0, 0))), compiler_params=_cparams("arbitrary"),
    )(x, ctx, dh, dxn, g_pre, ss)


def _ada_fwd(craw, w_ada, b_loc, *, tn):
    d, wn = w_ada.shape

    def body(c_ref, w_ref, b_ref, o_ref):
        act, _ = _silu_and_grad(c_ref[...])
        o_ref[...] = _dot(act.astype(BF), w_ref[...].astype(BF), 1, 0) + b_ref[...]

    return pl.pallas_call(
        body, name="ada_fwd", grid=(wn // tn,), out_shape=jax.ShapeDtypeStruct((16, wn), F32),
        in_specs=[pl.BlockSpec((16, d), lambda j: (0, 0)), pl.BlockSpec((d, tn), lambda j: (0, j)),
                  pl.BlockSpec((1, tn), lambda j: (0, j))],
        out_specs=pl.BlockSpec((16, tn), lambda j: (0, j)), compiler_params=_cparams("parallel"),
    )(craw, w_ada, b_loc)


def _ada_bwd(craw_t, dm, w, m, v, *, tm):
    d, wn = w.shape

    def body(ct_ref, dm_ref, w_ref, m_ref, v_ref, g_ref, dl_ref, nm_ref, nv_ref, da_ref):
        act, _ = _silu_and_grad(ct_ref[...])
        dmb = dm_ref[...].astype(BF)
        wv = w_ref[...]
        gv = _dot(act.astype(BF), dmb, 1, 0)
        da_ref[...] = _dot(dmb, wv.astype(BF), 1, 1)
        nm = ADAM_B1 * m_ref[...] + (1.0 - ADAM_B1) * gv
        nv = ADAM_B2 * v_ref[...] + (1.0 - ADAM_B2) * (gv * gv)
        m_hat = nm / (1.0 - ADAM_B1 ** ADAM_STEP)
        v_hat = nv / (1.0 - ADAM_B2 ** ADAM_STEP)
        g_ref[...] = gv
        dl_ref[...] = -ADAM_LR * (m_hat / (jnp.sqrt(v_hat) + ADAM_EPS) + ADAM_WD * wv)
        nm_ref[...] = nm
        nv_ref[...] = nv

    row = pl.BlockSpec((tm, wn), lambda i: (i, 0))
    return pl.pallas_call(
        body, name="ada_bwd", grid=(d // tm,),
        out_shape=tuple([jax.ShapeDtypeStruct((d, wn), F32)] * 4) + (jax.ShapeDtypeStruct((16, d), F32),),
        in_specs=[pl.BlockSpec((tm, 16), lambda i: (i, 0)), pl.BlockSpec((16, wn), lambda i: (0, 0)), row, row, row],
        out_specs=(row, row, row, row, pl.BlockSpec((16, tm), lambda i: (0, i))),
        compiler_params=_cparams("parallel"),
    )(craw_t, dm, w, m, v)


def _reduce_small(gath, d3):
    t = gath.shape[-1]

    def body(g_ref, dm_ref, s_ref):
        tot = g_ref[0]
        for b in range(1, 8):
            tot = tot + g_ref[b]
        s_ref[...] = tot
        dm_ref[...] = jnp.zeros_like(dm_ref)
        for b in range(8):
            dm_ref[b:b + 1, :] = g_ref[b][:, 0:d3]
        dm_ref[8:9, :] = tot[:, d3:2 * d3]

    return pl.pallas_call(
        body, name="reduce_small", out_shape=(jax.ShapeDtypeStruct((16, d3), F32), jax.ShapeDtypeStruct((1, t), F32)),
        in_specs=[VMEM_SPEC], out_specs=(VMEM_SPEC, VMEM_SPEC),
    )(gath)


def _cctx_grad(parts, c_ctx):
    def body(p_ref, c_ref, o_ref):
        tot = (p_ref[0] + p_ref[1]) + (p_ref[2] + p_ref[3])
        _, sgrad = _silu_and_grad(c_ref[...])
        o_ref[...] = tot * sgrad

    return pl.pallas_call(
        body, name="cctx_grad", out_shape=jax.ShapeDtypeStruct(c_ctx.shape, F32),
        in_specs=[VMEM_SPEC, VMEM_SPEC], out_specs=VMEM_SPEC,
    )(parts, c_ctx)


def _rope_tables(n, l):
    rows = n // GRID_W
    row = jnp.repeat(jnp.arange(rows, dtype=F32), GRID_W)
    col = jnp.tile(jnp.arange(GRID_W, dtype=F32), rows)
    inv = ROPE_THETA ** (-jnp.arange(ROPE_PAIRS, dtype=F32) / ROPE_PAIRS)
    ang = jnp.concatenate([row[:, None] * inv, col[:, None] * inv], axis=-1)
    cos, sin = jnp.cos(ang), jnp.sin(ang)
    cr, cc, sr, sc = cos[:, :ROPE_PAIRS], cos[:, ROPE_PAIRS:], sin[:, :ROPE_PAIRS], sin[:, ROPE_PAIRS:]
    ctab = jnp.concatenate([cr, cr, cc, cc], axis=-1)
    stab = jnp.concatenate([-sr, sr, -sc, sc], axis=-1)
    ctab = jnp.concatenate([jnp.ones((l, HEAD_DIM), F32), ctab], axis=0)
    stab = jnp.concatenate([jnp.zeros((l, HEAD_DIM), F32), stab], axis=0)
    return ctab, stab


def kernel(x, c, ctx, c_ctx, w_ada, b_ada, norm_pre, norm_post, w_in, q_norm, k_norm, pool_w, pool_scale, w_out, loss_target, m_c_ctx, m_w_ada, m_b_ada, m_norm_pre, m_norm_post, m_w_in, m_q_norm, m_k_norm, m_pool_w, m_pool_scale, m_w_out, v_c_ctx, v_w_ada, v_b_ada, v_norm_pre, v_norm_post, v_w_in, v_q_norm, v_k_norm, v_pool_w, v_pool_scale, v_w_out):
    x2, ctx2, tgt = x[0], ctx[0], loss_target[0]
    n, d = x2.shape
    l = ctx2.shape[0]
    s_len = l + n
    aw = d // 2
    pw = d - aw
    n_heads = aw // HEAD_DIM
    kvw = (n_heads // GQA_GROUP) * HEAD_DIM
    pg = pw // N_POOL_GROUPS
    k0, v0, ga0 = aw, aw + kvw, aw + 2 * kvw
    up0, gp0 = ga0 + aw, ga0 + aw + pw
    in_w = gp0 + pw
    d3 = 3 * d
    ada_w = w_ada.shape[-1]
    px, py, pc = lax.axis_index("x"), lax.axis_index("y"), lax.axis_index("c")
    chip = 2 * px + py
    tr = min(256, l)
    tq = min(2048, n)
    tk = min(256, l)
    ts = s_len // 8 if (s_len // 8) % 16 == 0 else tr
    scale = HEAD_DIM ** -0.5

    c_all = _allgather_small(c, chips_only=False, name="gather_c").reshape(8, d)
    craw = jnp.concatenate([c_all, c_ctx.reshape(1, d), jnp.zeros((7, d), F32)], axis=0)
    b_loc = lax.dynamic_slice(b_ada, (0, chip * ada_w), (1, ada_w))
    mod_part = _ada_fwd(craw, w_ada[0], b_loc, tn=min(512, ada_w))
    mod_all = _allgather_small(mod_part, chips_only=True, name="gather_mod")
    mod_all = jnp.transpose(mod_all, (1, 0, 2)).reshape(16, d3)
    me = 4 * px + 2 * py + pc
    mod_b = lax.dynamic_slice(mod_all, (me, 0), (1, d3))
    mod_c = mod_all[8:9]
    ss = jnp.concatenate([mod_b[:, d:2 * d], mod_b[:, 0:d], mod_c[:, d:2 * d], mod_c[:, 0:d]], axis=0)
    gate = mod_b[:, 2 * d:]

    w_own = _cast_bf16(w_in[0], name="cast_w_in")
    later_weights = _gather_side([_cast_into(w_out[0], 0, chip, name="cast_w_out"),
                                  _cast_into(pool_w[0], 1, chip, name="cast_pool_w")], [(0, 1), (1, 2)])
    ids = jnp.stack([chip, jnp.bitwise_xor(chip, 2), jnp.bitwise_xor(chip, 1), jnp.bitwise_xor(chip, 3)])
    ws = in_w // 4
    tn_p = ws // 3 if (ws // 3) % HEAD_DIM == 0 else ws

    ctab, stab = _rope_tables(n, l)
    proj, h, w_x, w_y = _in_proj_own(x2, ctx2, norm_pre, ss, w_own, ids[0:1], width=in_w, tm=tr, tn=tn_p,
                                     side=_gather_parts_side(w_own, (0, 1)))
    proj, w_d = _in_proj_parts(h, [w_x, w_y], ids[1:3], proj, width=in_w, tm=ts, tn=tn_p, name="in_proj_xy",
                               side=_gather_parts_side(w_own, (2,)))
    (proj,) = _in_proj_parts(h, [w_d], ids[3:4], proj, width=in_w, tm=ts, tn=tn_p, name="in_proj_diag")
    w_parts = [w_own, w_x, w_y, w_d]
    qr = _qk_prep(proj, q_norm, ctab[l:], stab[l:], row0=l, col0=0, width=aw, nrows=n, tm=tr, name="q_prep",
                  out_scale=scale * LOG2E)
    kr = _qk_prep(proj, k_norm, ctab, stab, row0=0, col0=k0, width=kvw, nrows=s_len, tm=tr, name="k_prep")
    v_all = proj[:, v0:v0 + kvw]
    tkf = min(4096, n)
    vt_ctx = jnp.transpose(v_all[:l].reshape(1, l, kvw // HEAD_DIM, HEAD_DIM), (2, 0, 3, 1))
    vt_lat = jnp.transpose(v_all[l:].reshape(n // tkf, tkf, kvw // HEAD_DIM, HEAD_DIM), (2, 0, 3, 1))
    attn_o, lse, w_out_f, pool_f = _flash_fwd(qr, kr, vt_ctx, vt_lat, tq=tq, side=later_weights)
    dmat, po_raw = _pool_fwd(proj, pool_f, l=l, n=n, col0=up0, tm=min(512, n // 2))
    yv = _merge(attn_o, po_raw, proj, pool_scale, l=l, ga0=ga0, gp0=gp0, tm=tr)
    out = _matmul(yv, w_out_f, ca=1, cb=0, tm=min(512, n), tn=min(1024, d), tk=d, out_dtype=F32, name="out_proj")

    dxn, d_out, post_st = _post(x2, out, tgt, gate, norm_post, tm=min(128, n))
    loss = 0.5 * jnp.sum(post_st[2]) / d
    loss = lax.psum(loss, ("x", "y", "c"))
    g_wout = _matmul(yv, d_out, ca=0, cb=0, tm=min(512, d), tn=min(1024, d), tk=n, out_dtype=BF, name="wgrad_out")
    dy, from_sib_wout = _matmul(d_out, w_out_f, ca=1, cb=1, tm=min(512, n), tn=min(1024, d), tk=d, out_dtype=BF,
                                name="dgrad_out", side=_pair_side([g_wout], [1]))
    pair_wout = _add_own_half(g_wout, from_sib_wout, 1, pc, name="pair_sum_1")
    dproj = _zero_rows(l, (s_len, in_w), name="dproj_init")
    dval, dproj, dps = _merge_bwd(dy, attn_o, po_raw, proj, pool_scale, dproj, l=l, ga0=ga0, gp0=gp0, tm=tr)
    g_pool = _pool_wgrad(dmat, dval, col0=aw, tk=min(512, n))
    dds = _pool_bwd_map(dval, pool_f, n=n, col0=aw, tm=min(1024, n))
    dproj = _pool_bwd_window(dds, dproj, n=n, tm=tr, row0=l, col0=up0)
    delta = _attn_delta(attn_o, dval, tq=tq)
    tqb = min(4096, n)
    stat_shape = (kvw // HEAD_DIM, GQA_GROUP, n // tqb, 1, tqb)
    dq, dk, dproj, *from_chips_wout = _flash_bwd(
        qr, dval, kr, proj, lse.reshape(stat_shape), delta.reshape(stat_shape), dproj, v_col0=v0, tq=tqb, tk=tk,
        scale=scale, side=_chips_side([pair_wout], [0]))
    red_wout = _sum_own_block(pair_wout, from_chips_wout, 0, 1, chip, pc, name="chip_sum_1")
    dproj, dgq = _qk_bwd(dq, proj, q_norm, ctab[l:], stab[l:], dproj, row0=l, col0=0, tm=tr, name="q_bwd", transposed=True)
    dproj, dgk = _qk_bwd(dk, proj, k_norm, ctab, stab, dproj, row0=0, col0=k0, tm=tr, name="k_bwd")
    tn_w = 1024 if in_w % 1024 == 0 else 512
    g_send, g_wout_s = _wgrad_rows(h, dproj, 1 - pc, tm=min(512, d // 2), tn=tn_w, name="wgrad_in_send",
                                   side=_join_side([red_wout], [1]))
    g_keep, from_sib_win, from_sib_pool = _wgrad_rows(
        h, dproj, pc, tm=min(512, d // 2), tn=tn_w, name="wgrad_in_keep",
        side=_both(_swap_side([g_send]), _pair_side([g_pool], [2])))
    pair_win = _add2(g_keep, from_sib_win, name="pair_sum_0")
    pair_pool = _add_own_half(g_pool, from_sib_pool, 2, pc, name="pair_sum_2")
    dh, *from_chips = _dgrad_in_parts(dproj, w_parts, ids, tm=ts, tn=min(512, d), name="dgrad_in",
                                      side=_chips_side([pair_win, pair_pool], [1, 1]))
    red_win = _sum_own_block(pair_win, from_chips[0:3], 1, 0, chip, pc, name="chip_sum_0")
    red_pool = _sum_own_block(pair_pool, from_chips[3:6], 1, 2, chip, pc, name="chip_sum_2")
    grad_x, pre_st = _prenorm_bwd(x2, ctx2, dh, dxn, norm_pre, ss, tm=min(128, l))

    zero_d = jnp.zeros((1, d), F32)
    packed = jnp.concatenate([pre_st[0:1], pre_st[1:2], post_st[0:1], pre_st[2:3], pre_st[3:4], zero_d,
                              pre_st[4:5], post_st[1:2], dgq[0:1], dgk[0:1], dps[0:1]], axis=1)
    gath, g_win_s, g_pool_s = _allgather_small(packed, chips_only=False, name="gather_small",
                                               side=_join_side([red_win, red_pool], [0, 2]))
    dm, sums = _reduce_small(gath, d3)
    o1 = 2 * d3
    g_npre, g_npost = sums[:, o1:o1 + d], sums[:, o1 + d:o1 + 2 * d]
    g_q, g_k = sums[:, o1 + 2 * d:o1 + 2 * d + HEAD_DIM], sums[:, o1 + 2 * d + HEAD_DIM:o1 + 2 * d + 2 * HEAD_DIM]
    g_ps = sums[:, o1 + 2 * d + 2 * HEAD_DIM:]
    g_bada = sums[:, 0:d3] + sums[:, d3:2 * d3]
    dm_loc = lax.dynamic_slice(dm, (0, chip * ada_w), (16, ada_w))
    g_wada, dl_wada, nm_wada, nv_wada, dact = _ada_bwd(craw.T, dm_loc, w_ada[0], m_w_ada[0], v_w_ada[0], tm=128)
    cparts = _allgather_small(dact[8:9], chips_only=True, name="gather_cctx")
    g_cctx = _cctx_grad(cparts, c_ctx.reshape(1, d)).reshape(d)

    def upd(w, g, m, v, name):
        return _adamw(w, g.reshape(w.shape), m, v, name=name)

    grads = {"c_ctx": g_cctx, "b_ada": g_bada, "norm_pre": g_npre, "norm_post": g_npost, "w_in": g_win_s[None],
             "q_norm": g_q, "k_norm": g_k, "pool_w": g_pool_s[None], "pool_scale": g_ps, "w_out": g_wout_s[None]}
    res = {"w_ada": (g_wada[None], dl_wada[None], nm_wada[None], nv_wada[None])}
    given = {"c_ctx": (c_ctx, m_c_ctx, v_c_ctx), "b_ada": (b_ada, m_b_ada, v_b_ada),
             "norm_pre": (norm_pre, m_norm_pre, v_norm_pre), "norm_post": (norm_post, m_norm_post, v_norm_post),
             "w_in": (w_in, m_w_in, v_w_in), "q_norm": (q_norm, m_q_norm, v_q_norm), "k_norm": (k_norm, m_k_norm, v_k_norm),
             "pool_w": (pool_w, m_pool_w, v_pool_w), "pool_scale": (pool_scale, m_pool_scale, v_pool_scale),
             "w_out": (w_out, m_w_out, v_w_out)}
    for nme, (w, m, v) in given.items():
        g = grads[nme].reshape(w.shape)
        res[nme] = upd(w, g, m, v, "adamw_" + nme)
    order = ["c_ctx", "w_ada", "b_ada", "norm_pre", "norm_post", "w_in", "q_norm", "k_norm", "pool_w", "pool_scale", "w_out"]
    return (loss, grad_x[None], *[res[k][0] for k in order], *[res[k][1] for k in order],
            *[res[k][2] for k in order], *[res[k][3] for k in order])
```
